```python
import math
import jax, jax.numpy as jnp
from jax import lax
import numpy as np

D_MODEL = 1024
BATCH = 4
SEQ = 4096
DEPTH = 2
DEC_BATCH = 32
DEC_SEQ = 1
PAST_LEN = 8192
PAGE_SIZE = 128

N_A = DEPTH // 2
HEAD_DIM = 64
MIX_WIDTH = D_MODEL
MEM_HEADS = 4
MEM_WIDTH = MEM_HEADS * HEAD_DIM
SEQ_WIDTH = MIX_WIDTH - MEM_WIDTH
SSM_GROUP = 16
SSM_GROUPS = SEQ_WIDTH // SSM_GROUP
SSM_STATE = 64
FOX_HEADS = SEQ_WIDTH // HEAD_DIM
N_MEM = 256
D_FF = (11 * D_MODEL) // 4
CONV_W = 3
Q_BLOCK = 128
EPS = 1e-6
NEG_INF = -1e30
EIG_CLIP = -1e-4
ATTN_SCALE = HEAD_DIM ** -0.5

kernel_name = 's5_fox_yoco_decoder_step'

F32 = jnp.float32


def rmsnorm(x, g):
    xf = x.astype(F32)
    y = xf * lax.rsqrt(jnp.mean(xf * xf, axis=-1, keepdims=True) + EPS)
    return y.astype(x.dtype) * g


def s5_discretize(lam_re, lam_im, log_dt, b_re, b_im):
    lr = jnp.minimum(lam_re.astype(F32), EIG_CLIP)
    li = lam_im.astype(F32)
    dt = jnp.exp(log_dt.astype(F32))[:, None]
    mag = jnp.exp(lr * dt)
    ar = mag * jnp.cos(li * dt)
    ai = mag * jnp.sin(li * dt)
    den = lr * lr + li * li
    nr = ar - 1.0
    fr = (nr * lr + ai * li) / den
    fi = (ai * lr - nr * li) / den
    br = b_re.astype(F32)
    bi = b_im.astype(F32)
    bbr = fr[..., None] * br - fi[..., None] * bi
    bbi = fr[..., None] * bi + fi[..., None] * br
    return ar, ai, bbr, bbi


def _ssm_combine(e1, e2):
    a1r, a1i, b1r, b1i = e1
    a2r, a2i, b2r, b2i = e2
    return (a2r * a1r - a2i * a1i, a2r * a1i + a2i * a1r,
            a2r * b1r - a2i * b1i + b2r, a2r * b1i + a2i * b1r + b2i)


def s5_mix(u, h0r, h0i, lam_re, lam_im, log_dt, b_re, b_im, c_re, c_im, d_skip, w_glu, b_glu):
    bsz, t, _ = u.shape
    ar, ai, bbr, bbi = s5_discretize(lam_re, lam_im, log_dt, b_re, b_im)
    uf = u.astype(F32).reshape(bsz, t, SSM_GROUPS, SSM_GROUP)
    bu_r = jnp.einsum('btgc,gpc->btgp', uf, bbr)
    bu_i = jnp.einsum('btgc,gpc->btgp', uf, bbi)
    a_r = jnp.broadcast_to(ar, (1, t) + ar.shape)
    a_i = jnp.broadcast_to(ai, (1, t) + ai.shape)
    pr, pim, hr, hi = lax.associative_scan(_ssm_combine, (a_r, a_i, bu_r, bu_i), axis=1)
    h0r = h0r.astype(F32)[:, None]
    h0i = h0i.astype(F32)[:, None]
    hr = hr + pr * h0r - pim * h0i
    hi = hi + pr * h0i + pim * h0r
    y = (jnp.einsum('btgp,gcp->btgc', hr, c_re.astype(F32))
         - jnp.einsum('btgp,gcp->btgc', hi, c_im.astype(F32)))
    y = y.reshape(bsz, t, SEQ_WIDTH) + d_skip.astype(F32) * u.astype(F32)
    y = jax.nn.gelu(y).astype(u.dtype)
    y = y * jax.nn.sigmoid(y @ w_glu + b_glu)
    return y, hr[:, -1], hi[:, -1]


def mem_kv(mem, g, w):
    kv = rmsnorm(mem, g) @ w
    k, v = jnp.split(kv, 2, axis=-1)
    shape = mem.shape[:2] + (MEM_HEADS, HEAD_DIM)
    return k.reshape(shape), v.reshape(shape)


def mem_attention(q, mk, mv):
    s = jnp.einsum('bthd,bmhd->bhtm', q, mk).astype(F32) * ATTN_SCALE
    p = jax.nn.softmax(s, axis=-1).astype(mv.dtype)
    return jnp.einsum('bhtm,bmhd->bthd', p, mv)


def shared_kv(s, g, w_kv, b_f):
    bsz, t, _ = s.shape
    z = rmsnorm(s, g) @ w_kv
    k = z[..., :SEQ_WIDTH].reshape(bsz, t, FOX_HEADS, HEAD_DIM)
    v = z[..., SEQ_WIDTH:2 * SEQ_WIDTH].reshape(bsz, t, FOX_HEADS, HEAD_DIM)
    logf = jax.nn.log_sigmoid(z[..., 2 * SEQ_WIDTH:].astype(F32) + b_f.astype(F32)).astype(s.dtype)
    return k, v, logf


def fox_block(q, k, v, cq, ck, qpos, kpos):
    s = jnp.einsum('bqhd,bkhd->bhqk', q, k).astype(F32) * ATTN_SCALE
    s = s + jnp.transpose(cq, (0, 2, 1))[..., None] - jnp.transpose(ck, (0, 2, 1))[:, :, None, :]
    mask = kpos[None, :] <= qpos[:, None]
    s = jnp.where(mask, s, NEG_INF)
    p = jax.nn.softmax(s, axis=-1).astype(v.dtype)
    return jnp.einsum('bhqk,bkhd->bqhd', p, v)


def fox_prompt(q, k, v, logf):
    bsz, t, h, dh = q.shape
    nb = t // Q_BLOCK
    c = jnp.cumsum(logf.astype(F32), axis=1)
    qb = q.reshape(bsz, nb, Q_BLOCK, h, dh).swapaxes(0, 1)
    cqb = c.reshape(bsz, nb, Q_BLOCK, h).swapaxes(0, 1)
    pos = jnp.arange(t, dtype=jnp.int32)
    posb = pos.reshape(nb, Q_BLOCK)
    out = lax.map(lambda blk: fox_block(blk[0], k, v, blk[1], c, blk[2], pos), (qb, cqb, posb))
    return out.swapaxes(0, 1).reshape(bsz, t, h, dh)


def fox_sample(q, k_new, v_new, logf_new, k_past, v_past, logf_past):
    k = jnp.concatenate([k_past, k_new.astype(k_past.dtype)], axis=1)
    v = jnp.concatenate([v_past, v_new.astype(v_past.dtype)], axis=1)
    logf = jnp.concatenate([logf_past.astype(F32), logf_new.astype(F32)], axis=1)
    c = jnp.cumsum(logf, axis=1)
    t_new = q.shape[1]
    total = k.shape[1]
    past = total - t_new
    kpos = jnp.arange(total, dtype=jnp.int32)
    return fox_block(q, k, v, c[:, past:], c, kpos[past:], kpos)


def causal_dwconv(u, prev, w, b):
    t = u.shape[1]
    ext = jnp.concatenate([prev.astype(u.dtype), u], axis=1)
    y = ext[:, 0:t] * w[0]
    for j in range(1, CONV_W):
        y = y + ext[:, j:j + t] * w[j]
    return y + b, ext[:, t:]


def conv_ffn(h, w_up, cw, cb, w_down, prev):
    u = h @ w_up
    y, new_prev = causal_dwconv(u, prev, cw, cb)
    g, val = jnp.split(y, 2, axis=-1)
    return (jax.nn.silu(g) * val) @ w_down, new_prev


def run_trunk(x, mem_ks, mem_vs, ssm_r0, ssm_i0, conv_prev, fox_attend, p):
    bsz, t, _ = x.shape
    new_r, new_i, new_conv = [], [], []
    kv_rows = None
    for l in range(DEPTH):
        if l == N_A:
            kv_rows = shared_kv(x, p['kv_norm'], p['w_kv'], p['b_f'])
        h = rmsnorm(x, p['norm_mix_pre'][l])
        z = h @ p['w_in'][l]
        z_seq = z[..., :SEQ_WIDTH]
        q_mem = z[..., SEQ_WIDTH:].reshape(bsz, t, MEM_HEADS, HEAD_DIM)
        if l < N_A:
            seq_out, hr, hi = s5_mix(z_seq, ssm_r0[l], ssm_i0[l], p['lam_re'][l], p['lam_im'][l],
                                     p['log_dt'][l], p['b_re'][l], p['b_im'][l], p['c_re'][l],
                                     p['c_im'][l], p['d_skip'][l], p['w_glu'][l], p['b_glu'][l])
            new_r.append(hr)
            new_i.append(hi)
        else:
            q = z_seq.reshape(bsz, t, FOX_HEADS, HEAD_DIM)
            seq_out = fox_attend(q, kv_rows[0], kv_rows[1], kv_rows[2]).reshape(bsz, t, SEQ_WIDTH)
        mem_out = mem_attention(q_mem, mem_ks[l], mem_vs[l]).reshape(bsz, t, MEM_WIDTH)
        o = jnp.concatenate([seq_out.astype(x.dtype), mem_out.astype(x.dtype)], axis=-1) @ p['w_out'][l]
        x = x + rmsnorm(o, p['norm_mix_post'][l])
        f, cp = conv_ffn(rmsnorm(x, p['norm_ffn_pre'][l]), p['w_up'][l], p['conv_w'][l],
                         p['conv_b'][l], p['w_down'][l], conv_prev[l])
        new_conv.append(cp)
        x = x + rmsnorm(f, p['norm_ffn_post'][l])
    return x, jnp.stack(new_r), jnp.stack(new_i), jnp.stack(new_conv), kv_rows


def setup_inputs(seed: int = 0) -> dict:
    key = jax.random.key(seed)
    ks = iter(jax.random.split(key, 48))

    def nrm(shape, scale):
        return jax.random.normal(next(ks), shape, F32) * scale

    n_pages = PAST_LEN // PAGE_SIZE
    n_used = DEC_BATCH * n_pages
    n_pool = n_used + max(1, n_used // 4)
    f2 = 2 * D_FF
    g, pp, cc = SSM_GROUPS, SSM_STATE, SSM_GROUP
    d = D_MODEL
    out = {}
    out['x_prompt'] = nrm((BATCH, SEQ, d), 1.0)
    out['x_sample'] = nrm((DEC_BATCH, DEC_SEQ, d), 1.0)
    out['state_ssm_re'] = nrm((N_A, DEC_BATCH, g, pp), 0.5)
    out['state_ssm_im'] = nrm((N_A, DEC_BATCH, g, pp), 0.5)
    out['cache_k'] = nrm((n_pool, PAGE_SIZE, FOX_HEADS, HEAD_DIM), 1.0)
    out['cache_v'] = nrm((n_pool, PAGE_SIZE, FOX_HEADS, HEAD_DIM), 1.0)
    out['cache_logf'] = jax.nn.log_sigmoid(nrm((n_pool, PAGE_SIZE, FOX_HEADS), 1.0) + 2.5)
    out['cache_mem_k'] = nrm((DEPTH, DEC_BATCH, N_MEM, MEM_HEADS, HEAD_DIM), 1.0)
    out['cache_mem_v'] = nrm((DEPTH, DEC_BATCH, N_MEM, MEM_HEADS, HEAD_DIM), 1.0)
    out['state_ffn_conv'] = nrm((DEPTH, DEC_BATCH, CONV_W - 1, f2), 1.0)
    out['page_table'] = jax.random.permutation(next(ks), n_pool)[:n_used].reshape(DEC_BATCH, n_pages).astype(jnp.int32)
    out['mem_prompt'] = nrm((BATCH, N_MEM, d), 1.0)
    out['w_in'] = nrm((DEPTH, d, MIX_WIDTH), d ** -0.5)
    out['w_out'] = nrm((DEPTH, MIX_WIDTH, d), MIX_WIDTH ** -0.5)
    out['norm_mix_pre'] = 1.0 + nrm((DEPTH, d), 0.05)
    out['norm_mix_post'] = 1.0 + nrm((DEPTH, d), 0.05)
    out['norm_ffn_pre'] = 1.0 + nrm((DEPTH, d), 0.05)
    out['norm_ffn_post'] = 1.0 + nrm((DEPTH, d), 0.05)
    out['mem_norm'] = 1.0 + nrm((DEPTH, d), 0.05)
    out['w_mem_kv'] = nrm((DEPTH, d, 2 * MEM_WIDTH), d ** -0.5)
    out['lam_re'] = -0.5 + nrm((N_A, g, pp), 0.01)
    out['lam_im'] = math.pi * jnp.arange(pp, dtype=F32) + nrm((N_A, g, pp), 0.01)
    out['log_dt'] = jax.random.uniform(next(ks), (N_A, g), F32, math.log(1e-3), math.log(1e-1))
    out['b_re'] = nrm((N_A, g, pp, cc), (2.0 * cc) ** -0.5)
    out['b_im'] = nrm((N_A, g, pp, cc), (2.0 * cc) ** -0.5)
    out['c_re'] = nrm((N_A, g, cc, pp), 0.5)
    out['c_im'] = nrm((N_A, g, cc, pp), 0.5)
    out['d_skip'] = nrm((N_A, SEQ_WIDTH), 0.5)
    out['w_glu'] = nrm((N_A, SEQ_WIDTH, SEQ_WIDTH), SEQ_WIDTH ** -0.5)
    out['b_glu'] = nrm((N_A, SEQ_WIDTH), 0.01)
    out['kv_norm'] = 1.0 + nrm((d,), 0.05)
    out['w_kv'] = nrm((d, 2 * SEQ_WIDTH + FOX_HEADS), d ** -0.5)
    out['b_f'] = jax.random.uniform(next(ks), (FOX_HEADS,), F32, 1.0, 4.0)
    out['w_up'] = nrm((DEPTH, d, f2), d ** -0.5)
    out['conv_w'] = nrm((DEPTH, CONV_W, f2), CONV_W ** -0.5)
    out['conv_b'] = nrm((DEPTH, f2), 0.01)
    out['w_down'] = nrm((DEPTH, D_FF, d), D_FF ** -0.5)
    return out


def reference(x_prompt, x_sample, state_ssm_re, state_ssm_im, cache_k, cache_v, cache_logf,
              cache_mem_k, cache_mem_v, state_ffn_conv, page_table, mem_prompt,
              w_in, w_out, norm_mix_pre, norm_mix_post, norm_ffn_pre, norm_ffn_post,
              mem_norm, w_mem_kv, lam_re, lam_im, log_dt, b_re, b_im, c_re, c_im, d_skip,
              w_glu, b_glu, kv_norm, w_kv, b_f, w_up, conv_w, conv_b, w_down):
    p = dict(w_in=w_in, w_out=w_out, norm_mix_pre=norm_mix_pre, norm_mix_post=norm_mix_post,
             norm_ffn_pre=norm_ffn_pre, norm_ffn_post=norm_ffn_post, lam_re=lam_re, lam_im=lam_im,
             log_dt=log_dt, b_re=b_re, b_im=b_im, c_re=c_re, c_im=c_im, d_skip=d_skip,
             w_glu=w_glu, b_glu=b_glu, kv_norm=kv_norm, w_kv=w_kv, b_f=b_f, w_up=w_up,
             conv_w=conv_w, conv_b=conv_b, w_down=w_down)

    bp = x_prompt.shape[0]
    mem_pairs = [mem_kv(mem_prompt, mem_norm[l], w_mem_kv[l]) for l in range(DEPTH)]
    p_mem_k = jnp.stack([m[0] for m in mem_pairs])
    p_mem_v = jnp.stack([m[1] for m in mem_pairs])
    zeros_ssm = jnp.zeros((N_A, bp, SSM_GROUPS, SSM_STATE), F32)
    zeros_conv = jnp.zeros((DEPTH, bp, CONV_W - 1, 2 * D_FF), x_prompt.dtype)
    y_prompt, p_ssm_re, p_ssm_im, p_conv, p_kv = run_trunk(
        x_prompt, p_mem_k, p_mem_v, zeros_ssm, zeros_ssm, zeros_conv, fox_prompt, p)
    p_k, p_v, p_logf = p_kv

    bs = x_sample.shape[0]
    k_past = cache_k[page_table].reshape(bs, -1, FOX_HEADS, HEAD_DIM)
    v_past = cache_v[page_table].reshape(bs, -1, FOX_HEADS, HEAD_DIM)
    logf_past = cache_logf[page_table].reshape(bs, -1, FOX_HEADS)

    def fox_attend_sample(q, k_new, v_new, logf_new):
        return fox_sample(q, k_new, v_new, logf_new, k_past, v_past, logf_past)

    y_sample, s_ssm_re, s_ssm_im, s_conv, s_kv = run_trunk(
        x_sample, cache_mem_k, cache_mem_v, state_ssm_re, state_ssm_im, state_ffn_conv,
        fox_attend_sample, p)
    s_k, s_v, s_logf = s_kv

    return (y_prompt, y_sample, p_ssm_re, p_ssm_im, p_k, p_v, p_logf, p_mem_k, p_mem_v, p_conv,
            s_ssm_re, s_ssm_im, s_k, s_v, s_logf, s_conv)
```

```python
import functools
import math

import jax
import jax.numpy as jnp
from jax import lax
from jax.experimental import pallas as pl
from jax.experimental.pallas import tpu as pltpu

F32 = jnp.float32
BF16 = jnp.bfloat16

D_MODEL = 1024
DEPTH = 2
N_A = DEPTH // 2
HEAD_DIM = 64
MEM_HEADS = 4
MEM_WIDTH = MEM_HEADS * HEAD_DIM
SEQ_WIDTH = D_MODEL - MEM_WIDTH
SSM_GROUP = 16
SSM_GROUPS = SEQ_WIDTH // SSM_GROUP
SSM_STATE = 64
SSM_WIDTH = SSM_GROUPS * SSM_STATE
FOX_HEADS = SEQ_WIDTH // HEAD_DIM
D_FF = (11 * D_MODEL) // 4
CONV_W = 3
EPS = 1e-6
NEG_INF = -1e30
EIG_CLIP = -1e-4
ATTN_SCALE = HEAD_DIM ** -0.5

LANES = 128
SUBLANES = 8
MXU_DIM = 256
VMEM_BYTES_V7X = 64 * 1024 * 1024
VMEM_LIMIT = (VMEM_BYTES_V7X * 7) // 8

SSM_BLOCKS = SEQ_WIDTH // MXU_DIM
SSM_BLOCK_STATES = SSM_WIDTH // SSM_BLOCKS
HEAD_PAIRS = FOX_HEADS // 2


def _params(semantics, vmem=None):
    return pltpu.CompilerParams(dimension_semantics=semantics, vmem_limit_bytes=vmem)


def _row_tile(m, cap):
    t = min(m, cap)
    assert m % t == 0, (m, t)
    return t


def _rms(x, g):
    return x * lax.rsqrt(jnp.mean(x * x, axis=-1, keepdims=True) + EPS) * g


def _sigmoid(x):
    return 1.0 / (1.0 + jnp.exp(-x))


def _log_sigmoid(x):
    return -(jnp.maximum(-x, 0.0) + jnp.log1p(jnp.exp(-jnp.abs(x))))


def _split3(x):
    hi = x.astype(BF16)
    r1 = x - hi.astype(F32)
    mid = r1.astype(BF16)
    lo = (r1 - mid.astype(F32)).astype(BF16)
    return hi, mid, lo


def _dot(a, b):
    return jnp.dot(a, b, preferred_element_type=F32)


def _dot_nt(a, b):
    return lax.dot_general(a, b, (((1,), (1,)), ((), ())), preferred_element_type=F32)


def _norm_linear_kernel(x_ref, g_ref, w_ref, *out_refs, splits):
    h = _rms(x_ref[...], g_ref[...]).astype(BF16)
    c0 = 0
    for o_ref, n in zip(out_refs, splits):
        o_ref[...] = _dot(h, w_ref[:, c0:c0 + n])
        c0 += n


def norm_linear(x, g, w, splits, tm_cap=512):
    m, d = x.shape
    tm = _row_tile(m, tm_cap)
    n_tot = sum(splits)
    assert w.shape == (d, n_tot)
    return pl.pallas_call(
        functools.partial(_norm_linear_kernel, splits=tuple(splits)),
        grid=(m // tm,),
        in_specs=[pl.BlockSpec((tm, d), lambda i: (i, 0)),
                  pl.BlockSpec((1, d), lambda i: (0, 0)),
                  pl.BlockSpec((d, n_tot), lambda i: (0, 0))],
        out_specs=[pl.BlockSpec((tm, n), lambda i: (i, 0)) for n in splits],
        out_shape=[jax.ShapeDtypeStruct((m, n), F32) for n in splits],
        compiler_params=_params(("parallel",)),
        name="norm_linear",
    )(x, g.reshape(1, d), w)


def _kv_kernel(x_ref, g_ref, wk_ref, wv_ref, wf_ref, bf_ref, tri_ref,
               k_ref, v_ref, kb_ref, vb_ref, lf_ref, c_ref, carry_ref, *, with_cumsum):
    h = _rms(x_ref[0], g_ref[...]).astype(BF16)
    k = _dot(h, wk_ref[...])
    v = _dot(h, wv_ref[...])
    k_ref[0] = k
    v_ref[0] = v
    kb_ref[0] = k.astype(BF16)
    vb_ref[0] = v.astype(BF16)
    logf = _log_sigmoid(_dot(h, wf_ref[...]) + bf_ref[...])
    lf_ref[0] = logf[:, :FOX_HEADS]
    if with_cumsum:
        @pl.when(pl.program_id(1) == 0)
        def _():
            carry_ref[...] = jnp.zeros_like(carry_ref)
        hi, mid, lo = _split3(logf)
        tri = tri_ref[...]
        cum = (_dot(tri, hi) + _dot(tri, mid)) + _dot(tri, lo) + carry_ref[...]
        c_ref[0] = cum[:, :FOX_HEADS]
        carry_ref[...] = cum[cum.shape[0] - 1:, :]
    else:
        c_ref[0] = logf[:, :FOX_HEADS]


def shared_kv_proj(x, g, w_kv, b_f, with_cumsum, tm_cap=512):
    b, t, d = x.shape
    tm = _row_tile(t, tm_cap)
    wk = w_kv[:, :SEQ_WIDTH].astype(BF16)
    wv = w_kv[:, SEQ_WIDTH:2 * SEQ_WIDTH].astype(BF16)
    wf = jnp.pad(w_kv[:, 2 * SEQ_WIDTH:], ((0, 0), (0, LANES - FOX_HEADS))).astype(BF16)
    bf = jnp.pad(b_f, (0, LANES - FOX_HEADS)).reshape(1, LANES)
    tri = jnp.tril(jnp.ones((tm, tm), F32)).astype(BF16)
    tok = lambda n: pl.BlockSpec((1, tm, n), lambda i, j: (i, j, 0))
    full = lambda s: pl.BlockSpec(s, lambda i, j: (0,) * len(s))
    return pl.pallas_call(
        functools.partial(_kv_kernel, with_cumsum=with_cumsum),
        grid=(b, t // tm),
        in_specs=[tok(d), full((1, d)), full((d, SEQ_WIDTH)), full((d, SEQ_WIDTH)),
                  full((d, LANES)), full((1, LANES)), full((tm, tm))],
        out_specs=[tok(SEQ_WIDTH), tok(SEQ_WIDTH), tok(SEQ_WIDTH), tok(SEQ_WIDTH),
                   tok(FOX_HEADS), tok(FOX_HEADS)],
        out_shape=[jax.ShapeDtypeStruct((b, t, SEQ_WIDTH), F32),
                   jax.ShapeDtypeStruct((b, t, SEQ_WIDTH), F32),
                   jax.ShapeDtypeStruct((b, t, SEQ_WIDTH), BF16),
                   jax.ShapeDtypeStruct((b, t, SEQ_WIDTH), BF16),
                   jax.ShapeDtypeStruct((b, t, FOX_HEADS), F32),
                   jax.ShapeDtypeStruct((b, t, FOX_HEADS), F32)],
        scratch_shapes=[pltpu.VMEM((1, LANES), F32)],
        compiler_params=_params(("parallel", "arbitrary")),
        name="shared_kv_proj",
    )(x, g.reshape(1, d), wk, wv, wf, bf, tri)


def _mem_attn_kernel(q_ref, mk_ref, mv_ref, o_ref):
    q = q_ref[0] * ATTN_SCALE
    mk = mk_ref[0].astype(BF16)
    mv = mv_ref[0].astype(BF16)
    lane = lax.broadcasted_iota(jnp.int32, (1, MEM_WIDTH), 1)
    out = jnp.zeros(q.shape, F32)
    for h in range(MEM_HEADS):
        in_head = (lane >= h * HEAD_DIM) & (lane < (h + 1) * HEAD_DIM)
        s = _dot_nt(jnp.where(in_head, q, 0.0).astype(BF16), mk)
        p = jnp.exp(s - jnp.max(s, axis=-1, keepdims=True))
        p = p / jnp.sum(p, axis=-1, keepdims=True)
        out = out + jnp.where(in_head, _dot(p.astype(BF16), mv), 0.0)
    o_ref[0] = out


def mem_attention(q_mem, mk, mv, tm_cap=512):
    b, t, w = q_mem.shape
    n_mem = mk.shape[1]
    tm = _row_tile(t, tm_cap)
    return pl.pallas_call(
        _mem_attn_kernel,
        grid=(b, t // tm),
        in_specs=[pl.BlockSpec((1, tm, w), lambda i, j: (i, j, 0)),
                  pl.BlockSpec((1, n_mem, w), lambda i, j: (i, 0, 0)),
                  pl.BlockSpec((1, n_mem, w), lambda i, j: (i, 0, 0))],
        out_specs=pl.BlockSpec((1, tm, w), lambda i, j: (i, j, 0)),
        out_shape=jax.ShapeDtypeStruct((b, t, w), F32),
        compiler_params=_params(("parallel", "parallel")),
        name="mem_attention",
    )(q_mem, mk, mv)


def _mix_out_kernel(x_ref, s_ref, m_ref, w_ref, g_ref, o_ref):
    o = (_dot(s_ref[...].astype(BF16), w_ref[:SEQ_WIDTH, :])
         + _dot(m_ref[...].astype(BF16), w_ref[SEQ_WIDTH:, :]))
    o_ref[...] = x_ref[...] + _rms(o, g_ref[...])


def mix_out(x, seq_out, mem_out, w_out, g, tm_cap=512):
    m, d = x.shape
    tm = _row_tile(m, tm_cap)
    row = lambda n: pl.BlockSpec((tm, n), lambda i: (i, 0))
    return pl.pallas_call(
        _mix_out_kernel,
        grid=(m // tm,),
        in_specs=[row(d), row(SEQ_WIDTH), row(MEM_WIDTH),
                  pl.BlockSpec((d, d), lambda i: (0, 0)),
                  pl.BlockSpec((1, d), lambda i: (0, 0))],
        out_specs=row(d),
        out_shape=jax.ShapeDtypeStruct((m, d), F32),
        compiler_params=_params(("parallel",)),
        name="mix_out",
    )(x, seq_out, mem_out, w_out, g.reshape(1, d))


FFN_CHUNK = MXU_DIM


def _ffn_seq_kernel(x_ref, prev_ref, gpre_ref, gpost_ref, wup_ref, cw_ref, cb_ref, wdn_ref,
                    o_ref, conv_ref, h_scr, carry_scr, *, tm):
    @pl.when(pl.program_id(1) == 0)
    def _():
        carry_scr[...] = prev_ref[0]

    x = x_ref[0]
    xn = _rms(x, gpre_ref[...]).astype(BF16)
    row = lax.broadcasted_iota(jnp.int32, (SUBLANES, 1), 0)

    def conv(col):
        u = _dot(xn, wup_ref[:, col:col + FFN_CHUNK])
        c0 = carry_scr[0:1, col:col + FFN_CHUNK]
        c1 = carry_scr[1:2, col:col + FFN_CHUNK]
        u1 = pltpu.roll(u, 1, 0)
        u2 = pltpu.roll(u, 2, 0)
        u1 = jnp.concatenate([jnp.where(row == 0, c1, u1[:SUBLANES]), u1[SUBLANES:]], axis=0)
        u2 = jnp.concatenate(
            [jnp.where(row == 0, c0, jnp.where(row == 1, c1, u2[:SUBLANES])), u2[SUBLANES:]],
            axis=0)
        carry_scr[:, col:col + FFN_CHUNK] = u[tm - 2:, :]
        w = cw_ref[:, col:col + FFN_CHUNK]
        return u2 * w[0:1] + u1 * w[1:2] + u * w[2:3] + cb_ref[:, col:col + FFN_CHUNK]

    for c in range(D_FF // FFN_CHUNK):
        gate = conv(c * FFN_CHUNK)
        val = conv(D_FF + c * FFN_CHUNK)
        h_scr[:, c * FFN_CHUNK:(c + 1) * FFN_CHUNK] = (gate * _sigmoid(gate) * val).astype(BF16)

    f = _dot(h_scr[...], wdn_ref[...])
    o_ref[0] = x + _rms(f, gpost_ref[...])
    conv_ref[0] = carry_scr[...]


def conv_ffn_seq(x, prev, g_pre, g_post, w_up, conv_w, conv_b, w_down, tm_cap=512):
    b, t, d = x.shape
    tm = _row_tile(t, tm_cap)
    assert tm >= 2 * SUBLANES
    f2 = 2 * D_FF
    full = lambda s: pl.BlockSpec(s, lambda i, j: (0,) * len(s), pipeline_mode=pl.Buffered(1))
    return pl.pallas_call(
        functools.partial(_ffn_seq_kernel, tm=tm),
        grid=(b, t // tm),
        in_specs=[pl.BlockSpec((1, tm, d), lambda i, j: (i, j, 0)),
                  pl.BlockSpec((1, CONV_W - 1, f2), lambda i, j: (i, 0, 0)),
                  full((1, d)), full((1, d)), full((d, f2)), full((CONV_W, f2)), full((1, f2)),
                  full((D_FF, d))],
        out_specs=[pl.BlockSpec((1, tm, d), lambda i, j: (i, j, 0)),
                   pl.BlockSpec((1, CONV_W - 1, f2), lambda i, j: (i, 0, 0))],
        out_shape=[jax.ShapeDtypeStruct((b, t, d), F32),
                   jax.ShapeDtypeStruct((b, CONV_W - 1, f2), F32)],
        scratch_shapes=[pltpu.VMEM((tm, D_FF), BF16), pltpu.VMEM((CONV_W - 1, f2), F32)],
        compiler_params=_params(("parallel", "arbitrary"), VMEM_LIMIT),
        name="conv_ffn_seq",
    )(x, prev, g_pre.reshape(1, d), g_post.reshape(1, d), w_up, conv_w, conv_b.reshape(1, f2),
      w_down)


def _ffn_step_kernel(x_ref, p0g_ref, p0v_ref, p1g_ref, p1v_ref, gpre_ref, gpost_ref,
                     wg_ref, wv_ref, cwg_ref, cwv_ref, cbg_ref, cbv_ref, wdn_ref,
                     o_ref, ug_ref, uv_ref, acc_scr):
    c = pl.program_id(0)

    @pl.when(c == 0)
    def _():
        acc_scr[...] = jnp.zeros_like(acc_scr)

    x = x_ref[...]
    xn = _rms(x, gpre_ref[...]).astype(BF16)

    def conv(w_ref, p0_ref, p1_ref, cw_ref, cb_ref, u_ref):
        u = _dot(xn, w_ref[...])
        u_ref[...] = u
        w = cw_ref[...]
        return p0_ref[...] * w[0:1] + p1_ref[...] * w[1:2] + u * w[2:3] + cb_ref[...]

    gate = conv(wg_ref, p0g_ref, p1g_ref, cwg_ref, cbg_ref, ug_ref)
    val = conv(wv_ref, p0v_ref, p1v_ref, cwv_ref, cbv_ref, uv_ref)
    acc_scr[...] += _dot((gate * _sigmoid(gate) * val).astype(BF16), wdn_ref[...])

    @pl.when(c == pl.num_programs(0) - 1)
    def _():
        o_ref[...] = x + _rms(acc_scr[...], gpost_ref[...])


def conv_ffn_step(x, prev, g_pre, g_post, w_up, conv_w, conv_b, w_down):
    s, d = x.shape
    f2 = 2 * D_FF
    fc = FFN_CHUNK
    nc = D_FF // fc
    prev2 = prev.reshape(s, (CONV_W - 1) * f2)
    cb = conv_b.reshape(1, f2)
    const = lambda shp: pl.BlockSpec(shp, lambda c: (0, 0))
    col = lambda rows, off: pl.BlockSpec((rows, fc), lambda c, off=off: (0, c + off))
    out, ug, uv = pl.pallas_call(
        _ffn_step_kernel,
        grid=(nc,),
        in_specs=[const((s, d)),
                  col(s, 0), col(s, nc), col(s, 2 * nc), col(s, 3 * nc),
                  const((1, d)), const((1, d)),
                  col(d, 0), col(d, nc),
                  col(CONV_W, 0), col(CONV_W, nc),
                  col(1, 0), col(1, nc),
                  pl.BlockSpec((fc, d), lambda c: (c, 0))],
        out_specs=[const((s, d)), col(s, 0), col(s, 0)],
        out_shape=[jax.ShapeDtypeStruct((s, d), F32),
                   jax.ShapeDtypeStruct((s, D_FF), F32),
                   jax.ShapeDtypeStruct((s, D_FF), F32)],
        scratch_shapes=[pltpu.VMEM((s, d), F32)],
        compiler_params=_params(("arbitrary",)),
        name="conv_ffn_step",
    )(x, prev2, prev2, prev2, prev2, g_pre.reshape(1, d), g_post.reshape(1, d),
      w_up, w_up, conv_w, conv_w, cb, cb, w_down)
    return out, jnp.concatenate([ug, uv], axis=-1)


def _s5_prep_kernel(lr_ref, li_ref, ldt_ref, br_ref, bi_ref, pr_ref, pi_ref, bbr_ref, bbi_ref):
    lr = jnp.minimum(lr_ref[...], EIG_CLIP)
    li = li_ref[...]
    dt = jnp.exp(ldt_ref[...])
    mag = jnp.exp(lr * dt)
    ar = mag * jnp.cos(li * dt)
    ai = mag * jnp.sin(li * dt)
    den = lr * lr + li * li
    nr = ar - 1.0
    fr = (nr * lr + ai * li) / den
    fi = (ai * lr - nr * li) / den
    br = br_ref[...]
    bi = bi_ref[...]
    bbr_ref[...] = fr * br - fi * bi
    bbi_ref[...] = fr * bi + fi * br
    pr, pi = ar, ai
    pr_ref[0] = pr
    pi_ref[0] = pi
    for n in range(1, SUBLANES):
        pr, pi = pr * ar - pi * ai, pr * ai + pi * ar
        pr_ref[n] = pr
        pi_ref[n] = pi


def _block_diag(m):
    nb, ng, r, c = m.shape
    eye = jnp.eye(ng, dtype=m.dtype)
    return jnp.einsum("kgrc,gh->kgrhc", m, eye).reshape(nb, ng * r, ng * c)


def s5_prepare(lam_re, lam_im, log_dt, b_re, b_im, c_re, c_im):
    g, p, c = b_re.shape
    rep = lambda a: jnp.repeat(a, c, axis=0)
    tr = lambda b: jnp.transpose(b, (0, 2, 1)).reshape(g * c, p)
    shp = jax.ShapeDtypeStruct((g * c, p), F32)
    pw = jax.ShapeDtypeStruct((SUBLANES, g * c, p), F32)
    pr, pi, bbr, bbi = pl.pallas_call(
        _s5_prep_kernel, out_shape=[pw, pw, shp, shp], name="s5_prepare",
    )(rep(lam_re), rep(lam_im), jnp.broadcast_to(rep(log_dt[:, None]), (g * c, p)),
      tr(b_re), tr(b_im))
    pr = pr[:, ::c, :].reshape(SUBLANES, g * p)
    pi = pi[:, ::c, :].reshape(SUBLANES, g * p)
    gb = g // SSM_BLOCKS
    b_in = lambda m: _block_diag(m.reshape(SSM_BLOCKS, gb, c, p)).astype(BF16)
    c_out = lambda m: _block_diag(
        jnp.transpose(m, (0, 2, 1)).reshape(SSM_BLOCKS, gb, p, c)).astype(BF16)
    return pr, pi, b_in(bbr), b_in(bbi), c_out(c_re), c_out(c_im)


def _s5_finish(y, u, dskip_ref, wglu_ref, bglu_ref):
    y = y + dskip_ref[...] * u
    y = jax.nn.gelu(y)
    return y * _sigmoid(_dot(y.astype(BF16), wglu_ref[...]) + bglu_ref[...])


def _s5_seq_kernel(u_ref, pr_ref, pi_ref, lvr_ref, lvi_ref, bre_ref, bim_ref, cre_ref, cim_ref,
                   dskip_ref, wglu_ref, bglu_ref, y_ref, hr_ref, hi_ref,
                   xr_scr, xi_scr, y_scr, *, tc):
    @pl.when(pl.program_id(1) == 0)
    def _():
        hr_ref[...] = jnp.zeros_like(hr_ref)
        hi_ref[...] = jnp.zeros_like(hi_ref)

    u = u_ref[0]
    ub = u.astype(BF16)
    nb = SSM_BLOCK_STATES
    last = SUBLANES - 1
    for k in range(SSM_BLOCKS):
        sl = slice(k * nb, (k + 1) * nb)
        uk = ub[:, k * MXU_DIM:(k + 1) * MXU_DIM]
        xr_scr[...] = _dot(uk, bre_ref[k])
        xi_scr[...] = _dot(uk, bim_ref[k])
        pr = pr_ref[:, sl]
        pi = pi_ref[:, sl]

        def group(gidx, carry):
            cr, ci = carry
            off = pl.multiple_of(gidx * SUBLANES, SUBLANES)
            xr = xr_scr[pl.ds(off, SUBLANES), :]
            xi = xi_scr[pl.ds(off, SUBLANES), :]
            for lv in range(3):
                sr = pltpu.roll(xr, 1 << lv, 0)
                si = pltpu.roll(xi, 1 << lv, 0)
                ar = lvr_ref[lv, :, sl]
                ai = lvi_ref[lv, :, sl]
                xr, xi = xr + ar * sr - ai * si, xi + ar * si + ai * sr
            hr = xr + pr * cr - pi * ci
            hi = xi + pr * ci + pi * cr
            xr_scr[pl.ds(off, SUBLANES), :] = hr
            xi_scr[pl.ds(off, SUBLANES), :] = hi
            bc = lambda h: jnp.broadcast_to(h[last:last + 1, :], h.shape)
            return bc(hr), bc(hi)

        bc0 = lambda ref: jnp.broadcast_to(ref[0, last:last + 1, sl], (SUBLANES, nb))
        cr, ci = lax.fori_loop(0, tc // SUBLANES, group, (bc0(hr_ref), bc0(hi_ref)))
        hr_ref[0, :, sl] = cr
        hi_ref[0, :, sl] = ci
        y_scr[:, k * MXU_DIM:(k + 1) * MXU_DIM] = (
            _dot(xr_scr[...].astype(BF16), cre_ref[k]) - _dot(xi_scr[...].astype(BF16), cim_ref[k]))

    y_ref[0] = _s5_finish(y_scr[...], u, dskip_ref, wglu_ref, bglu_ref)


def s5_mix_seq(u, prep, d_skip, w_glu, b_glu, tc_cap=256):
    b, t, w = u.shape
    tc = _row_tile(t, tc_cap)
    pr, pi, bre, bim, cre, cim = prep
    rows = jnp.arange(SUBLANES)[:, None]
    lvr = jnp.stack([jnp.where(rows >= (1 << l), pr[(1 << l) - 1][None], 0.0) for l in range(3)])
    lvi = jnp.stack([jnp.where(rows >= (1 << l), pi[(1 << l) - 1][None], 0.0) for l in range(3)])
    full = lambda a: pl.BlockSpec(a.shape, lambda i, j: (0,) * a.ndim)
    consts = [pr, pi, lvr, lvi, bre, bim, cre, cim, d_skip.reshape(1, w), w_glu, b_glu.reshape(1, w)]
    y, hr, hi = pl.pallas_call(
        functools.partial(_s5_seq_kernel, tc=tc),
        grid=(b, t // tc),
        in_specs=[pl.BlockSpec((1, tc, w), lambda i, j: (i, j, 0))] + [full(a) for a in consts],
        out_specs=[pl.BlockSpec((1, tc, w), lambda i, j: (i, j, 0)),
                   pl.BlockSpec((1, SUBLANES, SSM_WIDTH), lambda i, j: (i, 0, 0)),
                   pl.BlockSpec((1, SUBLANES, SSM_WIDTH), lambda i, j: (i, 0, 0))],
        out_shape=[jax.ShapeDtypeStruct((b, t, w), F32),
                   jax.ShapeDtypeStruct((b, SUBLANES, SSM_WIDTH), F32),
                   jax.ShapeDtypeStruct((b, SUBLANES, SSM_WIDTH), F32)],
        scratch_shapes=[pltpu.VMEM((tc, SSM_BLOCK_STATES), F32),
                        pltpu.VMEM((tc, SSM_BLOCK_STATES), F32),
                        pltpu.VMEM((tc, w), F32)],
        compiler_params=_params(("parallel", "arbitrary")),
        name="s5_mix_seq",
    )(u, *consts)
    return y, hr[:, SUBLANES - 1], hi[:, SUBLANES - 1]


def _s5_step_kernel(u_ref, h0r_ref, h0i_ref, ar_ref, ai_ref, bre_ref, bim_ref, cre_ref, cim_ref,
                    dskip_ref, wglu_ref, bglu_ref, y_ref, hr_ref, hi_ref, y_scr):
    u = u_ref[...]
    ub = u.astype(BF16)
    nb = SSM_BLOCK_STATES
    for k in range(SSM_BLOCKS):
        sl = slice(k * nb, (k + 1) * nb)
        uk = ub[:, k * MXU_DIM:(k + 1) * MXU_DIM]
        ar, ai = ar_ref[:, sl], ai_ref[:, sl]
        h0r, h0i = h0r_ref[:, sl], h0i_ref[:, sl]
        hr = _dot(uk, bre_ref[k]) + ar * h0r - ai * h0i
        hi = _dot(uk, bim_ref[k]) + ar * h0i + ai * h0r
        hr_ref[:, sl] = hr
        hi_ref[:, sl] = hi
        y_scr[:, k * MXU_DIM:(k + 1) * MXU_DIM] = (
            _dot(hr.astype(BF16), cre_ref[k]) - _dot(hi.astype(BF16), cim_ref[k]))
    y_ref[...] = _s5_finish(y_scr[...], u, dskip_ref, wglu_ref, bglu_ref)


def s5_mix_step(u, h0r, h0i, prep, d_skip, w_glu, b_glu):
    s, w = u.shape
    pr, pi, bre, bim, cre, cim = prep
    st = jax.ShapeDtypeStruct((s, SSM_WIDTH), F32)
    return pl.pallas_call(
        _s5_step_kernel,
        out_shape=[jax.ShapeDtypeStruct((s, w), F32), st, st],
        scratch_shapes=[pltpu.VMEM((s, w), F32)],
        name="s5_mix_step",
    )(u, h0r, h0i, pr[0:1], pi[0:1], bre, bim, cre, cim, d_skip.reshape(1, w), w_glu,
      b_glu.reshape(1, w))


FOX_TQ = 512
FOX_TK = 256


def _fox_seq_kernel(q_ref, k_ref, v_ref, c_ref, o_ref, acc_scr, *, tq, tk):
    i = pl.program_id(2)
    lane = lax.broadcasted_iota(jnp.int32, (1, LANES), 1)
    lo = lane < HEAD_DIM
    q = q_ref[0] * ATTN_SCALE
    qh = (jnp.where(lo, q, 0.0).astype(BF16), jnp.where(lo, 0.0, q).astype(BF16))
    acc_scr[...] = jnp.zeros_like(acc_scr)
    qpos = i * tq + lax.broadcasted_iota(jnp.int32, (tq, 1), 0)

    def chunk(j, carry, masked):
        off = pl.multiple_of(j * tk, tk)
        kc = k_ref[0, pl.ds(off, tk), :]
        vc = v_ref[0, pl.ds(off, tk), :]
        cc = c_ref[0, 0, j]
        if masked:
            visible = (j * tk + lax.broadcasted_iota(jnp.int32, (1, tk), 1)) <= qpos
        new = []
        for h in range(2):
            m, l = carry[2 * h], carry[2 * h + 1]
            s = _dot_nt(qh[h], kc) - cc[h:h + 1, :]
            if masked:
                s = jnp.where(visible, s, NEG_INF)
            m_new = jnp.maximum(m, jnp.max(s, axis=-1, keepdims=True))
            alpha = jnp.exp(m - m_new)
            p = jnp.exp(s - m_new)
            l = alpha * l + jnp.sum(p, axis=-1, keepdims=True)
            acc_scr[h] = alpha * acc_scr[h] + _dot(p.astype(BF16), vc)
            new += [m_new, l]
        return tuple(new)

    init = (jnp.full((tq, 1), NEG_INF, F32), jnp.zeros((tq, 1), F32)) * 2
    n_diag = tq // tk
    carry = lax.fori_loop(0, i * n_diag, functools.partial(chunk, masked=False), init)
    for d in range(n_diag):
        carry = chunk(i * n_diag + d, carry, masked=True)
    o_ref[0] = jnp.where(lo, acc_scr[0] / carry[1], acc_scr[1] / carry[3])


def fox_attention_seq(q, kb, vb, c):
    b, t, w = q.shape
    tq = _row_tile(t, FOX_TQ)
    tk = _row_tile(tq, FOX_TK)
    ct = jnp.transpose(c.reshape(b, t // tk, tk, HEAD_PAIRS, 2), (0, 3, 1, 4, 2))
    return pl.pallas_call(
        functools.partial(_fox_seq_kernel, tq=tq, tk=tk),
        grid=(b, HEAD_PAIRS, t // tq),
        in_specs=[pl.BlockSpec((1, tq, LANES), lambda bi, p, i: (bi, i, p)),
                  pl.BlockSpec((1, t, LANES), lambda bi, p, i: (bi, 0, p)),
                  pl.BlockSpec((1, t, LANES), lambda bi, p, i: (bi, 0, p)),
                  pl.BlockSpec((1, 1, t // tk, 2, tk), lambda bi, p, i: (bi, p, 0, 0, 0))],
        out_specs=pl.BlockSpec((1, tq, LANES), lambda bi, p, i: (bi, i, p)),
        out_shape=jax.ShapeDtypeStruct((b, t, w), F32),
        scratch_shapes=[pltpu.VMEM((2, tq, LANES), F32)],
        compiler_params=_params(("parallel", "parallel", "arbitrary")),
        name="fox_attention_seq",
    )(q, kb, vb, ct)


DEC_PAGES = 8


def _fox_dec_kernel(pt_ref, q_ref, kn_ref, vn_ref, lfn_ref, *rest, page, n_pages):
    ks = rest[:n_pages]
    vs = rest[n_pages:2 * n_pages]
    lfs = rest[2 * n_pages:3 * n_pages]
    tri_ref = rest[3 * n_pages]
    o_ref = rest[3 * n_pages + 1]
    m_scr, l_scr, csum_scr, acc_scr = rest[3 * n_pages + 2:]
    g = pl.program_id(1)
    hp = 2 * SUBLANES

    @pl.when(g == 0)
    def _():
        m_scr[...] = jnp.full_like(m_scr, NEG_INF)
        l_scr[...] = jnp.zeros_like(l_scr)
        csum_scr[...] = jnp.zeros_like(csum_scr)
        acc_scr[...] = jnp.zeros_like(acc_scr)

    lane = lax.broadcasted_iota(jnp.int32, (hp, SEQ_WIDTH), 1)
    head = lax.broadcasted_iota(jnp.int32, (hp, SEQ_WIDTH), 0)
    own = (lane >= head * HEAD_DIM) & (lane < (head + 1) * HEAD_DIM)
    qrow = q_ref[0] * ATTN_SCALE
    qbd = jnp.where(own, qrow, 0.0).astype(BF16)

    tri = tri_ref[...]
    base = csum_scr[...]
    s_parts, c_last = [], None
    for i in range(n_pages):
        lf = lfs[i][0]
        hi, mid, lo = _split3(lf)
        cum = (_dot(hi, tri) + _dot(mid, tri)) + _dot(lo, tri) + base
        base = jnp.broadcast_to(cum[:, page - 1:page], cum.shape)
        s_parts.append(_dot_nt(qbd, ks[i][0].astype(BF16)) - cum)
    csum_scr[...] = base
    s = jnp.concatenate(s_parts, axis=1)

    m_old = m_scr[...]
    m_new = jnp.maximum(m_old, jnp.max(s, axis=-1, keepdims=True))
    alpha = jnp.exp(m_old - m_new)
    p = jnp.exp(s - m_new)
    l_scr[...] = alpha * l_scr[...] + jnp.sum(p, axis=-1, keepdims=True)
    m_scr[...] = m_new
    pb = p.astype(BF16)
    pv = jnp.zeros((hp, SEQ_WIDTH), F32)
    for i in range(n_pages):
        pv = pv + _dot(pb[:, i * page:(i + 1) * page], vs[i][0].astype(BF16))
    acc_scr[...] = alpha * acc_scr[...] + pv

    @pl.when(g == pl.num_programs(1) - 1)
    def _():
        kn = kn_ref[0].astype(BF16).astype(F32)
        vn = vn_ref[0].astype(BF16).astype(F32)
        qf = qbd.astype(F32)
        c_new = csum_scr[:, 0:1] + lfn_ref[0]
        s_new = jnp.sum(qf * kn, axis=-1, keepdims=True) - c_new
        m_fin = jnp.maximum(m_scr[...], s_new)
        a_fin = jnp.exp(m_scr[...] - m_fin)
        p_new = jnp.exp(s_new - m_fin)
        l_fin = a_fin * l_scr[...] + p_new
        p_new = p_new.astype(BF16).astype(F32)
        o_all = (a_fin * acc_scr[...] + p_new * vn) / l_fin
        o_ref[0] = jnp.sum(jnp.where(own, o_all, 0.0), axis=0, keepdims=True)


def fox_attention_decode(q, k_new, v_new, logf_new, cache_k, cache_v, cache_logf, page_table):
    s, w = q.shape
    n_pool, page = cache_k.shape[:2]
    pages_per_seq = page_table.shape[1]
    n_pages = min(DEC_PAGES, pages_per_seq)
    assert pages_per_seq % n_pages == 0
    hp = 2 * SUBLANES
    ck = cache_k.reshape(n_pool, page, w)
    cv = cache_v.reshape(n_pool, page, w)
    clf = jnp.pad(jnp.transpose(cache_logf, (0, 2, 1)), ((0, 0), (0, hp - FOX_HEADS), (0, 0)))
    lfn = jnp.pad(logf_new, ((0, 0), (0, hp - FOX_HEADS))).reshape(s, hp, 1)
    tri = jnp.triu(jnp.ones((page, page), F32)).astype(BF16)
    row = pl.BlockSpec((1, 1, w), lambda b, g, pt: (b, 0, 0))

    def paged(shape, i):
        return pl.BlockSpec((1,) + shape, lambda b, g, pt, i=i: (pt[b, g * n_pages + i], 0, 0))

    grid_spec = pltpu.PrefetchScalarGridSpec(
        num_scalar_prefetch=1,
        grid=(s, pages_per_seq // n_pages),
        in_specs=([row, row, row, pl.BlockSpec((1, hp, 1), lambda b, g, pt: (b, 0, 0))]
                  + [paged((page, w), i) for i in range(n_pages)]
                  + [paged((page, w), i) for i in range(n_pages)]
                  + [paged((hp, page), i) for i in range(n_pages)]
                  + [pl.BlockSpec((page, page), lambda b, g, pt: (0, 0))]),
        out_specs=row,
        scratch_shapes=[pltpu.VMEM((hp, 1), F32), pltpu.VMEM((hp, 1), F32),
                        pltpu.VMEM((hp, page), F32), pltpu.VMEM((hp, w), F32)],
    )
    out = pl.pallas_call(
        functools.partial(_fox_dec_kernel, page=page, n_pages=n_pages),
        grid_spec=grid_spec,
        out_shape=jax.ShapeDtypeStruct((s, 1, w), F32),
        compiler_params=_params(("parallel", "arbitrary"), VMEM_LIMIT),
        name="fox_attention_decode",
    )(page_table, q.reshape(s, 1, w), k_new.reshape(s, 1, w), v_new.reshape(s, 1, w), lfn,
      *([ck] * n_pages), *([cv] * n_pages), *([clf] * n_pages), tri)
    return out.reshape(s, w)


def _trunk(x, mem_k, mem_v, conv_prev, ssm_state, fox_attend, p, s5_prep, sequential):
    assert N_A == 1
    b, t, d = x.shape
    m = b * t
    new_conv, ssm_out, kv = [], None, None
    for l in range(DEPTH):
        if l == N_A:
            xs = x if sequential else x.reshape(1, m, d)
            kv = shared_kv_proj(xs, p["kv_norm"], p["w_kv"], p["b_f"], with_cumsum=sequential)
            kv = tuple(a.reshape(b, t, a.shape[-1]) for a in kv)
        z_seq, q_mem = norm_linear(x.reshape(m, d), p["norm_mix_pre"][l], p["w_in"][l],
                                   (SEQ_WIDTH, MEM_WIDTH))
        if l < N_A:
            if sequential:
                seq_out, hr, hi = s5_mix_seq(z_seq.reshape(b, t, SEQ_WIDTH), s5_prep[l],
                                             p["d_skip"][l], p["w_glu"][l], p["b_glu"][l])
            else:
                seq_out, hr, hi = s5_mix_step(z_seq, ssm_state[0][l], ssm_state[1][l], s5_prep[l],
                                              p["d_skip"][l], p["w_glu"][l], p["b_glu"][l])
            ssm_out = (hr.reshape(b, SSM_GROUPS, SSM_STATE), hi.reshape(b, SSM_GROUPS, SSM_STATE))
        else:
            seq_out = fox_attend(z_seq, kv)
        mem_out = mem_attention(q_mem.reshape(b, t, MEM_WIDTH), mem_k[l], mem_v[l])
        x2 = mix_out(x.reshape(m, d), seq_out.reshape(m, SEQ_WIDTH), mem_out.reshape(m, MEM_WIDTH),
                     p["w_out"][l], p["norm_mix_post"][l])
        ffn_args = (p["norm_ffn_pre"][l], p["norm_ffn_post"][l], p["w_up"][l], p["conv_w"][l],
                    p["conv_b"][l], p["w_down"][l])
        if sequential:
            x3, cp = conv_ffn_seq(x2.reshape(b, t, d), conv_prev[l], *ffn_args)
        else:
            x3, u_new = conv_ffn_step(x2, conv_prev[l], *ffn_args)
            cp = jnp.stack([conv_prev[l][:, 1], u_new], axis=1)
        new_conv.append(cp)
        x = x3.reshape(b, t, d)
    return x, ssm_out, jnp.stack(new_conv), kv


def kernel(x_prompt, x_sample, state_ssm_re, state_ssm_im, cache_k, cache_v, cache_logf,
           cache_mem_k, cache_mem_v, state_ffn_conv, page_table, mem_prompt,
           w_in, w_out, norm_mix_pre, norm_mix_post, norm_ffn_pre, norm_ffn_post,
           mem_norm, w_mem_kv, lam_re, lam_im, log_dt, b_re, b_im, c_re, c_im, d_skip,
           w_glu, b_glu, kv_norm, w_kv, b_f, w_up, conv_w, conv_b, w_down):
    p = dict(w_in=w_in.astype(BF16), w_out=w_out.astype(BF16), norm_mix_pre=norm_mix_pre,
             norm_mix_post=norm_mix_post, norm_ffn_pre=norm_ffn_pre, norm_ffn_post=norm_ffn_post,
             d_skip=d_skip, w_glu=w_glu.astype(BF16), b_glu=b_glu, kv_norm=kv_norm, w_kv=w_kv,
             b_f=b_f, w_up=w_up.astype(BF16), conv_w=conv_w, conv_b=conv_b,
             w_down=w_down.astype(BF16))
    s5_prep = [s5_prepare(lam_re[l], lam_im[l], log_dt[l], b_re[l], b_im[l], c_re[l], c_im[l])
               for l in range(N_A)]

    bp, tp, d = x_prompt.shape
    n_mem = mem_prompt.shape[1]
    mem_pairs = [norm_linear(mem_prompt.reshape(bp * n_mem, d), mem_norm[l],
                             w_mem_kv[l].astype(BF16), (MEM_WIDTH, MEM_WIDTH)) for l in range(DEPTH)]
    p_mem_k = [mk.reshape(bp, n_mem, MEM_WIDTH) for mk, _ in mem_pairs]
    p_mem_v = [mv.reshape(bp, n_mem, MEM_WIDTH) for _, mv in mem_pairs]
    zeros_conv = jnp.zeros((DEPTH, bp, CONV_W - 1, 2 * D_FF), F32)

    def fox_prompt(q, kv):
        k, v, kb, vb, logf, c = kv
        return fox_attention_seq(q.reshape(bp, tp, SEQ_WIDTH), kb, vb, c)

    y_prompt, p_ssm, p_conv, p_kv = _trunk(x_prompt, p_mem_k, p_mem_v, zeros_conv, None,
                                           fox_prompt, p, s5_prep, sequential=True)
    head4 = lambda a, n: a.reshape(a.shape[0], a.shape[1], n, HEAD_DIM)
    mem5 = lambda ms: jnp.stack([head4(a, MEM_HEADS) for a in ms])

    bs = x_sample.shape[0]
    s_mem_k = [cache_mem_k[l].reshape(bs, n_mem, MEM_WIDTH) for l in range(DEPTH)]
    s_mem_v = [cache_mem_v[l].reshape(bs, n_mem, MEM_WIDTH) for l in range(DEPTH)]
    ssm0 = (state_ssm_re.reshape(N_A, bs, SSM_WIDTH), state_ssm_im.reshape(N_A, bs, SSM_WIDTH))

    def fox_sample(q, kv):
        k, v, kb, vb, logf, c = kv
        return fox_attention_decode(q, k.reshape(bs, SEQ_WIDTH), v.reshape(bs, SEQ_WIDTH),
                                    logf.reshape(bs, FOX_HEADS), cache_k, cache_v, cache_logf,
                                    page_table)

    y_sample, s_ssm, s_conv, s_kv = _trunk(x_sample, s_mem_k, s_mem_v, state_ffn_conv, ssm0,
                                           fox_sample, p, s5_prep, sequential=False)

    return (y_prompt, y_sample, p_ssm[0][None], p_ssm[1][None],
            head4(p_kv[0], FOX_HEADS), head4(p_kv[1], FOX_HEADS), p_kv[4],
            mem5(p_mem_k), mem5(p_mem_v), p_conv,
            s_ssm[0][None], s_ssm[1][None],
            head4(s_kv[0], FOX_HEADS), head4(s_kv[1], FOX_HEADS), s_kv[4], s_conv)
```

```python
import functools
import math

import jax
import jax.numpy as jnp
from jax import lax
from jax.experimental import pallas as pl
from jax.experimental.pallas import tpu as pltpu

F32 = jnp.float32
BF16 = jnp.bfloat16

D_MODEL = 1024
DEPTH = 2
N_A = DEPTH // 2
HEAD_DIM = 64
MEM_HEADS = 4
MEM_WIDTH = MEM_HEADS * HEAD_DIM
SEQ_WIDTH = D_MODEL - MEM_WIDTH
SSM_GROUP = 16
SSM_GROUPS = SEQ_WIDTH // SSM_GROUP
SSM_STATE = 64
SSM_WIDTH = SSM_GROUPS * SSM_STATE
FOX_HEADS = SEQ_WIDTH // HEAD_DIM
D_FF = (11 * D_MODEL) // 4
CONV_W = 3
EPS = 1e-6
NEG_INF = -1e30
EIG_CLIP = -1e-4
ATTN_SCALE = HEAD_DIM ** -0.5

LANES = 128
SUBLANES = 8
MXU_DIM = 256
VMEM_BYTES_V7X = 64 * 1024 * 1024
VMEM_LIMIT = (VMEM_BYTES_V7X * 7) // 8

SSM_BLOCKS = SEQ_WIDTH // MXU_DIM
SSM_BLOCK_STATES = SSM_WIDTH // SSM_BLOCKS
HEAD_PAIRS = FOX_HEADS // 2


def _params(semantics, vmem=None):
    return pltpu.CompilerParams(dimension_semantics=semantics, vmem_limit_bytes=vmem)


def _row_tile(m, cap):
    t = min(m, cap)
    assert m % t == 0, (m, t)
    return t


def _rms(x, g):
    return x * lax.rsqrt(jnp.mean(x * x, axis=-1, keepdims=True) + EPS) * g


def _sigmoid(x):
    return 1.0 / (1.0 + jnp.exp(-x))


def _log_sigmoid(x):
    return -(jnp.maximum(-x, 0.0) + jnp.log1p(jnp.exp(-jnp.abs(x))))


def _split3(x):
    hi = x.astype(BF16)
    r1 = x - hi.astype(F32)
    mid = r1.astype(BF16)
    lo = (r1 - mid.astype(F32)).astype(BF16)
    return hi, mid, lo


def _dot(a, b):
    return jnp.dot(a, b, preferred_element_type=F32)


def _dot_nt(a, b):
    return lax.dot_general(a, b, (((1,), (1,)), ((), ())), preferred_element_type=F32)


def _norm_linear_kernel(x_ref, g_ref, w_ref, *out_refs, splits):
    h = _rms(x_ref[...], g_ref[...]).astype(BF16)
    c0 = 0
    for o_ref, n in zip(out_refs, splits):
        o_ref[...] = _dot(h, w_ref[:, c0:c0 + n])
        c0 += n


def norm_linear(x, g, w, splits, tm_cap=512):
    m, d = x.shape
    tm = _row_tile(m, tm_cap)
    n_tot = sum(splits)
    assert w.shape == (d, n_tot)
    return pl.pallas_call(
        functools.partial(_norm_linear_kernel, splits=tuple(splits)),
        grid=(m // tm,),
        in_specs=[pl.BlockSpec((tm, d), lambda i: (i, 0)),
                  pl.BlockSpec((1, d), lambda i: (0, 0)),
                  pl.BlockSpec((d, n_tot), lambda i: (0, 0))],
        out_specs=[pl.BlockSpec((tm, n), lambda i: (i, 0)) for n in splits],
        out_shape=[jax.ShapeDtypeStruct((m, n), F32) for n in splits],
        compiler_params=_params(("parallel",)),
        name="norm_linear",
    )(x, g.reshape(1, d), w)


KV_TILE = 512
KAUG_WIDTH = 2 * LANES
BIAS_PIECES = 3


def _kv_weights(w_kv, b_f):
    wk = w_kv[:, :SEQ_WIDTH].astype(BF16)
    wf = jnp.pad(w_kv[:, 2 * SEQ_WIDTH:], ((0, 0), (0, LANES - FOX_HEADS))).astype(BF16)
    bf = jnp.pad(b_f, (0, LANES - FOX_HEADS)).reshape(1, LANES)
    return wk, wf, bf


def _kv_seq_kernel(x_ref, g_ref, wk_ref, wvt_ref, wf_ref, bf_ref, tri_ref, place_ref,
                   kt_ref, vt_ref, kaug_ref, vtb_ref, lf_ref, carry_ref):
    @pl.when(pl.program_id(1) == 0)
    def _():
        carry_ref[...] = jnp.zeros_like(carry_ref)

    h = _rms(x_ref[0], g_ref[...]).astype(BF16)
    k = _dot(h, wk_ref[...])
    kt_ref[0] = jnp.transpose(k)
    vt = _dot_nt(wvt_ref[...], h)
    vt_ref[0] = vt
    vtb_ref[0, 0] = vt.astype(BF16)
    logf = _log_sigmoid(_dot(h, wf_ref[...]) + bf_ref[...])
    lf_ref[0] = logf[:, :FOX_HEADS]
    tri = tri_ref[...]
    hi, mid, lo = _split3(logf)
    cum = (_dot(tri, hi) + _dot(tri, mid)) + _dot(tri, lo) + carry_ref[...]
    carry_ref[...] = cum[cum.shape[0] - 1:, :]
    bias = sum(_dot(piece, place_ref[j]) for j, piece in enumerate(_split3(-cum)))
    kb = k.astype(BF16)
    for p in range(HEAD_PAIRS):
        kaug_ref[0, :, p * KAUG_WIDTH:p * KAUG_WIDTH + LANES] = kb[:, p * LANES:(p + 1) * LANES]
        kaug_ref[0, :, p * KAUG_WIDTH + LANES:(p + 1) * KAUG_WIDTH] = (
            bias[:, p * LANES:(p + 1) * LANES].astype(BF16))


def shared_kv_proj_seq(x, g, w_kv, b_f):
    b, t, d = x.shape
    tm = _row_tile(t, KV_TILE)
    wk, wf, bf = _kv_weights(w_kv, b_f)
    wvt = jnp.transpose(w_kv)[SEQ_WIDTH:2 * SEQ_WIDTH].astype(BF16)
    tri = jnp.tril(jnp.ones((tm, tm), F32)).astype(BF16)
    hh = jnp.arange(FOX_HEADS)
    place = jnp.stack([
        jnp.zeros((LANES, SEQ_WIDTH), F32).at[hh, LANES * (hh // 2) + BIAS_PIECES * (hh % 2) + j].set(1.0)
        for j in range(BIAS_PIECES)]).astype(BF16)
    tok = lambda n: pl.BlockSpec((1, tm, n), lambda i, j: (i, j, 0))
    tr = pl.BlockSpec((1, SEQ_WIDTH, tm), lambda i, j: (i, 0, j))
    full = lambda s: pl.BlockSpec(s, lambda i, j: (0,) * len(s))
    return pl.pallas_call(
        _kv_seq_kernel,
        grid=(b, t // tm),
        in_specs=[tok(d), full((1, d)), full((d, SEQ_WIDTH)), full((SEQ_WIDTH, d)),
                  full((d, LANES)), full((1, LANES)), full((tm, tm)),
                  full((BIAS_PIECES, LANES, SEQ_WIDTH))],
        out_specs=[tr, tr, tok(HEAD_PAIRS * KAUG_WIDTH),
                   pl.BlockSpec((1, 1, SEQ_WIDTH, tm), lambda i, j: (i, j, 0, 0)),
                   tok(FOX_HEADS)],
        out_shape=[jax.ShapeDtypeStruct((b, SEQ_WIDTH, t), F32),
                   jax.ShapeDtypeStruct((b, SEQ_WIDTH, t), F32),
                   jax.ShapeDtypeStruct((b, t, HEAD_PAIRS * KAUG_WIDTH), BF16),
                   jax.ShapeDtypeStruct((b, t // tm, SEQ_WIDTH, tm), BF16),
                   jax.ShapeDtypeStruct((b, t, FOX_HEADS), F32)],
        scratch_shapes=[pltpu.VMEM((1, LANES), F32)],
        compiler_params=_params(("parallel", "arbitrary")),
        name="shared_kv_proj_seq",
    )(x, g.reshape(1, d), wk, wvt, wf, bf, tri, place)


def _kv_step_kernel(x_ref, g_ref, wk_ref, wv_ref, wf_ref, bf_ref, k_ref, v_ref, lf_ref):
    h = _rms(x_ref[...], g_ref[...]).astype(BF16)
    k_ref[...] = _dot(h, wk_ref[...])
    v_ref[...] = _dot(h, wv_ref[...])
    lf_ref[...] = _log_sigmoid(_dot(h, wf_ref[...]) + bf_ref[...])[:, :FOX_HEADS]


def shared_kv_proj_step(x, g, w_kv, b_f):
    s, d = x.shape
    wk, wf, bf = _kv_weights(w_kv, b_f)
    wv = w_kv[:, SEQ_WIDTH:2 * SEQ_WIDTH].astype(BF16)
    kv = jax.ShapeDtypeStruct((s, SEQ_WIDTH), F32)
    return pl.pallas_call(
        _kv_step_kernel,
        out_shape=[kv, kv, jax.ShapeDtypeStruct((s, FOX_HEADS), F32)],
        name="shared_kv_proj_step",
    )(x, g.reshape(1, d), wk, wv, wf, bf)


def _mem_kv_kernel(x_ref, g_ref, wt_ref, kt_ref, vt_ref):
    h = _rms(x_ref[0], g_ref[...]).astype(BF16)
    kvt = _dot_nt(wt_ref[...], h)
    kt_ref[0] = kvt[:MEM_WIDTH]
    vt_ref[0] = kvt[MEM_WIDTH:]


def mem_kv_proj(mem, g, w):
    b, n_mem, d = mem.shape
    out = jax.ShapeDtypeStruct((b, MEM_WIDTH, n_mem), F32)
    blk = pl.BlockSpec((1, MEM_WIDTH, n_mem), lambda i: (i, 0, 0))
    return pl.pallas_call(
        _mem_kv_kernel,
        grid=(b,),
        in_specs=[pl.BlockSpec((1, n_mem, d), lambda i: (i, 0, 0)),
                  pl.BlockSpec((1, d), lambda i: (0, 0)),
                  pl.BlockSpec((2 * MEM_WIDTH, d), lambda i: (0, 0))],
        out_specs=[blk, blk],
        out_shape=[out, out],
        compiler_params=_params(("parallel",)),
        name="mem_kv_proj",
    )(mem, g.reshape(1, d), jnp.transpose(w).astype(BF16))


def _mem_attn_kernel(q_ref, mkt_ref, mvt_ref, o_ref):
    q = q_ref[0] * ATTN_SCALE
    mkt = mkt_ref[0].astype(BF16)
    mvt = mvt_ref[0].astype(BF16)
    lane = lax.broadcasted_iota(jnp.int32, (1, MEM_WIDTH), 1)
    out = jnp.zeros(q.shape, F32)
    for h in range(MEM_HEADS):
        in_head = (lane >= h * HEAD_DIM) & (lane < (h + 1) * HEAD_DIM)
        s = _dot(jnp.where(in_head, q, 0.0).astype(BF16), mkt)
        p = jnp.exp(s - jnp.max(s, axis=-1, keepdims=True))
        p = p / jnp.sum(p, axis=-1, keepdims=True)
        out = out + jnp.where(in_head, _dot_nt(p.astype(BF16), mvt), 0.0)
    o_ref[0] = out


def mem_attention(q_mem, mkt, mvt, tm_cap=512):
    b, t, w = q_mem.shape
    n_mem = mkt.shape[2]
    tm = _row_tile(t, tm_cap)
    return pl.pallas_call(
        _mem_attn_kernel,
        grid=(b, t // tm),
        in_specs=[pl.BlockSpec((1, tm, w), lambda i, j: (i, j, 0)),
                  pl.BlockSpec((1, w, n_mem), lambda i, j: (i, 0, 0)),
                  pl.BlockSpec((1, w, n_mem), lambda i, j: (i, 0, 0))],
        out_specs=pl.BlockSpec((1, tm, w), lambda i, j: (i, j, 0)),
        out_shape=jax.ShapeDtypeStruct((b, t, w), F32),
        compiler_params=_params(("parallel", "parallel")),
        name="mem_attention",
    )(q_mem, mkt, mvt)


def _mix_out_kernel(x_ref, s_ref, m_ref, w_ref, g_ref, o_ref):
    o = (_dot(s_ref[...].astype(BF16), w_ref[:SEQ_WIDTH, :])
         + _dot(m_ref[...].astype(BF16), w_ref[SEQ_WIDTH:, :]))
    o_ref[...] = x_ref[...] + _rms(o, g_ref[...])


def mix_out(x, seq_out, mem_out, w_out, g, tm_cap=512):
    m, d = x.shape
    tm = _row_tile(m, tm_cap)
    row = lambda n: pl.BlockSpec((tm, n), lambda i: (i, 0))
    return pl.pallas_call(
        _mix_out_kernel,
        grid=(m // tm,),
        in_specs=[row(d), row(SEQ_WIDTH), row(MEM_WIDTH),
                  pl.BlockSpec((d, d), lambda i: (0, 0)),
                  pl.BlockSpec((1, d), lambda i: (0, 0))],
        out_specs=row(d),
        out_shape=jax.ShapeDtypeStruct((m, d), F32),
        compiler_params=_params(("parallel",)),
        name="mix_out",
    )(x, seq_out, mem_out, w_out, g.reshape(1, d))


FFN_CHUNK = MXU_DIM


def _ffn_seq_kernel(x_ref, prev_ref, gpre_ref, gpost_ref, wup_ref, cw_ref, cb_ref, wdn_ref,
                    o_ref, conv_ref, h_scr, carry_scr, *, tm):
    @pl.when(pl.program_id(1) == 0)
    def _():
        carry_scr[...] = prev_ref[0]

    x = x_ref[0]
    xn = _rms(x, gpre_ref[...]).astype(BF16)
    row = lax.broadcasted_iota(jnp.int32, (SUBLANES, 1), 0)

    def conv(col):
        u = _dot(xn, wup_ref[:, col:col + FFN_CHUNK])
        c0 = carry_scr[0:1, col:col + FFN_CHUNK]
        c1 = carry_scr[1:2, col:col + FFN_CHUNK]
        u1 = pltpu.roll(u, 1, 0)
        u2 = pltpu.roll(u, 2, 0)
        u1 = jnp.concatenate([jnp.where(row == 0, c1, u1[:SUBLANES]), u1[SUBLANES:]], axis=0)
        u2 = jnp.concatenate(
            [jnp.where(row == 0, c0, jnp.where(row == 1, c1, u2[:SUBLANES])), u2[SUBLANES:]],
            axis=0)
        carry_scr[:, col:col + FFN_CHUNK] = u[tm - 2:, :]
        w = cw_ref[:, col:col + FFN_CHUNK]
        return u2 * w[0:1] + u1 * w[1:2] + u * w[2:3] + cb_ref[:, col:col + FFN_CHUNK]

    for c in range(D_FF // FFN_CHUNK):
        gate = conv(c * FFN_CHUNK)
        val = conv(D_FF + c * FFN_CHUNK)
        h_scr[:, c * FFN_CHUNK:(c + 1) * FFN_CHUNK] = (gate * _sigmoid(gate) * val).astype(BF16)

    f = _dot(h_scr[...], wdn_ref[...])
    o_ref[0] = x + _rms(f, gpost_ref[...])
    conv_ref[0] = carry_scr[...]


def conv_ffn_seq(x, prev, g_pre, g_post, w_up, conv_w, conv_b, w_down, tm_cap=512):
    b, t, d = x.shape
    tm = _row_tile(t, tm_cap)
    assert tm >= 2 * SUBLANES
    f2 = 2 * D_FF
    full = lambda s: pl.BlockSpec(s, lambda i, j: (0,) * len(s), pipeline_mode=pl.Buffered(1))
    return pl.pallas_call(
        functools.partial(_ffn_seq_kernel, tm=tm),
        grid=(b, t // tm),
        in_specs=[pl.BlockSpec((1, tm, d), lambda i, j: (i, j, 0)),
                  pl.BlockSpec((1, CONV_W - 1, f2), lambda i, j: (i, 0, 0)),
                  full((1, d)), full((1, d)), full((d, f2)), full((CONV_W, f2)), full((1, f2)),
                  full((D_FF, d))],
        out_specs=[pl.BlockSpec((1, tm, d), lambda i, j: (i, j, 0)),
                   pl.BlockSpec((1, CONV_W - 1, f2), lambda i, j: (i, 0, 0))],
        out_shape=[jax.ShapeDtypeStruct((b, t, d), F32),
                   jax.ShapeDtypeStruct((b, CONV_W - 1, f2), F32)],
        scratch_shapes=[pltpu.VMEM((tm, D_FF), BF16), pltpu.VMEM((CONV_W - 1, f2), F32)],
        compiler_params=_params(("parallel", "arbitrary"), VMEM_LIMIT),
        name="conv_ffn_seq",
    )(x, prev, g_pre.reshape(1, d), g_post.reshape(1, d), w_up, conv_w, conv_b.reshape(1, f2),
      w_down)


def _ffn_step_kernel(x_ref, p0g_ref, p0v_ref, p1g_ref, p1v_ref, gpre_ref, gpost_ref,
                     wg_ref, wv_ref, cwg_ref, cwv_ref, cbg_ref, cbv_ref, wdn_ref,
                     o_ref, ug_ref, uv_ref, acc_scr):
    c = pl.program_id(0)

    @pl.when(c == 0)
    def _():
        acc_scr[...] = jnp.zeros_like(acc_scr)

    x = x_ref[...]
    xn = _rms(x, gpre_ref[...]).astype(BF16)

    def conv(w_ref, p0_ref, p1_ref, cw_ref, cb_ref, u_ref):
        u = _dot(xn, w_ref[...])
        u_ref[...] = u
        w = cw_ref[...]
        return p0_ref[...] * w[0:1] + p1_ref[...] * w[1:2] + u * w[2:3] + cb_ref[...]

    gate = conv(wg_ref, p0g_ref, p1g_ref, cwg_ref, cbg_ref, ug_ref)
    val = conv(wv_ref, p0v_ref, p1v_ref, cwv_ref, cbv_ref, uv_ref)
    acc_scr[...] += _dot((gate * _sigmoid(gate) * val).astype(BF16), wdn_ref[...])

    @pl.when(c == pl.num_programs(0) - 1)
    def _():
        o_ref[...] = x + _rms(acc_scr[...], gpost_ref[...])


def conv_ffn_step(x, prev, g_pre, g_post, w_up, conv_w, conv_b, w_down):
    s, d = x.shape
    f2 = 2 * D_FF
    fc = FFN_CHUNK
    nc = D_FF // fc
    prev2 = prev.reshape(s, (CONV_W - 1) * f2)
    cb = conv_b.reshape(1, f2)
    const = lambda shp: pl.BlockSpec(shp, lambda c: (0, 0))
    col = lambda rows, off: pl.BlockSpec((rows, fc), lambda c, off=off: (0, c + off))
    out, ug, uv = pl.pallas_call(
        _ffn_step_kernel,
        grid=(nc,),
        in_specs=[const((s, d)),
                  col(s, 0), col(s, nc), col(s, 2 * nc), col(s, 3 * nc),
                  const((1, d)), const((1, d)),
                  col(d, 0), col(d, nc),
                  col(CONV_W, 0), col(CONV_W, nc),
                  col(1, 0), col(1, nc),
                  pl.BlockSpec((fc, d), lambda c: (c, 0))],
        out_specs=[const((s, d)), col(s, 0), col(s, 0)],
        out_shape=[jax.ShapeDtypeStruct((s, d), F32),
                   jax.ShapeDtypeStruct((s, D_FF), F32),
                   jax.ShapeDtypeStruct((s, D_FF), F32)],
        scratch_shapes=[pltpu.VMEM((s, d), F32)],
        compiler_params=_params(("arbitrary",)),
        name="conv_ffn_step",
    )(x, prev2, prev2, prev2, prev2, g_pre.reshape(1, d), g_post.reshape(1, d),
      w_up, w_up, conv_w, conv_w, cb, cb, w_down)
    return out, jnp.concatenate([ug, uv], axis=-1)


def _s5_prep_kernel(lr_ref, li_ref, ldt_ref, br_ref, bi_ref, pr_ref, pi_ref, bbr_ref, bbi_ref):
    lr = jnp.minimum(lr_ref[...], EIG_CLIP)
    li = li_ref[...]
    dt = jnp.exp(ldt_ref[...])
    mag = jnp.exp(lr * dt)
    ar = mag * jnp.cos(li * dt)
    ai = mag * jnp.sin(li * dt)
    den = lr * lr + li * li
    nr = ar - 1.0
    fr = (nr * lr + ai * li) / den
    fi = (ai * lr - nr * li) / den
    br = br_ref[...]
    bi = bi_ref[...]
    bbr_ref[...] = fr * br - fi * bi
    bbi_ref[...] = fr * bi + fi * br
    pr, pi = ar, ai
    pr_ref[0] = pr
    pi_ref[0] = pi
    for n in range(1, SUBLANES):
        pr, pi = pr * ar - pi * ai, pr * ai + pi * ar
        pr_ref[n] = pr
        pi_ref[n] = pi


def _block_diag(m):
    nb, ng, r, c = m.shape
    eye = jnp.eye(ng, dtype=m.dtype)
    return jnp.einsum("kgrc,gh->kgrhc", m, eye).reshape(nb, ng * r, ng * c)


def s5_prepare(lam_re, lam_im, log_dt, b_re, b_im, c_re, c_im):
    g, p, c = b_re.shape
    rep = lambda a: jnp.repeat(a, c, axis=0)
    tr = lambda b: jnp.transpose(b, (0, 2, 1)).reshape(g * c, p)
    shp = jax.ShapeDtypeStruct((g * c, p), F32)
    pw = jax.ShapeDtypeStruct((SUBLANES, g * c, p), F32)
    pr, pi, bbr, bbi = pl.pallas_call(
        _s5_prep_kernel, out_shape=[pw, pw, shp, shp], name="s5_prepare",
    )(rep(lam_re), rep(lam_im), jnp.broadcast_to(rep(log_dt[:, None]), (g * c, p)),
      tr(b_re), tr(b_im))
    pr = pr[:, ::c, :].reshape(SUBLANES, g * p)
    pi = pi[:, ::c, :].reshape(SUBLANES, g * p)
    gb = g // SSM_BLOCKS
    b_in = lambda m: _block_diag(m.reshape(SSM_BLOCKS, gb, c, p)).astype(BF16)
    c_out = lambda m: _block_diag(
        jnp.transpose(m, (0, 2, 1)).reshape(SSM_BLOCKS, gb, p, c)).astype(BF16)
    return pr, pi, b_in(bbr), b_in(bbi), c_out(c_re), c_out(c_im)


def _s5_finish(y, u, dskip_ref, wglu_ref, bglu_ref):
    y = y + dskip_ref[...] * u
    y = jax.nn.gelu(y)
    return y * _sigmoid(_dot(y.astype(BF16), wglu_ref[...]) + bglu_ref[...])


def _s5_seq_kernel(u_ref, pr_ref, pi_ref, lvr_ref, lvi_ref, bre_ref, bim_ref, cre_ref, cim_ref,
                   dskip_ref, wglu_ref, bglu_ref, y_ref, hr_ref, hi_ref,
                   xr_scr, xi_scr, y_scr, *, tc):
    @pl.when(pl.program_id(1) == 0)
    def _():
        hr_ref[...] = jnp.zeros_like(hr_ref)
        hi_ref[...] = jnp.zeros_like(hi_ref)

    u = u_ref[0]
    ub = u.astype(BF16)
    nb = SSM_BLOCK_STATES
    last = SUBLANES - 1
    for k in range(SSM_BLOCKS):
        sl = slice(k * nb, (k + 1) * nb)
        uk = ub[:, k * MXU_DIM:(k + 1) * MXU_DIM]
        xr_scr[...] = _dot(uk, bre_ref[k])
        xi_scr[...] = _dot(uk, bim_ref[k])
        pr = pr_ref[:, sl]
        pi = pi_ref[:, sl]

        def group(gidx, carry):
            cr, ci = carry
            off = pl.multiple_of(gidx * SUBLANES, SUBLANES)
            xr = xr_scr[pl.ds(off, SUBLANES), :]
            xi = xi_scr[pl.ds(off, SUBLANES), :]
            for lv in range(3):
                sr = pltpu.roll(xr, 1 << lv, 0)
                si = pltpu.roll(xi, 1 << lv, 0)
                ar = lvr_ref[lv, :, sl]
                ai = lvi_ref[lv, :, sl]
                xr, xi = xr + ar * sr - ai * si, xi + ar * si + ai * sr
            hr = xr + pr * cr - pi * ci
            hi = xi + pr * ci + pi * cr
            xr_scr[pl.ds(off, SUBLANES), :] = hr
            xi_scr[pl.ds(off, SUBLANES), :] = hi
            bc = lambda h: jnp.broadcast_to(h[last:last + 1, :], h.shape)
            return bc(hr), bc(hi)

        bc0 = lambda ref: jnp.broadcast_to(ref[0, last:last + 1, sl], (SUBLANES, nb))
        cr, ci = lax.fori_loop(0, tc // SUBLANES, group, (bc0(hr_ref), bc0(hi_ref)))
        hr_ref[0, :, sl] = cr
        hi_ref[0, :, sl] = ci
        y_scr[:, k * MXU_DIM:(k + 1) * MXU_DIM] = (
            _dot(xr_scr[...].astype(BF16), cre_ref[k]) - _dot(xi_scr[...].astype(BF16), cim_ref[k]))

    y_ref[0] = _s5_finish(y_scr[...], u, dskip_ref, wglu_ref, bglu_ref)


def s5_mix_seq(u, prep, d_skip, w_glu, b_glu, tc_cap=256):
    b, t, w = u.shape
    tc = _row_tile(t, tc_cap)
    pr, pi, bre, bim, cre, cim = prep
    rows = jnp.arange(SUBLANES)[:, None]
    lvr = jnp.stack([jnp.where(rows >= (1 << l), pr[(1 << l) - 1][None], 0.0) for l in range(3)])
    lvi = jnp.stack([jnp.where(rows >= (1 << l), pi[(1 << l) - 1][None], 0.0) for l in range(3)])
    full = lambda a: pl.BlockSpec(a.shape, lambda i, j: (0,) * a.ndim)
    consts = [pr, pi, lvr, lvi, bre, bim, cre, cim, d_skip.reshape(1, w), w_glu, b_glu.reshape(1, w)]
    y, hr, hi = pl.pallas_call(
        functools.partial(_s5_seq_kernel, tc=tc),
        grid=(b, t // tc),
        in_specs=[pl.BlockSpec((1, tc, w), lambda i, j: (i, j, 0))] + [full(a) for a in consts],
        out_specs=[pl.BlockSpec((1, tc, w), lambda i, j: (i, j, 0)),
                   pl.BlockSpec((1, SUBLANES, SSM_WIDTH), lambda i, j: (i, 0, 0)),
                   pl.BlockSpec((1, SUBLANES, SSM_WIDTH), lambda i, j: (i, 0, 0))],
        out_shape=[jax.ShapeDtypeStruct((b, t, w), F32),
                   jax.ShapeDtypeStruct((b, SUBLANES, SSM_WIDTH), F32),
                   jax.ShapeDtypeStruct((b, SUBLANES, SSM_WIDTH), F32)],
        scratch_shapes=[pltpu.VMEM((tc, SSM_BLOCK_STATES), F32),
                        pltpu.VMEM((tc, SSM_BLOCK_STATES), F32),
                        pltpu.VMEM((tc, w), F32)],
        compiler_params=_params(("parallel", "arbitrary")),
        name="s5_mix_seq",
    )(u, *consts)
    return y, hr[:, SUBLANES - 1], hi[:, SUBLANES - 1]


def _s5_step_kernel(u_ref, h0r_ref, h0i_ref, ar_ref, ai_ref, bre_ref, bim_ref, cre_ref, cim_ref,
                    dskip_ref, wglu_ref, bglu_ref, y_ref, hr_ref, hi_ref, y_scr):
    u = u_ref[...]
    ub = u.astype(BF16)
    nb = SSM_BLOCK_STATES
    for k in range(SSM_BLOCKS):
        sl = slice(k * nb, (k + 1) * nb)
        uk = ub[:, k * MXU_DIM:(k + 1) * MXU_DIM]
        ar, ai = ar_ref[:, sl], ai_ref[:, sl]
        h0r, h0i = h0r_ref[:, sl], h0i_ref[:, sl]
        hr = _dot(uk, bre_ref[k]) + ar * h0r - ai * h0i
        hi = _dot(uk, bim_ref[k]) + ar * h0i + ai * h0r
        hr_ref[:, sl] = hr
        hi_ref[:, sl] = hi
        y_scr[:, k * MXU_DIM:(k + 1) * MXU_DIM] = (
            _dot(hr.astype(BF16), cre_ref[k]) - _dot(hi.astype(BF16), cim_ref[k]))
    y_ref[...] = _s5_finish(y_scr[...], u, dskip_ref, wglu_ref, bglu_ref)


def s5_mix_step(u, h0r, h0i, prep, d_skip, w_glu, b_glu):
    s, w = u.shape
    pr, pi, bre, bim, cre, cim = prep
    st = jax.ShapeDtypeStruct((s, SSM_WIDTH), F32)
    return pl.pallas_call(
        _s5_step_kernel,
        out_shape=[jax.ShapeDtypeStruct((s, w), F32), st, st],
        scratch_shapes=[pltpu.VMEM((s, w), F32)],
        name="s5_mix_step",
    )(u, h0r, h0i, pr[0:1], pi[0:1], bre, bim, cre, cim, d_skip.reshape(1, w), w_glu,
      b_glu.reshape(1, w))


FOX_TQ = 512
FOX_STRIP = MXU_DIM


def _fox_seq_kernel(q_ref, kaug_ref, vt_ref, o_ref, s_scr, p_scr, acc_scr, *, tq, tk):
    i = pl.program_id(2)
    qt = jnp.transpose(q_ref[0] * ATTN_SCALE)
    row = lax.broadcasted_iota(jnp.int32, (LANES, 1), 0)
    qa = []
    for h in range(2):
        own = (row >= h * HEAD_DIM) & (row < (h + 1) * HEAD_DIM)
        ones = (row >= h * BIAS_PIECES) & (row < (h + 1) * BIAS_PIECES)
        qa.append(jnp.concatenate(
            [jnp.where(own, qt, 0.0), jnp.broadcast_to(jnp.where(ones, 1.0, 0.0), qt.shape)],
            axis=0).astype(BF16))
    acc_scr[...] = jnp.zeros_like(acc_scr)
    n_sub = tq // FOX_STRIP
    strips = [(h, qs) for h in range(2) for qs in range(n_sub)]

    def chunk(j, carry, masked):
        ka = kaug_ref[0, pl.ds(pl.multiple_of(j * tk, tk), tk), :]
        lanes = lambda qs: slice(qs * FOX_STRIP, (qs + 1) * FOX_STRIP)
        for n, (h, qs) in enumerate(strips):
            s_scr[n] = _dot(ka, qa[h][:, lanes(qs)])
        new, alphas = [], []
        for n, (h, qs) in enumerate(strips):
            m, l = carry[2 * n], carry[2 * n + 1]
            if masked:
                qpos = i * tq + qs * FOX_STRIP + lax.broadcasted_iota(jnp.int32, (1, FOX_STRIP), 1)
                visible = (j * tk + lax.broadcasted_iota(jnp.int32, (tk, 1), 0)) <= qpos
                s_scr[n] = jnp.where(visible, s_scr[n], NEG_INF)
            m_new = jnp.maximum(m, jnp.max(s_scr[n], axis=0, keepdims=True))
            alpha = jnp.exp(m - m_new)
            p = jnp.exp(s_scr[n] - m_new)
            l = alpha * l + jnp.sum(p, axis=0, keepdims=True)
            p_scr[n] = p.astype(BF16)
            new += [m_new, l]
            alphas.append(alpha)
        for n, (h, qs) in enumerate(strips):
            vt = vt_ref[0, j, h * HEAD_DIM:(h + 1) * HEAD_DIM, :]
            acc_scr[h, :, lanes(qs)] = (alphas[n] * acc_scr[h, :, lanes(qs)]
                                        + _dot(vt, p_scr[n]))
        return tuple(new)

    init = (jnp.full((1, FOX_STRIP), NEG_INF, F32), jnp.zeros((1, FOX_STRIP), F32)) * len(strips)
    n_full = (i * tq) // tk
    carry = lax.fori_loop(0, n_full, functools.partial(chunk, masked=False), init)
    carry = chunk(n_full, carry, masked=True)
    l_head = lambda h: jnp.concatenate(
        [carry[2 * (h * n_sub + qs) + 1] for qs in range(n_sub)], axis=1)
    ot = jnp.concatenate([acc_scr[0] / l_head(0), acc_scr[1] / l_head(1)], axis=0)
    o_ref[0] = jnp.transpose(ot)


def fox_attention_seq(q, kaug, vtb):
    b, t, w = q.shape
    n_chunks, tk = vtb.shape[1], vtb.shape[3]
    tq = _row_tile(t, FOX_TQ)
    assert tk % tq == 0 and n_chunks * tk == t
    return pl.pallas_call(
        functools.partial(_fox_seq_kernel, tq=tq, tk=tk),
        grid=(b, HEAD_PAIRS, t // tq),
        in_specs=[pl.BlockSpec((1, tq, LANES), lambda bi, p, i: (bi, i, p)),
                  pl.BlockSpec((1, t, KAUG_WIDTH), lambda bi, p, i: (bi, 0, p)),
                  pl.BlockSpec((1, n_chunks, LANES, tk), lambda bi, p, i: (bi, 0, p, 0))],
        out_specs=pl.BlockSpec((1, tq, LANES), lambda bi, p, i: (bi, i, p)),
        out_shape=jax.ShapeDtypeStruct((b, t, w), F32),
        scratch_shapes=[pltpu.VMEM((2 * tq // FOX_STRIP, tk, FOX_STRIP), F32),
                        pltpu.VMEM((2 * tq // FOX_STRIP, tk, FOX_STRIP), BF16),
                        pltpu.VMEM((2, HEAD_DIM, tq), F32)],
        compiler_params=_params(("parallel", "parallel", "arbitrary")),
        name="fox_attention_seq",
    )(q, kaug, vtb)


DEC_PAGES = 8


def _fox_dec_kernel(pt_ref, q_ref, kn_ref, vn_ref, lfn_ref, *rest, page, n_pages):
    kts = rest[:n_pages]
    vts = rest[n_pages:2 * n_pages]
    lfs = rest[2 * n_pages:3 * n_pages]
    tri_ref = rest[3 * n_pages]
    o_ref = rest[3 * n_pages + 1]
    m_scr, l_scr, csum_scr, acc_scr, qcol_scr = rest[3 * n_pages + 2:]
    g = pl.program_id(1)
    hp = 2 * SUBLANES
    qrow = q_ref[0] * ATTN_SCALE

    @pl.when(g == 0)
    def _():
        m_scr[...] = jnp.full_like(m_scr, NEG_INF)
        l_scr[...] = jnp.zeros_like(l_scr)
        csum_scr[...] = jnp.zeros_like(csum_scr)
        acc_scr[...] = jnp.zeros_like(acc_scr)
        qcol_scr[...] = jnp.transpose(jnp.broadcast_to(qrow, (page, SEQ_WIDTH)))

    tri = tri_ref[...]
    base = csum_scr[...]
    head_row = lax.broadcasted_iota(jnp.int32, (hp, page), 0)
    s_parts = []
    for i in range(n_pages):
        hi, mid, lo = _split3(lfs[i][0])
        cum = (_dot(hi, tri) + _dot(mid, tri)) + _dot(lo, tri) + base
        base = jnp.broadcast_to(cum[:, page - 1:page], cum.shape)
        qk = jnp.zeros((hp, page), F32)
        for h in range(FOX_HEADS):
            rows = slice(h * HEAD_DIM, (h + 1) * HEAD_DIM)
            r = jnp.sum(kts[i][0, rows, :] * qcol_scr[rows, :], axis=0, keepdims=True)
            qk = jnp.where(head_row == h, r, qk)
        s_parts.append(qk - cum)
    csum_scr[...] = base
    s = jnp.concatenate(s_parts, axis=1)

    m_old = m_scr[...]
    m_new = jnp.maximum(m_old, jnp.max(s, axis=-1, keepdims=True))
    alpha = jnp.exp(m_old - m_new)
    p = jnp.exp(s - m_new)
    l_scr[...] = alpha * l_scr[...] + jnp.sum(p, axis=-1, keepdims=True)
    m_scr[...] = m_new
    for h in range(FOX_HEADS):
        rows = slice(h * HEAD_DIM, (h + 1) * HEAD_DIM)
        acc = acc_scr[rows, :] * alpha[h:h + 1, :]
        for i in range(n_pages):
            acc = acc + vts[i][0, rows, :] * p[h:h + 1, i * page:(i + 1) * page]
        acc_scr[rows, :] = acc

    @pl.when(g == pl.num_programs(1) - 1)
    def _():
        lane = lax.broadcasted_iota(jnp.int32, (hp, SEQ_WIDTH), 1)
        head = lax.broadcasted_iota(jnp.int32, (hp, SEQ_WIDTH), 0)
        own = (lane >= head * HEAD_DIM) & (lane < (head + 1) * HEAD_DIM)
        spread = lambda col: jnp.sum(jnp.where(own, col, 0.0), axis=0, keepdims=True)
        c_new = csum_scr[:, 0:1] + lfn_ref[0]
        s_new = jnp.sum(jnp.where(own, qrow * kn_ref[0], 0.0), axis=-1, keepdims=True) - c_new
        m_fin = jnp.maximum(m_scr[...], s_new)
        a_fin = jnp.exp(m_scr[...] - m_fin)
        p_new = jnp.exp(s_new - m_fin)
        l_fin = a_fin * l_scr[...] + p_new
        acc_row = jnp.sum(jnp.transpose(acc_scr[...]), axis=0, keepdims=True)
        o_ref[0] = (spread(a_fin) * acc_row + spread(p_new) * vn_ref[0]) / spread(l_fin)


def fox_attention_decode(q, k_new, v_new, logf_new, cache_k, cache_v, cache_logf, page_table):
    s, w = q.shape
    n_pool, page = cache_k.shape[:2]
    assert page == LANES
    pages_per_seq = page_table.shape[1]
    n_pages = min(DEC_PAGES, pages_per_seq)
    assert pages_per_seq % n_pages == 0
    hp = 2 * SUBLANES
    ckt = jnp.transpose(cache_k, (0, 2, 3, 1)).reshape(n_pool, w, page)
    cvt = jnp.transpose(cache_v, (0, 2, 3, 1)).reshape(n_pool, w, page)
    clf = jnp.pad(jnp.transpose(cache_logf, (0, 2, 1)), ((0, 0), (0, hp - FOX_HEADS), (0, 0)))
    lfn = jnp.pad(logf_new, ((0, 0), (0, hp - FOX_HEADS))).reshape(s, hp, 1)
    tri = jnp.triu(jnp.ones((page, page), F32)).astype(BF16)
    row = pl.BlockSpec((1, 1, w), lambda b, g, pt: (b, 0, 0))

    def paged(shape, i):
        return pl.BlockSpec((1,) + shape, lambda b, g, pt, i=i: (pt[b, g * n_pages + i], 0, 0))

    grid_spec = pltpu.PrefetchScalarGridSpec(
        num_scalar_prefetch=1,
        grid=(s, pages_per_seq // n_pages),
        in_specs=([row, row, row, pl.BlockSpec((1, hp, 1), lambda b, g, pt: (b, 0, 0))]
                  + [paged((w, page), i) for i in range(n_pages)]
                  + [paged((w, page), i) for i in range(n_pages)]
                  + [paged((hp, page), i) for i in range(n_pages)]
                  + [pl.BlockSpec((page, page), lambda b, g, pt: (0, 0))]),
        out_specs=row,
        scratch_shapes=[pltpu.VMEM((hp, 1), F32), pltpu.VMEM((hp, 1), F32),
                        pltpu.VMEM((hp, page), F32), pltpu.VMEM((w, page), F32),
                        pltpu.VMEM((w, page), F32)],
    )
    out = pl.pallas_call(
        functools.partial(_fox_dec_kernel, page=page, n_pages=n_pages),
        grid_spec=grid_spec,
        out_shape=jax.ShapeDtypeStruct((s, 1, w), F32),
        compiler_params=_params(("parallel", "arbitrary"), VMEM_LIMIT),
        name="fox_attention_decode",
    )(page_table, q.reshape(s, 1, w), k_new.reshape(s, 1, w), v_new.reshape(s, 1, w), lfn,
      *([ckt] * n_pages), *([cvt] * n_pages), *([clf] * n_pages), tri)
    return out.reshape(s, w)


def _trunk(x, mem_k, mem_v, conv_prev, ssm_state, fox_attend, p, s5_prep, sequential):
    assert N_A == 1
    b, t, d = x.shape
    m = b * t
    new_conv, ssm_out, kv = [], None, None
    for l in range(DEPTH):
        if l == N_A:
            if sequential:
                kv = shared_kv_proj_seq(x, p["kv_norm"], p["w_kv"], p["b_f"])
            else:
                kv = shared_kv_proj_step(x.reshape(m, d), p["kv_norm"], p["w_kv"], p["b_f"])
        z_seq, q_mem = norm_linear(x.reshape(m, d), p["norm_mix_pre"][l], p["w_in"][l],
                                   (SEQ_WIDTH, MEM_WIDTH))
        if l < N_A:
            if sequential:
                seq_out, hr, hi = s5_mix_seq(z_seq.reshape(b, t, SEQ_WIDTH), s5_prep[l],
                                             p["d_skip"][l], p["w_glu"][l], p["b_glu"][l])
            else:
                seq_out, hr, hi = s5_mix_step(z_seq, ssm_state[0][l], ssm_state[1][l], s5_prep[l],
                                              p["d_skip"][l], p["w_glu"][l], p["b_glu"][l])
            ssm_out = (hr.reshape(b, SSM_GROUPS, SSM_STATE), hi.reshape(b, SSM_GROUPS, SSM_STATE))
        else:
            seq_out = fox_attend(z_seq, kv)
        mem_out = mem_attention(q_mem.reshape(b, t, MEM_WIDTH), mem_k[l], mem_v[l])
        x2 = mix_out(x.reshape(m, d), seq_out.reshape(m, SEQ_WIDTH), mem_out.reshape(m, MEM_WIDTH),
                     p["w_out"][l], p["norm_mix_post"][l])
        ffn_args = (p["norm_ffn_pre"][l], p["norm_ffn_post"][l], p["w_up"][l], p["conv_w"][l],
                    p["conv_b"][l], p["w_down"][l])
        if sequential:
            x3, cp = conv_ffn_seq(x2.reshape(b, t, d), conv_prev[l], *ffn_args)
        else:
            x3, u_new = conv_ffn_step(x2, conv_prev[l], *ffn_args)
            cp = jnp.stack([conv_prev[l][:, 1], u_new], axis=1)
        new_conv.append(cp)
        x = x3.reshape(b, t, d)
    return x, ssm_out, jnp.stack(new_conv), kv


def kernel(x_prompt, x_sample, state_ssm_re, state_ssm_im, cache_k, cache_v, cache_logf,
           cache_mem_k, cache_mem_v, state_ffn_conv, page_table, mem_prompt,
           w_in, w_out, norm_mix_pre, norm_mix_post, norm_ffn_pre, norm_ffn_post,
           mem_norm, w_mem_kv, lam_re, lam_im, log_dt, b_re, b_im, c_re, c_im, d_skip,
           w_glu, b_glu, kv_norm, w_kv, b_f, w_up, conv_w, conv_b, w_down):
    p = dict(w_in=w_in.astype(BF16), w_out=w_out.astype(BF16), norm_mix_pre=norm_mix_pre,
             norm_mix_post=norm_mix_post, norm_ffn_pre=norm_ffn_pre, norm_ffn_post=norm_ffn_post,
             d_skip=d_skip, w_glu=w_glu.astype(BF16), b_glu=b_glu, kv_norm=kv_norm, w_kv=w_kv,
             b_f=b_f, w_up=w_up.astype(BF16), conv_w=conv_w, conv_b=conv_b,
             w_down=w_down.astype(BF16))
    s5_prep = [s5_prepare(lam_re[l], lam_im[l], log_dt[l], b_re[l], b_im[l], c_re[l], c_im[l])
               for l in range(N_A)]

    bp, tp, d = x_prompt.shape
    n_mem = mem_prompt.shape[1]
    mem_pairs = [mem_kv_proj(mem_prompt, mem_norm[l], w_mem_kv[l]) for l in range(DEPTH)]
    p_mem_kt = [mkt for mkt, _ in mem_pairs]
    p_mem_vt = [mvt for _, mvt in mem_pairs]
    zeros_conv = jnp.zeros((DEPTH, bp, CONV_W - 1, 2 * D_FF), F32)

    def fox_prompt(q, kv):
        kt, vt, kaug, vtb, logf = kv
        return fox_attention_seq(q.reshape(bp, tp, SEQ_WIDTH), kaug, vtb)

    y_prompt, p_ssm, p_conv, p_kv = _trunk(x_prompt, p_mem_kt, p_mem_vt, zeros_conv, None,
                                           fox_prompt, p, s5_prep, sequential=True)
    untr = lambda a, n: jnp.transpose(a.reshape(a.shape[0], n, HEAD_DIM, a.shape[2]), (0, 3, 1, 2))
    mem5 = lambda ms: jnp.stack([untr(a, MEM_HEADS) for a in ms])

    bs = x_sample.shape[0]
    tr_mem = lambda a: jnp.transpose(a, (0, 2, 3, 1)).reshape(bs, MEM_WIDTH, n_mem)
    s_mem_kt = [tr_mem(cache_mem_k[l]) for l in range(DEPTH)]
    s_mem_vt = [tr_mem(cache_mem_v[l]) for l in range(DEPTH)]
    ssm0 = (state_ssm_re.reshape(N_A, bs, SSM_WIDTH), state_ssm_im.reshape(N_A, bs, SSM_WIDTH))

    def fox_sample(q, kv):
        k, v, logf = kv
        return fox_attention_decode(q, k, v, logf, cache_k, cache_v, cache_logf, page_table)

    y_sample, s_ssm, s_conv, s_kv = _trunk(x_sample, s_mem_kt, s_mem_vt, state_ffn_conv, ssm0,
                                           fox_sample, p, s5_prep, sequential=False)
    head4 = lambda a: a.reshape(bs, 1, FOX_HEADS, HEAD_DIM)

    return (y_prompt, y_sample, p_ssm[0][None], p_ssm[1][None],
            untr(p_kv[0], FOX_HEADS), untr(p_kv[1], FOX_HEADS), p_kv[4],
            mem5(p_mem_kt), mem5(p_mem_vt), p_conv,
            s_ssm[0][None], s_ssm[1][None],
            head4(s_kv[0]), head4(s_kv[1]), s_kv[2].reshape(bs, 1, FOX_HEADS), s_conv)
```

```python
import functools
import math

import jax
import jax.numpy as jnp
from jax import lax
from jax.experimental import pallas as pl
from jax.experimental.pallas import tpu as pltpu

F32 = jnp.float32
BF16 = jnp.bfloat16

D_MODEL = 1024
DEPTH = 2
N_A = DEPTH // 2
HEAD_DIM = 64
MEM_HEADS = 4
MEM_WIDTH = MEM_HEADS * HEAD_DIM
SEQ_WIDTH = D_MODEL - MEM_WIDTH
SSM_GROUP = 16
SSM_GROUPS = SEQ_WIDTH // SSM_GROUP
SSM_STATE = 64
SSM_WIDTH = SSM_GROUPS * SSM_STATE
FOX_HEADS = SEQ_WIDTH // HEAD_DIM
D_FF = (11 * D_MODEL) // 4
CONV_W = 3
EPS = 1e-6
NEG_INF = -1e30
EIG_CLIP = -1e-4
ATTN_SCALE = HEAD_DIM ** -0.5
LOG2E = math.log2(math.e)

LANES = 128
SUBLANES = 8
MXU_DIM = 256
VMEM_BYTES_V7X = 64 * 1024 * 1024
VMEM_LIMIT = (VMEM_BYTES_V7X * 7) // 8

SSM_BLOCKS = SEQ_WIDTH // MXU_DIM
SSM_BLOCK_STATES = SSM_WIDTH // SSM_BLOCKS
HEAD_PAIRS = FOX_HEADS // 2


def _params(semantics, vmem=None):
    return pltpu.CompilerParams(dimension_semantics=semantics, vmem_limit_bytes=vmem)


def _row_tile(m, cap):
    t = min(m, cap)
    assert m % t == 0, (m, t)
    return t


def _rms(x, g):
    return x * lax.rsqrt(jnp.mean(x * x, axis=-1, keepdims=True) + EPS) * g


def _sigmoid(x):
    return 1.0 / (1.0 + jnp.exp(-x))


def _log_sigmoid(x):
    return -(jnp.maximum(-x, 0.0) + jnp.log1p(jnp.exp(-jnp.abs(x))))


def _split3(x):
    hi = x.astype(BF16)
    r1 = x - hi.astype(F32)
    mid = r1.astype(BF16)
    lo = (r1 - mid.astype(F32)).astype(BF16)
    return hi, mid, lo


def _dot(a, b):
    return jnp.dot(a, b, preferred_element_type=F32)


def _dot_nt(a, b):
    return lax.dot_general(a, b, (((1,), (1,)), ((), ())), preferred_element_type=F32)


def _norm_linear_kernel(x_ref, g_ref, w_ref, *out_refs, splits):
    h = _rms(x_ref[...], g_ref[...]).astype(BF16)
    c0 = 0
    for o_ref, n in zip(out_refs, splits):
        o_ref[...] = _dot(h, w_ref[:, c0:c0 + n])
        c0 += n


def norm_linear(x, g, w, splits, tm_cap=512):
    m, d = x.shape
    tm = _row_tile(m, tm_cap)
    n_tot = sum(splits)
    assert w.shape == (d, n_tot)
    return pl.pallas_call(
        functools.partial(_norm_linear_kernel, splits=tuple(splits)),
        grid=(m // tm,),
        in_specs=[pl.BlockSpec((tm, d), lambda i: (i, 0)),
                  pl.BlockSpec((1, d), lambda i: (0, 0)),
                  pl.BlockSpec((d, n_tot), lambda i: (0, 0))],
        out_specs=[pl.BlockSpec((tm, n), lambda i: (i, 0)) for n in splits],
        out_shape=[jax.ShapeDtypeStruct((m, n), F32) for n in splits],
        compiler_params=_params(("parallel",)),
        name="norm_linear",
    )(x, g.reshape(1, d), w)


KV_TILE = 512
KAUG_WIDTH = 2 * LANES
BIAS_PIECES = 3


def _kv_weights(w_kv, b_f):
    wk = w_kv[:, :SEQ_WIDTH].astype(BF16)
    wf = jnp.pad(w_kv[:, 2 * SEQ_WIDTH:], ((0, 0), (0, LANES - FOX_HEADS))).astype(BF16)
    bf = jnp.pad(b_f, (0, LANES - FOX_HEADS)).reshape(1, LANES)
    return wk, wf, bf


def _kv_seq_kernel(x_ref, g_ref, wk_ref, wvt_ref, wf_ref, bf_ref, tri_ref, place_ref,
                   kt_ref, vt_ref, kaug_ref, vtb_ref, lf_ref, carry_ref):
    @pl.when(pl.program_id(1) == 0)
    def _():
        carry_ref[...] = jnp.zeros_like(carry_ref)

    h = _rms(x_ref[0], g_ref[...]).astype(BF16)
    k = _dot(h, wk_ref[...])
    kt_ref[0] = jnp.transpose(k)
    vt = _dot_nt(wvt_ref[...], h)
    vt_ref[0] = vt
    vtb_ref[0, 0] = vt.astype(BF16)
    logf = _log_sigmoid(_dot(h, wf_ref[...]) + bf_ref[...])
    lf_ref[0] = logf[:, :FOX_HEADS]
    tri = tri_ref[...]
    hi, mid, lo = _split3(logf)
    cum = (_dot(tri, hi) + _dot(tri, mid)) + _dot(tri, lo) + carry_ref[...]
    carry_ref[...] = cum[cum.shape[0] - 1:, :]
    pieces = _split3(cum * -LOG2E)
    bias = sum(_dot(piece, place_ref[j]) for j, piece in enumerate(pieces))
    kb = k.astype(BF16)
    for p in range(HEAD_PAIRS):
        kaug_ref[0, :, p * KAUG_WIDTH:p * KAUG_WIDTH + LANES] = kb[:, p * LANES:(p + 1) * LANES]
        kaug_ref[0, :, p * KAUG_WIDTH + LANES:(p + 1) * KAUG_WIDTH] = (
            bias[:, p * LANES:(p + 1) * LANES].astype(BF16))


def shared_kv_proj_seq(x, g, w_kv, b_f):
    b, t, d = x.shape
    tm = _row_tile(t, KV_TILE)
    wk, wf, bf = _kv_weights(w_kv, b_f)
    wvt = jnp.transpose(w_kv)[SEQ_WIDTH:2 * SEQ_WIDTH].astype(BF16)
    tri = jnp.tril(jnp.ones((tm, tm), F32)).astype(BF16)
    hh = jnp.arange(FOX_HEADS)
    place = jnp.stack([
        jnp.zeros((LANES, SEQ_WIDTH), F32).at[hh, LANES * (hh // 2) + BIAS_PIECES * (hh % 2) + j].set(1.0)
        for j in range(BIAS_PIECES)]).astype(BF16)
    tok = lambda n: pl.BlockSpec((1, tm, n), lambda i, j: (i, j, 0))
    tr = pl.BlockSpec((1, SEQ_WIDTH, tm), lambda i, j: (i, 0, j))
    full = lambda s: pl.BlockSpec(s, lambda i, j: (0,) * len(s))
    return pl.pallas_call(
        _kv_seq_kernel,
        grid=(b, t // tm),
        in_specs=[tok(d), full((1, d)), full((d, SEQ_WIDTH)), full((SEQ_WIDTH, d)),
                  full((d, LANES)), full((1, LANES)), full((tm, tm)),
                  full((BIAS_PIECES, LANES, SEQ_WIDTH))],
        out_specs=[tr, tr, tok(HEAD_PAIRS * KAUG_WIDTH),
                   pl.BlockSpec((1, 1, SEQ_WIDTH, tm), lambda i, j: (i, j, 0, 0)),
                   tok(FOX_HEADS)],
        out_shape=[jax.ShapeDtypeStruct((b, SEQ_WIDTH, t), F32),
                   jax.ShapeDtypeStruct((b, SEQ_WIDTH, t), F32),
                   jax.ShapeDtypeStruct((b, t, HEAD_PAIRS * KAUG_WIDTH), BF16),
                   jax.ShapeDtypeStruct((b, t // tm, SEQ_WIDTH, tm), BF16),
                   jax.ShapeDtypeStruct((b, t, FOX_HEADS), F32)],
        scratch_shapes=[pltpu.VMEM((1, LANES), F32)],
        compiler_params=_params(("parallel", "arbitrary")),
        name="shared_kv_proj_seq",
    )(x, g.reshape(1, d), wk, wvt, wf, bf, tri, place)


def _kv_step_kernel(x_ref, g_ref, wk_ref, wv_ref, wf_ref, bf_ref, k_ref, v_ref, lf_ref):
    h = _rms(x_ref[...], g_ref[...]).astype(BF16)
    k_ref[...] = _dot(h, wk_ref[...])
    v_ref[...] = _dot(h, wv_ref[...])
    lf_ref[...] = _log_sigmoid(_dot(h, wf_ref[...]) + bf_ref[...])[:, :FOX_HEADS]


def shared_kv_proj_step(x, g, w_kv, b_f):
    s, d = x.shape
    wk, wf, bf = _kv_weights(w_kv, b_f)
    wv = w_kv[:, SEQ_WIDTH:2 * SEQ_WIDTH].astype(BF16)
    kv = jax.ShapeDtypeStruct((s, SEQ_WIDTH), F32)
    return pl.pallas_call(
        _kv_step_kernel,
        out_shape=[kv, kv, jax.ShapeDtypeStruct((s, FOX_HEADS), F32)],
        name="shared_kv_proj_step",
    )(x, g.reshape(1, d), wk, wv, wf, bf)


def _mem_kv_kernel(x_ref, g_ref, wt_ref, kt_ref, vt_ref):
    h = _rms(x_ref[0], g_ref[...]).astype(BF16)
    kvt = _dot_nt(wt_ref[...], h)
    kt_ref[0] = kvt[:MEM_WIDTH]
    vt_ref[0] = kvt[MEM_WIDTH:]


def mem_kv_proj(mem, g, w):
    b, n_mem, d = mem.shape
    out = jax.ShapeDtypeStruct((b, MEM_WIDTH, n_mem), F32)
    blk = pl.BlockSpec((1, MEM_WIDTH, n_mem), lambda i: (i, 0, 0))
    return pl.pallas_call(
        _mem_kv_kernel,
        grid=(b,),
        in_specs=[pl.BlockSpec((1, n_mem, d), lambda i: (i, 0, 0)),
                  pl.BlockSpec((1, d), lambda i: (0, 0)),
                  pl.BlockSpec((2 * MEM_WIDTH, d), lambda i: (0, 0))],
        out_specs=[blk, blk],
        out_shape=[out, out],
        compiler_params=_params(("parallel",)),
        name="mem_kv_proj",
    )(mem, g.reshape(1, d), jnp.transpose(w).astype(BF16))


def _mem_attn_kernel(q_ref, mkt_ref, mvt_ref, o_ref):
    q = q_ref[0] * ATTN_SCALE
    mkt = mkt_ref[0].astype(BF16)
    mvt = mvt_ref[0].astype(BF16)
    lane = lax.broadcasted_iota(jnp.int32, (1, MEM_WIDTH), 1)
    out = jnp.zeros(q.shape, F32)
    for h in range(MEM_HEADS):
        in_head = (lane >= h * HEAD_DIM) & (lane < (h + 1) * HEAD_DIM)
        s = _dot(jnp.where(in_head, q, 0.0).astype(BF16), mkt)
        p = jnp.exp(s - jnp.max(s, axis=-1, keepdims=True))
        p = p / jnp.sum(p, axis=-1, keepdims=True)
        out = out + jnp.where(in_head, _dot_nt(p.astype(BF16), mvt), 0.0)
    o_ref[0] = out


def mem_attention(q_mem, mkt, mvt, tm_cap=512):
    b, t, w = q_mem.shape
    n_mem = mkt.shape[2]
    tm = _row_tile(t, tm_cap)
    return pl.pallas_call(
        _mem_attn_kernel,
        grid=(b, t // tm),
        in_specs=[pl.BlockSpec((1, tm, w), lambda i, j: (i, j, 0)),
                  pl.BlockSpec((1, w, n_mem), lambda i, j: (i, 0, 0)),
                  pl.BlockSpec((1, w, n_mem), lambda i, j: (i, 0, 0))],
        out_specs=pl.BlockSpec((1, tm, w), lambda i, j: (i, j, 0)),
        out_shape=jax.ShapeDtypeStruct((b, t, w), F32),
        compiler_params=_params(("parallel", "parallel")),
        name="mem_attention",
    )(q_mem, mkt, mvt)


def _mix_out_kernel(x_ref, s_ref, m_ref, w_ref, g_ref, o_ref):
    o = (_dot(s_ref[...].astype(BF16), w_ref[:SEQ_WIDTH, :])
         + _dot(m_ref[...].astype(BF16), w_ref[SEQ_WIDTH:, :]))
    o_ref[...] = x_ref[...] + _rms(o, g_ref[...])


def mix_out(x, seq_out, mem_out, w_out, g, tm_cap=512):
    m, d = x.shape
    tm = _row_tile(m, tm_cap)
    row = lambda n: pl.BlockSpec((tm, n), lambda i: (i, 0))
    return pl.pallas_call(
        _mix_out_kernel,
        grid=(m // tm,),
        in_specs=[row(d), row(SEQ_WIDTH), row(MEM_WIDTH),
                  pl.BlockSpec((d, d), lambda i: (0, 0)),
                  pl.BlockSpec((1, d), lambda i: (0, 0))],
        out_specs=row(d),
        out_shape=jax.ShapeDtypeStruct((m, d), F32),
        compiler_params=_params(("parallel",)),
        name="mix_out",
    )(x, seq_out, mem_out, w_out, g.reshape(1, d))


FFN_CHUNK = MXU_DIM


def _ffn_seq_kernel(x_ref, prev_ref, gpre_ref, gpost_ref, wup_ref, cw_ref, cb_ref, wdn_ref,
                    o_ref, conv_ref, h_scr, carry_scr, *, tm):
    @pl.when(pl.program_id(1) == 0)
    def _():
        carry_scr[...] = prev_ref[0]

    x = x_ref[0]
    xn = _rms(x, gpre_ref[...]).astype(BF16)
    row = lax.broadcasted_iota(jnp.int32, (SUBLANES, 1), 0)

    def conv(col):
        u = _dot(xn, wup_ref[:, col:col + FFN_CHUNK])
        c0 = carry_scr[0:1, col:col + FFN_CHUNK]
        c1 = carry_scr[1:2, col:col + FFN_CHUNK]
        u1 = pltpu.roll(u, 1, 0)
        u2 = pltpu.roll(u, 2, 0)
        u1 = jnp.concatenate([jnp.where(row == 0, c1, u1[:SUBLANES]), u1[SUBLANES:]], axis=0)
        u2 = jnp.concatenate(
            [jnp.where(row == 0, c0, jnp.where(row == 1, c1, u2[:SUBLANES])), u2[SUBLANES:]],
            axis=0)
        carry_scr[:, col:col + FFN_CHUNK] = u[tm - 2:, :]
        w = cw_ref[:, col:col + FFN_CHUNK]
        return u2 * w[0:1] + u1 * w[1:2] + u * w[2:3] + cb_ref[:, col:col + FFN_CHUNK]

    for c in range(D_FF // FFN_CHUNK):
        gate = conv(c * FFN_CHUNK)
        val = conv(D_FF + c * FFN_CHUNK)
        h_scr[:, c * FFN_CHUNK:(c + 1) * FFN_CHUNK] = (gate * _sigmoid(gate) * val).astype(BF16)

    f = _dot(h_scr[...], wdn_ref[...])
    o_ref[0] = x + _rms(f, gpost_ref[...])
    conv_ref[0] = carry_scr[...]


def conv_ffn_seq(x, prev, g_pre, g_post, w_up, conv_w, conv_b, w_down, tm_cap=512):
    b, t, d = x.shape
    tm = _row_tile(t, tm_cap)
    assert tm >= 2 * SUBLANES
    f2 = 2 * D_FF
    full = lambda s: pl.BlockSpec(s, lambda i, j: (0,) * len(s), pipeline_mode=pl.Buffered(1))
    return pl.pallas_call(
        functools.partial(_ffn_seq_kernel, tm=tm),
        grid=(b, t // tm),
        in_specs=[pl.BlockSpec((1, tm, d), lambda i, j: (i, j, 0)),
                  pl.BlockSpec((1, CONV_W - 1, f2), lambda i, j: (i, 0, 0)),
                  full((1, d)), full((1, d)), full((d, f2)), full((CONV_W, f2)), full((1, f2)),
                  full((D_FF, d))],
        out_specs=[pl.BlockSpec((1, tm, d), lambda i, j: (i, j, 0)),
                   pl.BlockSpec((1, CONV_W - 1, f2), lambda i, j: (i, 0, 0))],
        out_shape=[jax.ShapeDtypeStruct((b, t, d), F32),
                   jax.ShapeDtypeStruct((b, CONV_W - 1, f2), F32)],
        scratch_shapes=[pltpu.VMEM((tm, D_FF), BF16), pltpu.VMEM((CONV_W - 1, f2), F32)],
        compiler_params=_params(("parallel", "arbitrary"), VMEM_LIMIT),
        name="conv_ffn_seq",
    )(x, prev, g_pre.reshape(1, d), g_post.reshape(1, d), w_up, conv_w, conv_b.reshape(1, f2),
      w_down)


def _ffn_step_kernel(x_ref, p0g_ref, p0v_ref, p1g_ref, p1v_ref, gpre_ref, gpost_ref,
                     wg_ref, wv_ref, cwg_ref, cwv_ref, cbg_ref, cbv_ref, wdn_ref,
                     o_ref, ug_ref, uv_ref, acc_scr):
    c = pl.program_id(0)

    @pl.when(c == 0)
    def _():
        acc_scr[...] = jnp.zeros_like(acc_scr)

    x = x_ref[...]
    xn = _rms(x, gpre_ref[...]).astype(BF16)

    def conv(w_ref, p0_ref, p1_ref, cw_ref, cb_ref, u_ref):
        u = _dot(xn, w_ref[...])
        u_ref[...] = u
        w = cw_ref[...]
        return p0_ref[...] * w[0:1] + p1_ref[...] * w[1:2] + u * w[2:3] + cb_ref[...]

    gate = conv(wg_ref, p0g_ref, p1g_ref, cwg_ref, cbg_ref, ug_ref)
    val = conv(wv_ref, p0v_ref, p1v_ref, cwv_ref, cbv_ref, uv_ref)
    acc_scr[...] += _dot((gate * _sigmoid(gate) * val).astype(BF16), wdn_ref[...])

    @pl.when(c == pl.num_programs(0) - 1)
    def _():
        o_ref[...] = x + _rms(acc_scr[...], gpost_ref[...])


def conv_ffn_step(x, prev, g_pre, g_post, w_up, conv_w, conv_b, w_down):
    s, d = x.shape
    f2 = 2 * D_FF
    fc = FFN_CHUNK
    nc = D_FF // fc
    prev2 = prev.reshape(s, (CONV_W - 1) * f2)
    cb = conv_b.reshape(1, f2)
    const = lambda shp: pl.BlockSpec(shp, lambda c: (0, 0))
    col = lambda rows, off: pl.BlockSpec((rows, fc), lambda c, off=off: (0, c + off))
    out, ug, uv = pl.pallas_call(
        _ffn_step_kernel,
        grid=(nc,),
        in_specs=[const((s, d)),
                  col(s, 0), col(s, nc), col(s, 2 * nc), col(s, 3 * nc),
                  const((1, d)), const((1, d)),
                  col(d, 0), col(d, nc),
                  col(CONV_W, 0), col(CONV_W, nc),
                  col(1, 0), col(1, nc),
                  pl.BlockSpec((fc, d), lambda c: (c, 0))],
        out_specs=[const((s, d)), col(s, 0), col(s, 0)],
        out_shape=[jax.ShapeDtypeStruct((s, d), F32),
                   jax.ShapeDtypeStruct((s, D_FF), F32),
                   jax.ShapeDtypeStruct((s, D_FF), F32)],
        scratch_shapes=[pltpu.VMEM((s, d), F32)],
        compiler_params=_params(("arbitrary",)),
        name="conv_ffn_step",
    )(x, prev2, prev2, prev2, prev2, g_pre.reshape(1, d), g_post.reshape(1, d),
      w_up, w_up, conv_w, conv_w, cb, cb, w_down)
    return out, jnp.concatenate([ug, uv], axis=-1)


def _s5_prep_kernel(lr_ref, li_ref, ldt_ref, br_ref, bi_ref, pr_ref, pi_ref, bbr_ref, bbi_ref,
                    abr_ref, abi_ref):
    lr = jnp.minimum(lr_ref[...], EIG_CLIP)
    li = li_ref[...]
    dt = jnp.exp(ldt_ref[...])
    mag = jnp.exp(lr * dt)
    ar = mag * jnp.cos(li * dt)
    ai = mag * jnp.sin(li * dt)
    den = lr * lr + li * li
    nr = ar - 1.0
    fr = (nr * lr + ai * li) / den
    fi = (ai * lr - nr * li) / den
    br = br_ref[...]
    bi = bi_ref[...]
    bbr = fr * br - fi * bi
    bbi = fr * bi + fi * br
    bbr_ref[...] = bbr
    bbi_ref[...] = bbi
    abr_ref[...] = ar * bbr - ai * bbi
    abi_ref[...] = ar * bbi + ai * bbr
    pr, pi = ar, ai
    pr_ref[0] = pr
    pi_ref[0] = pi
    for n in range(1, SUBLANES):
        pr, pi = pr * ar - pi * ai, pr * ai + pi * ar
        pr_ref[n] = pr
        pi_ref[n] = pi


def _block_diag(m):
    nb, ng, r, c = m.shape
    eye = jnp.eye(ng, dtype=m.dtype)
    return jnp.einsum("kgrc,gh->kgrhc", m, eye).reshape(nb, ng * r, ng * c)


def s5_prepare(lam_re, lam_im, log_dt, b_re, b_im, c_re, c_im):
    g, p, c = b_re.shape
    rep = lambda a: jnp.repeat(a, c, axis=0)
    tr = lambda b: jnp.transpose(b, (0, 2, 1)).reshape(g * c, p)
    shp = jax.ShapeDtypeStruct((g * c, p), F32)
    pw = jax.ShapeDtypeStruct((SUBLANES, g * c, p), F32)
    pr, pi, bbr, bbi, abr, abi = pl.pallas_call(
        _s5_prep_kernel, out_shape=[pw, pw, shp, shp, shp, shp], name="s5_prepare",
    )(rep(lam_re), rep(lam_im), jnp.broadcast_to(rep(log_dt[:, None]), (g * c, p)),
      tr(b_re), tr(b_im))
    pr = pr[:, ::c, :].reshape(SUBLANES, g * p)
    pi = pi[:, ::c, :].reshape(SUBLANES, g * p)
    gb = g // SSM_BLOCKS
    b_in = lambda m: _block_diag(m.reshape(SSM_BLOCKS, gb, c, p)).astype(BF16)
    c_out = lambda m: _block_diag(
        jnp.transpose(m, (0, 2, 1)).reshape(SSM_BLOCKS, gb, p, c)).astype(BF16)
    two = lambda m, am: jnp.concatenate([b_in(m), b_in(am)], axis=1)
    return pr, pi, b_in(bbr), b_in(bbi), two(bbr, abr), two(bbi, abi), c_out(c_re), c_out(c_im)


def _s5_finish(y, u, dskip_ref, wglu_ref, bglu_ref):
    y = y + dskip_ref[...] * u
    y = jax.nn.gelu(y)
    return y * _sigmoid(_dot(y.astype(BF16), wglu_ref[...]) + bglu_ref[...])


def _s5_seq_kernel(u_ref, pr_ref, pi_ref, lvr_ref, lvi_ref, bre_ref, bim_ref, cre_ref, cim_ref,
                   dskip_ref, wglu_ref, bglu_ref, y_ref, hr_ref, hi_ref,
                   xr_scr, xi_scr, y_scr, *, tc):
    @pl.when(pl.program_id(1) == 0)
    def _():
        hr_ref[...] = jnp.zeros_like(hr_ref)
        hi_ref[...] = jnp.zeros_like(hi_ref)

    u = u_ref[0]
    ub = u.astype(BF16)
    first = lax.broadcasted_iota(jnp.int32, (tc, 1), 0) % SUBLANES == 0
    ub_prev = jnp.where(first, 0.0, pltpu.roll(u, 1, 0)).astype(BF16)
    nb = SSM_BLOCK_STATES
    last = SUBLANES - 1
    bc = lambda h: jnp.broadcast_to(h[last:last + 1, :], h.shape)
    block = lambda k: slice(k * nb, (k + 1) * nb)

    def project_in(k):
        cols = slice(k * MXU_DIM, (k + 1) * MXU_DIM)
        uk = jnp.concatenate([ub[:, cols], ub_prev[:, cols]], axis=1)
        xr_scr[k] = _dot(uk, bre_ref[k])
        xi_scr[k] = _dot(uk, bim_ref[k])

    def scan(k):
        sl = block(k)
        pr = pr_ref[:, sl]
        pi = pi_ref[:, sl]
        cr, ci = bc(hr_ref[0, :, sl]), bc(hi_ref[0, :, sl])
        for g in range(tc // SUBLANES):
            rows = slice(g * SUBLANES, (g + 1) * SUBLANES)
            xr = xr_scr[k, rows, :]
            xi = xi_scr[k, rows, :]
            for lv in range(1, 3):
                sr = pltpu.roll(xr, 1 << lv, 0)
                si = pltpu.roll(xi, 1 << lv, 0)
                ar = lvr_ref[lv, :, sl]
                ai = lvi_ref[lv, :, sl]
                xr, xi = xr + ar * sr - ai * si, xi + ar * si + ai * sr
            hr = xr + pr * cr - pi * ci
            hi = xi + pr * ci + pi * cr
            xr_scr[k, rows, :] = hr
            xi_scr[k, rows, :] = hi
            cr, ci = bc(hr), bc(hi)
        hr_ref[0, :, sl] = cr
        hi_ref[0, :, sl] = ci

    def project_out(k):
        y_scr[:, k * MXU_DIM:(k + 1) * MXU_DIM] = (
            _dot(xr_scr[k].astype(BF16), cre_ref[k]) - _dot(xi_scr[k].astype(BF16), cim_ref[k]))

    project_in(0)
    for k in range(SSM_BLOCKS):
        if k + 1 < SSM_BLOCKS:
            project_in(k + 1)
        scan(k)
        project_out(k)

    y_ref[0] = _s5_finish(y_scr[...], u, dskip_ref, wglu_ref, bglu_ref)


def s5_mix_seq(u, prep, d_skip, w_glu, b_glu, tc_cap=256):
    b, t, w = u.shape
    tc = _row_tile(t, tc_cap)
    pr, pi, _, _, bre, bim, cre, cim = prep
    rows = jnp.arange(SUBLANES)[:, None]
    lvr = jnp.stack([jnp.where(rows >= (1 << l), pr[(1 << l) - 1][None], 0.0) for l in range(3)])
    lvi = jnp.stack([jnp.where(rows >= (1 << l), pi[(1 << l) - 1][None], 0.0) for l in range(3)])
    full = lambda a: pl.BlockSpec(a.shape, lambda i, j: (0,) * a.ndim)
    consts = [pr, pi, lvr, lvi, bre, bim, cre, cim, d_skip.reshape(1, w), w_glu, b_glu.reshape(1, w)]
    y, hr, hi = pl.pallas_call(
        functools.partial(_s5_seq_kernel, tc=tc),
        grid=(b, t // tc),
        in_specs=[pl.BlockSpec((1, tc, w), lambda i, j: (i, j, 0))] + [full(a) for a in consts],
        out_specs=[pl.BlockSpec((1, tc, w), lambda i, j: (i, j, 0)),
                   pl.BlockSpec((1, SUBLANES, SSM_WIDTH), lambda i, j: (i, 0, 0)),
                   pl.BlockSpec((1, SUBLANES, SSM_WIDTH), lambda i, j: (i, 0, 0))],
        out_shape=[jax.ShapeDtypeStruct((b, t, w), F32),
                   jax.ShapeDtypeStruct((b, SUBLANES, SSM_WIDTH), F32),
                   jax.ShapeDtypeStruct((b, SUBLANES, SSM_WIDTH), F32)],
        scratch_shapes=[pltpu.VMEM((SSM_BLOCKS, tc, SSM_BLOCK_STATES), F32),
                        pltpu.VMEM((SSM_BLOCKS, tc, SSM_BLOCK_STATES), F32),
                        pltpu.VMEM((tc, w), F32)],
        compiler_params=_params(("parallel", "arbitrary")),
        name="s5_mix_seq",
    )(u, *consts)
    return y, hr[:, SUBLANES - 1], hi[:, SUBLANES - 1]


def _s5_step_kernel(u_ref, h0r_ref, h0i_ref, ar_ref, ai_ref, bre_ref, bim_ref, cre_ref, cim_ref,
                    dskip_ref, wglu_ref, bglu_ref, y_ref, hr_ref, hi_ref, y_scr):
    u = u_ref[...]
    ub = u.astype(BF16)
    nb = SSM_BLOCK_STATES
    for k in range(SSM_BLOCKS):
        sl = slice(k * nb, (k + 1) * nb)
        uk = ub[:, k * MXU_DIM:(k + 1) * MXU_DIM]
        ar, ai = ar_ref[:, sl], ai_ref[:, sl]
        h0r, h0i = h0r_ref[:, sl], h0i_ref[:, sl]
        hr = _dot(uk, bre_ref[k]) + ar * h0r - ai * h0i
        hi = _dot(uk, bim_ref[k]) + ar * h0i + ai * h0r
        hr_ref[:, sl] = hr
        hi_ref[:, sl] = hi
        y_scr[:, k * MXU_DIM:(k + 1) * MXU_DIM] = (
            _dot(hr.astype(BF16), cre_ref[k]) - _dot(hi.astype(BF16), cim_ref[k]))
    y_ref[...] = _s5_finish(y_scr[...], u, dskip_ref, wglu_ref, bglu_ref)


def s5_mix_step(u, h0r, h0i, prep, d_skip, w_glu, b_glu):
    s, w = u.shape
    pr, pi, bre, bim, _, _, cre, cim = prep
    st = jax.ShapeDtypeStruct((s, SSM_WIDTH), F32)
    return pl.pallas_call(
        _s5_step_kernel,
        out_shape=[jax.ShapeDtypeStruct((s, w), F32), st, st],
        scratch_shapes=[pltpu.VMEM((s, w), F32)],
        name="s5_mix_step",
    )(u, h0r, h0i, pr[0:1], pi[0:1], bre, bim, cre, cim, d_skip.reshape(1, w), w_glu,
      b_glu.reshape(1, w))


FOX_TQ = 512
FOX_STRIP = MXU_DIM


def _fox_seq_kernel(q_ref, kaug_ref, vt_ref, o_ref, s_scr, p_scr, acc_scr, m_scr, l_scr,
                    *, tq, tk):
    i = pl.program_id(2)
    qt = jnp.transpose(q_ref[0] * (ATTN_SCALE * LOG2E))
    row = lax.broadcasted_iota(jnp.int32, (LANES, 1), 0)
    qa = []
    for h in range(2):
        own = (row >= h * HEAD_DIM) & (row < (h + 1) * HEAD_DIM)
        ones = (row >= h * BIAS_PIECES) & (row < (h + 1) * BIAS_PIECES)
        qa.append(jnp.concatenate(
            [jnp.where(own, qt, 0.0), jnp.broadcast_to(jnp.where(ones, 1.0, 0.0), qt.shape)],
            axis=0).astype(BF16))
    acc_scr[...] = jnp.zeros_like(acc_scr)
    m_scr[...] = jnp.full_like(m_scr, NEG_INF)
    l_scr[...] = jnp.zeros_like(l_scr)
    n_sub = tq // FOX_STRIP
    strips = [(h, qs) for h in range(2) for qs in range(n_sub)]
    lanes = lambda qs: slice(qs * FOX_STRIP, (qs + 1) * FOX_STRIP)
    ones_rows = jnp.ones((2 * SUBLANES, tk), BF16)

    def scores(j, slot):
        ka = kaug_ref[0, pl.ds(pl.multiple_of(j * tk, tk), tk), :]
        for n, (h, qs) in enumerate(strips):
            s_scr[slot, n] = _dot(ka, qa[h][:, lanes(qs)])

    def absorb(j, slot, masked):
        alphas = []
        for n, (h, qs) in enumerate(strips):
            if masked:
                qpos = i * tq + qs * FOX_STRIP + lax.broadcasted_iota(jnp.int32, (1, FOX_STRIP), 1)
                visible = (j * tk + lax.broadcasted_iota(jnp.int32, (tk, 1), 0)) <= qpos
                s_scr[slot, n] = jnp.where(visible, s_scr[slot, n], NEG_INF)
            m = m_scr[n]
            m_new = jnp.maximum(m, jnp.max(s_scr[slot, n], axis=0, keepdims=True))
            m_scr[n] = m_new
            alphas.append(jnp.exp2(m - m_new))
            p_scr[n] = jnp.exp2(s_scr[slot, n] - m_new).astype(BF16)
        for n, (h, qs) in enumerate(strips):
            vt = jnp.concatenate(
                [vt_ref[0, j, h * HEAD_DIM:(h + 1) * HEAD_DIM, :], ones_rows], axis=0)
            pv = _dot(vt, p_scr[n])
            acc_scr[h, :, lanes(qs)] = alphas[n] * acc_scr[h, :, lanes(qs)] + pv[:HEAD_DIM]
            l_scr[n] = alphas[n] * l_scr[n] + pv[HEAD_DIM:HEAD_DIM + 1]

    n_full = (i * tq) // tk
    scores(0, 0)

    def pair(jj, _):
        j = 2 * jj
        scores(j + 1, 1)
        absorb(j, 0, masked=False)
        scores(j + 2, 0)
        absorb(j + 1, 1, masked=False)
        return 0

    lax.fori_loop(0, n_full // 2, pair, 0)
    last_even = (n_full // 2) * 2

    @pl.when(n_full % 2 == 1)
    def _():
        scores(last_even + 1, 1)
        absorb(last_even, 0, masked=False)
        absorb(last_even + 1, 1, masked=True)

    @pl.when(n_full % 2 == 0)
    def _():
        absorb(last_even, 0, masked=True)

    l_head = lambda h: jnp.concatenate(
        [l_scr[h * n_sub + qs] for qs in range(n_sub)], axis=1)
    ot = jnp.concatenate([acc_scr[0] / l_head(0), acc_scr[1] / l_head(1)], axis=0)
    o_ref[0] = jnp.transpose(ot)


def fox_attention_seq(q, kaug, vtb):
    b, t, w = q.shape
    n_chunks, tk = vtb.shape[1], vtb.shape[3]
    tq = _row_tile(t, FOX_TQ)
    assert tk % tq == 0 and n_chunks * tk == t
    n_strips = 2 * tq // FOX_STRIP
    return pl.pallas_call(
        functools.partial(_fox_seq_kernel, tq=tq, tk=tk),
        grid=(b, HEAD_PAIRS, t // tq),
        in_specs=[pl.BlockSpec((1, tq, LANES), lambda bi, p, i: (bi, i, p)),
                  pl.BlockSpec((1, t, KAUG_WIDTH), lambda bi, p, i: (bi, 0, p)),
                  pl.BlockSpec((1, n_chunks, LANES, tk), lambda bi, p, i: (bi, 0, p, 0))],
        out_specs=pl.BlockSpec((1, tq, LANES), lambda bi, p, i: (bi, i, p)),
        out_shape=jax.ShapeDtypeStruct((b, t, w), F32),
        scratch_shapes=[pltpu.VMEM((2, n_strips, tk, FOX_STRIP), F32),
                        pltpu.VMEM((n_strips, tk, FOX_STRIP), BF16),
                        pltpu.VMEM((2, HEAD_DIM, tq), F32),
                        pltpu.VMEM((n_strips, 1, FOX_STRIP), F32),
                        pltpu.VMEM((n_strips, 1, FOX_STRIP), F32)],
        compiler_params=_params(("parallel", "parallel", "arbitrary")),
        name="fox_attention_seq",
    )(q, kaug, vtb)


DEC_PAGES = 16


def _fox_dec_kernel(pt_ref, q_ref, kn_ref, vn_ref, lfn_ref, *rest, page, n_pages):
    kts = rest[:n_pages]
    vts = rest[n_pages:2 * n_pages]
    lfs = rest[2 * n_pages:3 * n_pages]
    tri_ref = rest[3 * n_pages]
    o_ref = rest[3 * n_pages + 1]
    m_scr, l_scr, csum_scr, acc_scr, qcol_scr = rest[3 * n_pages + 2:]
    g = pl.program_id(1)
    hp = 2 * SUBLANES
    qrow = q_ref[0] * ATTN_SCALE

    @pl.when(g == 0)
    def _():
        m_scr[...] = jnp.full_like(m_scr, NEG_INF)
        l_scr[...] = jnp.zeros_like(l_scr)
        csum_scr[...] = jnp.zeros_like(csum_scr)
        acc_scr[...] = jnp.zeros_like(acc_scr)
        qcol_scr[...] = jnp.transpose(jnp.broadcast_to(qrow, (page, SEQ_WIDTH)))

    tri = tri_ref[...]
    base = csum_scr[...]
    head_row = lax.broadcasted_iota(jnp.int32, (hp, page), 0)
    s_parts = []
    for i in range(n_pages):
        hi, mid, lo = _split3(lfs[i][0])
        cum = (_dot(hi, tri) + _dot(mid, tri)) + _dot(lo, tri) + base
        base = jnp.broadcast_to(cum[:, page - 1:page], cum.shape)
        qk = jnp.zeros((hp, page), F32)
        for h in range(FOX_HEADS):
            rows = slice(h * HEAD_DIM, (h + 1) * HEAD_DIM)
            r = jnp.sum(kts[i][0, rows, :] * qcol_scr[rows, :], axis=0, keepdims=True)
            qk = jnp.where(head_row == h, r, qk)
        s_parts.append(qk - cum)
    csum_scr[...] = base
    s = jnp.concatenate(s_parts, axis=1)

    m_old = m_scr[...]
    m_new = jnp.maximum(m_old, jnp.max(s, axis=-1, keepdims=True))
    alpha = jnp.exp(m_old - m_new)
    p = jnp.exp(s - m_new)
    l_scr[...] = alpha * l_scr[...] + jnp.sum(p, axis=-1, keepdims=True)
    m_scr[...] = m_new
    for h in range(FOX_HEADS):
        rows = slice(h * HEAD_DIM, (h + 1) * HEAD_DIM)
        acc = acc_scr[rows, :] * alpha[h:h + 1, :]
        for i in range(n_pages):
            acc = acc + vts[i][0, rows, :] * p[h:h + 1, i * page:(i + 1) * page]
        acc_scr[rows, :] = acc

    @pl.when(g == pl.num_programs(1) - 1)
    def _():
        lane = lax.broadcasted_iota(jnp.int32, (hp, SEQ_WIDTH), 1)
        head = lax.broadcasted_iota(jnp.int32, (hp, SEQ_WIDTH), 0)
        own = (lane >= head * HEAD_DIM) & (lane < (head + 1) * HEAD_DIM)
        spread = lambda col: jnp.sum(jnp.where(own, col, 0.0), axis=0, keepdims=True)
        c_new = csum_scr[:, 0:1] + lfn_ref[0]
        s_new = jnp.sum(jnp.where(own, qrow * kn_ref[0], 0.0), axis=-1, keepdims=True) - c_new
        m_fin = jnp.maximum(m_scr[...], s_new)
        a_fin = jnp.exp(m_scr[...] - m_fin)
        p_new = jnp.exp(s_new - m_fin)
        l_fin = a_fin * l_scr[...] + p_new
        acc_row = jnp.sum(jnp.transpose(acc_scr[...]), axis=0, keepdims=True)
        o_ref[0] = (spread(a_fin) * acc_row + spread(p_new) * vn_ref[0]) / spread(l_fin)


def fox_attention_decode(q, k_new, v_new, logf_new, cache_k, cache_v, cache_logf, page_table):
    s, w = q.shape
    n_pool, page = cache_k.shape[:2]
    assert page == LANES
    pages_per_seq = page_table.shape[1]
    n_pages = min(DEC_PAGES, pages_per_seq)
    assert pages_per_seq % n_pages == 0
    hp = 2 * SUBLANES
    ckt = jnp.transpose(cache_k, (0, 2, 3, 1)).reshape(n_pool, w, page)
    cvt = jnp.transpose(cache_v, (0, 2, 3, 1)).reshape(n_pool, w, page)
    clf = jnp.pad(jnp.transpose(cache_logf, (0, 2, 1)), ((0, 0), (0, hp - FOX_HEADS), (0, 0)))
    lfn = jnp.pad(logf_new, ((0, 0), (0, hp - FOX_HEADS))).reshape(s, hp, 1)
    tri = jnp.triu(jnp.ones((page, page), F32)).astype(BF16)
    row = pl.BlockSpec((1, 1, w), lambda b, g, pt: (b, 0, 0))

    def paged(shape, i):
        return pl.BlockSpec((1,) + shape, lambda b, g, pt, i=i: (pt[b, g * n_pages + i], 0, 0))

    grid_spec = pltpu.PrefetchScalarGridSpec(
        num_scalar_prefetch=1,
        grid=(s, pages_per_seq // n_pages),
        in_specs=([row, row, row, pl.BlockSpec((1, hp, 1), lambda b, g, pt: (b, 0, 0))]
                  + [paged((w, page), i) for i in range(n_pages)]
                  + [paged((w, page), i) for i in range(n_pages)]
                  + [paged((hp, page), i) for i in range(n_pages)]
                  + [pl.BlockSpec((page, page), lambda b, g, pt: (0, 0))]),
        out_specs=row,
        scratch_shapes=[pltpu.VMEM((hp, 1), F32), pltpu.VMEM((hp, 1), F32),
                        pltpu.VMEM((hp, page), F32), pltpu.VMEM((w, page), F32),
                        pltpu.VMEM((w, page), F32)],
    )
    out = pl.pallas_call(
        functools.partial(_fox_dec_kernel, page=page, n_pages=n_pages),
        grid_spec=grid_spec,
        out_shape=jax.ShapeDtypeStruct((s, 1, w), F32),
        compiler_params=_params(("parallel", "arbitrary"), VMEM_LIMIT),
        name="fox_attention_decode",
    )(page_table, q.reshape(s, 1, w), k_new.reshape(s, 1, w), v_new.reshape(s, 1, w), lfn,
      *([ckt] * n_pages), *([cvt] * n_pages), *([clf] * n_pages), tri)
    return out.reshape(s, w)


def _trunk(x, mem_k, mem_v, conv_prev, ssm_state, fox_attend, p, s5_prep, sequential):
    assert N_A == 1
    b, t, d = x.shape
    m = b * t
    new_conv, ssm_out, kv = [], None, None
    for l in range(DEPTH):
        if l == N_A:
            if sequential:
                kv = shared_kv_proj_seq(x, p["kv_norm"], p["w_kv"], p["b_f"])
            else:
                kv = shared_kv_proj_step(x.reshape(m, d), p["kv_norm"], p["w_kv"], p["b_f"])
        z_seq, q_mem = norm_linear(x.reshape(m, d), p["norm_mix_pre"][l], p["w_in"][l],
                                   (SEQ_WIDTH, MEM_WIDTH))
        if l < N_A:
            if sequential:
                seq_out, hr, hi = s5_mix_seq(z_seq.reshape(b, t, SEQ_WIDTH), s5_prep[l],
                                             p["d_skip"][l], p["w_glu"][l], p["b_glu"][l])
            else:
                seq_out, hr, hi = s5_mix_step(z_seq, ssm_state[0][l], ssm_state[1][l], s5_prep[l],
                                              p["d_skip"][l], p["w_glu"][l], p["b_glu"][l])
            ssm_out = (hr.reshape(b, SSM_GROUPS, SSM_STATE), hi.reshape(b, SSM_GROUPS, SSM_STATE))
        else:
            seq_out = fox_attend(z_seq, kv)
        mem_out = mem_attention(q_mem.reshape(b, t, MEM_WIDTH), mem_k[l], mem_v[l])
        x2 = mix_out(x.reshape(m, d), seq_out.reshape(m, SEQ_WIDTH), mem_out.reshape(m, MEM_WIDTH),
                     p["w_out"][l], p["norm_mix_post"][l])
        ffn_args = (p["norm_ffn_pre"][l], p["norm_ffn_post"][l], p["w_up"][l], p["conv_w"][l],
                    p["conv_b"][l], p["w_down"][l])
        if sequential:
            x3, cp = conv_ffn_seq(x2.reshape(b, t, d), conv_prev[l], *ffn_args)
        else:
            x3, u_new = conv_ffn_step(x2, conv_prev[l], *ffn_args)
            cp = jnp.stack([conv_prev[l][:, 1], u_new], axis=1)
        new_conv.append(cp)
        x = x3.reshape(b, t, d)
    return x, ssm_out, jnp.stack(new_conv), kv


def kernel(x_prompt, x_sample, state_ssm_re, state_ssm_im, cache_k, cache_v, cache_logf,
           cache_mem_k, cache_mem_v, state_ffn_conv, page_table, mem_prompt,
           w_in, w_out, norm_mix_pre, norm_mix_post, norm_ffn_pre, norm_ffn_post,
           mem_norm, w_mem_kv, lam_re, lam_im, log_dt, b_re, b_im, c_re, c_im, d_skip,
           w_glu, b_glu, kv_norm, w_kv, b_f, w_up, conv_w, conv_b, w_down):
    p = dict(w_in=w_in.astype(BF16), w_out=w_out.astype(BF16), norm_mix_pre=norm_mix_pre,
             norm_mix_post=norm_mix_post, norm_ffn_pre=norm_ffn_pre, norm_ffn_post=norm_ffn_post,
             d_skip=d_skip, w_glu=w_glu.astype(BF16), b_glu=b_glu, kv_norm=kv_norm, w_kv=w_kv,
             b_f=b_f, w_up=w_up.astype(BF16), conv_w=conv_w, conv_b=conv_b,
             w_down=w_down.astype(BF16))
    s5_prep = [s5_prepare(lam_re[l], lam_im[l], log_dt[l], b_re[l], b_im[l], c_re[l], c_im[l])
               for l in range(N_A)]

    bp, tp, d = x_prompt.shape
    n_mem = mem_prompt.shape[1]
    mem_pairs = [mem_kv_proj(mem_prompt, mem_norm[l], w_mem_kv[l]) for l in range(DEPTH)]
    p_mem_kt = [mkt for mkt, _ in mem_pairs]
    p_mem_vt = [mvt for _, mvt in mem_pairs]
    zeros_conv = jnp.zeros((DEPTH, bp, CONV_W - 1, 2 * D_FF), F32)

    def fox_prompt(q, kv):
        kt, vt, kaug, vtb, logf = kv
        return fox_attention_seq(q.reshape(bp, tp, SEQ_WIDTH), kaug, vtb)

    y_prompt, p_ssm, p_conv, p_kv = _trunk(x_prompt, p_mem_kt, p_mem_vt, zeros_conv, None,
                                           fox_prompt, p, s5_prep, sequential=True)
    untr = lambda a, n: jnp.transpose(a.reshape(a.shape[0], n, HEAD_DIM, a.shape[2]), (0, 3, 1, 2))
    mem5 = lambda ms: jnp.stack([untr(a, MEM_HEADS) for a in ms])

    bs = x_sample.shape[0]
    tr_mem = lambda a: jnp.transpose(a, (0, 2, 3, 1)).reshape(bs, MEM_WIDTH, n_mem)
    s_mem_kt = [tr_mem(cache_mem_k[l]) for l in range(DEPTH)]
    s_mem_vt = [tr_mem(cache_mem_v[l]) for l in range(DEPTH)]
    ssm0 = (state_ssm_re.reshape(N_A, bs, SSM_WIDTH), state_ssm_im.reshape(N_A, bs, SSM_WIDTH))

    def fox_sample(q, kv):
        k, v, logf = kv
        return fox_attention_decode(q, k, v, logf, cache_k, cache_v, cache_logf, page_table)

    y_sample, s_ssm, s_conv, s_kv = _trunk(x_sample, s_mem_kt, s_mem_vt, state_ffn_conv, ssm0,
                                           fox_sample, p, s5_prep, sequential=False)
    head4 = lambda a: a.reshape(bs, 1, FOX_HEADS, HEAD_DIM)

    return (y_prompt, y_sample, p_ssm[0][None], p_ssm[1][None],
            untr(p_kv[0], FOX_HEADS), untr(p_kv[1], FOX_HEADS), p_kv[4],
            mem5(p_mem_kt), mem5(p_mem_vt), p_conv,
            s_ssm[0][None], s_ssm[1][None],
            head4(s_kv[0]), head4(s_kv[1]), s_kv[2].reshape(bs, 1, FOX_HEADS), s_conv)
```

```python
import functools
import math

import jax
import jax.numpy as jnp
from jax import lax
from jax.experimental import pallas as pl
from jax.experimental.pallas import tpu as pltpu

F32 = jnp.float32
BF16 = jnp.bfloat16

D_MODEL = 1024
DEPTH = 2
N_A = DEPTH // 2
HEAD_DIM = 64
MEM_HEADS = 4
MEM_WIDTH = MEM_HEADS * HEAD_DIM
SEQ_WIDTH = D_MODEL - MEM_WIDTH
SSM_GROUP = 16
SSM_GROUPS = SEQ_WIDTH // SSM_GROUP
SSM_STATE = 64
SSM_WIDTH = SSM_GROUPS * SSM_STATE
FOX_HEADS = SEQ_WIDTH // HEAD_DIM
D_FF = (11 * D_MODEL) // 4
CONV_W = 3
EPS = 1e-6
NEG_INF = -1e30
EIG_CLIP = -1e-4
ATTN_SCALE = HEAD_DIM ** -0.5
LOG2E = math.log2(math.e)

LANES = 128
SUBLANES = 8
MXU_DIM = 256
VMEM_BYTES_V7X = 64 * 1024 * 1024
VMEM_LIMIT = (VMEM_BYTES_V7X * 7) // 8

SSM_BLOCKS = SEQ_WIDTH // MXU_DIM
SSM_BLOCK_STATES = SSM_WIDTH // SSM_BLOCKS
HEAD_PAIRS = FOX_HEADS // 2


def _params(semantics, vmem=None):
    return pltpu.CompilerParams(dimension_semantics=semantics, vmem_limit_bytes=vmem)


def _row_tile(m, cap):
    t = min(m, cap)
    assert m % t == 0, (m, t)
    return t


def _rms(x, g):
    return x * lax.rsqrt(jnp.mean(x * x, axis=-1, keepdims=True) + EPS) * g


def _sigmoid(x):
    return 1.0 / (1.0 + jnp.exp(-x))


def _log_sigmoid(x):
    return -(jnp.maximum(-x, 0.0) + jnp.log1p(jnp.exp(-jnp.abs(x))))


def _split3(x):
    hi = x.astype(BF16)
    r1 = x - hi.astype(F32)
    mid = r1.astype(BF16)
    lo = (r1 - mid.astype(F32)).astype(BF16)
    return hi, mid, lo


def _dot(a, b):
    return jnp.dot(a, b, preferred_element_type=F32)


def _dot_nt(a, b):
    return lax.dot_general(a, b, (((1,), (1,)), ((), ())), preferred_element_type=F32)


def _norm_linear_kernel(x_ref, g_ref, w_ref, *out_refs, splits):
    h = _rms(x_ref[...], g_ref[...]).astype(BF16)
    c0 = 0
    for o_ref, n in zip(out_refs, splits):
        o_ref[...] = _dot(h, w_ref[:, c0:c0 + n])
        c0 += n


def norm_linear(x, g, w, splits, tm_cap=512):
    m, d = x.shape
    tm = _row_tile(m, tm_cap)
    n_tot = sum(splits)
    assert w.shape == (d, n_tot)
    return pl.pallas_call(
        functools.partial(_norm_linear_kernel, splits=tuple(splits)),
        grid=(m // tm,),
        in_specs=[pl.BlockSpec((tm, d), lambda i: (i, 0)),
                  pl.BlockSpec((1, d), lambda i: (0, 0)),
                  pl.BlockSpec((d, n_tot), lambda i: (0, 0))],
        out_specs=[pl.BlockSpec((tm, n), lambda i: (i, 0)) for n in splits],
        out_shape=[jax.ShapeDtypeStruct((m, n), F32) for n in splits],
        compiler_params=_params(("parallel",)),
        name="norm_linear",
    )(x, g.reshape(1, d), w)


KV_TILE = 512
KAUG_WIDTH = 2 * LANES
BIAS_PIECES = 3


def _kv_weights(w_kv, b_f):
    wk = w_kv[:, :SEQ_WIDTH].astype(BF16)
    wf = jnp.pad(w_kv[:, 2 * SEQ_WIDTH:], ((0, 0), (0, LANES - FOX_HEADS))).astype(BF16)
    bf = jnp.pad(b_f, (0, LANES - FOX_HEADS)).reshape(1, LANES)
    return wk, wf, bf


def _kv_seq_kernel(x_ref, g_ref, wk_ref, wvt_ref, wf_ref, bf_ref, tri_ref, place_ref,
                   kt_ref, vt_ref, kaug_ref, vtb_ref, lf_ref, carry_ref):
    @pl.when(pl.program_id(1) == 0)
    def _():
        carry_ref[...] = jnp.zeros_like(carry_ref)

    h = _rms(x_ref[0], g_ref[...]).astype(BF16)
    k = _dot(h, wk_ref[...])
    kt_ref[0] = jnp.transpose(k)
    vt = _dot_nt(wvt_ref[...], h)
    vt_ref[0] = vt
    vtb_ref[0, 0] = vt.astype(BF16)
    logf = _log_sigmoid(_dot(h, wf_ref[...]) + bf_ref[...])
    lf_ref[0] = logf[:, :FOX_HEADS]
    tri = tri_ref[...]
    hi, mid, lo = _split3(logf)
    cum = (_dot(tri, hi) + _dot(tri, mid)) + _dot(tri, lo) + carry_ref[...]
    carry_ref[...] = cum[cum.shape[0] - 1:, :]
    pieces = _split3(cum * -LOG2E)
    bias = sum(_dot(piece, place_ref[j]) for j, piece in enumerate(pieces))
    kb = k.astype(BF16)
    for p in range(HEAD_PAIRS):
        kaug_ref[0, :, p * KAUG_WIDTH:p * KAUG_WIDTH + LANES] = kb[:, p * LANES:(p + 1) * LANES]
        kaug_ref[0, :, p * KAUG_WIDTH + LANES:(p + 1) * KAUG_WIDTH] = (
            bias[:, p * LANES:(p + 1) * LANES].astype(BF16))


def shared_kv_proj_seq(x, g, w_kv, b_f):
    b, t, d = x.shape
    tm = _row_tile(t, KV_TILE)
    wk, wf, bf = _kv_weights(w_kv, b_f)
    wvt = jnp.transpose(w_kv)[SEQ_WIDTH:2 * SEQ_WIDTH].astype(BF16)
    tri = jnp.tril(jnp.ones((tm, tm), F32)).astype(BF16)
    hh = jnp.arange(FOX_HEADS)
    place = jnp.stack([
        jnp.zeros((LANES, SEQ_WIDTH), F32).at[hh, LANES * (hh // 2) + BIAS_PIECES * (hh % 2) + j].set(1.0)
        for j in range(BIAS_PIECES)]).astype(BF16)
    tok = lambda n: pl.BlockSpec((1, tm, n), lambda i, j: (i, j, 0))
    tr = pl.BlockSpec((1, SEQ_WIDTH, tm), lambda i, j: (i, 0, j))
    full = lambda s: pl.BlockSpec(s, lambda i, j: (0,) * len(s))
    return pl.pallas_call(
        _kv_seq_kernel,
        grid=(b, t // tm),
        in_specs=[tok(d), full((1, d)), full((d, SEQ_WIDTH)), full((SEQ_WIDTH, d)),
                  full((d, LANES)), full((1, LANES)), full((tm, tm)),
                  full((BIAS_PIECES, LANES, SEQ_WIDTH))],
        out_specs=[tr, tr, tok(HEAD_PAIRS * KAUG_WIDTH),
                   pl.BlockSpec((1, 1, SEQ_WIDTH, tm), lambda i, j: (i, j, 0, 0)),
                   tok(FOX_HEADS)],
        out_shape=[jax.ShapeDtypeStruct((b, SEQ_WIDTH, t), F32),
                   jax.ShapeDtypeStruct((b, SEQ_WIDTH, t), F32),
                   jax.ShapeDtypeStruct((b, t, HEAD_PAIRS * KAUG_WIDTH), BF16),
                   jax.ShapeDtypeStruct((b, t // tm, SEQ_WIDTH, tm), BF16),
                   jax.ShapeDtypeStruct((b, t, FOX_HEADS), F32)],
        scratch_shapes=[pltpu.VMEM((1, LANES), F32)],
        compiler_params=_params(("parallel", "arbitrary")),
        name="shared_kv_proj_seq",
    )(x, g.reshape(1, d), wk, wvt, wf, bf, tri, place)


def _kv_step_kernel(x_ref, g_ref, wk_ref, wv_ref, wf_ref, bf_ref, k_ref, v_ref, lf_ref):
    h = _rms(x_ref[...], g_ref[...]).astype(BF16)
    k_ref[...] = _dot(h, wk_ref[...])
    v_ref[...] = _dot(h, wv_ref[...])
    lf_ref[...] = _log_sigmoid(_dot(h, wf_ref[...]) + bf_ref[...])[:, :FOX_HEADS]


def shared_kv_proj_step(x, g, w_kv, b_f):
    s, d = x.shape
    wk, wf, bf = _kv_weights(w_kv, b_f)
    wv = w_kv[:, SEQ_WIDTH:2 * SEQ_WIDTH].astype(BF16)
    kv = jax.ShapeDtypeStruct((s, SEQ_WIDTH), F32)
    return pl.pallas_call(
        _kv_step_kernel,
        out_shape=[kv, kv, jax.ShapeDtypeStruct((s, FOX_HEADS), F32)],
        name="shared_kv_proj_step",
    )(x, g.reshape(1, d), wk, wv, wf, bf)


def _mem_kv_kernel(x_ref, g_ref, wt_ref, kt_ref, vt_ref):
    h = _rms(x_ref[0], g_ref[...]).astype(BF16)
    kvt = _dot_nt(wt_ref[...], h)
    kt_ref[0] = kvt[:MEM_WIDTH]
    vt_ref[0] = kvt[MEM_WIDTH:]


def mem_kv_proj(mem, g, w):
    b, n_mem, d = mem.shape
    out = jax.ShapeDtypeStruct((b, MEM_WIDTH, n_mem), F32)
    blk = pl.BlockSpec((1, MEM_WIDTH, n_mem), lambda i: (i, 0, 0))
    return pl.pallas_call(
        _mem_kv_kernel,
        grid=(b,),
        in_specs=[pl.BlockSpec((1, n_mem, d), lambda i: (i, 0, 0)),
                  pl.BlockSpec((1, d), lambda i: (0, 0)),
                  pl.BlockSpec((2 * MEM_WIDTH, d), lambda i: (0, 0))],
        out_specs=[blk, blk],
        out_shape=[out, out],
        compiler_params=_params(("parallel",)),
        name="mem_kv_proj",
    )(mem, g.reshape(1, d), jnp.transpose(w).astype(BF16))


def _mem_attn_tile(q, mkt, mvt):
    q = q * ATTN_SCALE
    mkt = mkt.astype(BF16)
    mvt = mvt.astype(BF16)
    lane = lax.broadcasted_iota(jnp.int32, (1, MEM_WIDTH), 1)
    out = jnp.zeros(q.shape, F32)
    for h in range(MEM_HEADS):
        in_head = (lane >= h * HEAD_DIM) & (lane < (h + 1) * HEAD_DIM)
        s = _dot(jnp.where(in_head, q, 0.0).astype(BF16), mkt)
        p = jnp.exp(s - jnp.max(s, axis=-1, keepdims=True))
        p = p / jnp.sum(p, axis=-1, keepdims=True)
        out = out + jnp.where(in_head, _dot_nt(p.astype(BF16), mvt), 0.0)
    return out


def _mem_attn_kernel(q_ref, mkt_ref, mvt_ref, o_ref):
    for s in range(q_ref.shape[0]):
        o_ref[s] = _mem_attn_tile(q_ref[s], mkt_ref[s], mvt_ref[s])


def mem_attention(q_mem, mkt, mvt, seqs_per_step=SUBLANES):
    b, t, w = q_mem.shape
    n_mem = mkt.shape[2]
    bs = _row_tile(b, seqs_per_step)
    return pl.pallas_call(
        _mem_attn_kernel,
        grid=(b // bs,),
        in_specs=[pl.BlockSpec((bs, t, w), lambda i: (i, 0, 0)),
                  pl.BlockSpec((bs, w, n_mem), lambda i: (i, 0, 0)),
                  pl.BlockSpec((bs, w, n_mem), lambda i: (i, 0, 0))],
        out_specs=pl.BlockSpec((bs, t, w), lambda i: (i, 0, 0)),
        out_shape=jax.ShapeDtypeStruct((b, t, w), F32),
        compiler_params=_params(("parallel",)),
        name="mem_attention",
    )(q_mem, mkt, mvt)


def _mix_out_tile(x, seq, mem, w_ref, g):
    o = (_dot(seq.astype(BF16), w_ref[:SEQ_WIDTH, :]) + _dot(mem.astype(BF16), w_ref[SEQ_WIDTH:, :]))
    return x + _rms(o, g)


def _mix_out_kernel(x_ref, s_ref, m_ref, w_ref, g_ref, o_ref):
    o_ref[...] = _mix_out_tile(x_ref[...], s_ref[...], m_ref[...], w_ref, g_ref[...])


def mix_out(x, seq_out, mem_out, w_out, g, tm_cap=512):
    m, d = x.shape
    tm = _row_tile(m, tm_cap)
    row = lambda n: pl.BlockSpec((tm, n), lambda i: (i, 0))
    return pl.pallas_call(
        _mix_out_kernel,
        grid=(m // tm,),
        in_specs=[row(d), row(SEQ_WIDTH), row(MEM_WIDTH),
                  pl.BlockSpec((d, d), lambda i: (0, 0)),
                  pl.BlockSpec((1, d), lambda i: (0, 0))],
        out_specs=row(d),
        out_shape=jax.ShapeDtypeStruct((m, d), F32),
        compiler_params=_params(("parallel",)),
        name="mix_out",
    )(x, seq_out, mem_out, w_out, g.reshape(1, d))


FFN_CHUNK = MXU_DIM


def _tail_seq_kernel(x_ref, seq_ref, qm_ref, mkt_ref, mvt_ref, wout_ref, gmix_ref, prev_ref,
                     gpre_ref, gpost_ref, wup_ref, cw_ref, cb_ref, wdn_ref,
                     o_ref, conv_ref, h_scr, carry_scr, *, tm):
    @pl.when(pl.program_id(1) == 0)
    def _():
        carry_scr[...] = prev_ref[0]

    mem = _mem_attn_tile(qm_ref[0], mkt_ref[0], mvt_ref[0])
    x = _mix_out_tile(x_ref[0], seq_ref[0], mem, wout_ref, gmix_ref[...])
    xn = _rms(x, gpre_ref[...]).astype(BF16)
    row = lax.broadcasted_iota(jnp.int32, (SUBLANES, 1), 0)

    def conv(col):
        u = _dot(xn, wup_ref[:, col:col + FFN_CHUNK])
        c0 = carry_scr[0:1, col:col + FFN_CHUNK]
        c1 = carry_scr[1:2, col:col + FFN_CHUNK]
        u1 = pltpu.roll(u, 1, 0)
        u2 = pltpu.roll(u, 2, 0)
        u1 = jnp.concatenate([jnp.where(row == 0, c1, u1[:SUBLANES]), u1[SUBLANES:]], axis=0)
        u2 = jnp.concatenate(
            [jnp.where(row == 0, c0, jnp.where(row == 1, c1, u2[:SUBLANES])), u2[SUBLANES:]],
            axis=0)
        carry_scr[:, col:col + FFN_CHUNK] = u[tm - 2:, :]
        w = cw_ref[:, col:col + FFN_CHUNK]
        return u2 * w[0:1] + u1 * w[1:2] + u * w[2:3] + cb_ref[:, col:col + FFN_CHUNK]

    for c in range(D_FF // FFN_CHUNK):
        gate = conv(c * FFN_CHUNK)
        val = conv(D_FF + c * FFN_CHUNK)
        h_scr[:, c * FFN_CHUNK:(c + 1) * FFN_CHUNK] = (gate * _sigmoid(gate) * val).astype(BF16)

    f = _dot(h_scr[...], wdn_ref[...])
    o_ref[0] = x + _rms(f, gpost_ref[...])
    conv_ref[0] = carry_scr[...]


def layer_tail_seq(x, seq_out, q_mem, mkt, mvt, w_out, g_mix, prev, g_pre, g_post, w_up, conv_w,
                   conv_b, w_down, tm_cap=512):
    b, t, d = x.shape
    n_mem = mkt.shape[2]
    tm = _row_tile(t, tm_cap)
    assert tm >= 2 * SUBLANES
    f2 = 2 * D_FF
    full = lambda s: pl.BlockSpec(s, lambda i, j: (0,) * len(s), pipeline_mode=pl.Buffered(1))
    tok = lambda n: pl.BlockSpec((1, tm, n), lambda i, j: (i, j, 0))
    per_seq = lambda r, c: pl.BlockSpec((1, r, c), lambda i, j: (i, 0, 0))
    return pl.pallas_call(
        functools.partial(_tail_seq_kernel, tm=tm),
        grid=(b, t // tm),
        in_specs=[tok(d), tok(SEQ_WIDTH), tok(MEM_WIDTH),
                  per_seq(MEM_WIDTH, n_mem), per_seq(MEM_WIDTH, n_mem),
                  full((d, d)), full((1, d)), per_seq(CONV_W - 1, f2),
                  full((1, d)), full((1, d)), full((d, f2)), full((CONV_W, f2)), full((1, f2)),
                  full((D_FF, d))],
        out_specs=[tok(d), per_seq(CONV_W - 1, f2)],
        out_shape=[jax.ShapeDtypeStruct((b, t, d), F32),
                   jax.ShapeDtypeStruct((b, CONV_W - 1, f2), F32)],
        scratch_shapes=[pltpu.VMEM((tm, D_FF), BF16), pltpu.VMEM((CONV_W - 1, f2), F32)],
        compiler_params=_params(("parallel", "arbitrary"), VMEM_LIMIT),
        name="layer_tail_seq",
    )(x, seq_out, q_mem, mkt, mvt, w_out, g_mix.reshape(1, d), prev, g_pre.reshape(1, d),
      g_post.reshape(1, d), w_up, conv_w, conv_b.reshape(1, f2), w_down)


def _ffn_step_kernel(x_ref, p0g_ref, p0v_ref, p1g_ref, p1v_ref, gpre_ref, gpost_ref,
                     wg_ref, wv_ref, cwg_ref, cwv_ref, cbg_ref, cbv_ref, wdn_ref,
                     o_ref, ug_ref, uv_ref, acc_scr):
    c = pl.program_id(0)

    @pl.when(c == 0)
    def _():
        acc_scr[...] = jnp.zeros_like(acc_scr)

    x = x_ref[...]
    xn = _rms(x, gpre_ref[...]).astype(BF16)

    def conv(w_ref, p0_ref, p1_ref, cw_ref, cb_ref, u_ref):
        u = _dot(xn, w_ref[...])
        u_ref[...] = u
        w = cw_ref[...]
        return p0_ref[...] * w[0:1] + p1_ref[...] * w[1:2] + u * w[2:3] + cb_ref[...]

    gate = conv(wg_ref, p0g_ref, p1g_ref, cwg_ref, cbg_ref, ug_ref)
    val = conv(wv_ref, p0v_ref, p1v_ref, cwv_ref, cbv_ref, uv_ref)
    acc_scr[...] += _dot((gate * _sigmoid(gate) * val).astype(BF16), wdn_ref[...])

    @pl.when(c == pl.num_programs(0) - 1)
    def _():
        o_ref[...] = x + _rms(acc_scr[...], gpost_ref[...])


def conv_ffn_step(x, prev, g_pre, g_post, w_up, conv_w, conv_b, w_down):
    s, d = x.shape
    f2 = 2 * D_FF
    fc = FFN_CHUNK
    nc = D_FF // fc
    prev2 = prev.reshape(s, (CONV_W - 1) * f2)
    cb = conv_b.reshape(1, f2)
    const = lambda shp: pl.BlockSpec(shp, lambda c: (0, 0))
    col = lambda rows, off: pl.BlockSpec((rows, fc), lambda c, off=off: (0, c + off))
    out, ug, uv = pl.pallas_call(
        _ffn_step_kernel,
        grid=(nc,),
        in_specs=[const((s, d)),
                  col(s, 0), col(s, nc), col(s, 2 * nc), col(s, 3 * nc),
                  const((1, d)), const((1, d)),
                  col(d, 0), col(d, nc),
                  col(CONV_W, 0), col(CONV_W, nc),
                  col(1, 0), col(1, nc),
                  pl.BlockSpec((fc, d), lambda c: (c, 0))],
        out_specs=[const((s, d)), col(s, 0), col(s, 0)],
        out_shape=[jax.ShapeDtypeStruct((s, d), F32),
                   jax.ShapeDtypeStruct((s, D_FF), F32),
                   jax.ShapeDtypeStruct((s, D_FF), F32)],
        scratch_shapes=[pltpu.VMEM((s, d), F32)],
        compiler_params=_params(("arbitrary",)),
        name="conv_ffn_step",
    )(x, prev2, prev2, prev2, prev2, g_pre.reshape(1, d), g_post.reshape(1, d),
      w_up, w_up, conv_w, conv_w, cb, cb, w_down)
    return out, jnp.concatenate([ug, uv], axis=-1)


def _s5_prep_kernel(lr_ref, li_ref, ldt_ref, br_ref, bi_ref, pr_ref, pi_ref, bbr_ref, bbi_ref,
                    abr_ref, abi_ref):
    lr = jnp.minimum(lr_ref[...], EIG_CLIP)
    li = li_ref[...]
    dt = jnp.exp(ldt_ref[...])
    mag = jnp.exp(lr * dt)
    ar = mag * jnp.cos(li * dt)
    ai = mag * jnp.sin(li * dt)
    den = lr * lr + li * li
    nr = ar - 1.0
    fr = (nr * lr + ai * li) / den
    fi = (ai * lr - nr * li) / den
    br = br_ref[...]
    bi = bi_ref[...]
    bbr = fr * br - fi * bi
    bbi = fr * bi + fi * br
    bbr_ref[...] = bbr
    bbi_ref[...] = bbi
    abr_ref[...] = ar * bbr - ai * bbi
    abi_ref[...] = ar * bbi + ai * bbr
    pr, pi = ar, ai
    pr_ref[0] = pr
    pi_ref[0] = pi
    for n in range(1, SUBLANES):
        pr, pi = pr * ar - pi * ai, pr * ai + pi * ar
        pr_ref[n] = pr
        pi_ref[n] = pi


def _block_diag(m):
    nb, ng, r, c = m.shape
    eye = jnp.eye(ng, dtype=m.dtype)
    return jnp.einsum("kgrc,gh->kgrhc", m, eye).reshape(nb, ng * r, ng * c)


def s5_prepare(lam_re, lam_im, log_dt, b_re, b_im, c_re, c_im):
    g, p, c = b_re.shape
    rep = lambda a: jnp.repeat(a, c, axis=0)
    tr = lambda b: jnp.transpose(b, (0, 2, 1)).reshape(g * c, p)
    shp = jax.ShapeDtypeStruct((g * c, p), F32)
    pw = jax.ShapeDtypeStruct((SUBLANES, g * c, p), F32)
    pr, pi, bbr, bbi, abr, abi = pl.pallas_call(
        _s5_prep_kernel, out_shape=[pw, pw, shp, shp, shp, shp], name="s5_prepare",
    )(rep(lam_re), rep(lam_im), jnp.broadcast_to(rep(log_dt[:, None]), (g * c, p)),
      tr(b_re), tr(b_im))
    pr = pr[:, ::c, :].reshape(SUBLANES, g * p)
    pi = pi[:, ::c, :].reshape(SUBLANES, g * p)
    gb = g // SSM_BLOCKS
    b_in = lambda m: _block_diag(m.reshape(SSM_BLOCKS, gb, c, p)).astype(BF16)
    c_out = lambda m: _block_diag(
        jnp.transpose(m, (0, 2, 1)).reshape(SSM_BLOCKS, gb, p, c)).astype(BF16)
    two = lambda m, am: jnp.concatenate([b_in(m), b_in(am)], axis=1)
    return pr, pi, b_in(bbr), b_in(bbi), two(bbr, abr), two(bbi, abi), c_out(c_re), c_out(c_im)


def _s5_finish(y, u, dskip_ref, wglu_ref, bglu_ref):
    y = y + dskip_ref[...] * u
    y = jax.nn.gelu(y)
    return y * _sigmoid(_dot(y.astype(BF16), wglu_ref[...]) + bglu_ref[...])


def _s5_seq_kernel(u_ref, pr_ref, pi_ref, lvr_ref, lvi_ref, bre_ref, bim_ref, cre_ref, cim_ref,
                   dskip_ref, wglu_ref, bglu_ref, y_ref, hr_ref, hi_ref,
                   xr_scr, xi_scr, y_scr, *, tc):
    @pl.when(pl.program_id(1) == 0)
    def _():
        hr_ref[...] = jnp.zeros_like(hr_ref)
        hi_ref[...] = jnp.zeros_like(hi_ref)

    u = u_ref[0]
    ub = u.astype(BF16)
    first = lax.broadcasted_iota(jnp.int32, (tc, 1), 0) % SUBLANES == 0
    ub_prev = jnp.where(first, 0.0, pltpu.roll(u, 1, 0)).astype(BF16)
    nb = SSM_BLOCK_STATES
    last = SUBLANES - 1
    bc = lambda h: jnp.broadcast_to(h[last:last + 1, :], h.shape)
    block = lambda k: slice(k * nb, (k + 1) * nb)

    def project_in(k):
        cols = slice(k * MXU_DIM, (k + 1) * MXU_DIM)
        uk = jnp.concatenate([ub[:, cols], ub_prev[:, cols]], axis=1)
        xr_scr[k] = _dot(uk, bre_ref[k])
        xi_scr[k] = _dot(uk, bim_ref[k])

    def scan(k):
        sl = block(k)
        pr = pr_ref[:, sl]
        pi = pi_ref[:, sl]
        cr, ci = bc(hr_ref[0, :, sl]), bc(hi_ref[0, :, sl])
        for g in range(tc // SUBLANES):
            rows = slice(g * SUBLANES, (g + 1) * SUBLANES)
            xr = xr_scr[k, rows, :]
            xi = xi_scr[k, rows, :]
            for lv in range(1, 3):
                sr = pltpu.roll(xr, 1 << lv, 0)
                si = pltpu.roll(xi, 1 << lv, 0)
                ar = lvr_ref[lv, :, sl]
                ai = lvi_ref[lv, :, sl]
                xr, xi = xr + ar * sr - ai * si, xi + ar * si + ai * sr
            hr = xr + pr * cr - pi * ci
            hi = xi + pr * ci + pi * cr
            xr_scr[k, rows, :] = hr
            xi_scr[k, rows, :] = hi
            cr, ci = bc(hr), bc(hi)
        hr_ref[0, :, sl] = cr
        hi_ref[0, :, sl] = ci

    def project_out(k):
        y_scr[:, k * MXU_DIM:(k + 1) * MXU_DIM] = (
            _dot(xr_scr[k].astype(BF16), cre_ref[k]) - _dot(xi_scr[k].astype(BF16), cim_ref[k]))

    project_in(0)
    for k in range(SSM_BLOCKS):
        if k + 1 < SSM_BLOCKS:
            project_in(k + 1)
        scan(k)
        project_out(k)

    y_ref[0] = _s5_finish(y_scr[...], u, dskip_ref, wglu_ref, bglu_ref)


def s5_mix_seq(u, prep, d_skip, w_glu, b_glu, tc_cap=256):
    b, t, w = u.shape
    tc = _row_tile(t, tc_cap)
    pr, pi, _, _, bre, bim, cre, cim = prep
    rows = jnp.arange(SUBLANES)[:, None]
    lvr = jnp.stack([jnp.where(rows >= (1 << l), pr[(1 << l) - 1][None], 0.0) for l in range(3)])
    lvi = jnp.stack([jnp.where(rows >= (1 << l), pi[(1 << l) - 1][None], 0.0) for l in range(3)])
    full = lambda a: pl.BlockSpec(a.shape, lambda i, j: (0,) * a.ndim)
    consts = [pr, pi, lvr, lvi, bre, bim, cre, cim, d_skip.reshape(1, w), w_glu, b_glu.reshape(1, w)]
    y, hr, hi = pl.pallas_call(
        functools.partial(_s5_seq_kernel, tc=tc),
        grid=(b, t // tc),
        in_specs=[pl.BlockSpec((1, tc, w), lambda i, j: (i, j, 0))] + [full(a) for a in consts],
        out_specs=[pl.BlockSpec((1, tc, w), lambda i, j: (i, j, 0)),
                   pl.BlockSpec((1, SUBLANES, SSM_WIDTH), lambda i, j: (i, 0, 0)),
                   pl.BlockSpec((1, SUBLANES, SSM_WIDTH), lambda i, j: (i, 0, 0))],
        out_shape=[jax.ShapeDtypeStruct((b, t, w), F32),
                   jax.ShapeDtypeStruct((b, SUBLANES, SSM_WIDTH), F32),
                   jax.ShapeDtypeStruct((b, SUBLANES, SSM_WIDTH), F32)],
        scratch_shapes=[pltpu.VMEM((SSM_BLOCKS, tc, SSM_BLOCK_STATES), F32),
                        pltpu.VMEM((SSM_BLOCKS, tc, SSM_BLOCK_STATES), F32),
                        pltpu.VMEM((tc, w), F32)],
        compiler_params=_params(("parallel", "arbitrary")),
        name="s5_mix_seq",
    )(u, *consts)
    return y, hr[:, SUBLANES - 1], hi[:, SUBLANES - 1]


def _s5_step_kernel(u_ref, h0r_ref, h0i_ref, ar_ref, ai_ref, bre_ref, bim_ref, cre_ref, cim_ref,
                    dskip_ref, wglu_ref, bglu_ref, y_ref, hr_ref, hi_ref, y_scr):
    u = u_ref[...]
    ub = u.astype(BF16)
    nb = SSM_BLOCK_STATES
    for k in range(SSM_BLOCKS):
        sl = slice(k * nb, (k + 1) * nb)
        uk = ub[:, k * MXU_DIM:(k + 1) * MXU_DIM]
        ar, ai = ar_ref[:, sl], ai_ref[:, sl]
        h0r, h0i = h0r_ref[:, sl], h0i_ref[:, sl]
        hr = _dot(uk, bre_ref[k]) + ar * h0r - ai * h0i
        hi = _dot(uk, bim_ref[k]) + ar * h0i + ai * h0r
        hr_ref[:, sl] = hr
        hi_ref[:, sl] = hi
        y_scr[:, k * MXU_DIM:(k + 1) * MXU_DIM] = (
            _dot(hr.astype(BF16), cre_ref[k]) - _dot(hi.astype(BF16), cim_ref[k]))
    y_ref[...] = _s5_finish(y_scr[...], u, dskip_ref, wglu_ref, bglu_ref)


def s5_mix_step(u, h0r, h0i, prep, d_skip, w_glu, b_glu):
    s, w = u.shape
    pr, pi, bre, bim, _, _, cre, cim = prep
    st = jax.ShapeDtypeStruct((s, SSM_WIDTH), F32)
    return pl.pallas_call(
        _s5_step_kernel,
        out_shape=[jax.ShapeDtypeStruct((s, w), F32), st, st],
        scratch_shapes=[pltpu.VMEM((s, w), F32)],
        name="s5_mix_step",
    )(u, h0r, h0i, pr[0:1], pi[0:1], bre, bim, cre, cim, d_skip.reshape(1, w), w_glu,
      b_glu.reshape(1, w))


FOX_TQ = 512
FOX_STRIP = MXU_DIM


def _fox_seq_kernel(q_ref, kaug_ref, vt_ref, tri_ref, o_ref, s_scr, p_scr, acc_scr, m_scr, l_scr,
                    *, tq, tk):
    i = pl.program_id(2)
    qt = jnp.transpose(q_ref[0] * (ATTN_SCALE * LOG2E))
    row = lax.broadcasted_iota(jnp.int32, (LANES, 1), 0)
    qa = []
    for h in range(2):
        own = (row >= h * HEAD_DIM) & (row < (h + 1) * HEAD_DIM)
        ones = (row >= h * BIAS_PIECES) & (row < (h + 1) * BIAS_PIECES)
        qa.append(jnp.concatenate(
            [jnp.where(own, qt, 0.0), jnp.broadcast_to(jnp.where(ones, 1.0, 0.0), qt.shape)],
            axis=0).astype(BF16))
    acc_scr[...] = jnp.zeros_like(acc_scr)
    m_scr[...] = jnp.full_like(m_scr, NEG_INF)
    l_scr[...] = jnp.zeros_like(l_scr)
    n_sub = tq // FOX_STRIP
    strips = [(h, qs) for h in range(2) for qs in range(n_sub)]
    lanes = lambda qs: slice(qs * FOX_STRIP, (qs + 1) * FOX_STRIP)
    ones_rows = jnp.ones((2 * SUBLANES, tk), BF16)

    def scores(j, slot):
        ka = kaug_ref[0, pl.ds(pl.multiple_of(j * tk, tk), tk), :]
        for n, (h, qs) in enumerate(strips):
            s_scr[slot, n] = _dot(ka, qa[h][:, lanes(qs)])

    def absorb(j, slot, diag):
        alphas, n_keys = [], []
        for n, (h, qs) in enumerate(strips):
            m = m_scr[n]
            if diag:
                full_rows = slice(0, qs * FOX_STRIP)
                tri_rows = slice(qs * FOX_STRIP, (qs + 1) * FOX_STRIP)
                tri = s_scr[slot, n, tri_rows, :] + tri_ref[...]
                m_new = jnp.maximum(m, jnp.max(tri, axis=0, keepdims=True))
                if qs:
                    m_new = jnp.maximum(
                        m_new, jnp.max(s_scr[slot, n, full_rows, :], axis=0, keepdims=True))
                    p_scr[n, full_rows, :] = jnp.exp2(
                        s_scr[slot, n, full_rows, :] - m_new).astype(BF16)
                p_scr[n, tri_rows, :] = jnp.exp2(tri - m_new).astype(BF16)
                n_keys.append((qs + 1) * FOX_STRIP)
            else:
                m_new = jnp.maximum(m, jnp.max(s_scr[slot, n], axis=0, keepdims=True))
                p_scr[n] = jnp.exp2(s_scr[slot, n] - m_new).astype(BF16)
                n_keys.append(tk)
            m_scr[n] = m_new
            alphas.append(jnp.exp2(m - m_new))
        for n, (h, qs) in enumerate(strips):
            vt = jnp.concatenate(
                [vt_ref[0, j, h * HEAD_DIM:(h + 1) * HEAD_DIM, :n_keys[n]],
                 ones_rows[:, :n_keys[n]]], axis=0)
            pv = _dot(vt, p_scr[n, :n_keys[n], :])
            acc_scr[h, :, lanes(qs)] = alphas[n] * acc_scr[h, :, lanes(qs)] + pv[:HEAD_DIM]
            l_scr[n] = alphas[n] * l_scr[n] + pv[HEAD_DIM:HEAD_DIM + 1]

    n_full = i
    scores(0, 0)

    def pair(jj, _):
        j = 2 * jj
        scores(j + 1, 1)
        absorb(j, 0, diag=False)
        scores(j + 2, 0)
        absorb(j + 1, 1, diag=False)
        return 0

    lax.fori_loop(0, n_full // 2, pair, 0)
    last_even = (n_full // 2) * 2

    @pl.when(n_full % 2 == 1)
    def _():
        scores(last_even + 1, 1)
        absorb(last_even, 0, diag=False)
        absorb(last_even + 1, 1, diag=True)

    @pl.when(n_full % 2 == 0)
    def _():
        absorb(last_even, 0, diag=True)

    l_head = lambda h: jnp.concatenate(
        [l_scr[h * n_sub + qs] for qs in range(n_sub)], axis=1)
    ot = jnp.concatenate([acc_scr[0] / l_head(0), acc_scr[1] / l_head(1)], axis=0)
    o_ref[0] = jnp.transpose(ot)


def fox_attention_seq(q, kaug, vtb):
    b, t, w = q.shape
    n_chunks, tk = vtb.shape[1], vtb.shape[3]
    tq = _row_tile(t, FOX_TQ)
    assert tk == tq and n_chunks * tk == t and tq % FOX_STRIP == 0
    n_strips = 2 * tq // FOX_STRIP
    idx = jnp.arange(FOX_STRIP)
    tri = jnp.where(idx[:, None] <= idx[None, :], 0.0, NEG_INF).astype(F32)
    return pl.pallas_call(
        functools.partial(_fox_seq_kernel, tq=tq, tk=tk),
        grid=(b, HEAD_PAIRS, t // tq),
        in_specs=[pl.BlockSpec((1, tq, LANES), lambda bi, p, i: (bi, i, p)),
                  pl.BlockSpec((1, t, KAUG_WIDTH), lambda bi, p, i: (bi, 0, p)),
                  pl.BlockSpec((1, n_chunks, LANES, tk), lambda bi, p, i: (bi, 0, p, 0)),
                  pl.BlockSpec((FOX_STRIP, FOX_STRIP), lambda bi, p, i: (0, 0))],
        out_specs=pl.BlockSpec((1, tq, LANES), lambda bi, p, i: (bi, i, p)),
        out_shape=jax.ShapeDtypeStruct((b, t, w), F32),
        scratch_shapes=[pltpu.VMEM((2, n_strips, tk, FOX_STRIP), F32),
                        pltpu.VMEM((n_strips, tk, FOX_STRIP), BF16),
                        pltpu.VMEM((2, HEAD_DIM, tq), F32),
                        pltpu.VMEM((n_strips, 1, FOX_STRIP), F32),
                        pltpu.VMEM((n_strips, 1, FOX_STRIP), F32)],
        compiler_params=_params(("parallel", "parallel", "arbitrary")),
        name="fox_attention_seq",
    )(q, kaug, vtb, tri)


DEC_PAGES = 16


def _fox_dec_kernel(pt_ref, q_ref, kn_ref, vn_ref, lfn_ref, *rest, page, n_pages):
    kts = rest[:n_pages]
    vts = rest[n_pages:2 * n_pages]
    lfs = rest[2 * n_pages:3 * n_pages]
    tri_ref = rest[3 * n_pages]
    o_ref = rest[3 * n_pages + 1]
    m_scr, l_scr, csum_scr, acc_scr, qcol_scr = rest[3 * n_pages + 2:]
    g = pl.program_id(1)
    hp = 2 * SUBLANES
    qrow = q_ref[0] * ATTN_SCALE

    @pl.when(g == 0)
    def _():
        m_scr[...] = jnp.full_like(m_scr, NEG_INF)
        l_scr[...] = jnp.zeros_like(l_scr)
        csum_scr[...] = jnp.zeros_like(csum_scr)
        acc_scr[...] = jnp.zeros_like(acc_scr)
        qcol_scr[...] = jnp.transpose(jnp.broadcast_to(qrow, (page, SEQ_WIDTH)))

    tri = tri_ref[...]
    base = csum_scr[...]
    head_row = lax.broadcasted_iota(jnp.int32, (hp, page), 0)
    s_parts = []
    for i in range(n_pages):
        hi, mid, lo = _split3(lfs[i][0])
        cum = (_dot(hi, tri) + _dot(mid, tri)) + _dot(lo, tri) + base
        base = jnp.broadcast_to(cum[:, page - 1:page], cum.shape)
        qk = jnp.zeros((hp, page), F32)
        for h in range(FOX_HEADS):
            rows = slice(h * HEAD_DIM, (h + 1) * HEAD_DIM)
            r = jnp.sum(kts[i][0, rows, :] * qcol_scr[rows, :], axis=0, keepdims=True)
            qk = jnp.where(head_row == h, r, qk)
        s_parts.append(qk - cum)
    csum_scr[...] = base
    s = jnp.concatenate(s_parts, axis=1)

    m_old = m_scr[...]
    m_new = jnp.maximum(m_old, jnp.max(s, axis=-1, keepdims=True))
    alpha = jnp.exp(m_old - m_new)
    p = jnp.exp(s - m_new)
    l_scr[...] = alpha * l_scr[...] + jnp.sum(p, axis=-1, keepdims=True)
    m_scr[...] = m_new
    for h in range(FOX_HEADS):
        rows = slice(h * HEAD_DIM, (h + 1) * HEAD_DIM)
        acc = acc_scr[rows, :] * alpha[h:h + 1, :]
        for i in range(n_pages):
            acc = acc + vts[i][0, rows, :] * p[h:h + 1, i * page:(i + 1) * page]
        acc_scr[rows, :] = acc

    @pl.when(g == pl.num_programs(1) - 1)
    def _():
        lane = lax.broadcasted_iota(jnp.int32, (hp, SEQ_WIDTH), 1)
        head = lax.broadcasted_iota(jnp.int32, (hp, SEQ_WIDTH), 0)
        own = (lane >= head * HEAD_DIM) & (lane < (head + 1) * HEAD_DIM)
        spread = lambda col: jnp.sum(jnp.where(own, col, 0.0), axis=0, keepdims=True)
        c_new = csum_scr[:, 0:1] + lfn_ref[0]
        s_new = jnp.sum(jnp.where(own, qrow * kn_ref[0], 0.0), axis=-1, keepdims=True) - c_new
        m_fin = jnp.maximum(m_scr[...], s_new)
        a_fin = jnp.exp(m_scr[...] - m_fin)
        p_new = jnp.exp(s_new - m_fin)
        l_fin = a_fin * l_scr[...] + p_new
        acc_row = jnp.sum(jnp.transpose(acc_scr[...]), axis=0, keepdims=True)
        o_ref[0] = (spread(a_fin) * acc_row + spread(p_new) * vn_ref[0]) / spread(l_fin)


def fox_attention_decode(q, k_new, v_new, logf_new, cache_k, cache_v, cache_logf, page_table):
    s, w = q.shape
    n_pool, page = cache_k.shape[:2]
    assert page == LANES
    pages_per_seq = page_table.shape[1]
    n_pages = min(DEC_PAGES, pages_per_seq)
    assert pages_per_seq % n_pages == 0
    hp = 2 * SUBLANES
    ckt = jnp.transpose(cache_k, (0, 2, 3, 1)).reshape(n_pool, w, page)
    cvt = jnp.transpose(cache_v, (0, 2, 3, 1)).reshape(n_pool, w, page)
    clf = jnp.pad(jnp.transpose(cache_logf, (0, 2, 1)), ((0, 0), (0, hp - FOX_HEADS), (0, 0)))
    lfn = jnp.pad(logf_new, ((0, 0), (0, hp - FOX_HEADS))).reshape(s, hp, 1)
    tri = jnp.triu(jnp.ones((page, page), F32)).astype(BF16)
    row = pl.BlockSpec((1, 1, w), lambda b, g, pt: (b, 0, 0))

    def paged(shape, i):
        return pl.BlockSpec((1,) + shape, lambda b, g, pt, i=i: (pt[b, g * n_pages + i], 0, 0))

    grid_spec = pltpu.PrefetchScalarGridSpec(
        num_scalar_prefetch=1,
        grid=(s, pages_per_seq // n_pages),
        in_specs=([row, row, row, pl.BlockSpec((1, hp, 1), lambda b, g, pt: (b, 0, 0))]
                  + [paged((w, page), i) for i in range(n_pages)]
                  + [paged((w, page), i) for i in range(n_pages)]
                  + [paged((hp, page), i) for i in range(n_pages)]
                  + [pl.BlockSpec((page, page), lambda b, g, pt: (0, 0))]),
        out_specs=row,
        scratch_shapes=[pltpu.VMEM((hp, 1), F32), pltpu.VMEM((hp, 1), F32),
                        pltpu.VMEM((hp, page), F32), pltpu.VMEM((w, page), F32),
                        pltpu.VMEM((w, page), F32)],
    )
    out = pl.pallas_call(
        functools.partial(_fox_dec_kernel, page=page, n_pages=n_pages),
        grid_spec=grid_spec,
        out_shape=jax.ShapeDtypeStruct((s, 1, w), F32),
        compiler_params=_params(("parallel", "arbitrary"), VMEM_LIMIT),
        name="fox_attention_decode",
    )(page_table, q.reshape(s, 1, w), k_new.reshape(s, 1, w), v_new.reshape(s, 1, w), lfn,
      *([ckt] * n_pages), *([cvt] * n_pages), *([clf] * n_pages), tri)
    return out.reshape(s, w)


def _trunk(x, mem_k, mem_v, conv_prev, ssm_state, fox_attend, p, s5_prep, sequential):
    assert N_A == 1
    b, t, d = x.shape
    m = b * t
    new_conv, ssm_out, kv = [], None, None
    for l in range(DEPTH):
        if l == N_A:
            if sequential:
                kv = shared_kv_proj_seq(x, p["kv_norm"], p["w_kv"], p["b_f"])
            else:
                kv = shared_kv_proj_step(x.reshape(m, d), p["kv_norm"], p["w_kv"], p["b_f"])
        z_seq, q_mem = norm_linear(x.reshape(m, d), p["norm_mix_pre"][l], p["w_in"][l],
                                   (SEQ_WIDTH, MEM_WIDTH))
        if l < N_A:
            if sequential:
                seq_out, hr, hi = s5_mix_seq(z_seq.reshape(b, t, SEQ_WIDTH), s5_prep[l],
                                             p["d_skip"][l], p["w_glu"][l], p["b_glu"][l])
            else:
                seq_out, hr, hi = s5_mix_step(z_seq, ssm_state[0][l], ssm_state[1][l], s5_prep[l],
                                              p["d_skip"][l], p["w_glu"][l], p["b_glu"][l])
            ssm_out = (hr.reshape(b, SSM_GROUPS, SSM_STATE), hi.reshape(b, SSM_GROUPS, SSM_STATE))
        else:
            seq_out = fox_attend(z_seq, kv)
        ffn_args = (p["norm_ffn_pre"][l], p["norm_ffn_post"][l], p["w_up"][l], p["conv_w"][l],
                    p["conv_b"][l], p["w_down"][l])
        if sequential:
            x3, cp = layer_tail_seq(x, seq_out.reshape(b, t, SEQ_WIDTH),
                                    q_mem.reshape(b, t, MEM_WIDTH), mem_k[l], mem_v[l],
                                    p["w_out"][l], p["norm_mix_post"][l], conv_prev[l], *ffn_args)
        else:
            mem_out = mem_attention(q_mem.reshape(b, t, MEM_WIDTH), mem_k[l], mem_v[l])
            x2 = mix_out(x.reshape(m, d), seq_out, mem_out.reshape(m, MEM_WIDTH),
                         p["w_out"][l], p["norm_mix_post"][l])
            x3, u_new = conv_ffn_step(x2, conv_prev[l], *ffn_args)
            cp = jnp.stack([conv_prev[l][:, 1], u_new], axis=1)
        new_conv.append(cp)
        x = x3.reshape(b, t, d)
    return x, ssm_out, jnp.stack(new_conv), kv


def kernel(x_prompt, x_sample, state_ssm_re, state_ssm_im, cache_k, cache_v, cache_logf,
           cache_mem_k, cache_mem_v, state_ffn_conv, page_table, mem_prompt,
           w_in, w_out, norm_mix_pre, norm_mix_post, norm_ffn_pre, norm_ffn_post,
           mem_norm, w_mem_kv, lam_re, lam_im, log_dt, b_re, b_im, c_re, c_im, d_skip,
           w_glu, b_glu, kv_norm, w_kv, b_f, w_up, conv_w, conv_b, w_down):
    p = dict(w_in=w_in.astype(BF16), w_out=w_out.astype(BF16), norm_mix_pre=norm_mix_pre,
             norm_mix_post=norm_mix_post, norm_ffn_pre=norm_ffn_pre, norm_ffn_post=norm_ffn_post,
             d_skip=d_skip, w_glu=w_glu.astype(BF16), b_glu=b_glu, kv_norm=kv_norm, w_kv=w_kv,
             b_f=b_f, w_up=w_up.astype(BF16), conv_w=conv_w, conv_b=conv_b,
             w_down=w_down.astype(BF16))
    s5_prep = [s5_prepare(lam_re[l], lam_im[l], log_dt[l], b_re[l], b_im[l], c_re[l], c_im[l])
               for l in range(N_A)]

    bp, tp, d = x_prompt.shape
    n_mem = mem_prompt.shape[1]
    mem_pairs = [mem_kv_proj(mem_prompt, mem_norm[l], w_mem_kv[l]) for l in range(DEPTH)]
    p_mem_kt = [mkt for mkt, _ in mem_pairs]
    p_mem_vt = [mvt for _, mvt in mem_pairs]
    zeros_conv = jnp.zeros((DEPTH, bp, CONV_W - 1, 2 * D_FF), F32)

    def fox_prompt(q, kv):
        kt, vt, kaug, vtb, logf = kv
        return fox_attention_seq(q.reshape(bp, tp, SEQ_WIDTH), kaug, vtb)

    y_prompt, p_ssm, p_conv, p_kv = _trunk(x_prompt, p_mem_kt, p_mem_vt, zeros_conv, None,
                                           fox_prompt, p, s5_prep, sequential=True)
    untr = lambda a, n: jnp.transpose(a.reshape(a.shape[0], n, HEAD_DIM, a.shape[2]), (0, 3, 1, 2))
    mem5 = lambda ms: jnp.stack([untr(a, MEM_HEADS) for a in ms])

    bs = x_sample.shape[0]
    tr_mem = lambda a: jnp.transpose(a, (0, 2, 3, 1)).reshape(bs, MEM_WIDTH, n_mem)
    s_mem_kt = [tr_mem(cache_mem_k[l]) for l in range(DEPTH)]
    s_mem_vt = [tr_mem(cache_mem_v[l]) for l in range(DEPTH)]
    ssm0 = (state_ssm_re.reshape(N_A, bs, SSM_WIDTH), state_ssm_im.reshape(N_A, bs, SSM_WIDTH))

    def fox_sample(q, kv):
        k, v, logf = kv
        return fox_attention_decode(q, k, v, logf, cache_k, cache_v, cache_logf, page_table)

    y_sample, s_ssm, s_conv, s_kv = _trunk(x_sample, s_mem_kt, s_mem_vt, state_ffn_conv, ssm0,
                                           fox_sample, p, s5_prep, sequential=False)
    head4 = lambda a: a.reshape(bs, 1, FOX_HEADS, HEAD_DIM)

    return (y_prompt, y_sample, p_ssm[0][None], p_ssm[1][None],
            untr(p_kv[0], FOX_HEADS), untr(p_kv[1], FOX_HEADS), p_kv[4],
            mem5(p_mem_kt), mem5(p_mem_vt), p_conv,
            s_ssm[0][None], s_ssm[1][None],
            head4(s_kv[0]), head4(s_kv[1]), s_kv[2].reshape(bs, 1, FOX_HEADS), s_conv)
```

```python
import functools
import math

import jax
import jax.numpy as jnp
from jax import lax
from jax.experimental import pallas as pl
from jax.experimental.pallas import tpu as pltpu

F32 = jnp.float32
BF16 = jnp.bfloat16

D_MODEL = 1024
DEPTH = 2
N_A = DEPTH // 2
HEAD_DIM = 64
MEM_HEADS = 4
MEM_WIDTH = MEM_HEADS * HEAD_DIM
SEQ_WIDTH = D_MODEL - MEM_WIDTH
SSM_GROUP = 16
SSM_GROUPS = SEQ_WIDTH // SSM_GROUP
SSM_STATE = 64
SSM_WIDTH = SSM_GROUPS * SSM_STATE
FOX_HEADS = SEQ_WIDTH // HEAD_DIM
D_FF = (11 * D_MODEL) // 4
CONV_W = 3
EPS = 1e-6
NEG_INF = -1e30
EIG_CLIP = -1e-4
ATTN_SCALE = HEAD_DIM ** -0.5
LOG2E = math.log2(math.e)

LANES = 128
SUBLANES = 8
MXU_DIM = 256
VMEM_BYTES_V7X = 64 * 1024 * 1024
VMEM_LIMIT = (VMEM_BYTES_V7X * 7) // 8

SSM_BLOCKS = SEQ_WIDTH // MXU_DIM
SSM_BLOCK_STATES = SSM_WIDTH // SSM_BLOCKS
HEAD_PAIRS = FOX_HEADS // 2


def _params(semantics, vmem=None):
    return pltpu.CompilerParams(dimension_semantics=semantics, vmem_limit_bytes=vmem)


def _row_tile(m, cap):
    t = min(m, cap)
    assert m % t == 0, (m, t)
    return t


def _rms(x, g):
    return x * lax.rsqrt(jnp.mean(x * x, axis=-1, keepdims=True) + EPS) * g


def _sigmoid(x):
    return 1.0 / (1.0 + jnp.exp(-x))


def _log_sigmoid(x):
    return -(jnp.maximum(-x, 0.0) + jnp.log1p(jnp.exp(-jnp.abs(x))))


def _split3(x):
    hi = x.astype(BF16)
    r1 = x - hi.astype(F32)
    mid = r1.astype(BF16)
    lo = (r1 - mid.astype(F32)).astype(BF16)
    return hi, mid, lo


def _dot(a, b):
    return jnp.dot(a, b, preferred_element_type=F32)


def _dot_nt(a, b):
    return lax.dot_general(a, b, (((1,), (1,)), ((), ())), preferred_element_type=F32)


def _norm_linear_kernel(x_ref, g_ref, w_ref, *out_refs, splits):
    h = _rms(x_ref[...], g_ref[...]).astype(BF16)
    c0 = 0
    for o_ref, n in zip(out_refs, splits):
        o_ref[...] = _dot(h, w_ref[:, c0:c0 + n])
        c0 += n


def norm_linear(x, g, w, splits, tm_cap=512):
    m, d = x.shape
    tm = _row_tile(m, tm_cap)
    n_tot = sum(splits)
    assert w.shape == (d, n_tot)
    return pl.pallas_call(
        functools.partial(_norm_linear_kernel, splits=tuple(splits)),
        grid=(m // tm,),
        in_specs=[pl.BlockSpec((tm, d), lambda i: (i, 0)),
                  pl.BlockSpec((1, d), lambda i: (0, 0)),
                  pl.BlockSpec((d, n_tot), lambda i: (0, 0))],
        out_specs=[pl.BlockSpec((tm, n), lambda i: (i, 0)) for n in splits],
        out_shape=[jax.ShapeDtypeStruct((m, n), F32) for n in splits],
        compiler_params=_params(("parallel",)),
        name="norm_linear",
    )(x, g.reshape(1, d), w)


KV_TILE = 512
KAUG_WIDTH = 2 * LANES
BIAS_PIECES = 3


def _kv_weights(w_kv, b_f):
    wk = w_kv[:, :SEQ_WIDTH].astype(BF16)
    wf = jnp.pad(w_kv[:, 2 * SEQ_WIDTH:], ((0, 0), (0, LANES - FOX_HEADS))).astype(BF16)
    bf = jnp.pad(b_f, (0, LANES - FOX_HEADS)).reshape(1, LANES)
    return wk, wf, bf


def _kv_seq_kernel(x_ref, g_ref, wk_ref, wvt_ref, wf_ref, bf_ref, tri_ref, place_ref,
                   kt_ref, vt_ref, kaug_ref, vtb_ref, lf_ref, carry_ref):
    @pl.when(pl.program_id(1) == 0)
    def _():
        carry_ref[...] = jnp.zeros_like(carry_ref)

    h = _rms(x_ref[0], g_ref[...]).astype(BF16)
    k = _dot(h, wk_ref[...])
    kt_ref[0] = jnp.transpose(k)
    vt = _dot_nt(wvt_ref[...], h)
    vt_ref[0] = vt
    vtb_ref[0, 0] = vt.astype(BF16)
    logf = _log_sigmoid(_dot(h, wf_ref[...]) + bf_ref[...])
    lf_ref[0] = logf[:, :FOX_HEADS]
    c3 = _dot(tri_ref[...], jnp.concatenate(_split3(logf), axis=1))
    cum = (c3[:, :LANES] + c3[:, LANES:2 * LANES]) + c3[:, 2 * LANES:] + carry_ref[...]
    carry_ref[...] = cum[cum.shape[0] - 1:, :]
    pieces = jnp.concatenate(_split3(cum * -LOG2E), axis=1)
    bias = _dot(pieces, place_ref[...]).astype(BF16)
    kb = k.astype(BF16)
    for p in range(HEAD_PAIRS):
        kaug_ref[0, :, p * KAUG_WIDTH:p * KAUG_WIDTH + LANES] = kb[:, p * LANES:(p + 1) * LANES]
        kaug_ref[0, :, p * KAUG_WIDTH + LANES:(p + 1) * KAUG_WIDTH] = bias


def shared_kv_proj_seq(x, g, w_kv, b_f):
    b, t, d = x.shape
    tm = _row_tile(t, KV_TILE)
    wk, wf, bf = _kv_weights(w_kv, b_f)
    wvt = jnp.transpose(w_kv)[SEQ_WIDTH:2 * SEQ_WIDTH].astype(BF16)
    tri = jnp.tril(jnp.ones((tm, tm), F32)).astype(BF16)
    hh = jnp.arange(FOX_HEADS)
    place = jnp.zeros((BIAS_PIECES * LANES, LANES), F32)
    for j in range(BIAS_PIECES):
        place = place.at[LANES * j + hh, BIAS_PIECES * hh + j].set(1.0)
    place = place.astype(BF16)
    tok = lambda n: pl.BlockSpec((1, tm, n), lambda i, j: (i, j, 0))
    tr = pl.BlockSpec((1, SEQ_WIDTH, tm), lambda i, j: (i, 0, j))
    full = lambda s: pl.BlockSpec(s, lambda i, j: (0,) * len(s))
    return pl.pallas_call(
        _kv_seq_kernel,
        grid=(b, t // tm),
        in_specs=[tok(d), full((1, d)), full((d, SEQ_WIDTH)), full((SEQ_WIDTH, d)),
                  full((d, LANES)), full((1, LANES)), full((tm, tm)),
                  full((BIAS_PIECES * LANES, LANES))],
        out_specs=[tr, tr, tok(HEAD_PAIRS * KAUG_WIDTH),
                   pl.BlockSpec((1, 1, SEQ_WIDTH, tm), lambda i, j: (i, j, 0, 0)),
                   tok(FOX_HEADS)],
        out_shape=[jax.ShapeDtypeStruct((b, SEQ_WIDTH, t), F32),
                   jax.ShapeDtypeStruct((b, SEQ_WIDTH, t), F32),
                   jax.ShapeDtypeStruct((b, t, HEAD_PAIRS * KAUG_WIDTH), BF16),
                   jax.ShapeDtypeStruct((b, t // tm, SEQ_WIDTH, tm), BF16),
                   jax.ShapeDtypeStruct((b, t, FOX_HEADS), F32)],
        scratch_shapes=[pltpu.VMEM((1, LANES), F32)],
        compiler_params=_params(("parallel", "arbitrary")),
        name="shared_kv_proj_seq",
    )(x, g.reshape(1, d), wk, wvt, wf, bf, tri, place)


def _kv_step_kernel(x_ref, g_ref, wk_ref, wv_ref, wf_ref, bf_ref, k_ref, v_ref, lf_ref):
    h = _rms(x_ref[...], g_ref[...]).astype(BF16)
    k_ref[...] = _dot(h, wk_ref[...])
    v_ref[...] = _dot(h, wv_ref[...])
    lf_ref[...] = _log_sigmoid(_dot(h, wf_ref[...]) + bf_ref[...])[:, :FOX_HEADS]


def shared_kv_proj_step(x, g, w_kv, b_f):
    s, d = x.shape
    wk, wf, bf = _kv_weights(w_kv, b_f)
    wv = w_kv[:, SEQ_WIDTH:2 * SEQ_WIDTH].astype(BF16)
    kv = jax.ShapeDtypeStruct((s, SEQ_WIDTH), F32)
    return pl.pallas_call(
        _kv_step_kernel,
        out_shape=[kv, kv, jax.ShapeDtypeStruct((s, FOX_HEADS), F32)],
        name="shared_kv_proj_step",
    )(x, g.reshape(1, d), wk, wv, wf, bf)


def _mem_kv_kernel(x_ref, g_ref, wt_ref, kt_ref, vt_ref):
    h = _rms(x_ref[0], g_ref[...]).astype(BF16)
    kvt = _dot_nt(wt_ref[...], h)
    kt_ref[0] = kvt[:MEM_WIDTH]
    vt_ref[0] = kvt[MEM_WIDTH:]


def mem_kv_proj(mem, g, w):
    b, n_mem, d = mem.shape
    out = jax.ShapeDtypeStruct((b, MEM_WIDTH, n_mem), F32)
    blk = pl.BlockSpec((1, MEM_WIDTH, n_mem), lambda i: (i, 0, 0))
    return pl.pallas_call(
        _mem_kv_kernel,
        grid=(b,),
        in_specs=[pl.BlockSpec((1, n_mem, d), lambda i: (i, 0, 0)),
                  pl.BlockSpec((1, d), lambda i: (0, 0)),
                  pl.BlockSpec((2 * MEM_WIDTH, d), lambda i: (0, 0))],
        out_specs=[blk, blk],
        out_shape=[out, out],
        compiler_params=_params(("parallel",)),
        name="mem_kv_proj",
    )(mem, g.reshape(1, d), jnp.transpose(w).astype(BF16))


def _mem_attn_tile(q, mkt, mvt):
    q = q * ATTN_SCALE
    mkt = mkt.astype(BF16)
    mvt = mvt.astype(BF16)
    lane = lax.broadcasted_iota(jnp.int32, (1, MEM_WIDTH), 1)
    out = jnp.zeros(q.shape, F32)
    for h in range(MEM_HEADS):
        in_head = (lane >= h * HEAD_DIM) & (lane < (h + 1) * HEAD_DIM)
        s = _dot(jnp.where(in_head, q, 0.0).astype(BF16), mkt)
        p = jnp.exp(s - jnp.max(s, axis=-1, keepdims=True))
        p = p / jnp.sum(p, axis=-1, keepdims=True)
        out = out + jnp.where(in_head, _dot_nt(p.astype(BF16), mvt), 0.0)
    return out


def _mem_attn_kernel(q_ref, mkt_ref, mvt_ref, o_ref):
    for s in range(q_ref.shape[0]):
        o_ref[s] = _mem_attn_tile(q_ref[s], mkt_ref[s], mvt_ref[s])


def mem_attention(q_mem, mkt, mvt, layer, seqs_per_step=SUBLANES):
    b, t, w = q_mem.shape
    n_mem = mkt.shape[3]
    bs = _row_tile(b, seqs_per_step)
    mem = pl.BlockSpec((None, bs, w, n_mem), lambda i: (layer, i, 0, 0))
    return pl.pallas_call(
        _mem_attn_kernel,
        grid=(b // bs,),
        in_specs=[pl.BlockSpec((bs, t, w), lambda i: (i, 0, 0)), mem, mem],
        out_specs=pl.BlockSpec((bs, t, w), lambda i: (i, 0, 0)),
        out_shape=jax.ShapeDtypeStruct((b, t, w), F32),
        compiler_params=_params(("parallel",)),
        name="mem_attention",
    )(q_mem, mkt, mvt)


def _mix_out_tile(x, seq, mem, w_ref, g):
    o = (_dot(seq.astype(BF16), w_ref[:SEQ_WIDTH, :]) + _dot(mem.astype(BF16), w_ref[SEQ_WIDTH:, :]))
    return x + _rms(o, g)


def _mix_out_kernel(x_ref, s_ref, m_ref, w_ref, g_ref, o_ref):
    o_ref[...] = _mix_out_tile(x_ref[...], s_ref[...], m_ref[...], w_ref, g_ref[...])


def mix_out(x, seq_out, mem_out, w_out, g, tm_cap=512):
    m, d = x.shape
    tm = _row_tile(m, tm_cap)
    row = lambda n: pl.BlockSpec((tm, n), lambda i: (i, 0))
    return pl.pallas_call(
        _mix_out_kernel,
        grid=(m // tm,),
        in_specs=[row(d), row(SEQ_WIDTH), row(MEM_WIDTH),
                  pl.BlockSpec((d, d), lambda i: (0, 0)),
                  pl.BlockSpec((1, d), lambda i: (0, 0))],
        out_specs=row(d),
        out_shape=jax.ShapeDtypeStruct((m, d), F32),
        compiler_params=_params(("parallel",)),
        name="mix_out",
    )(x, seq_out, mem_out, w_out, g.reshape(1, d))


FFN_CHUNK = MXU_DIM


def _tail_seq_kernel(x_ref, seq_ref, qm_ref, mkt_ref, mvt_ref, wout_ref, gmix_ref, prev_ref,
                     gpre_ref, gpost_ref, wup_ref, cw_ref, cb_ref, wdn_ref,
                     o_ref, conv_ref, h_scr, carry_scr, *, tm):
    @pl.when(pl.program_id(1) == 0)
    def _():
        carry_scr[...] = prev_ref[0]

    mem = _mem_attn_tile(qm_ref[0], mkt_ref[0], mvt_ref[0])
    x = _mix_out_tile(x_ref[0], seq_ref[0], mem, wout_ref, gmix_ref[...])
    xn = _rms(x, gpre_ref[...]).astype(BF16)
    row = lax.broadcasted_iota(jnp.int32, (SUBLANES, 1), 0)

    def conv(col):
        u = _dot(xn, wup_ref[:, col:col + FFN_CHUNK])
        c0 = carry_scr[0:1, col:col + FFN_CHUNK]
        c1 = carry_scr[1:2, col:col + FFN_CHUNK]
        u1 = pltpu.roll(u, 1, 0)
        u2 = pltpu.roll(u, 2, 0)
        u1 = jnp.concatenate([jnp.where(row == 0, c1, u1[:SUBLANES]), u1[SUBLANES:]], axis=0)
        u2 = jnp.concatenate(
            [jnp.where(row == 0, c0, jnp.where(row == 1, c1, u2[:SUBLANES])), u2[SUBLANES:]],
            axis=0)
        carry_scr[:, col:col + FFN_CHUNK] = u[tm - 2:, :]
        w = cw_ref[:, col:col + FFN_CHUNK]
        return u2 * w[0:1] + u1 * w[1:2] + u * w[2:3] + cb_ref[:, col:col + FFN_CHUNK]

    for c in range(D_FF // FFN_CHUNK):
        gate = conv(c * FFN_CHUNK)
        val = conv(D_FF + c * FFN_CHUNK)
        h_scr[:, c * FFN_CHUNK:(c + 1) * FFN_CHUNK] = (gate * _sigmoid(gate) * val).astype(BF16)

    f = _dot(h_scr[...], wdn_ref[...])
    o_ref[0] = x + _rms(f, gpost_ref[...])
    conv_ref[0] = carry_scr[...]


def layer_tail_seq(x, seq_out, q_mem, mkt, mvt, w_out, g_mix, prev, g_pre, g_post, w_up, conv_w,
                   conv_b, w_down, tm_cap=512):
    b, t, d = x.shape
    n_mem = mkt.shape[2]
    tm = _row_tile(t, tm_cap)
    assert tm >= 2 * SUBLANES
    f2 = 2 * D_FF
    full = lambda s: pl.BlockSpec(s, lambda i, j: (0,) * len(s), pipeline_mode=pl.Buffered(1))
    tok = lambda n: pl.BlockSpec((1, tm, n), lambda i, j: (i, j, 0))
    per_seq = lambda r, c: pl.BlockSpec((1, r, c), lambda i, j: (i, 0, 0))
    return pl.pallas_call(
        functools.partial(_tail_seq_kernel, tm=tm),
        grid=(b, t // tm),
        in_specs=[tok(d), tok(SEQ_WIDTH), tok(MEM_WIDTH),
                  per_seq(MEM_WIDTH, n_mem), per_seq(MEM_WIDTH, n_mem),
                  full((d, d)), full((1, d)), per_seq(CONV_W - 1, f2),
                  full((1, d)), full((1, d)), full((d, f2)), full((CONV_W, f2)), full((1, f2)),
                  full((D_FF, d))],
        out_specs=[tok(d), per_seq(CONV_W - 1, f2)],
        out_shape=[jax.ShapeDtypeStruct((b, t, d), F32),
                   jax.ShapeDtypeStruct((b, CONV_W - 1, f2), F32)],
        scratch_shapes=[pltpu.VMEM((tm, D_FF), BF16), pltpu.VMEM((CONV_W - 1, f2), F32)],
        compiler_params=_params(("parallel", "arbitrary"), VMEM_LIMIT),
        name="layer_tail_seq",
    )(x, seq_out, q_mem, mkt, mvt, w_out, g_mix.reshape(1, d), prev, g_pre.reshape(1, d),
      g_post.reshape(1, d), w_up, conv_w, conv_b.reshape(1, f2), w_down)


def _ffn_step_kernel(x_ref, p0g_ref, p0v_ref, p1g_ref, p1v_ref, gpre_ref, gpost_ref,
                     wg_ref, wv_ref, cwg_ref, cwv_ref, cbg_ref, cbv_ref, wdn_ref,
                     o_ref, ug_ref, uv_ref, acc_scr):
    c = pl.program_id(0)

    @pl.when(c == 0)
    def _():
        acc_scr[...] = jnp.zeros_like(acc_scr)

    x = x_ref[...]
    xn = _rms(x, gpre_ref[...]).astype(BF16)

    def conv(w_ref, p0_ref, p1_ref, cw_ref, cb_ref, u_ref):
        u = _dot(xn, w_ref[...])
        u_ref[...] = u
        w = cw_ref[...]
        return p0_ref[...] * w[0:1] + p1_ref[...] * w[1:2] + u * w[2:3] + cb_ref[...]

    gate = conv(wg_ref, p0g_ref, p1g_ref, cwg_ref, cbg_ref, ug_ref)
    val = conv(wv_ref, p0v_ref, p1v_ref, cwv_ref, cbv_ref, uv_ref)
    acc_scr[...] += _dot((gate * _sigmoid(gate) * val).astype(BF16), wdn_ref[...])

    @pl.when(c == pl.num_programs(0) - 1)
    def _():
        o_ref[...] = x + _rms(acc_scr[...], gpost_ref[...])


def conv_ffn_step(x, prev, g_pre, g_post, w_up, conv_w, conv_b, w_down):
    s, d = x.shape
    f2 = 2 * D_FF
    fc = FFN_CHUNK
    nc = D_FF // fc
    prev2 = prev.reshape(s, (CONV_W - 1) * f2)
    cb = conv_b.reshape(1, f2)
    const = lambda shp: pl.BlockSpec(shp, lambda c: (0, 0))
    col = lambda rows, off: pl.BlockSpec((rows, fc), lambda c, off=off: (0, c + off))
    out, ug, uv = pl.pallas_call(
        _ffn_step_kernel,
        grid=(nc,),
        in_specs=[const((s, d)),
                  col(s, 0), col(s, nc), col(s, 2 * nc), col(s, 3 * nc),
                  const((1, d)), const((1, d)),
                  col(d, 0), col(d, nc),
                  col(CONV_W, 0), col(CONV_W, nc),
                  col(1, 0), col(1, nc),
                  pl.BlockSpec((fc, d), lambda c: (c, 0))],
        out_specs=[const((s, d)), col(s, 0), col(s, 0)],
        out_shape=[jax.ShapeDtypeStruct((s, d), F32),
                   jax.ShapeDtypeStruct((s, D_FF), F32),
                   jax.ShapeDtypeStruct((s, D_FF), F32)],
        scratch_shapes=[pltpu.VMEM((s, d), F32)],
        compiler_params=_params(("arbitrary",)),
        name="conv_ffn_step",
    )(x, prev2, prev2, prev2, prev2, g_pre.reshape(1, d), g_post.reshape(1, d),
      w_up, w_up, conv_w, conv_w, cb, cb, w_down)
    return out, jnp.concatenate([ug, uv], axis=-1)


def _s5_prep_kernel(lr_ref, li_ref, ldt_ref, br_ref, bi_ref, pr_ref, pi_ref, bbr_ref, bbi_ref,
                    abr_ref, abi_ref):
    lr = jnp.minimum(lr_ref[...], EIG_CLIP)
    li = li_ref[...]
    dt = jnp.exp(ldt_ref[...])
    mag = jnp.exp(lr * dt)
    ar = mag * jnp.cos(li * dt)
    ai = mag * jnp.sin(li * dt)
    den = lr * lr + li * li
    nr = ar - 1.0
    fr = (nr * lr + ai * li) / den
    fi = (ai * lr - nr * li) / den
    br = br_ref[...]
    bi = bi_ref[...]
    bbr = fr * br - fi * bi
    bbi = fr * bi + fi * br
    bbr_ref[...] = bbr
    bbi_ref[...] = bbi
    abr_ref[...] = ar * bbr - ai * bbi
    abi_ref[...] = ar * bbi + ai * bbr
    pr, pi = ar, ai
    pr_ref[0] = pr
    pi_ref[0] = pi
    for n in range(1, SUBLANES):
        pr, pi = pr * ar - pi * ai, pr * ai + pi * ar
        pr_ref[n] = pr
        pi_ref[n] = pi


def _block_diag(m):
    nb, ng, r, c = m.shape
    eye = jnp.eye(ng, dtype=m.dtype)
    return jnp.einsum("kgrc,gh->kgrhc", m, eye).reshape(nb, ng * r, ng * c)


def s5_prepare(lam_re, lam_im, log_dt, b_re, b_im, c_re, c_im):
    g, p, c = b_re.shape
    rep = lambda a: jnp.repeat(a, c, axis=0)
    tr = lambda b: jnp.transpose(b, (0, 2, 1)).reshape(g * c, p)
    shp = jax.ShapeDtypeStruct((g * c, p), F32)
    pw = jax.ShapeDtypeStruct((SUBLANES, g * c, p), F32)
    pr, pi, bbr, bbi, abr, abi = pl.pallas_call(
        _s5_prep_kernel, out_shape=[pw, pw, shp, shp, shp, shp], name="s5_prepare",
    )(rep(lam_re), rep(lam_im), jnp.broadcast_to(rep(log_dt[:, None]), (g * c, p)),
      tr(b_re), tr(b_im))
    pr = pr[:, ::c, :].reshape(SUBLANES, g * p)
    pi = pi[:, ::c, :].reshape(SUBLANES, g * p)
    gb = g // SSM_BLOCKS
    b_in = lambda m: _block_diag(m.reshape(SSM_BLOCKS, gb, c, p)).astype(BF16)
    c_out = lambda m: _block_diag(
        jnp.transpose(m, (0, 2, 1)).reshape(SSM_BLOCKS, gb, p, c)).astype(BF16)
    two = lambda m, am: jnp.concatenate([b_in(m), b_in(am)], axis=1)
    return pr, pi, b_in(bbr), b_in(bbi), two(bbr, abr), two(bbi, abi), c_out(c_re), c_out(c_im)


def _s5_finish(y, u, dskip_ref, wglu_ref, bglu_ref):
    y = y + dskip_ref[...] * u
    y = jax.nn.gelu(y)
    return y * _sigmoid(_dot(y.astype(BF16), wglu_ref[...]) + bglu_ref[...])


def _s5_seq_kernel(u_ref, pr_ref, pi_ref, lvr_ref, lvi_ref, bre_ref, bim_ref, cre_ref, cim_ref,
                   dskip_ref, wglu_ref, bglu_ref, y_ref, hr_ref, hi_ref,
                   xr_scr, xi_scr, y_scr, *, tc):
    @pl.when(pl.program_id(1) == 0)
    def _():
        hr_ref[...] = jnp.zeros_like(hr_ref)
        hi_ref[...] = jnp.zeros_like(hi_ref)

    u = u_ref[0]
    ub = u.astype(BF16)
    first = lax.broadcasted_iota(jnp.int32, (tc, 1), 0) % SUBLANES == 0
    ub_prev = jnp.where(first, 0.0, pltpu.roll(u, 1, 0)).astype(BF16)
    nb = SSM_BLOCK_STATES
    last = SUBLANES - 1
    bc = lambda h: jnp.broadcast_to(h[last:last + 1, :], h.shape)
    block = lambda k: slice(k * nb, (k + 1) * nb)

    def project_in(k):
        cols = slice(k * MXU_DIM, (k + 1) * MXU_DIM)
        uk = jnp.concatenate([ub[:, cols], ub_prev[:, cols]], axis=1)
        xr_scr[k] = _dot(uk, bre_ref[k])
        xi_scr[k] = _dot(uk, bim_ref[k])

    def scan(k):
        sl = block(k)
        pr = pr_ref[:, sl]
        pi = pi_ref[:, sl]
        cr, ci = bc(hr_ref[0, :, sl]), bc(hi_ref[0, :, sl])
        for g in range(tc // SUBLANES):
            rows = slice(g * SUBLANES, (g + 1) * SUBLANES)
            xr = xr_scr[k, rows, :]
            xi = xi_scr[k, rows, :]
            for lv in range(1, 3):
                sr = pltpu.roll(xr, 1 << lv, 0)
                si = pltpu.roll(xi, 1 << lv, 0)
                ar = lvr_ref[lv, :, sl]
                ai = lvi_ref[lv, :, sl]
                xr, xi = xr + ar * sr - ai * si, xi + ar * si + ai * sr
            hr = xr + pr * cr - pi * ci
            hi = xi + pr * ci + pi * cr
            xr_scr[k, rows, :] = hr
            xi_scr[k, rows, :] = hi
            cr, ci = bc(hr), bc(hi)
        hr_ref[0, :, sl] = cr
        hi_ref[0, :, sl] = ci

    def project_out(k):
        y_scr[:, k * MXU_DIM:(k + 1) * MXU_DIM] = (
            _dot(xr_scr[k].astype(BF16), cre_ref[k]) - _dot(xi_scr[k].astype(BF16), cim_ref[k]))

    project_in(0)
    for k in range(SSM_BLOCKS):
        if k + 1 < SSM_BLOCKS:
            project_in(k + 1)
        scan(k)
        project_out(k)

    y_ref[0] = _s5_finish(y_scr[...], u, dskip_ref, wglu_ref, bglu_ref)


def s5_mix_seq(u, prep, d_skip, w_glu, b_glu, tc_cap=256):
    b, t, w = u.shape
    tc = _row_tile(t, tc_cap)
    pr, pi, _, _, bre, bim, cre, cim = prep
    rows = jnp.arange(SUBLANES)[:, None]
    lvr = jnp.stack([jnp.where(rows >= (1 << l), pr[(1 << l) - 1][None], 0.0) for l in range(3)])
    lvi = jnp.stack([jnp.where(rows >= (1 << l), pi[(1 << l) - 1][None], 0.0) for l in range(3)])
    full = lambda a: pl.BlockSpec(a.shape, lambda i, j: (0,) * a.ndim)
    consts = [pr, pi, lvr, lvi, bre, bim, cre, cim, d_skip.reshape(1, w), w_glu, b_glu.reshape(1, w)]
    y, hr, hi = pl.pallas_call(
        functools.partial(_s5_seq_kernel, tc=tc),
        grid=(b, t // tc),
        in_specs=[pl.BlockSpec((1, tc, w), lambda i, j: (i, j, 0))] + [full(a) for a in consts],
        out_specs=[pl.BlockSpec((1, tc, w), lambda i, j: (i, j, 0)),
                   pl.BlockSpec((1, SUBLANES, SSM_WIDTH), lambda i, j: (i, 0, 0)),
                   pl.BlockSpec((1, SUBLANES, SSM_WIDTH), lambda i, j: (i, 0, 0))],
        out_shape=[jax.ShapeDtypeStruct((b, t, w), F32),
                   jax.ShapeDtypeStruct((b, SUBLANES, SSM_WIDTH), F32),
                   jax.ShapeDtypeStruct((b, SUBLANES, SSM_WIDTH), F32)],
        scratch_shapes=[pltpu.VMEM((SSM_BLOCKS, tc, SSM_BLOCK_STATES), F32),
                        pltpu.VMEM((SSM_BLOCKS, tc, SSM_BLOCK_STATES), F32),
                        pltpu.VMEM((tc, w), F32)],
        compiler_params=_params(("parallel", "arbitrary")),
        name="s5_mix_seq",
    )(u, *consts)
    return y, hr[:, SUBLANES - 1], hi[:, SUBLANES - 1]


def _s5_step_kernel(u_ref, h0r_ref, h0i_ref, ar_ref, ai_ref, bre_ref, bim_ref, cre_ref, cim_ref,
                    dskip_ref, wglu_ref, bglu_ref, y_ref, hr_ref, hi_ref, y_scr):
    u = u_ref[...]
    ub = u.astype(BF16)
    nb = SSM_BLOCK_STATES
    for k in range(SSM_BLOCKS):
        sl = slice(k * nb, (k + 1) * nb)
        uk = ub[:, k * MXU_DIM:(k + 1) * MXU_DIM]
        ar, ai = ar_ref[:, sl], ai_ref[:, sl]
        h0r, h0i = h0r_ref[:, sl], h0i_ref[:, sl]
        hr = _dot(uk, bre_ref[k]) + ar * h0r - ai * h0i
        hi = _dot(uk, bim_ref[k]) + ar * h0i + ai * h0r
        hr_ref[:, sl] = hr
        hi_ref[:, sl] = hi
        y_scr[:, k * MXU_DIM:(k + 1) * MXU_DIM] = (
            _dot(hr.astype(BF16), cre_ref[k]) - _dot(hi.astype(BF16), cim_ref[k]))
    y_ref[...] = _s5_finish(y_scr[...], u, dskip_ref, wglu_ref, bglu_ref)


def s5_mix_step(u, h0r, h0i, prep, d_skip, w_glu, b_glu):
    s, w = u.shape
    pr, pi, bre, bim, _, _, cre, cim = prep
    st = jax.ShapeDtypeStruct((s, SSM_WIDTH), F32)
    return pl.pallas_call(
        _s5_step_kernel,
        out_shape=[jax.ShapeDtypeStruct((s, w), F32), st, st],
        scratch_shapes=[pltpu.VMEM((s, w), F32)],
        name="s5_mix_step",
    )(u, h0r, h0i, pr[0:1], pi[0:1], bre, bim, cre, cim, d_skip.reshape(1, w), w_glu,
      b_glu.reshape(1, w))


FOX_TQ = 2 * KV_TILE
FOX_STRIP = MXU_DIM


def _fox_seq_kernel(q_ref, kaug_ref, vt_ref, tri_ref, o_ref, s_scr, p_scr, acc_scr, m_scr, l_scr,
                    *, tq, tk):
    i = pl.program_id(2)
    qt = jnp.transpose(q_ref[0] * (ATTN_SCALE * LOG2E))
    row = lax.broadcasted_iota(jnp.int32, (LANES, 1), 0)
    qa = []
    for h in range(2):
        own = (row >= h * HEAD_DIM) & (row < (h + 1) * HEAD_DIM)
        head = 2 * pl.program_id(1) + h
        ones = (row >= head * BIAS_PIECES) & (row < (head + 1) * BIAS_PIECES)
        qa.append(jnp.concatenate(
            [jnp.where(own, qt, 0.0), jnp.broadcast_to(jnp.where(ones, 1.0, 0.0), qt.shape)],
            axis=0).astype(BF16))
    acc_scr[...] = jnp.zeros_like(acc_scr)
    m_scr[...] = jnp.full_like(m_scr, NEG_INF)
    l_scr[...] = jnp.zeros_like(l_scr)
    n_sub = tq // FOX_STRIP
    strips = [(h, qs) for h in range(2) for qs in range(n_sub)]
    lanes = lambda qs: slice(qs * FOX_STRIP, (qs + 1) * FOX_STRIP)
    ones_rows = jnp.ones((2 * SUBLANES, tk), BF16)

    def visible_keys(qs, c):
        if c is None:
            return tk, 0
        first = qs * FOX_STRIP - c * tk
        if first < 0:
            return 0, 0
        return (tk, 0) if first >= tk else (first, FOX_STRIP)

    def scores(j, slot, c=None):
        ka = kaug_ref[0, pl.ds(pl.multiple_of(j * tk, tk), tk), :]
        for n, (h, qs) in enumerate(strips):
            if sum(visible_keys(qs, c)):
                s_scr[slot, n] = _dot(ka, qa[h][:, lanes(qs)])

    def absorb(j, slot, c=None):
        alphas, n_keys = [], []
        for n, (h, qs) in enumerate(strips):
            n_full_keys, n_tri = visible_keys(qs, c)
            n_keys.append(n_full_keys + n_tri)
            if not n_keys[n]:
                alphas.append(None)
                continue
            m = m_scr[n]
            full_rows = slice(0, n_full_keys)
            tri_rows = slice(n_full_keys, n_full_keys + n_tri)
            m_new = m
            if n_tri:
                tri = s_scr[slot, n, tri_rows, :] + tri_ref[...]
                m_new = jnp.maximum(m_new, jnp.max(tri, axis=0, keepdims=True))
            if n_full_keys:
                m_new = jnp.maximum(
                    m_new, jnp.max(s_scr[slot, n, full_rows, :], axis=0, keepdims=True))
                p_scr[n, full_rows, :] = jnp.exp2(
                    s_scr[slot, n, full_rows, :] - m_new).astype(BF16)
            if n_tri:
                p_scr[n, tri_rows, :] = jnp.exp2(tri - m_new).astype(BF16)
            m_scr[n] = m_new
            alphas.append(jnp.exp2(m - m_new))
        for n, (h, qs) in enumerate(strips):
            if not n_keys[n]:
                continue
            vt = jnp.concatenate(
                [vt_ref[0, j, h * HEAD_DIM:(h + 1) * HEAD_DIM, :n_keys[n]],
                 ones_rows[:, :n_keys[n]]], axis=0)
            pv = _dot(vt, p_scr[n, :n_keys[n], :])
            acc_scr[h, :, lanes(qs)] = alphas[n] * acc_scr[h, :, lanes(qs)] + pv[:HEAD_DIM]
            l_scr[n] = alphas[n] * l_scr[n] + pv[HEAD_DIM:HEAD_DIM + 1]

    n_before = 2 * i
    scores(0, 0)

    def pair(jj, _):
        j = 2 * jj
        scores(j + 1, 1)
        absorb(j, 0)
        scores(j + 2, 0)
        absorb(j + 1, 1)
        return 0

    lax.fori_loop(0, i, pair, 0)
    if tq == tk:
        absorb(0, 0, c=0)
    else:
        scores(n_before + 1, 1, c=1)
        absorb(n_before, 0, c=0)
        absorb(n_before + 1, 1, c=1)

    l_head = lambda h: jnp.concatenate(
        [l_scr[h * n_sub + qs] for qs in range(n_sub)], axis=1)
    ot = jnp.concatenate([acc_scr[0] / l_head(0), acc_scr[1] / l_head(1)], axis=0)
    o_ref[0] = jnp.transpose(ot)


def fox_attention_seq(q, kaug, vtb):
    b, t, w = q.shape
    n_chunks, tk = vtb.shape[1], vtb.shape[3]
    tq = _row_tile(t, FOX_TQ)
    assert tq in (tk, 2 * tk) and n_chunks * tk == t and tk % FOX_STRIP == 0
    n_strips = 2 * tq // FOX_STRIP
    idx = jnp.arange(FOX_STRIP)
    tri = jnp.where(idx[:, None] <= idx[None, :], 0.0, NEG_INF).astype(F32)
    return pl.pallas_call(
        functools.partial(_fox_seq_kernel, tq=tq, tk=tk),
        grid=(b, HEAD_PAIRS, t // tq),
        in_specs=[pl.BlockSpec((1, tq, LANES), lambda bi, p, i: (bi, i, p)),
                  pl.BlockSpec((1, t, KAUG_WIDTH), lambda bi, p, i: (bi, 0, p)),
                  pl.BlockSpec((1, n_chunks, LANES, tk), lambda bi, p, i: (bi, 0, p, 0)),
                  pl.BlockSpec((FOX_STRIP, FOX_STRIP), lambda bi, p, i: (0, 0))],
        out_specs=pl.BlockSpec((1, tq, LANES), lambda bi, p, i: (bi, i, p)),
        out_shape=jax.ShapeDtypeStruct((b, t, w), F32),
        scratch_shapes=[pltpu.VMEM((2, n_strips, tk, FOX_STRIP), F32),
                        pltpu.VMEM((n_strips, tk, FOX_STRIP), BF16),
                        pltpu.VMEM((2, HEAD_DIM, tq), F32),
                        pltpu.VMEM((n_strips, 1, FOX_STRIP), F32),
                        pltpu.VMEM((n_strips, 1, FOX_STRIP), F32)],
        compiler_params=_params(("parallel", "parallel", "arbitrary")),
        name="fox_attention_seq",
    )(q, kaug, vtb, tri)


DEC_PAGES = 16


def _fox_dec_kernel(pt_ref, q_ref, kn_ref, vn_ref, lfn_ref, *rest, page, n_pages):
    kts = rest[:n_pages]
    vts = rest[n_pages:2 * n_pages]
    lfs = rest[2 * n_pages:3 * n_pages]
    tri_ref = rest[3 * n_pages]
    o_ref = rest[3 * n_pages + 1]
    m_scr, l_scr, csum_scr, acc_scr, qcol_scr = rest[3 * n_pages + 2:]
    g = pl.program_id(1)
    hp = FOX_HEADS
    qrow = q_ref[0] * ATTN_SCALE

    @pl.when(g == 0)
    def _():
        m_scr[...] = jnp.full_like(m_scr, NEG_INF)
        l_scr[...] = jnp.zeros_like(l_scr)
        csum_scr[...] = jnp.zeros_like(csum_scr)
        acc_scr[...] = jnp.zeros_like(acc_scr)
        qcol_scr[...] = jnp.transpose(jnp.broadcast_to(qrow, (page, SEQ_WIDTH)))

    tri = tri_ref[...]
    base = csum_scr[...]
    head_row = lax.broadcasted_iota(jnp.int32, (hp, page), 0)
    s_parts = []
    for i in range(n_pages):
        hi, mid, lo = _split3(lfs[i][0])
        cum = (_dot(hi, tri) + _dot(mid, tri)) + _dot(lo, tri) + base
        base = jnp.broadcast_to(cum[:, page - 1:page], cum.shape)
        qk = jnp.zeros((hp, page), F32)
        for h in range(FOX_HEADS):
            rows = slice(h * HEAD_DIM, (h + 1) * HEAD_DIM)
            r = jnp.sum(kts[i][0, rows, :] * qcol_scr[rows, :], axis=0, keepdims=True)
            qk = jnp.where(head_row == h, r, qk)
        s_parts.append(qk - cum)
    csum_scr[...] = base
    s = jnp.concatenate(s_parts, axis=1)

    m_old = m_scr[...]
    m_new = jnp.maximum(m_old, jnp.max(s, axis=-1, keepdims=True))
    alpha = jnp.exp(m_old - m_new)
    p = jnp.exp(s - m_new)
    l_scr[...] = alpha * l_scr[...] + jnp.sum(p, axis=-1, keepdims=True)
    m_scr[...] = m_new
    for h in range(FOX_HEADS):
        rows = slice(h * HEAD_DIM, (h + 1) * HEAD_DIM)
        acc = acc_scr[rows, :] * alpha[h:h + 1, :]
        for i in range(n_pages):
            acc = acc + vts[i][0, rows, :] * p[h:h + 1, i * page:(i + 1) * page]
        acc_scr[rows, :] = acc

    @pl.when(g == pl.num_programs(1) - 1)
    def _():
        lane = lax.broadcasted_iota(jnp.int32, (hp, SEQ_WIDTH), 1)
        head = lax.broadcasted_iota(jnp.int32, (hp, SEQ_WIDTH), 0)
        own = (lane >= head * HEAD_DIM) & (lane < (head + 1) * HEAD_DIM)
        spread = lambda col: jnp.sum(jnp.where(own, col, 0.0), axis=0, keepdims=True)
        c_new = csum_scr[:, 0:1] + lfn_ref[0]
        s_new = jnp.sum(jnp.where(own, qrow * kn_ref[0], 0.0), axis=-1, keepdims=True) - c_new
        m_fin = jnp.maximum(m_scr[...], s_new)
        a_fin = jnp.exp(m_scr[...] - m_fin)
        p_new = jnp.exp(s_new - m_fin)
        l_fin = a_fin * l_scr[...] + p_new
        acc_row = jnp.sum(jnp.transpose(acc_scr[...]), axis=0, keepdims=True)
        o_ref[0] = (spread(a_fin) * acc_row + spread(p_new) * vn_ref[0]) / spread(l_fin)


def fox_attention_decode(q, k_new, v_new, logf_new, cache_k, cache_v, cache_logf, page_table):
    s, w = q.shape
    n_pool, page = cache_k.shape[:2]
    assert page == LANES
    pages_per_seq = page_table.shape[1]
    n_pages = min(DEC_PAGES, pages_per_seq)
    assert pages_per_seq % n_pages == 0
    hp = FOX_HEADS
    ckt = jnp.transpose(cache_k, (0, 2, 3, 1)).reshape(n_pool, w, page)
    cvt = jnp.transpose(cache_v, (0, 2, 3, 1)).reshape(n_pool, w, page)
    clf = jnp.transpose(cache_logf, (0, 2, 1))
    lfn = logf_new.reshape(s, hp, 1)
    tri = jnp.triu(jnp.ones((page, page), F32)).astype(BF16)
    row = pl.BlockSpec((1, 1, w), lambda b, g, pt: (b, 0, 0))

    def paged(shape, i):
        return pl.BlockSpec((1,) + shape, lambda b, g, pt, i=i: (pt[b, g * n_pages + i], 0, 0))

    grid_spec = pltpu.PrefetchScalarGridSpec(
        num_scalar_prefetch=1,
        grid=(s, pages_per_seq // n_pages),
        in_specs=([row, row, row, pl.BlockSpec((1, hp, 1), lambda b, g, pt: (b, 0, 0))]
                  + [paged((w, page), i) for i in range(n_pages)]
                  + [paged((w, page), i) for i in range(n_pages)]
                  + [paged((hp, page), i) for i in range(n_pages)]
                  + [pl.BlockSpec((page, page), lambda b, g, pt: (0, 0))]),
        out_specs=row,
        scratch_shapes=[pltpu.VMEM((hp, 1), F32), pltpu.VMEM((hp, 1), F32),
                        pltpu.VMEM((hp, page), F32), pltpu.VMEM((w, page), F32),
                        pltpu.VMEM((w, page), F32)],
    )
    out = pl.pallas_call(
        functools.partial(_fox_dec_kernel, page=page, n_pages=n_pages),
        grid_spec=grid_spec,
        out_shape=jax.ShapeDtypeStruct((s, 1, w), F32),
        compiler_params=_params(("parallel", "arbitrary"), VMEM_LIMIT),
        name="fox_attention_decode",
    )(page_table, q.reshape(s, 1, w), k_new.reshape(s, 1, w), v_new.reshape(s, 1, w), lfn,
      *([ckt] * n_pages), *([cvt] * n_pages), *([clf] * n_pages), tri)
    return out.reshape(s, w)


def _trunk(x, mem_k, mem_v, conv_prev, ssm_state, fox_attend, p, s5_prep, sequential):
    assert N_A == 1
    b, t, d = x.shape
    m = b * t
    new_conv, ssm_out, kv = [], None, None
    for l in range(DEPTH):
        if l == N_A:
            if sequential:
                kv = shared_kv_proj_seq(x, p["kv_norm"], p["w_kv"], p["b_f"])
            else:
                kv = shared_kv_proj_step(x.reshape(m, d), p["kv_norm"], p["w_kv"], p["b_f"])
        z_seq, q_mem = norm_linear(x.reshape(m, d), p["norm_mix_pre"][l], p["w_in"][l],
                                   (SEQ_WIDTH, MEM_WIDTH))
        if l < N_A:
            if sequential:
                seq_out, hr, hi = s5_mix_seq(z_seq.reshape(b, t, SEQ_WIDTH), s5_prep[l],
                                             p["d_skip"][l], p["w_glu"][l], p["b_glu"][l])
            else:
                seq_out, hr, hi = s5_mix_step(z_seq, ssm_state[0][l], ssm_state[1][l], s5_prep[l],
                                              p["d_skip"][l], p["w_glu"][l], p["b_glu"][l])
            ssm_out = (hr.reshape(b, SSM_GROUPS, SSM_STATE), hi.reshape(b, SSM_GROUPS, SSM_STATE))
        else:
            seq_out = fox_attend(z_seq, kv)
        ffn_args = (p["norm_ffn_pre"][l], p["norm_ffn_post"][l], p["w_up"][l], p["conv_w"][l],
                    p["conv_b"][l], p["w_down"][l])
        if sequential:
            x3, cp = layer_tail_seq(x, seq_out.reshape(b, t, SEQ_WIDTH),
                                    q_mem.reshape(b, t, MEM_WIDTH), mem_k[l], mem_v[l],
                                    p["w_out"][l], p["norm_mix_post"][l], conv_prev[l], *ffn_args)
        else:
            mem_out = mem_attention(q_mem.reshape(b, t, MEM_WIDTH), mem_k, mem_v, l)
            x2 = mix_out(x.reshape(m, d), seq_out, mem_out.reshape(m, MEM_WIDTH),
                         p["w_out"][l], p["norm_mix_post"][l])
            x3, u_new = conv_ffn_step(x2, conv_prev[l], *ffn_args)
            cp = jnp.stack([conv_prev[l][:, 1], u_new], axis=1)
        new_conv.append(cp)
        x = x3.reshape(b, t, d)
    return x, ssm_out, jnp.stack(new_conv), kv


def kernel(x_prompt, x_sample, state_ssm_re, state_ssm_im, cache_k, cache_v, cache_logf,
           cache_mem_k, cache_mem_v, state_ffn_conv, page_table, mem_prompt,
           w_in, w_out, norm_mix_pre, norm_mix_post, norm_ffn_pre, norm_ffn_post,
           mem_norm, w_mem_kv, lam_re, lam_im, log_dt, b_re, b_im, c_re, c_im, d_skip,
           w_glu, b_glu, kv_norm, w_kv, b_f, w_up, conv_w, conv_b, w_down):
    per_layer_bf16 = lambda w: [w[l].astype(BF16) for l in range(w.shape[0])]
    p = dict(w_in=per_layer_bf16(w_in), w_out=per_layer_bf16(w_out), norm_mix_pre=norm_mix_pre,
             norm_mix_post=norm_mix_post, norm_ffn_pre=norm_ffn_pre, norm_ffn_post=norm_ffn_post,
             d_skip=d_skip, w_glu=per_layer_bf16(w_glu), b_glu=b_glu, kv_norm=kv_norm, w_kv=w_kv,
             b_f=b_f, w_up=per_layer_bf16(w_up), conv_w=conv_w, conv_b=conv_b,
             w_down=per_layer_bf16(w_down))
    s5_prep = [s5_prepare(lam_re[l], lam_im[l], log_dt[l], b_re[l], b_im[l], c_re[l], c_im[l])
               for l in range(N_A)]

    bp, tp, d = x_prompt.shape
    n_mem = mem_prompt.shape[1]
    mem_pairs = [mem_kv_proj(mem_prompt, mem_norm[l], w_mem_kv[l]) for l in range(DEPTH)]
    p_mem_kt = [mkt for mkt, _ in mem_pairs]
    p_mem_vt = [mvt for _, mvt in mem_pairs]
    zeros_conv = jnp.zeros((DEPTH, bp, CONV_W - 1, 2 * D_FF), F32)

    def fox_prompt(q, kv):
        kt, vt, kaug, vtb, logf = kv
        return fox_attention_seq(q.reshape(bp, tp, SEQ_WIDTH), kaug, vtb)

    y_prompt, p_ssm, p_conv, p_kv = _trunk(x_prompt, p_mem_kt, p_mem_vt, zeros_conv, None,
                                           fox_prompt, p, s5_prep, sequential=True)
    untr = lambda a, n: jnp.transpose(a.reshape(a.shape[0], n, HEAD_DIM, a.shape[2]), (0, 3, 1, 2))
    mem5 = lambda ms: jnp.stack([untr(a, MEM_HEADS) for a in ms])

    bs = x_sample.shape[0]
    tr_mem = lambda a: jnp.transpose(a, (0, 1, 3, 4, 2)).reshape(DEPTH, bs, MEM_WIDTH, n_mem)
    s_mem_kt = tr_mem(cache_mem_k)
    s_mem_vt = tr_mem(cache_mem_v)
    ssm0 = (state_ssm_re.reshape(N_A, bs, SSM_WIDTH), state_ssm_im.reshape(N_A, bs, SSM_WIDTH))

    def fox_sample(q, kv):
        k, v, logf = kv
        return fox_attention_decode(q, k, v, logf, cache_k, cache_v, cache_logf, page_table)

    y_sample, s_ssm, s_conv, s_kv = _trunk(x_sample, s_mem_kt, s_mem_vt, state_ffn_conv, ssm0,
                                           fox_sample, p, s5_prep, sequential=False)
    head4 = lambda a: a.reshape(bs, 1, FOX_HEADS, HEAD_DIM)

    return (y_prompt, y_sample, p_ssm[0][None], p_ssm[1][None],
            untr(p_kv[0], FOX_HEADS), untr(p_kv[1], FOX_HEADS), p_kv[4],
            mem5(p_mem_kt), mem5(p_mem_vt), p_conv,
            s_ssm[0][None], s_ssm[1][None],
            head4(s_kv[0]), head4(s_kv[1]), s_kv[2].reshape(bs, 1, FOX_HEADS), s_conv)
```

```python
import functools
import math

import jax
import jax.numpy as jnp
from jax import lax
from jax.experimental import pallas as pl
from jax.experimental.pallas import tpu as pltpu

F32 = jnp.float32
BF16 = jnp.bfloat16

D_MODEL = 1024
DEPTH = 2
N_A = DEPTH // 2
HEAD_DIM = 64
MEM_HEADS = 4
MEM_WIDTH = MEM_HEADS * HEAD_DIM
SEQ_WIDTH = D_MODEL - MEM_WIDTH
SSM_GROUP = 16
SSM_GROUPS = SEQ_WIDTH // SSM_GROUP
SSM_STATE = 64
SSM_WIDTH = SSM_GROUPS * SSM_STATE
FOX_HEADS = SEQ_WIDTH // HEAD_DIM
D_FF = (11 * D_MODEL) // 4
CONV_W = 3
EPS = 1e-6
NEG_INF = -1e30
EIG_CLIP = -1e-4
ATTN_SCALE = HEAD_DIM ** -0.5
LOG2E = math.log2(math.e)

LANES = 128
SUBLANES = 8
MXU_DIM = 256
VMEM_BYTES_V7X = 64 * 1024 * 1024
VMEM_LIMIT = (VMEM_BYTES_V7X * 7) // 8

SSM_BLOCKS = SEQ_WIDTH // MXU_DIM
SSM_BLOCK_STATES = SSM_WIDTH // SSM_BLOCKS
HEAD_PAIRS = FOX_HEADS // 2


def _params(semantics, vmem=None):
    return pltpu.CompilerParams(dimension_semantics=semantics, vmem_limit_bytes=vmem)


def _row_tile(m, cap):
    t = min(m, cap)
    assert m % t == 0, (m, t)
    return t


def _rms(x, g):
    return x * lax.rsqrt(jnp.mean(x * x, axis=-1, keepdims=True) + EPS) * g


def _sigmoid(x):
    return 1.0 / (1.0 + jnp.exp(-x))


def _log_sigmoid(x):
    return -(jnp.maximum(-x, 0.0) + jnp.log1p(jnp.exp(-jnp.abs(x))))


def _split3(x):
    hi = x.astype(BF16)
    r1 = x - hi.astype(F32)
    mid = r1.astype(BF16)
    lo = (r1 - mid.astype(F32)).astype(BF16)
    return hi, mid, lo


def _dot(a, b):
    return jnp.dot(a, b, preferred_element_type=F32)


def _dot_nt(a, b):
    return lax.dot_general(a, b, (((1,), (1,)), ((), ())), preferred_element_type=F32)


def _norm_linear_kernel(x_ref, g_ref, w_ref, *out_refs, splits):
    h = _rms(x_ref[...], g_ref[...]).astype(BF16)
    c0 = 0
    for o_ref, n in zip(out_refs, splits):
        o_ref[...] = _dot(h, w_ref[:, c0:c0 + n])
        c0 += n


def norm_linear(x, g, w, splits, tm_cap=512):
    m, d = x.shape
    tm = _row_tile(m, tm_cap)
    n_tot = sum(splits)
    assert w.shape == (d, n_tot)
    return pl.pallas_call(
        functools.partial(_norm_linear_kernel, splits=tuple(splits)),
        grid=(m // tm,),
        in_specs=[pl.BlockSpec((tm, d), lambda i: (i, 0)),
                  pl.BlockSpec((1, d), lambda i: (0, 0)),
                  pl.BlockSpec((d, n_tot), lambda i: (0, 0))],
        out_specs=[pl.BlockSpec((tm, n), lambda i: (i, 0)) for n in splits],
        out_shape=[jax.ShapeDtypeStruct((m, n), F32) for n in splits],
        compiler_params=_params(("parallel",)),
        name="norm_linear",
    )(x, g.reshape(1, d), w)


KV_TILE = 512
KAUG_WIDTH = 2 * LANES
BIAS_PIECES = 3


def _kv_weights(w_kv, b_f):
    wk = w_kv[:, :SEQ_WIDTH].astype(BF16)
    wf = jnp.pad(w_kv[:, 2 * SEQ_WIDTH:], ((0, 0), (0, LANES - FOX_HEADS))).astype(BF16)
    bf = jnp.pad(b_f, (0, LANES - FOX_HEADS)).reshape(1, LANES)
    return wk, wf, bf


def _kv_seq_kernel(x_ref, g_ref, wk_ref, wvt_ref, wf_ref, bf_ref, tri_ref, place_ref,
                   gin_ref, wqt_ref, wm_ref,
                   kt_ref, vt_ref, kaug_ref, vtb_ref, lf_ref, qt_ref, qm_ref, carry_ref):
    @pl.when(pl.program_id(1) == 0)
    def _():
        carry_ref[...] = jnp.zeros_like(carry_ref)

    x = x_ref[0]
    xhat = x * lax.rsqrt(jnp.mean(x * x, axis=-1, keepdims=True) + EPS)
    hin = (xhat * gin_ref[...]).astype(BF16)
    qt_ref[0] = (_dot_nt(wqt_ref[...], hin) * (ATTN_SCALE * LOG2E)).astype(BF16)
    qm_ref[0] = _dot(hin, wm_ref[...])
    h = (xhat * g_ref[...]).astype(BF16)
    k = _dot(h, wk_ref[...])
    kt_ref[0] = jnp.transpose(k)
    vt = _dot_nt(wvt_ref[...], h)
    vt_ref[0] = vt
    vtb_ref[0, 0] = vt.astype(BF16)
    logf = _log_sigmoid(_dot(h, wf_ref[...]) + bf_ref[...])
    lf_ref[0] = logf[:, :FOX_HEADS]
    c3 = _dot(tri_ref[...], jnp.concatenate(_split3(logf), axis=1))
    cum = (c3[:, :LANES] + c3[:, LANES:2 * LANES]) + c3[:, 2 * LANES:] + carry_ref[...]
    carry_ref[...] = cum[cum.shape[0] - 1:, :]
    pieces = jnp.concatenate(_split3(cum * -LOG2E), axis=1)
    bias = _dot(pieces, place_ref[...]).astype(BF16)
    kb = k.astype(BF16)
    for p in range(HEAD_PAIRS):
        kaug_ref[0, :, p * KAUG_WIDTH:p * KAUG_WIDTH + LANES] = kb[:, p * LANES:(p + 1) * LANES]
        kaug_ref[0, :, p * KAUG_WIDTH + LANES:(p + 1) * KAUG_WIDTH] = bias


def shared_kv_proj_seq(x, g, w_kv, b_f, g_in, w_in):
    b, t, d = x.shape
    wqt = jnp.transpose(w_in[:, :SEQ_WIDTH])
    wm = w_in[:, SEQ_WIDTH:]
    tm = _row_tile(t, KV_TILE)
    wk, wf, bf = _kv_weights(w_kv, b_f)
    wvt = jnp.transpose(w_kv)[SEQ_WIDTH:2 * SEQ_WIDTH].astype(BF16)
    tri = jnp.tril(jnp.ones((tm, tm), F32)).astype(BF16)
    hh = jnp.arange(FOX_HEADS)
    place = jnp.zeros((BIAS_PIECES * LANES, LANES), F32)
    for j in range(BIAS_PIECES):
        place = place.at[LANES * j + hh, BIAS_PIECES * hh + j].set(1.0)
    place = place.astype(BF16)
    tok = lambda n: pl.BlockSpec((1, tm, n), lambda i, j: (i, j, 0))
    tr = pl.BlockSpec((1, SEQ_WIDTH, tm), lambda i, j: (i, 0, j))
    full = lambda s: pl.BlockSpec(s, lambda i, j: (0,) * len(s))
    return pl.pallas_call(
        _kv_seq_kernel,
        grid=(b, t // tm),
        in_specs=[tok(d), full((1, d)), full((d, SEQ_WIDTH)), full((SEQ_WIDTH, d)),
                  full((d, LANES)), full((1, LANES)), full((tm, tm)),
                  full((BIAS_PIECES * LANES, LANES)),
                  full((1, d)), full((SEQ_WIDTH, d)), full((d, MEM_WIDTH))],
        out_specs=[tr, tr, tok(HEAD_PAIRS * KAUG_WIDTH),
                   pl.BlockSpec((1, 1, SEQ_WIDTH, tm), lambda i, j: (i, j, 0, 0)),
                   tok(FOX_HEADS), tr, tok(MEM_WIDTH)],
        out_shape=[jax.ShapeDtypeStruct((b, SEQ_WIDTH, t), F32),
                   jax.ShapeDtypeStruct((b, SEQ_WIDTH, t), F32),
                   jax.ShapeDtypeStruct((b, t, HEAD_PAIRS * KAUG_WIDTH), BF16),
                   jax.ShapeDtypeStruct((b, t // tm, SEQ_WIDTH, tm), BF16),
                   jax.ShapeDtypeStruct((b, t, FOX_HEADS), F32),
                   jax.ShapeDtypeStruct((b, SEQ_WIDTH, t), BF16),
                   jax.ShapeDtypeStruct((b, t, MEM_WIDTH), F32)],
        scratch_shapes=[pltpu.VMEM((1, LANES), F32)],
        compiler_params=_params(("parallel", "arbitrary")),
        name="shared_kv_proj_seq",
    )(x, g.reshape(1, d), wk, wvt, wf, bf, tri, place, g_in.reshape(1, d), wqt, wm)


def _kv_step_kernel(x_ref, g_ref, wk_ref, wv_ref, wf_ref, bf_ref, k_ref, v_ref, lf_ref):
    h = _rms(x_ref[...], g_ref[...]).astype(BF16)
    k_ref[...] = _dot(h, wk_ref[...])
    v_ref[...] = _dot(h, wv_ref[...])
    lf_ref[...] = _log_sigmoid(_dot(h, wf_ref[...]) + bf_ref[...])[:, :FOX_HEADS]


def shared_kv_proj_step(x, g, w_kv, b_f):
    s, d = x.shape
    wk, wf, bf = _kv_weights(w_kv, b_f)
    wv = w_kv[:, SEQ_WIDTH:2 * SEQ_WIDTH].astype(BF16)
    kv = jax.ShapeDtypeStruct((s, SEQ_WIDTH), F32)
    return pl.pallas_call(
        _kv_step_kernel,
        out_shape=[kv, kv, jax.ShapeDtypeStruct((s, FOX_HEADS), F32)],
        name="shared_kv_proj_step",
    )(x, g.reshape(1, d), wk, wv, wf, bf)


def _mem_kv_kernel(x_ref, g_ref, wt_ref, kt_ref, vt_ref):
    h = _rms(x_ref[0], g_ref[...]).astype(BF16)
    kvt = _dot_nt(wt_ref[...], h)
    kt_ref[0] = kvt[:MEM_WIDTH]
    vt_ref[0] = kvt[MEM_WIDTH:]


def mem_kv_proj(mem, g, w):
    b, n_mem, d = mem.shape
    out = jax.ShapeDtypeStruct((b, MEM_WIDTH, n_mem), F32)
    blk = pl.BlockSpec((1, MEM_WIDTH, n_mem), lambda i: (i, 0, 0))
    return pl.pallas_call(
        _mem_kv_kernel,
        grid=(b,),
        in_specs=[pl.BlockSpec((1, n_mem, d), lambda i: (i, 0, 0)),
                  pl.BlockSpec((1, d), lambda i: (0, 0)),
                  pl.BlockSpec((2 * MEM_WIDTH, d), lambda i: (0, 0))],
        out_specs=[blk, blk],
        out_shape=[out, out],
        compiler_params=_params(("parallel",)),
        name="mem_kv_proj",
    )(mem, g.reshape(1, d), jnp.transpose(w).astype(BF16))


def _mem_attn_tile(q, mkt, mvt):
    q = q * ATTN_SCALE
    mkt = mkt.astype(BF16)
    mvt = mvt.astype(BF16)
    lane = lax.broadcasted_iota(jnp.int32, (1, MEM_WIDTH), 1)
    out = jnp.zeros(q.shape, F32)
    for h in range(MEM_HEADS):
        in_head = (lane >= h * HEAD_DIM) & (lane < (h + 1) * HEAD_DIM)
        s = _dot(jnp.where(in_head, q, 0.0).astype(BF16), mkt)
        p = jnp.exp(s - jnp.max(s, axis=-1, keepdims=True))
        p = p / jnp.sum(p, axis=-1, keepdims=True)
        out = out + jnp.where(in_head, _dot_nt(p.astype(BF16), mvt), 0.0)
    return out


def _mem_attn_row(q, mkt, mvt):
    lane = lax.broadcasted_iota(jnp.int32, (MEM_HEADS, MEM_WIDTH), 1)
    head = lax.broadcasted_iota(jnp.int32, (MEM_HEADS, MEM_WIDTH), 0)
    own = (lane >= head * HEAD_DIM) & (lane < (head + 1) * HEAD_DIM)
    s = _dot(jnp.where(own, q * ATTN_SCALE, 0.0).astype(BF16), mkt.astype(BF16))
    p = jnp.exp(s - jnp.max(s, axis=-1, keepdims=True))
    p = p / jnp.sum(p, axis=-1, keepdims=True)
    o = _dot_nt(p.astype(BF16), mvt.astype(BF16))
    return jnp.sum(jnp.where(own, o, 0.0), axis=0, keepdims=True)


def _mem_attn_kernel(q_ref, mkt_ref, mvt_ref, o_ref):
    attend = _mem_attn_row if q_ref.shape[1] == 1 else _mem_attn_tile
    for s in range(q_ref.shape[0]):
        o_ref[s] = attend(q_ref[s], mkt_ref[s], mvt_ref[s])


def mem_attention(q_mem, mkt, mvt, layer, seqs_per_step=SUBLANES):
    b, t, w = q_mem.shape
    n_mem = mkt.shape[3]
    bs = _row_tile(b, seqs_per_step)
    mem = pl.BlockSpec((None, bs, w, n_mem), lambda i: (layer, i, 0, 0))
    return pl.pallas_call(
        _mem_attn_kernel,
        grid=(b // bs,),
        in_specs=[pl.BlockSpec((bs, t, w), lambda i: (i, 0, 0)), mem, mem],
        out_specs=pl.BlockSpec((bs, t, w), lambda i: (i, 0, 0)),
        out_shape=jax.ShapeDtypeStruct((b, t, w), F32),
        compiler_params=_params(("parallel",)),
        name="mem_attention",
    )(q_mem, mkt, mvt)


def _mix_out_tile(x, seq, mem, w_ref, g):
    o = (_dot(seq.astype(BF16), w_ref[:SEQ_WIDTH, :]) + _dot(mem.astype(BF16), w_ref[SEQ_WIDTH:, :]))
    return x + _rms(o, g)


def _mix_out_kernel(x_ref, s_ref, m_ref, w_ref, g_ref, o_ref):
    o_ref[...] = _mix_out_tile(x_ref[...], s_ref[...], m_ref[...], w_ref, g_ref[...])


def mix_out(x, seq_out, mem_out, w_out, g, tm_cap=512):
    m, d = x.shape
    tm = _row_tile(m, tm_cap)
    row = lambda n: pl.BlockSpec((tm, n), lambda i: (i, 0))
    return pl.pallas_call(
        _mix_out_kernel,
        grid=(m // tm,),
        in_specs=[row(d), row(SEQ_WIDTH), row(MEM_WIDTH),
                  pl.BlockSpec((d, d), lambda i: (0, 0)),
                  pl.BlockSpec((1, d), lambda i: (0, 0))],
        out_specs=row(d),
        out_shape=jax.ShapeDtypeStruct((m, d), F32),
        compiler_params=_params(("parallel",)),
        name="mix_out",
    )(x, seq_out, mem_out, w_out, g.reshape(1, d))


FFN_CHUNK = MXU_DIM
FFN_STEP_CHUNK = D_FF // 2
assert FFN_STEP_CHUNK % LANES == 0


def _tail_seq_kernel(x_ref, seq_ref, qm_ref, mkt_ref, mvt_ref, wout_ref, gmix_ref, prev_ref,
                     gpre_ref, gpost_ref, wup_ref, cw_ref, cb_ref, wdn_ref,
                     o_ref, conv_ref, h_scr, carry_scr, *, tm):
    @pl.when(pl.program_id(1) == 0)
    def _():
        carry_scr[...] = prev_ref[0]

    mem = _mem_attn_tile(qm_ref[0], mkt_ref[0], mvt_ref[0])
    x = _mix_out_tile(x_ref[0], seq_ref[0], mem, wout_ref, gmix_ref[...])
    xn = _rms(x, gpre_ref[...]).astype(BF16)
    row = lax.broadcasted_iota(jnp.int32, (SUBLANES, 1), 0)

    def up(col):
        return _dot(xn, wup_ref[:, col:col + FFN_CHUNK])

    def conv(u, col):
        c0 = carry_scr[0:1, col:col + FFN_CHUNK]
        c1 = carry_scr[1:2, col:col + FFN_CHUNK]
        u1 = pltpu.roll(u, 1, 0)
        u2 = pltpu.roll(u, 2, 0)
        u1 = jnp.concatenate([jnp.where(row == 0, c1, u1[:SUBLANES]), u1[SUBLANES:]], axis=0)
        u2 = jnp.concatenate(
            [jnp.where(row == 0, c0, jnp.where(row == 1, c1, u2[:SUBLANES])), u2[SUBLANES:]],
            axis=0)
        carry_scr[:, col:col + FFN_CHUNK] = u[tm - 2:, :]
        w = cw_ref[:, col:col + FFN_CHUNK]
        return u2 * w[0:1] + u1 * w[1:2] + u * w[2:3] + cb_ref[:, col:col + FFN_CHUNK]

    for c in range(D_FF // FFN_CHUNK):
        gate = conv(up(c * FFN_CHUNK), c * FFN_CHUNK)
        val = conv(up(D_FF + c * FFN_CHUNK), D_FF + c * FFN_CHUNK)
        h_scr[:, c * FFN_CHUNK:(c + 1) * FFN_CHUNK] = (gate * _sigmoid(gate) * val).astype(BF16)

    f = _dot(h_scr[...], wdn_ref[...])
    o_ref[0] = x + _rms(f, gpost_ref[...])
    conv_ref[0] = carry_scr[...]


def layer_tail_seq(x, seq_out, q_mem, mkt, mvt, w_out, g_mix, prev, g_pre, g_post, w_up, conv_w,
                   conv_b, w_down, tm_cap=512):
    b, t, d = x.shape
    n_mem = mkt.shape[2]
    tm = _row_tile(t, tm_cap)
    assert tm >= 2 * SUBLANES
    f2 = 2 * D_FF
    full = lambda s: pl.BlockSpec(s, lambda i, j: (0,) * len(s), pipeline_mode=pl.Buffered(1))
    tok = lambda n: pl.BlockSpec((1, tm, n), lambda i, j: (i, j, 0))
    per_seq = lambda r, c: pl.BlockSpec((1, r, c), lambda i, j: (i, 0, 0))
    return pl.pallas_call(
        functools.partial(_tail_seq_kernel, tm=tm),
        grid=(b, t // tm),
        in_specs=[tok(d), tok(SEQ_WIDTH), tok(MEM_WIDTH),
                  per_seq(MEM_WIDTH, n_mem), per_seq(MEM_WIDTH, n_mem),
                  full((d, d)), full((1, d)), per_seq(CONV_W - 1, f2),
                  full((1, d)), full((1, d)), full((d, f2)), full((CONV_W, f2)), full((1, f2)),
                  full((D_FF, d))],
        out_specs=[tok(d), per_seq(CONV_W - 1, f2)],
        out_shape=[jax.ShapeDtypeStruct((b, t, d), F32),
                   jax.ShapeDtypeStruct((b, CONV_W - 1, f2), F32)],
        scratch_shapes=[pltpu.VMEM((tm, D_FF), BF16), pltpu.VMEM((CONV_W - 1, f2), F32)],
        compiler_params=_params(("parallel", "arbitrary"), VMEM_LIMIT),
        name="layer_tail_seq",
    )(x, seq_out, q_mem, mkt, mvt, w_out, g_mix.reshape(1, d), prev, g_pre.reshape(1, d),
      g_post.reshape(1, d), w_up, conv_w, conv_b.reshape(1, f2), w_down)


def _ffn_step_kernel(x_ref, p0g_ref, p0v_ref, p1g_ref, p1v_ref, gpre_ref, gpost_ref,
                     wg_ref, wv_ref, cwg_ref, cwv_ref, cbg_ref, cbv_ref, wdn_ref,
                     o_ref, ug_ref, uv_ref, acc_scr):
    c = pl.program_id(0)

    @pl.when(c == 0)
    def _():
        acc_scr[...] = jnp.zeros_like(acc_scr)

    x = x_ref[...]
    xn = _rms(x, gpre_ref[...]).astype(BF16)

    def conv(w_ref, p0_ref, p1_ref, cw_ref, cb_ref, u_ref):
        u = _dot(xn, w_ref[...])
        u_ref[...] = u
        w = cw_ref[...]
        return p0_ref[...] * w[0:1] + p1_ref[...] * w[1:2] + u * w[2:3] + cb_ref[...]

    gate = conv(wg_ref, p0g_ref, p1g_ref, cwg_ref, cbg_ref, ug_ref)
    val = conv(wv_ref, p0v_ref, p1v_ref, cwv_ref, cbv_ref, uv_ref)
    acc_scr[...] += _dot((gate * _sigmoid(gate) * val).astype(BF16), wdn_ref[...])

    @pl.when(c == pl.num_programs(0) - 1)
    def _():
        o_ref[...] = x + _rms(acc_scr[...], gpost_ref[...])


def conv_ffn_step(x, prev, g_pre, g_post, w_up, conv_w, conv_b, w_down):
    s, d = x.shape
    f2 = 2 * D_FF
    fc = FFN_STEP_CHUNK
    nc = D_FF // fc
    prev2 = prev.reshape(s, (CONV_W - 1) * f2)
    cb = conv_b.reshape(1, f2)
    const = lambda shp: pl.BlockSpec(shp, lambda c: (0, 0))
    col = lambda rows, off: pl.BlockSpec((rows, fc), lambda c, off=off: (0, c + off))
    out, ug, uv = pl.pallas_call(
        _ffn_step_kernel,
        grid=(nc,),
        in_specs=[const((s, d)),
                  col(s, 0), col(s, nc), col(s, 2 * nc), col(s, 3 * nc),
                  const((1, d)), const((1, d)),
                  col(d, 0), col(d, nc),
                  col(CONV_W, 0), col(CONV_W, nc),
                  col(1, 0), col(1, nc),
                  pl.BlockSpec((fc, d), lambda c: (c, 0))],
        out_specs=[const((s, d)), col(s, 0), col(s, 0)],
        out_shape=[jax.ShapeDtypeStruct((s, d), F32),
                   jax.ShapeDtypeStruct((s, D_FF), F32),
                   jax.ShapeDtypeStruct((s, D_FF), F32)],
        scratch_shapes=[pltpu.VMEM((s, d), F32)],
        compiler_params=_params(("arbitrary",)),
        name="conv_ffn_step",
    )(x, prev2, prev2, prev2, prev2, g_pre.reshape(1, d), g_post.reshape(1, d),
      w_up, w_up, conv_w, conv_w, cb, cb, w_down)
    return out, jnp.concatenate([ug, uv], axis=-1)


def _s5_prep_kernel(lr_ref, li_ref, ldt_ref, br_ref, bi_ref, pr_ref, pi_ref, bbr_ref, bbi_ref,
                    abr_ref, abi_ref):
    lr = jnp.minimum(lr_ref[...], EIG_CLIP)
    li = li_ref[...]
    dt = jnp.exp(ldt_ref[...])
    mag = jnp.exp(lr * dt)
    ar = mag * jnp.cos(li * dt)
    ai = mag * jnp.sin(li * dt)
    den = lr * lr + li * li
    nr = ar - 1.0
    fr = (nr * lr + ai * li) / den
    fi = (ai * lr - nr * li) / den
    br = br_ref[...]
    bi = bi_ref[...]
    bbr = fr * br - fi * bi
    bbi = fr * bi + fi * br
    bbr_ref[...] = bbr
    bbi_ref[...] = bbi
    abr_ref[...] = ar * bbr - ai * bbi
    abi_ref[...] = ar * bbi + ai * bbr
    pr, pi = ar, ai
    pr_ref[0] = pr
    pi_ref[0] = pi
    for n in range(1, SUBLANES):
        pr, pi = pr * ar - pi * ai, pr * ai + pi * ar
        pr_ref[n] = pr
        pi_ref[n] = pi


def _block_diag(m):
    nb, ng, r, c = m.shape
    eye = jnp.eye(ng, dtype=m.dtype)
    return jnp.einsum("kgrc,gh->kgrhc", m, eye).reshape(nb, ng * r, ng * c)


def s5_prepare(lam_re, lam_im, log_dt, b_re, b_im, c_re, c_im):
    g, p, c = b_re.shape
    rep = lambda a: jnp.repeat(a, c, axis=0)
    tr = lambda b: jnp.transpose(b, (0, 2, 1)).reshape(g * c, p)
    shp = jax.ShapeDtypeStruct((g * c, p), F32)
    pw = jax.ShapeDtypeStruct((SUBLANES, g * c, p), F32)
    pr, pi, bbr, bbi, abr, abi = pl.pallas_call(
        _s5_prep_kernel, out_shape=[pw, pw, shp, shp, shp, shp], name="s5_prepare",
    )(rep(lam_re), rep(lam_im), jnp.broadcast_to(rep(log_dt[:, None]), (g * c, p)),
      tr(b_re), tr(b_im))
    pr = pr[:, ::c, :].reshape(SUBLANES, g * p)
    pi = pi[:, ::c, :].reshape(SUBLANES, g * p)
    gb = g // SSM_BLOCKS
    b_in = lambda m: _block_diag(m.reshape(SSM_BLOCKS, gb, c, p)).astype(BF16)
    c_out = lambda m: _block_diag(
        jnp.transpose(m, (0, 2, 1)).reshape(SSM_BLOCKS, gb, p, c)).astype(BF16)
    two = lambda m, am: jnp.concatenate([b_in(m), b_in(am)], axis=1)
    return pr, pi, b_in(bbr), b_in(bbi), two(bbr, abr), two(bbi, abi), c_out(c_re), c_out(c_im)


def _s5_finish(y, u, dskip_ref, wglu_ref, bglu_ref):
    y = y + dskip_ref[...] * u
    y = jax.nn.gelu(y)
    return y * _sigmoid(_dot(y.astype(BF16), wglu_ref[...]) + bglu_ref[...])


def _s5_seq_kernel(u_ref, pr_ref, pi_ref, lvr_ref, lvi_ref, bre_ref, bim_ref, cre_ref, cim_ref,
                   dskip_ref, wglu_ref, bglu_ref, y_ref, hr_ref, hi_ref,
                   xr_scr, xi_scr, y_scr, *, tc):
    @pl.when(pl.program_id(1) == 0)
    def _():
        hr_ref[...] = jnp.zeros_like(hr_ref)
        hi_ref[...] = jnp.zeros_like(hi_ref)

    u = u_ref[0]
    ub = u.astype(BF16)
    first = lax.broadcasted_iota(jnp.int32, (tc, 1), 0) % SUBLANES == 0
    ub_prev = jnp.where(first, 0.0, pltpu.roll(u, 1, 0)).astype(BF16)
    nb = SSM_BLOCK_STATES
    last = SUBLANES - 1
    bc = lambda h: jnp.broadcast_to(h[last:last + 1, :], h.shape)
    block = lambda k: slice(k * nb, (k + 1) * nb)

    def project_in(k):
        cols = slice(k * MXU_DIM, (k + 1) * MXU_DIM)
        uk = jnp.concatenate([ub[:, cols], ub_prev[:, cols]], axis=1)
        xr_scr[k] = _dot(uk, bre_ref[k])
        xi_scr[k] = _dot(uk, bim_ref[k])

    def scan(k):
        sl = block(k)
        pr = pr_ref[:, sl]
        pi = pi_ref[:, sl]
        cr, ci = bc(hr_ref[0, :, sl]), bc(hi_ref[0, :, sl])
        for g in range(tc // SUBLANES):
            rows = slice(g * SUBLANES, (g + 1) * SUBLANES)
            xr = xr_scr[k, rows, :]
            xi = xi_scr[k, rows, :]
            for lv in range(1, 3):
                sr = pltpu.roll(xr, 1 << lv, 0)
                si = pltpu.roll(xi, 1 << lv, 0)
                ar = lvr_ref[lv, :, sl]
                ai = lvi_ref[lv, :, sl]
                xr, xi = xr + ar * sr - ai * si, xi + ar * si + ai * sr
            hr = xr + pr * cr - pi * ci
            hi = xi + pr * ci + pi * cr
            xr_scr[k, rows, :] = hr
            xi_scr[k, rows, :] = hi
            cr, ci = bc(hr), bc(hi)
        hr_ref[0, :, sl] = cr
        hi_ref[0, :, sl] = ci

    def project_out(k):
        y_scr[:, k * MXU_DIM:(k + 1) * MXU_DIM] = (
            _dot(xr_scr[k].astype(BF16), cre_ref[k]) - _dot(xi_scr[k].astype(BF16), cim_ref[k]))

    project_in(0)
    for k in range(SSM_BLOCKS):
        if k + 1 < SSM_BLOCKS:
            project_in(k + 1)
        scan(k)
        project_out(k)

    y_ref[0] = _s5_finish(y_scr[...], u, dskip_ref, wglu_ref, bglu_ref)


def s5_mix_seq(u, prep, d_skip, w_glu, b_glu, tc_cap=512):
    b, t, w = u.shape
    tc = _row_tile(t, tc_cap)
    pr, pi, _, _, bre, bim, cre, cim = prep
    rows = jnp.arange(SUBLANES)[:, None]
    lvr = jnp.stack([jnp.where(rows >= (1 << l), pr[(1 << l) - 1][None], 0.0) for l in range(3)])
    lvi = jnp.stack([jnp.where(rows >= (1 << l), pi[(1 << l) - 1][None], 0.0) for l in range(3)])
    full = lambda a: pl.BlockSpec(a.shape, lambda i, j: (0,) * a.ndim)
    consts = [pr, pi, lvr, lvi, bre, bim, cre, cim, d_skip.reshape(1, w), w_glu, b_glu.reshape(1, w)]
    y, hr, hi = pl.pallas_call(
        functools.partial(_s5_seq_kernel, tc=tc),
        grid=(b, t // tc),
        in_specs=[pl.BlockSpec((1, tc, w), lambda i, j: (i, j, 0))] + [full(a) for a in consts],
        out_specs=[pl.BlockSpec((1, tc, w), lambda i, j: (i, j, 0)),
                   pl.BlockSpec((1, SUBLANES, SSM_WIDTH), lambda i, j: (i, 0, 0)),
                   pl.BlockSpec((1, SUBLANES, SSM_WIDTH), lambda i, j: (i, 0, 0))],
        out_shape=[jax.ShapeDtypeStruct((b, t, w), F32),
                   jax.ShapeDtypeStruct((b, SUBLANES, SSM_WIDTH), F32),
                   jax.ShapeDtypeStruct((b, SUBLANES, SSM_WIDTH), F32)],
        scratch_shapes=[pltpu.VMEM((SSM_BLOCKS, tc, SSM_BLOCK_STATES), F32),
                        pltpu.VMEM((SSM_BLOCKS, tc, SSM_BLOCK_STATES), F32),
                        pltpu.VMEM((tc, w), F32)],
        compiler_params=_params(("parallel", "arbitrary")),
        name="s5_mix_seq",
    )(u, *consts)
    return y, hr[:, SUBLANES - 1], hi[:, SUBLANES - 1]


def _s5_step_kernel(u_ref, h0r_ref, h0i_ref, ar_ref, ai_ref, bre_ref, bim_ref, cre_ref, cim_ref,
                    dskip_ref, wglu_ref, bglu_ref, y_ref, hr_ref, hi_ref, y_scr):
    u = u_ref[...]
    ub = u.astype(BF16)
    nb = SSM_BLOCK_STATES
    for k in range(SSM_BLOCKS):
        sl = slice(k * nb, (k + 1) * nb)
        uk = ub[:, k * MXU_DIM:(k + 1) * MXU_DIM]
        ar, ai = ar_ref[:, sl], ai_ref[:, sl]
        h0r, h0i = h0r_ref[:, sl], h0i_ref[:, sl]
        hr = _dot(uk, bre_ref[k]) + ar * h0r - ai * h0i
        hi = _dot(uk, bim_ref[k]) + ar * h0i + ai * h0r
        hr_ref[:, sl] = hr
        hi_ref[:, sl] = hi
        y_scr[:, k * MXU_DIM:(k + 1) * MXU_DIM] = (
            _dot(hr.astype(BF16), cre_ref[k]) - _dot(hi.astype(BF16), cim_ref[k]))
    y_ref[...] = _s5_finish(y_scr[...], u, dskip_ref, wglu_ref, bglu_ref)


def s5_mix_step(u, h0r, h0i, prep, d_skip, w_glu, b_glu):
    s, w = u.shape
    pr, pi, bre, bim, _, _, cre, cim = prep
    st = jax.ShapeDtypeStruct((s, SSM_WIDTH), F32)
    return pl.pallas_call(
        _s5_step_kernel,
        out_shape=[jax.ShapeDtypeStruct((s, w), F32), st, st],
        scratch_shapes=[pltpu.VMEM((s, w), F32)],
        name="s5_mix_step",
    )(u, h0r, h0i, pr[0:1], pi[0:1], bre, bim, cre, cim, d_skip.reshape(1, w), w_glu,
      b_glu.reshape(1, w))


FOX_TQ = 2 * KV_TILE
FOX_STRIP = MXU_DIM


def _fox_seq_kernel(q_ref, kaug_ref, vt_ref, tri_ref, o_ref, s_scr, p_scr, acc_scr, m_scr, l_scr,
                    *, tq, tk):
    i = pl.program_id(2)
    qt = q_ref[0].astype(F32)
    row = lax.broadcasted_iota(jnp.int32, (LANES, 1), 0)
    qa = []
    for h in range(2):
        own = (row >= h * HEAD_DIM) & (row < (h + 1) * HEAD_DIM)
        head = 2 * pl.program_id(1) + h
        ones = (row >= head * BIAS_PIECES) & (row < (head + 1) * BIAS_PIECES)
        qa.append(jnp.concatenate(
            [jnp.where(own, qt, 0.0), jnp.broadcast_to(jnp.where(ones, 1.0, 0.0), qt.shape)],
            axis=0).astype(BF16))
    acc_scr[...] = jnp.zeros_like(acc_scr)
    m_scr[...] = jnp.full_like(m_scr, NEG_INF)
    l_scr[...] = jnp.zeros_like(l_scr)
    n_sub = tq // FOX_STRIP
    strips = [(h, qs) for h in range(2) for qs in range(n_sub)]
    lanes = lambda qs: slice(qs * FOX_STRIP, (qs + 1) * FOX_STRIP)
    ones_rows = jnp.ones((2 * SUBLANES, tk), BF16)

    def visible_keys(qs, c):
        if c is None:
            return tk, 0
        first = qs * FOX_STRIP - c * tk
        if first < 0:
            return 0, 0
        return (tk, 0) if first >= tk else (first, FOX_STRIP)

    def scores(j, slot, c=None):
        ka = kaug_ref[0, pl.ds(pl.multiple_of(j * tk, tk), tk), :]
        for n, (h, qs) in enumerate(strips):
            if sum(visible_keys(qs, c)):
                s_scr[slot, n] = _dot(ka, qa[h][:, lanes(qs)])

    def absorb(j, slot, c=None):
        alphas, n_keys = [], []
        for n, (h, qs) in enumerate(strips):
            n_full_keys, n_tri = visible_keys(qs, c)
            n_keys.append(n_full_keys + n_tri)
            if not n_keys[n]:
                alphas.append(None)
                continue
            m = m_scr[n]
            full_rows = slice(0, n_full_keys)
            tri_rows = slice(n_full_keys, n_full_keys + n_tri)
            m_new = m
            if n_tri:
                tri = s_scr[slot, n, tri_rows, :] + tri_ref[...]
                m_new = jnp.maximum(m_new, jnp.max(tri, axis=0, keepdims=True))
            if n_full_keys:
                m_new = jnp.maximum(
                    m_new, jnp.max(s_scr[slot, n, full_rows, :], axis=0, keepdims=True))
                p_scr[n, full_rows, :] = jnp.exp2(
                    s_scr[slot, n, full_rows, :] - m_new).astype(BF16)
            if n_tri:
                p_scr[n, tri_rows, :] = jnp.exp2(tri - m_new).astype(BF16)
            m_scr[n] = m_new
            alphas.append(jnp.exp2(m - m_new))
        for n, (h, qs) in enumerate(strips):
            if not n_keys[n]:
                continue
            vt = jnp.concatenate(
                [vt_ref[0, j, h * HEAD_DIM:(h + 1) * HEAD_DIM, :n_keys[n]],
                 ones_rows[:, :n_keys[n]]], axis=0)
            pv = _dot(vt, p_scr[n, :n_keys[n], :])
            acc_scr[h, :, lanes(qs)] = alphas[n] * acc_scr[h, :, lanes(qs)] + pv[:HEAD_DIM]
            l_scr[n] = alphas[n] * l_scr[n] + pv[HEAD_DIM:HEAD_DIM + 1]

    n_before = 2 * i
    scores(0, 0)

    def pair(jj, _):
        j = 2 * jj
        scores(j + 1, 1)
        absorb(j, 0)
        scores(j + 2, 0)
        absorb(j + 1, 1)
        return 0

    lax.fori_loop(0, i, pair, 0)
    if tq == tk:
        absorb(0, 0, c=0)
    else:
        scores(n_before + 1, 1, c=1)
        absorb(n_before, 0, c=0)
        absorb(n_before + 1, 1, c=1)

    l_head = lambda h: jnp.concatenate(
        [l_scr[h * n_sub + qs] for qs in range(n_sub)], axis=1)
    ot = jnp.concatenate([acc_scr[0] / l_head(0), acc_scr[1] / l_head(1)], axis=0)
    o_ref[0] = jnp.transpose(ot)


def fox_attention_seq(qt, kaug, vtb):
    b, w, t = qt.shape
    n_chunks, tk = vtb.shape[1], vtb.shape[3]
    tq = _row_tile(t, FOX_TQ)
    assert tq in (tk, 2 * tk) and n_chunks * tk == t and tk % FOX_STRIP == 0
    n_strips = 2 * tq // FOX_STRIP
    idx = jnp.arange(FOX_STRIP)
    tri = jnp.where(idx[:, None] <= idx[None, :], 0.0, NEG_INF).astype(F32)
    return pl.pallas_call(
        functools.partial(_fox_seq_kernel, tq=tq, tk=tk),
        grid=(b, HEAD_PAIRS, t // tq),
        in_specs=[pl.BlockSpec((1, LANES, tq), lambda bi, p, i: (bi, p, i)),
                  pl.BlockSpec((1, t, KAUG_WIDTH), lambda bi, p, i: (bi, 0, p)),
                  pl.BlockSpec((1, n_chunks, LANES, tk), lambda bi, p, i: (bi, 0, p, 0)),
                  pl.BlockSpec((FOX_STRIP, FOX_STRIP), lambda bi, p, i: (0, 0))],
        out_specs=pl.BlockSpec((1, tq, LANES), lambda bi, p, i: (bi, i, p)),
        out_shape=jax.ShapeDtypeStruct((b, t, w), F32),
        scratch_shapes=[pltpu.VMEM((2, n_strips, tk, FOX_STRIP), F32),
                        pltpu.VMEM((n_strips, tk, FOX_STRIP), BF16),
                        pltpu.VMEM((2, HEAD_DIM, tq), F32),
                        pltpu.VMEM((n_strips, 1, FOX_STRIP), F32),
                        pltpu.VMEM((n_strips, 1, FOX_STRIP), F32)],
        compiler_params=_params(("parallel", "parallel", "arbitrary")),
        name="fox_attention_seq",
    )(qt, kaug, vtb, tri)


DEC_PAGES = 16


def _fox_dec_kernel(pt_ref, q_ref, kn_ref, vn_ref, lfn_ref, *rest, page, n_pages):
    kts = rest[:n_pages]
    vts = rest[n_pages:2 * n_pages]
    lfs = rest[2 * n_pages:3 * n_pages]
    tri_ref = rest[3 * n_pages]
    o_ref = rest[3 * n_pages + 1]
    m_scr, l_scr, csum_scr, acc_scr, qcol_scr = rest[3 * n_pages + 2:]
    g = pl.program_id(1)
    hp = FOX_HEADS
    qrow = q_ref[0] * ATTN_SCALE

    @pl.when(g == 0)
    def _():
        m_scr[...] = jnp.full_like(m_scr, NEG_INF)
        l_scr[...] = jnp.zeros_like(l_scr)
        csum_scr[...] = jnp.zeros_like(csum_scr)
        acc_scr[...] = jnp.zeros_like(acc_scr)
        qcol_scr[...] = jnp.transpose(jnp.broadcast_to(qrow, (page, SEQ_WIDTH)))

    tri = tri_ref[...]
    base = csum_scr[...]
    head_row = lax.broadcasted_iota(jnp.int32, (hp, page), 0)
    s_parts = []
    for i in range(n_pages):
        hi, mid, lo = _split3(lfs[i][0])
        cum = (_dot(hi, tri) + _dot(mid, tri)) + _dot(lo, tri) + base
        base = jnp.broadcast_to(cum[:, page - 1:page], cum.shape)
        qk = jnp.zeros((hp, page), F32)
        for h in range(FOX_HEADS):
            rows = slice(h * HEAD_DIM, (h + 1) * HEAD_DIM)
            r = jnp.sum(kts[i][0, rows, :] * qcol_scr[rows, :], axis=0, keepdims=True)
            qk = jnp.where(head_row == h, r, qk)
        s_parts.append(qk - cum)
    csum_scr[...] = base
    s = jnp.concatenate(s_parts, axis=1)

    m_old = m_scr[...]
    m_new = jnp.maximum(m_old, jnp.max(s, axis=-1, keepdims=True))
    alpha = jnp.exp(m_old - m_new)
    p = jnp.exp(s - m_new)
    l_scr[...] = alpha * l_scr[...] + jnp.sum(p, axis=-1, keepdims=True)
    m_scr[...] = m_new
    for h in range(FOX_HEADS):
        rows = slice(h * HEAD_DIM, (h + 1) * HEAD_DIM)
        acc = acc_scr[rows, :] * alpha[h:h + 1, :]
        for i in range(n_pages):
            acc = acc + vts[i][0, rows, :] * p[h:h + 1, i * page:(i + 1) * page]
        acc_scr[rows, :] = acc

    @pl.when(g == pl.num_programs(1) - 1)
    def _():
        lane = lax.broadcasted_iota(jnp.int32, (hp, SEQ_WIDTH), 1)
        head = lax.broadcasted_iota(jnp.int32, (hp, SEQ_WIDTH), 0)
        own = (lane >= head * HEAD_DIM) & (lane < (head + 1) * HEAD_DIM)
        spread = lambda col: jnp.sum(jnp.where(own, col, 0.0), axis=0, keepdims=True)
        c_new = csum_scr[:, 0:1] + lfn_ref[0]
        s_new = jnp.sum(jnp.where(own, qrow * kn_ref[0], 0.0), axis=-1, keepdims=True) - c_new
        m_fin = jnp.maximum(m_scr[...], s_new)
        a_fin = jnp.exp(m_scr[...] - m_fin)
        p_new = jnp.exp(s_new - m_fin)
        l_fin = a_fin * l_scr[...] + p_new
        acc_row = jnp.sum(jnp.transpose(acc_scr[...]), axis=0, keepdims=True)
        o_ref[0] = (spread(a_fin) * acc_row + spread(p_new) * vn_ref[0]) / spread(l_fin)


def fox_attention_decode(q, k_new, v_new, logf_new, cache_k, cache_v, cache_logf, page_table):
    s, w = q.shape
    n_pool, page = cache_k.shape[:2]
    assert page == LANES
    pages_per_seq = page_table.shape[1]
    n_pages = min(DEC_PAGES, pages_per_seq)
    assert pages_per_seq % n_pages == 0
    hp = FOX_HEADS
    ckt = jnp.transpose(cache_k, (0, 2, 3, 1)).reshape(n_pool, w, page)
    cvt = jnp.transpose(cache_v, (0, 2, 3, 1)).reshape(n_pool, w, page)
    clf = jnp.transpose(cache_logf, (0, 2, 1))
    lfn = logf_new.reshape(s, hp, 1)
    tri = jnp.triu(jnp.ones((page, page), F32)).astype(BF16)
    row = pl.BlockSpec((1, 1, w), lambda b, g, pt: (b, 0, 0))

    def paged(shape, i):
        return pl.BlockSpec((1,) + shape, lambda b, g, pt, i=i: (pt[b, g * n_pages + i], 0, 0))

    grid_spec = pltpu.PrefetchScalarGridSpec(
        num_scalar_prefetch=1,
        grid=(s, pages_per_seq // n_pages),
        in_specs=([row, row, row, pl.BlockSpec((1, hp, 1), lambda b, g, pt: (b, 0, 0))]
                  + [paged((w, page), i) for i in range(n_pages)]
                  + [paged((w, page), i) for i in range(n_pages)]
                  + [paged((hp, page), i) for i in range(n_pages)]
                  + [pl.BlockSpec((page, page), lambda b, g, pt: (0, 0))]),
        out_specs=row,
        scratch_shapes=[pltpu.VMEM((hp, 1), F32), pltpu.VMEM((hp, 1), F32),
                        pltpu.VMEM((hp, page), F32), pltpu.VMEM((w, page), F32),
                        pltpu.VMEM((w, page), F32)],
    )
    out = pl.pallas_call(
        functools.partial(_fox_dec_kernel, page=page, n_pages=n_pages),
        grid_spec=grid_spec,
        out_shape=jax.ShapeDtypeStruct((s, 1, w), F32),
        compiler_params=_params(("parallel", "arbitrary"), VMEM_LIMIT),
        name="fox_attention_decode",
    )(page_table, q.reshape(s, 1, w), k_new.reshape(s, 1, w), v_new.reshape(s, 1, w), lfn,
      *([ckt] * n_pages), *([cvt] * n_pages), *([clf] * n_pages), tri)
    return out.reshape(s, w)


def _trunk(x, mem_k, mem_v, conv_prev, ssm_state, fox_attend, p, s5_prep, sequential):
    assert N_A == 1
    b, t, d = x.shape
    m = b * t
    new_conv, ssm_out, kv = [], None, None
    for l in range(DEPTH):
        if l == N_A and sequential:
            kv = shared_kv_proj_seq(x, p["kv_norm"], p["w_kv"], p["b_f"],
                                    p["norm_mix_pre"][l], p["w_in"][l])
            z_seq, q_mem = kv[5], kv[6]
        else:
            if l == N_A:
                kv = shared_kv_proj_step(x.reshape(m, d), p["kv_norm"], p["w_kv"], p["b_f"])
            z_seq, q_mem = norm_linear(x.reshape(m, d), p["norm_mix_pre"][l], p["w_in"][l],
                                       (SEQ_WIDTH, MEM_WIDTH))
        if l < N_A:
            if sequential:
                seq_out, hr, hi = s5_mix_seq(z_seq.reshape(b, t, SEQ_WIDTH), s5_prep[l],
                                             p["d_skip"][l], p["w_glu"][l], p["b_glu"][l])
            else:
                seq_out, hr, hi = s5_mix_step(z_seq, ssm_state[0][l], ssm_state[1][l], s5_prep[l],
                                              p["d_skip"][l], p["w_glu"][l], p["b_glu"][l])
            ssm_out = (hr.reshape(b, SSM_GROUPS, SSM_STATE), hi.reshape(b, SSM_GROUPS, SSM_STATE))
        else:
            seq_out = fox_attend(z_seq, kv)
        ffn_args = (p["norm_ffn_pre"][l], p["norm_ffn_post"][l], p["w_up"][l], p["conv_w"][l],
                    p["conv_b"][l], p["w_down"][l])
        if sequential:
            x3, cp = layer_tail_seq(x, seq_out.reshape(b, t, SEQ_WIDTH),
                                    q_mem.reshape(b, t, MEM_WIDTH), mem_k[l], mem_v[l],
                                    p["w_out"][l], p["norm_mix_post"][l], conv_prev[l], *ffn_args)
        else:
            mem_out = mem_attention(q_mem.reshape(b, t, MEM_WIDTH), mem_k, mem_v, l)
            x2 = mix_out(x.reshape(m, d), seq_out, mem_out.reshape(m, MEM_WIDTH),
                         p["w_out"][l], p["norm_mix_post"][l])
            x3, u_new = conv_ffn_step(x2, conv_prev[l], *ffn_args)
            cp = jnp.stack([conv_prev[l][:, 1], u_new], axis=1)
        new_conv.append(cp)
        x = x3.reshape(b, t, d)
    return x, ssm_out, jnp.stack(new_conv), kv


def kernel(x_prompt, x_sample, state_ssm_re, state_ssm_im, cache_k, cache_v, cache_logf,
           cache_mem_k, cache_mem_v, state_ffn_conv, page_table, mem_prompt,
           w_in, w_out, norm_mix_pre, norm_mix_post, norm_ffn_pre, norm_ffn_post,
           mem_norm, w_mem_kv, lam_re, lam_im, log_dt, b_re, b_im, c_re, c_im, d_skip,
           w_glu, b_glu, kv_norm, w_kv, b_f, w_up, conv_w, conv_b, w_down):
    per_layer_bf16 = lambda w: [w[l].astype(BF16) for l in range(w.shape[0])]
    p = dict(w_in=per_layer_bf16(w_in), w_out=per_layer_bf16(w_out), norm_mix_pre=norm_mix_pre,
             norm_mix_post=norm_mix_post, norm_ffn_pre=norm_ffn_pre, norm_ffn_post=norm_ffn_post,
             d_skip=d_skip, w_glu=per_layer_bf16(w_glu), b_glu=b_glu, kv_norm=kv_norm, w_kv=w_kv,
             b_f=b_f, w_up=per_layer_bf16(w_up), conv_w=conv_w, conv_b=conv_b,
             w_down=per_layer_bf16(w_down))
    s5_prep = [s5_prepare(lam_re[l], lam_im[l], log_dt[l], b_re[l], b_im[l], c_re[l], c_im[l])
               for l in range(N_A)]

    bp, tp, d = x_prompt.shape
    n_mem = mem_prompt.shape[1]
    mem_pairs = [mem_kv_proj(mem_prompt, mem_norm[l], w_mem_kv[l]) for l in range(DEPTH)]
    p_mem_kt = [mkt for mkt, _ in mem_pairs]
    p_mem_vt = [mvt for _, mvt in mem_pairs]
    zeros_conv = jnp.zeros((DEPTH, bp, CONV_W - 1, 2 * D_FF), F32)

    def fox_prompt(qt, kv):
        return fox_attention_seq(qt, kv[2], kv[3])

    y_prompt, p_ssm, p_conv, p_kv = _trunk(x_prompt, p_mem_kt, p_mem_vt, zeros_conv, None,
                                           fox_prompt, p, s5_prep, sequential=True)
    untr = lambda a, n: jnp.transpose(a.reshape(a.shape[0], n, HEAD_DIM, a.shape[2]), (0, 3, 1, 2))
    mem5 = lambda ms: jnp.stack([untr(a, MEM_HEADS) for a in ms])

    bs = x_sample.shape[0]
    tr_mem = lambda a: jnp.transpose(a, (0, 1, 3, 4, 2)).reshape(DEPTH, bs, MEM_WIDTH, n_mem)
    s_mem_kt = tr_mem(cache_mem_k)
    s_mem_vt = tr_mem(cache_mem_v)
    ssm0 = (state_ssm_re.reshape(N_A, bs, SSM_WIDTH), state_ssm_im.reshape(N_A, bs, SSM_WIDTH))

    def fox_sample(q, kv):
        k, v, logf = kv
        return fox_attention_decode(q, k, v, logf, cache_k, cache_v, cache_logf, page_table)

    y_sample, s_ssm, s_conv, s_kv = _trunk(x_sample, s_mem_kt, s_mem_vt, state_ffn_conv, ssm0,
                                           fox_sample, p, s5_prep, sequential=False)
    head4 = lambda a: a.reshape(bs, 1, FOX_HEADS, HEAD_DIM)

    return (y_prompt, y_sample, p_ssm[0][None], p_ssm[1][None],
            untr(p_kv[0], FOX_HEADS), untr(p_kv[1], FOX_HEADS), p_kv[4],
            mem5(p_mem_kt), mem5(p_mem_vt), p_conv,
            s_ssm[0][None], s_ssm[1][None],
            head4(s_kv[0]), head4(s_kv[1]), s_kv[2].reshape(bs, 1, FOX_HEADS), s_conv)
```

```python
import functools
import math

import jax
import jax.numpy as jnp
from jax import lax
from jax.experimental import pallas as pl
from jax.experimental.pallas import tpu as pltpu

F32 = jnp.float32
BF16 = jnp.bfloat16

D_MODEL = 1024
DEPTH = 2
N_A = DEPTH // 2
HEAD_DIM = 64
MEM_HEADS = 4
MEM_WIDTH = MEM_HEADS * HEAD_DIM
SEQ_WIDTH = D_MODEL - MEM_WIDTH
SSM_GROUP = 16
SSM_GROUPS = SEQ_WIDTH // SSM_GROUP
SSM_STATE = 64
SSM_WIDTH = SSM_GROUPS * SSM_STATE
FOX_HEADS = SEQ_WIDTH // HEAD_DIM
D_FF = (11 * D_MODEL) // 4
CONV_W = 3
EPS = 1e-6
NEG_INF = -1e30
EIG_CLIP = -1e-4
ATTN_SCALE = HEAD_DIM ** -0.5
LOG2E = math.log2(math.e)

LANES = 128
SUBLANES = 8
MXU_DIM = 256
VMEM_BYTES_V7X = 64 * 1024 * 1024
VMEM_LIMIT = (VMEM_BYTES_V7X * 7) // 8

SSM_BLOCKS = SEQ_WIDTH // MXU_DIM
SSM_BLOCK_STATES = SSM_WIDTH // SSM_BLOCKS
HEAD_PAIRS = FOX_HEADS // 2


def _params(semantics, vmem=None):
    return pltpu.CompilerParams(dimension_semantics=semantics, vmem_limit_bytes=vmem)


def _row_tile(m, cap):
    t = min(m, cap)
    assert m % t == 0, (m, t)
    return t


def _rms(x, g):
    return x * lax.rsqrt(jnp.mean(x * x, axis=-1, keepdims=True) + EPS) * g


def _sigmoid(x):
    return 1.0 / (1.0 + jnp.exp(-x))


def _log_sigmoid(x):
    return -(jnp.maximum(-x, 0.0) + jnp.log1p(jnp.exp(-jnp.abs(x))))


def _split3(x):
    hi = x.astype(BF16)
    r1 = x - hi.astype(F32)
    mid = r1.astype(BF16)
    lo = (r1 - mid.astype(F32)).astype(BF16)
    return hi, mid, lo


def _dot(a, b):
    return jnp.dot(a, b, preferred_element_type=F32)


def _dot_nt(a, b):
    return lax.dot_general(a, b, (((1,), (1,)), ((), ())), preferred_element_type=F32)


def _norm_linear_kernel(x_ref, g_ref, w_ref, *out_refs, splits):
    h = _rms(x_ref[...], g_ref[...]).astype(BF16)
    c0 = 0
    for o_ref, n in zip(out_refs, splits):
        o_ref[...] = _dot(h, w_ref[:, c0:c0 + n])
        c0 += n


def norm_linear(x, g, w, splits, tm_cap=512):
    m, d = x.shape
    tm = _row_tile(m, tm_cap)
    n_tot = sum(splits)
    assert w.shape == (d, n_tot)
    return pl.pallas_call(
        functools.partial(_norm_linear_kernel, splits=tuple(splits)),
        grid=(m // tm,),
        in_specs=[pl.BlockSpec((tm, d), lambda i: (i, 0)),
                  pl.BlockSpec((1, d), lambda i: (0, 0)),
                  pl.BlockSpec((d, n_tot), lambda i: (0, 0))],
        out_specs=[pl.BlockSpec((tm, n), lambda i: (i, 0)) for n in splits],
        out_shape=[jax.ShapeDtypeStruct((m, n), F32) for n in splits],
        compiler_params=_params(("parallel",)),
        name="norm_linear",
    )(x, g.reshape(1, d), w)


KV_TILE = 512
KAUG_WIDTH = 2 * LANES
BIAS_PIECES = 3


def _kv_weights(w_kv, b_f):
    wk = w_kv[:, :SEQ_WIDTH].astype(BF16)
    wf = jnp.pad(w_kv[:, 2 * SEQ_WIDTH:], ((0, 0), (0, LANES - FOX_HEADS))).astype(BF16)
    bf = jnp.pad(b_f, (0, LANES - FOX_HEADS)).reshape(1, LANES)
    return wk, wf, bf


def _kv_seq_kernel(x_ref, g_ref, wk_ref, wvt_ref, wf_ref, bf_ref, tri_ref, place_ref,
                   gin_ref, wqt_ref, wm_ref,
                   kt_ref, vt_ref, kaug_ref, vtb_ref, lf_ref, qt_ref, qm_ref, carry_ref):
    @pl.when(pl.program_id(1) == 0)
    def _():
        carry_ref[...] = jnp.zeros_like(carry_ref)

    x = x_ref[0]
    xhat = x * lax.rsqrt(jnp.mean(x * x, axis=-1, keepdims=True) + EPS)
    hin = (xhat * gin_ref[...]).astype(BF16)
    qt_ref[0] = (_dot_nt(wqt_ref[...], hin) * (ATTN_SCALE * LOG2E)).astype(BF16)
    qm_ref[0] = _dot(hin, wm_ref[...])
    h = (xhat * g_ref[...]).astype(BF16)
    k = _dot(h, wk_ref[...])
    kt_ref[0] = jnp.transpose(k)
    vt = _dot_nt(wvt_ref[...], h)
    vt_ref[0] = vt
    vtb_ref[0, 0] = vt.astype(BF16)
    logf = _log_sigmoid(_dot(h, wf_ref[...]) + bf_ref[...])
    lf_ref[0] = logf[:, :FOX_HEADS]
    c3 = _dot(tri_ref[...], jnp.concatenate(_split3(logf), axis=1))
    cum = (c3[:, :LANES] + c3[:, LANES:2 * LANES]) + c3[:, 2 * LANES:] + carry_ref[...]
    carry_ref[...] = cum[cum.shape[0] - 1:, :]
    pieces = jnp.concatenate(_split3(cum * -LOG2E), axis=1)
    bias = _dot(pieces, place_ref[...]).astype(BF16)
    kb = k.astype(BF16)
    for p in range(HEAD_PAIRS):
        kaug_ref[0, :, p * KAUG_WIDTH:p * KAUG_WIDTH + LANES] = kb[:, p * LANES:(p + 1) * LANES]
        kaug_ref[0, :, p * KAUG_WIDTH + LANES:(p + 1) * KAUG_WIDTH] = bias


def shared_kv_proj_seq(x, g, w_kv, b_f, g_in, w_in):
    b, t, d = x.shape
    wqt = jnp.transpose(w_in[:, :SEQ_WIDTH])
    wm = w_in[:, SEQ_WIDTH:]
    tm = _row_tile(t, KV_TILE)
    wk, wf, bf = _kv_weights(w_kv, b_f)
    wvt = jnp.transpose(w_kv)[SEQ_WIDTH:2 * SEQ_WIDTH].astype(BF16)
    tri = jnp.tril(jnp.ones((tm, tm), F32)).astype(BF16)
    hh = jnp.arange(FOX_HEADS)
    place = jnp.zeros((BIAS_PIECES * LANES, LANES), F32)
    for j in range(BIAS_PIECES):
        place = place.at[LANES * j + hh, BIAS_PIECES * hh + j].set(1.0)
    place = place.astype(BF16)
    tok = lambda n: pl.BlockSpec((1, tm, n), lambda i, j: (i, j, 0))
    tr = pl.BlockSpec((1, SEQ_WIDTH, tm), lambda i, j: (i, 0, j))
    full = lambda s: pl.BlockSpec(s, lambda i, j: (0,) * len(s))
    return pl.pallas_call(
        _kv_seq_kernel,
        grid=(b, t // tm),
        in_specs=[tok(d), full((1, d)), full((d, SEQ_WIDTH)), full((SEQ_WIDTH, d)),
                  full((d, LANES)), full((1, LANES)), full((tm, tm)),
                  full((BIAS_PIECES * LANES, LANES)),
                  full((1, d)), full((SEQ_WIDTH, d)), full((d, MEM_WIDTH))],
        out_specs=[tr, tr, tok(HEAD_PAIRS * KAUG_WIDTH),
                   pl.BlockSpec((1, 1, SEQ_WIDTH, tm), lambda i, j: (i, j, 0, 0)),
                   tok(FOX_HEADS), tr, tok(MEM_WIDTH)],
        out_shape=[jax.ShapeDtypeStruct((b, SEQ_WIDTH, t), F32),
                   jax.ShapeDtypeStruct((b, SEQ_WIDTH, t), F32),
                   jax.ShapeDtypeStruct((b, t, HEAD_PAIRS * KAUG_WIDTH), BF16),
                   jax.ShapeDtypeStruct((b, t // tm, SEQ_WIDTH, tm), BF16),
                   jax.ShapeDtypeStruct((b, t, FOX_HEADS), F32),
                   jax.ShapeDtypeStruct((b, SEQ_WIDTH, t), BF16),
                   jax.ShapeDtypeStruct((b, t, MEM_WIDTH), F32)],
        scratch_shapes=[pltpu.VMEM((1, LANES), F32)],
        compiler_params=_params(("parallel", "arbitrary")),
        name="shared_kv_proj_seq",
    )(x, g.reshape(1, d), wk, wvt, wf, bf, tri, place, g_in.reshape(1, d), wqt, wm)


def _kv_step_kernel(x_ref, g_ref, wk_ref, wv_ref, wf_ref, bf_ref, k_ref, v_ref, lf_ref):
    h = _rms(x_ref[...], g_ref[...]).astype(BF16)
    k_ref[...] = _dot(h, wk_ref[...])
    v_ref[...] = _dot(h, wv_ref[...])
    lf_ref[...] = _log_sigmoid(_dot(h, wf_ref[...]) + bf_ref[...])[:, :FOX_HEADS]


def shared_kv_proj_step(x, g, w_kv, b_f):
    s, d = x.shape
    wk, wf, bf = _kv_weights(w_kv, b_f)
    wv = w_kv[:, SEQ_WIDTH:2 * SEQ_WIDTH].astype(BF16)
    kv = jax.ShapeDtypeStruct((s, SEQ_WIDTH), F32)
    return pl.pallas_call(
        _kv_step_kernel,
        out_shape=[kv, kv, jax.ShapeDtypeStruct((s, FOX_HEADS), F32)],
        name="shared_kv_proj_step",
    )(x, g.reshape(1, d), wk, wv, wf, bf)


def _mem_kv_kernel(x_ref, g_ref, wt_ref, kt_ref, vt_ref):
    h = _rms(x_ref[0], g_ref[...]).astype(BF16)
    kvt = _dot_nt(wt_ref[...], h)
    kt_ref[0] = kvt[:MEM_WIDTH]
    vt_ref[0] = kvt[MEM_WIDTH:]


def mem_kv_proj(mem, g, w):
    b, n_mem, d = mem.shape
    out = jax.ShapeDtypeStruct((b, MEM_WIDTH, n_mem), F32)
    blk = pl.BlockSpec((1, MEM_WIDTH, n_mem), lambda i: (i, 0, 0))
    return pl.pallas_call(
        _mem_kv_kernel,
        grid=(b,),
        in_specs=[pl.BlockSpec((1, n_mem, d), lambda i: (i, 0, 0)),
                  pl.BlockSpec((1, d), lambda i: (0, 0)),
                  pl.BlockSpec((2 * MEM_WIDTH, d), lambda i: (0, 0))],
        out_specs=[blk, blk],
        out_shape=[out, out],
        compiler_params=_params(("parallel",)),
        name="mem_kv_proj",
    )(mem, g.reshape(1, d), jnp.transpose(w).astype(BF16))


def _mem_attn_tile(q, mkt, mvt):
    q = q * ATTN_SCALE
    mkt = mkt.astype(BF16)
    mvt = mvt.astype(BF16)
    lane = lax.broadcasted_iota(jnp.int32, (1, MEM_WIDTH), 1)
    out = jnp.zeros(q.shape, F32)
    for h in range(MEM_HEADS):
        in_head = (lane >= h * HEAD_DIM) & (lane < (h + 1) * HEAD_DIM)
        s = _dot(jnp.where(in_head, q, 0.0).astype(BF16), mkt)
        p = jnp.exp(s - jnp.max(s, axis=-1, keepdims=True))
        p = p / jnp.sum(p, axis=-1, keepdims=True)
        out = out + jnp.where(in_head, _dot_nt(p.astype(BF16), mvt), 0.0)
    return out


def _mem_attn_row(q, mkt, mvt):
    lane = lax.broadcasted_iota(jnp.int32, (MEM_HEADS, MEM_WIDTH), 1)
    head = lax.broadcasted_iota(jnp.int32, (MEM_HEADS, MEM_WIDTH), 0)
    own = (lane >= head * HEAD_DIM) & (lane < (head + 1) * HEAD_DIM)
    s = _dot(jnp.where(own, q * ATTN_SCALE, 0.0).astype(BF16), mkt.astype(BF16))
    p = jnp.exp(s - jnp.max(s, axis=-1, keepdims=True))
    p = p / jnp.sum(p, axis=-1, keepdims=True)
    o = _dot_nt(p.astype(BF16), mvt.astype(BF16))
    return jnp.sum(jnp.where(own, o, 0.0), axis=0, keepdims=True)


def _mem_attn_kernel(q_ref, mkt_ref, mvt_ref, o_ref):
    attend = _mem_attn_row if q_ref.shape[1] == 1 else _mem_attn_tile
    for s in range(q_ref.shape[0]):
        o_ref[s] = attend(q_ref[s], mkt_ref[s], mvt_ref[s])


def mem_attention(q_mem, mkt, mvt, layer, seqs_per_step=SUBLANES):
    b, t, w = q_mem.shape
    n_mem = mkt.shape[3]
    bs = _row_tile(b, seqs_per_step)
    mem = pl.BlockSpec((None, bs, w, n_mem), lambda i: (layer, i, 0, 0))
    return pl.pallas_call(
        _mem_attn_kernel,
        grid=(b // bs,),
        in_specs=[pl.BlockSpec((bs, t, w), lambda i: (i, 0, 0)), mem, mem],
        out_specs=pl.BlockSpec((bs, t, w), lambda i: (i, 0, 0)),
        out_shape=jax.ShapeDtypeStruct((b, t, w), F32),
        compiler_params=_params(("parallel",)),
        name="mem_attention",
    )(q_mem, mkt, mvt)


def _mix_out_tile(x, seq, mem, w_ref, g):
    o = (_dot(seq.astype(BF16), w_ref[:SEQ_WIDTH, :]) + _dot(mem.astype(BF16), w_ref[SEQ_WIDTH:, :]))
    return x + _rms(o, g)


def _mix_out_kernel(x_ref, s_ref, m_ref, w_ref, g_ref, o_ref):
    o_ref[...] = _mix_out_tile(x_ref[...], s_ref[...], m_ref[...], w_ref, g_ref[...])


def mix_out(x, seq_out, mem_out, w_out, g, tm_cap=512):
    m, d = x.shape
    tm = _row_tile(m, tm_cap)
    row = lambda n: pl.BlockSpec((tm, n), lambda i: (i, 0))
    return pl.pallas_call(
        _mix_out_kernel,
        grid=(m // tm,),
        in_specs=[row(d), row(SEQ_WIDTH), row(MEM_WIDTH),
                  pl.BlockSpec((d, d), lambda i: (0, 0)),
                  pl.BlockSpec((1, d), lambda i: (0, 0))],
        out_specs=row(d),
        out_shape=jax.ShapeDtypeStruct((m, d), F32),
        compiler_params=_params(("parallel",)),
        name="mix_out",
    )(x, seq_out, mem_out, w_out, g.reshape(1, d))


FFN_CHUNK = MXU_DIM
FFN_STEP_CHUNK = D_FF // 2
assert FFN_STEP_CHUNK % LANES == 0


def _tail_seq_kernel(x_ref, seq_ref, qm_ref, mkt_ref, mvt_ref, wout_ref, gmix_ref, prev_ref,
                     gpre_ref, gpost_ref, wup_ref, cw_ref, cb_ref, wdn_ref,
                     o_ref, conv_ref, h_scr, carry_scr, *, tm):
    @pl.when(pl.program_id(1) == 0)
    def _():
        carry_scr[...] = prev_ref[0]

    mem = _mem_attn_tile(qm_ref[0], mkt_ref[0], mvt_ref[0])
    x = _mix_out_tile(x_ref[0], seq_ref[0], mem, wout_ref, gmix_ref[...])
    xn = _rms(x, gpre_ref[...]).astype(BF16)
    row = lax.broadcasted_iota(jnp.int32, (SUBLANES, 1), 0)

    def up(col):
        return _dot(xn, wup_ref[:, col:col + FFN_CHUNK])

    def conv(u, col):
        c0 = carry_scr[0:1, col:col + FFN_CHUNK]
        c1 = carry_scr[1:2, col:col + FFN_CHUNK]
        u1 = pltpu.roll(u, 1, 0)
        u2 = pltpu.roll(u, 2, 0)
        u1 = jnp.concatenate([jnp.where(row == 0, c1, u1[:SUBLANES]), u1[SUBLANES:]], axis=0)
        u2 = jnp.concatenate(
            [jnp.where(row == 0, c0, jnp.where(row == 1, c1, u2[:SUBLANES])), u2[SUBLANES:]],
            axis=0)
        carry_scr[:, col:col + FFN_CHUNK] = u[tm - 2:, :]
        w = cw_ref[:, col:col + FFN_CHUNK]
        return u2 * w[0:1] + u1 * w[1:2] + u * w[2:3] + cb_ref[:, col:col + FFN_CHUNK]

    for c in range(D_FF // FFN_CHUNK):
        gate = conv(up(c * FFN_CHUNK), c * FFN_CHUNK)
        val = conv(up(D_FF + c * FFN_CHUNK), D_FF + c * FFN_CHUNK)
        h_scr[:, c * FFN_CHUNK:(c + 1) * FFN_CHUNK] = (gate * _sigmoid(gate) * val).astype(BF16)

    f = _dot(h_scr[...], wdn_ref[...])
    o_ref[0] = x + _rms(f, gpost_ref[...])
    conv_ref[0] = carry_scr[...]


def layer_tail_seq(x, seq_out, q_mem, mkt, mvt, w_out, g_mix, prev, g_pre, g_post, w_up, conv_w,
                   conv_b, w_down, layer, tm_cap=512):
    b, t, d = x.shape
    n_mem = mkt.shape[2]
    tm = _row_tile(t, tm_cap)
    assert tm >= 2 * SUBLANES
    f2 = 2 * D_FF
    full = lambda s: pl.BlockSpec(s, lambda i, j: (0,) * len(s), pipeline_mode=pl.Buffered(1))
    of_layer = lambda s: pl.BlockSpec((None,) + s, lambda i, j: (layer,) + (0,) * len(s),
                                      pipeline_mode=pl.Buffered(1))
    tok = lambda n: pl.BlockSpec((1, tm, n), lambda i, j: (i, j, 0))
    per_seq = lambda r, c: pl.BlockSpec((1, r, c), lambda i, j: (i, 0, 0))
    return pl.pallas_call(
        functools.partial(_tail_seq_kernel, tm=tm),
        grid=(b, t // tm),
        in_specs=[tok(d), tok(SEQ_WIDTH), tok(MEM_WIDTH),
                  per_seq(MEM_WIDTH, n_mem), per_seq(MEM_WIDTH, n_mem),
                  full((d, d)), full((1, d)), per_seq(CONV_W - 1, f2),
                  full((1, d)), full((1, d)), of_layer((d, f2)), full((CONV_W, f2)), full((1, f2)),
                  of_layer((D_FF, d))],
        out_specs=[tok(d), per_seq(CONV_W - 1, f2)],
        out_shape=[jax.ShapeDtypeStruct((b, t, d), F32),
                   jax.ShapeDtypeStruct((b, CONV_W - 1, f2), F32)],
        scratch_shapes=[pltpu.VMEM((tm, D_FF), BF16), pltpu.VMEM((CONV_W - 1, f2), F32)],
        compiler_params=_params(("parallel", "arbitrary"), VMEM_LIMIT),
        name="layer_tail_seq",
    )(x, seq_out, q_mem, mkt, mvt, w_out, g_mix.reshape(1, d), prev, g_pre.reshape(1, d),
      g_post.reshape(1, d), w_up, conv_w, conv_b.reshape(1, f2), w_down)


def _ffn_step_kernel(x_ref, p0g_ref, p0v_ref, p1g_ref, p1v_ref, gpre_ref, gpost_ref,
                     wg_ref, wv_ref, cwg_ref, cwv_ref, cbg_ref, cbv_ref, wdn_ref,
                     o_ref, ug_ref, uv_ref, acc_scr):
    c = pl.program_id(0)

    @pl.when(c == 0)
    def _():
        acc_scr[...] = jnp.zeros_like(acc_scr)

    x = x_ref[...]
    xn = _rms(x, gpre_ref[...]).astype(BF16)

    def conv(w_ref, p0_ref, p1_ref, cw_ref, cb_ref, u_ref):
        u = _dot(xn, w_ref[...])
        u_ref[...] = u
        w = cw_ref[...]
        return p0_ref[...] * w[0:1] + p1_ref[...] * w[1:2] + u * w[2:3] + cb_ref[...]

    gate = conv(wg_ref, p0g_ref, p1g_ref, cwg_ref, cbg_ref, ug_ref)
    val = conv(wv_ref, p0v_ref, p1v_ref, cwv_ref, cbv_ref, uv_ref)
    acc_scr[...] += _dot((gate * _sigmoid(gate) * val).astype(BF16), wdn_ref[...])

    @pl.when(c == pl.num_programs(0) - 1)
    def _():
        o_ref[...] = x + _rms(acc_scr[...], gpost_ref[...])


def conv_ffn_step(x, prev, g_pre, g_post, w_up, conv_w, conv_b, w_down, layer):
    s, d = x.shape
    f2 = 2 * D_FF
    fc = FFN_STEP_CHUNK
    nc = D_FF // fc
    prev2 = prev.reshape(s, (CONV_W - 1) * f2)
    cb = conv_b.reshape(1, f2)
    const = lambda shp: pl.BlockSpec(shp, lambda c: (0, 0))
    col = lambda rows, off: pl.BlockSpec((rows, fc), lambda c, off=off: (0, c + off))
    up_col = lambda off: pl.BlockSpec((None, d, fc), lambda c, off=off: (layer, 0, c + off))
    out, ug, uv = pl.pallas_call(
        _ffn_step_kernel,
        grid=(nc,),
        in_specs=[const((s, d)),
                  col(s, 0), col(s, nc), col(s, 2 * nc), col(s, 3 * nc),
                  const((1, d)), const((1, d)),
                  up_col(0), up_col(nc),
                  col(CONV_W, 0), col(CONV_W, nc),
                  col(1, 0), col(1, nc),
                  pl.BlockSpec((None, fc, d), lambda c: (layer, c, 0))],
        out_specs=[const((s, d)), col(s, 0), col(s, 0)],
        out_shape=[jax.ShapeDtypeStruct((s, d), F32),
                   jax.ShapeDtypeStruct((s, D_FF), F32),
                   jax.ShapeDtypeStruct((s, D_FF), F32)],
        scratch_shapes=[pltpu.VMEM((s, d), F32)],
        compiler_params=_params(("arbitrary",)),
        name="conv_ffn_step",
    )(x, prev2, prev2, prev2, prev2, g_pre.reshape(1, d), g_post.reshape(1, d),
      w_up, w_up, conv_w, conv_w, cb, cb, w_down)
    return out, jnp.concatenate([ug, uv], axis=-1)


def _s5_prep_kernel(lr_ref, li_ref, ldt_ref, br_ref, bi_ref, crt_ref, cit_ref, tile_b_ref,
                    tile_c_ref, pr_ref, pi_ref, bre_ref, bim_ref, bre2_ref, bim2_ref, cre_ref,
                    cim_ref):
    lr = jnp.minimum(lr_ref[...], EIG_CLIP)
    li = li_ref[...]
    dt = jnp.exp(ldt_ref[...])
    mag = jnp.exp(lr * dt)
    ar = mag * jnp.cos(li * dt)
    ai = mag * jnp.sin(li * dt)
    den = lr * lr + li * li
    nr = ar - 1.0
    fr = (nr * lr + ai * li) / den
    fi = (ai * lr - nr * li) / den
    br = br_ref[...]
    bi = bi_ref[...]
    bbr = fr * br - fi * bi
    bbi = fr * bi + fi * br

    def block_diag(rows, tile_ref, group_rows, group_cols):
        wide = _dot(rows.astype(BF16), tile_ref[...])
        r = lax.broadcasted_iota(jnp.int32, wide.shape, 0) // group_rows
        c = lax.broadcasted_iota(jnp.int32, wide.shape, 1) // group_cols
        return jnp.where(r == c, wide, 0.0).astype(BF16)

    nrow = MXU_DIM
    for k in range(SSM_BLOCKS):
        rows = slice(k * nrow, (k + 1) * nrow)
        b_maps = [(bbr, bre_ref, bre2_ref, 0), (bbi, bim_ref, bim2_ref, 0),
                  (ar * bbr - ai * bbi, None, bre2_ref, nrow),
                  (ar * bbi + ai * bbr, None, bim2_ref, nrow)]
        for m, plain_ref, stacked_ref, off in b_maps:
            blk = block_diag(m[rows], tile_b_ref, SSM_GROUP, SSM_STATE)
            if plain_ref is not None:
                plain_ref[k] = blk
            stacked_ref[k, off:off + nrow, :] = blk
        crow = slice(k * SSM_BLOCK_STATES, (k + 1) * SSM_BLOCK_STATES)
        cre_ref[k] = block_diag(crt_ref[crow, :], tile_c_ref, SSM_STATE, SSM_GROUP)
        cim_ref[k] = block_diag(cit_ref[crow, :], tile_c_ref, SSM_STATE, SSM_GROUP)
    pr, pi = ar, ai
    pr_ref[0] = pr
    pi_ref[0] = pi
    for n in range(1, SUBLANES):
        pr, pi = pr * ar - pi * ai, pr * ai + pi * ar
        pr_ref[n] = pr
        pi_ref[n] = pi


def s5_prepare(lam_re, lam_im, log_dt, b_re, b_im, c_re, c_im):
    g, p, c = b_re.shape
    assert (g, p, c) == (SSM_GROUPS, SSM_STATE, SSM_GROUP)
    rep = lambda a: jnp.repeat(a, c, axis=0)
    tr = lambda b: jnp.transpose(b, (0, 2, 1)).reshape(g * c, p)
    trc = lambda m: jnp.transpose(m, (0, 2, 1)).reshape(g * p, c)
    tile_b = (jnp.arange(SSM_BLOCK_STATES)[None, :] % p == jnp.arange(p)[:, None]).astype(BF16)
    tile_c = (jnp.arange(MXU_DIM)[None, :] % c == jnp.arange(c)[:, None]).astype(BF16)
    pw = jax.ShapeDtypeStruct((SUBLANES, g * c, p), F32)
    b1 = jax.ShapeDtypeStruct((SSM_BLOCKS, MXU_DIM, SSM_BLOCK_STATES), BF16)
    b2 = jax.ShapeDtypeStruct((SSM_BLOCKS, 2 * MXU_DIM, SSM_BLOCK_STATES), BF16)
    c1 = jax.ShapeDtypeStruct((SSM_BLOCKS, SSM_BLOCK_STATES, MXU_DIM), BF16)
    pr, pi, bre, bim, bre2, bim2, cre, cim = pl.pallas_call(
        _s5_prep_kernel, out_shape=[pw, pw, b1, b1, b2, b2, c1, c1], name="s5_prepare",
    )(rep(lam_re), rep(lam_im), jnp.broadcast_to(rep(log_dt[:, None]), (g * c, p)),
      tr(b_re), tr(b_im), trc(c_re), trc(c_im), tile_b, tile_c)
    pr = pr[:, ::c, :].reshape(SUBLANES, g * p)
    pi = pi[:, ::c, :].reshape(SUBLANES, g * p)
    return pr, pi, bre, bim, bre2, bim2, cre, cim


def _s5_finish(y, u, dskip_ref, wglu_ref, bglu_ref):
    y = y + dskip_ref[...] * u
    y = jax.nn.gelu(y)
    return y * _sigmoid(_dot(y.astype(BF16), wglu_ref[...]) + bglu_ref[...])


def _s5_seq_kernel(x_ref, gin_ref, win_ref, pr_ref, pi_ref, lvr_ref, lvi_ref, bre_ref, bim_ref,
                   cre_ref, cim_ref, dskip_ref, wglu_ref, bglu_ref, y_ref, qm_ref, hr_ref, hi_ref,
                   xr_scr, xi_scr, y_scr, *, tc):
    @pl.when(pl.program_id(1) == 0)
    def _():
        hr_ref[...] = jnp.zeros_like(hr_ref)
        hi_ref[...] = jnp.zeros_like(hi_ref)

    hin = _rms(x_ref[0], gin_ref[...]).astype(BF16)
    u = _dot(hin, win_ref[:, :SEQ_WIDTH])
    qm_ref[0] = _dot(hin, win_ref[:, SEQ_WIDTH:])
    ub = u.astype(BF16)
    first = lax.broadcasted_iota(jnp.int32, (tc, 1), 0) % SUBLANES == 0
    ub_prev = jnp.where(first, 0.0, pltpu.roll(u, 1, 0)).astype(BF16)
    nb = SSM_BLOCK_STATES
    last = SUBLANES - 1
    bc = lambda h: jnp.broadcast_to(h[last:last + 1, :], h.shape)
    block = lambda k: slice(k * nb, (k + 1) * nb)

    def project_in(k):
        cols = slice(k * MXU_DIM, (k + 1) * MXU_DIM)
        uk = jnp.concatenate([ub[:, cols], ub_prev[:, cols]], axis=1)
        xr_scr[k] = _dot(uk, bre_ref[k])
        xi_scr[k] = _dot(uk, bim_ref[k])

    def scan(k):
        sl = block(k)
        pr = pr_ref[:, sl]
        pi = pi_ref[:, sl]
        cr, ci = bc(hr_ref[0, :, sl]), bc(hi_ref[0, :, sl])
        for g in range(tc // SUBLANES):
            rows = slice(g * SUBLANES, (g + 1) * SUBLANES)
            xr = xr_scr[k, rows, :]
            xi = xi_scr[k, rows, :]
            for lv in range(1, 3):
                sr = pltpu.roll(xr, 1 << lv, 0)
                si = pltpu.roll(xi, 1 << lv, 0)
                ar = lvr_ref[lv, :, sl]
                ai = lvi_ref[lv, :, sl]
                xr, xi = xr + ar * sr - ai * si, xi + ar * si + ai * sr
            hr = xr + pr * cr - pi * ci
            hi = xi + pr * ci + pi * cr
            xr_scr[k, rows, :] = hr
            xi_scr[k, rows, :] = hi
            cr, ci = bc(hr), bc(hi)
        hr_ref[0, :, sl] = cr
        hi_ref[0, :, sl] = ci

    def project_out(k):
        y_scr[:, k * MXU_DIM:(k + 1) * MXU_DIM] = (
            _dot(xr_scr[k].astype(BF16), cre_ref[k]) - _dot(xi_scr[k].astype(BF16), cim_ref[k]))

    project_in(0)
    for k in range(SSM_BLOCKS):
        if k + 1 < SSM_BLOCKS:
            project_in(k + 1)
        scan(k)
        project_out(k)

    y_ref[0] = _s5_finish(y_scr[...], u, dskip_ref, wglu_ref, bglu_ref)


def s5_mix_seq(x, g_in, w_in, prep, d_skip, w_glu, b_glu, tc_cap=512):
    b, t, d = x.shape
    w = SEQ_WIDTH
    tc = _row_tile(t, tc_cap)
    pr, pi, _, _, bre, bim, cre, cim = prep
    rows = jnp.arange(SUBLANES)[:, None]
    lvr = jnp.stack([jnp.where(rows >= (1 << l), pr[(1 << l) - 1][None], 0.0) for l in range(3)])
    lvi = jnp.stack([jnp.where(rows >= (1 << l), pi[(1 << l) - 1][None], 0.0) for l in range(3)])
    full = lambda a: pl.BlockSpec(a.shape, lambda i, j: (0,) * a.ndim)
    consts = [g_in.reshape(1, d), w_in, pr, pi, lvr, lvi, bre, bim, cre, cim, d_skip.reshape(1, w),
              w_glu, b_glu.reshape(1, w)]
    tok = lambda n: pl.BlockSpec((1, tc, n), lambda i, j: (i, j, 0))
    y, qm, hr, hi = pl.pallas_call(
        functools.partial(_s5_seq_kernel, tc=tc),
        grid=(b, t // tc),
        in_specs=[tok(d)] + [full(a) for a in consts],
        out_specs=[tok(w), tok(MEM_WIDTH),
                   pl.BlockSpec((1, SUBLANES, SSM_WIDTH), lambda i, j: (i, 0, 0)),
                   pl.BlockSpec((1, SUBLANES, SSM_WIDTH), lambda i, j: (i, 0, 0))],
        out_shape=[jax.ShapeDtypeStruct((b, t, w), F32),
                   jax.ShapeDtypeStruct((b, t, MEM_WIDTH), F32),
                   jax.ShapeDtypeStruct((b, SUBLANES, SSM_WIDTH), F32),
                   jax.ShapeDtypeStruct((b, SUBLANES, SSM_WIDTH), F32)],
        scratch_shapes=[pltpu.VMEM((SSM_BLOCKS, tc, SSM_BLOCK_STATES), F32),
                        pltpu.VMEM((SSM_BLOCKS, tc, SSM_BLOCK_STATES), F32),
                        pltpu.VMEM((tc, w), F32)],
        compiler_params=_params(("parallel", "arbitrary")),
        name="s5_mix_seq",
    )(x, *consts)
    return y, qm, hr[:, SUBLANES - 1], hi[:, SUBLANES - 1]


def _s5_step_kernel(u_ref, h0r_ref, h0i_ref, ar_ref, ai_ref, bre_ref, bim_ref, cre_ref, cim_ref,
                    dskip_ref, wglu_ref, bglu_ref, y_ref, hr_ref, hi_ref, y_scr):
    u = u_ref[...]
    ub = u.astype(BF16)
    nb = SSM_BLOCK_STATES
    for k in range(SSM_BLOCKS):
        sl = slice(k * nb, (k + 1) * nb)
        uk = ub[:, k * MXU_DIM:(k + 1) * MXU_DIM]
        ar, ai = ar_ref[:, sl], ai_ref[:, sl]
        h0r, h0i = h0r_ref[:, sl], h0i_ref[:, sl]
        hr = _dot(uk, bre_ref[k]) + ar * h0r - ai * h0i
        hi = _dot(uk, bim_ref[k]) + ar * h0i + ai * h0r
        hr_ref[:, sl] = hr
        hi_ref[:, sl] = hi
        y_scr[:, k * MXU_DIM:(k + 1) * MXU_DIM] = (
            _dot(hr.astype(BF16), cre_ref[k]) - _dot(hi.astype(BF16), cim_ref[k]))
    y_ref[...] = _s5_finish(y_scr[...], u, dskip_ref, wglu_ref, bglu_ref)


def s5_mix_step(u, h0r, h0i, prep, d_skip, w_glu, b_glu):
    s, w = u.shape
    pr, pi, bre, bim, _, _, cre, cim = prep
    st = jax.ShapeDtypeStruct((s, SSM_WIDTH), F32)
    return pl.pallas_call(
        _s5_step_kernel,
        out_shape=[jax.ShapeDtypeStruct((s, w), F32), st, st],
        scratch_shapes=[pltpu.VMEM((s, w), F32)],
        name="s5_mix_step",
    )(u, h0r, h0i, pr[0:1], pi[0:1], bre, bim, cre, cim, d_skip.reshape(1, w), w_glu,
      b_glu.reshape(1, w))


FOX_TQ = 2 * KV_TILE
FOX_STRIP = MXU_DIM


def _fox_seq_kernel(q_ref, kaug_ref, vt_ref, tri_ref, o_ref, s_scr, p_scr, acc_scr, m_scr, l_scr,
                    *, tq, tk):
    i = pl.program_id(2)
    qt = q_ref[0].astype(F32)
    row = lax.broadcasted_iota(jnp.int32, (LANES, 1), 0)
    qa = []
    for h in range(2):
        own = (row >= h * HEAD_DIM) & (row < (h + 1) * HEAD_DIM)
        head = 2 * pl.program_id(1) + h
        ones = (row >= head * BIAS_PIECES) & (row < (head + 1) * BIAS_PIECES)
        qa.append(jnp.concatenate(
            [jnp.where(own, qt, 0.0), jnp.broadcast_to(jnp.where(ones, 1.0, 0.0), qt.shape)],
            axis=0).astype(BF16))
    acc_scr[...] = jnp.zeros_like(acc_scr)
    m_scr[...] = jnp.full_like(m_scr, NEG_INF)
    l_scr[...] = jnp.zeros_like(l_scr)
    n_sub = tq // FOX_STRIP
    strips = [(h, qs) for h in range(2) for qs in range(n_sub)]
    lanes = lambda qs: slice(qs * FOX_STRIP, (qs + 1) * FOX_STRIP)
    ones_rows = jnp.ones((2 * SUBLANES, tk), BF16)

    def visible_keys(qs, c):
        if c is None:
            return tk, 0
        first = qs * FOX_STRIP - c * tk
        if first < 0:
            return 0, 0
        return (tk, 0) if first >= tk else (first, FOX_STRIP)

    def scores(j, slot, c=None):
        ka = kaug_ref[0, pl.ds(pl.multiple_of(j * tk, tk), tk), :]
        for n, (h, qs) in enumerate(strips):
            if sum(visible_keys(qs, c)):
                s_scr[slot, n] = _dot(ka, qa[h][:, lanes(qs)])

    def absorb(j, slot, c=None):
        alphas, n_keys = [], []
        for n, (h, qs) in enumerate(strips):
            n_full_keys, n_tri = visible_keys(qs, c)
            n_keys.append(n_full_keys + n_tri)
            if not n_keys[n]:
                alphas.append(None)
                continue
            m = m_scr[n]
            full_rows = slice(0, n_full_keys)
            tri_rows = slice(n_full_keys, n_full_keys + n_tri)
            m_new = m
            if n_tri:
                tri = s_scr[slot, n, tri_rows, :] + tri_ref[...]
                m_new = jnp.maximum(m_new, jnp.max(tri, axis=0, keepdims=True))
            if n_full_keys:
                m_new = jnp.maximum(
                    m_new, jnp.max(s_scr[slot, n, full_rows, :], axis=0, keepdims=True))
                p_scr[n, full_rows, :] = jnp.exp2(
                    s_scr[slot, n, full_rows, :] - m_new).astype(BF16)
            if n_tri:
                p_scr[n, tri_rows, :] = jnp.exp2(tri - m_new).astype(BF16)
            m_scr[n] = m_new
            alphas.append(jnp.exp2(m - m_new))
        for n, (h, qs) in enumerate(strips):
            if not n_keys[n]:
                continue
            vt = jnp.concatenate(
                [vt_ref[0, j, h * HEAD_DIM:(h + 1) * HEAD_DIM, :n_keys[n]],
                 ones_rows[:, :n_keys[n]]], axis=0)
            pv = _dot(vt, p_scr[n, :n_keys[n], :])
            acc_scr[h, :, lanes(qs)] = alphas[n] * acc_scr[h, :, lanes(qs)] + pv[:HEAD_DIM]
            l_scr[n] = alphas[n] * l_scr[n] + pv[HEAD_DIM:HEAD_DIM + 1]

    n_before = 2 * i
    scores(0, 0)

    def pair(jj, _):
        j = 2 * jj
        scores(j + 1, 1)
        absorb(j, 0)
        scores(j + 2, 0)
        absorb(j + 1, 1)
        return 0

    lax.fori_loop(0, i, pair, 0)
    if tq == tk:
        absorb(0, 0, c=0)
    else:
        scores(n_before + 1, 1, c=1)
        absorb(n_before, 0, c=0)
        absorb(n_before + 1, 1, c=1)

    l_head = lambda h: jnp.concatenate(
        [l_scr[h * n_sub + qs] for qs in range(n_sub)], axis=1)
    ot = jnp.concatenate([acc_scr[0] / l_head(0), acc_scr[1] / l_head(1)], axis=0)
    o_ref[0] = jnp.transpose(ot)


def fox_attention_seq(qt, kaug, vtb):
    b, w, t = qt.shape
    n_chunks, tk = vtb.shape[1], vtb.shape[3]
    tq = _row_tile(t, FOX_TQ)
    assert tq in (tk, 2 * tk) and n_chunks * tk == t and tk % FOX_STRIP == 0
    n_strips = 2 * tq // FOX_STRIP
    idx = jnp.arange(FOX_STRIP)
    tri = jnp.where(idx[:, None] <= idx[None, :], 0.0, NEG_INF).astype(F32)
    return pl.pallas_call(
        functools.partial(_fox_seq_kernel, tq=tq, tk=tk),
        grid=(b, HEAD_PAIRS, t // tq),
        in_specs=[pl.BlockSpec((1, LANES, tq), lambda bi, p, i: (bi, p, i)),
                  pl.BlockSpec((1, t, KAUG_WIDTH), lambda bi, p, i: (bi, 0, p)),
                  pl.BlockSpec((1, n_chunks, LANES, tk), lambda bi, p, i: (bi, 0, p, 0)),
                  pl.BlockSpec((FOX_STRIP, FOX_STRIP), lambda bi, p, i: (0, 0))],
        out_specs=pl.BlockSpec((1, tq, LANES), lambda bi, p, i: (bi, i, p)),
        out_shape=jax.ShapeDtypeStruct((b, t, w), F32),
        scratch_shapes=[pltpu.VMEM((2, n_strips, tk, FOX_STRIP), F32),
                        pltpu.VMEM((n_strips, tk, FOX_STRIP), BF16),
                        pltpu.VMEM((2, HEAD_DIM, tq), F32),
                        pltpu.VMEM((n_strips, 1, FOX_STRIP), F32),
                        pltpu.VMEM((n_strips, 1, FOX_STRIP), F32)],
        compiler_params=_params(("parallel", "parallel", "arbitrary")),
        name="fox_attention_seq",
    )(qt, kaug, vtb, tri)


DEC_PAGES = 16


def _fox_dec_kernel(pt_ref, q_ref, kn_ref, vn_ref, lfn_ref, *rest, page, n_pages):
    kts = rest[:n_pages]
    vts = rest[n_pages:2 * n_pages]
    lfs = rest[2 * n_pages:3 * n_pages]
    tri_ref = rest[3 * n_pages]
    o_ref = rest[3 * n_pages + 1]
    m_scr, l_scr, csum_scr, acc_scr, qcol_scr = rest[3 * n_pages + 2:]
    g = pl.program_id(1)
    hp = FOX_HEADS
    qrow = q_ref[0] * ATTN_SCALE

    @pl.when(g == 0)
    def _():
        m_scr[...] = jnp.full_like(m_scr, NEG_INF)
        l_scr[...] = jnp.zeros_like(l_scr)
        csum_scr[...] = jnp.zeros_like(csum_scr)
        acc_scr[...] = jnp.zeros_like(acc_scr)
        qcol_scr[...] = jnp.transpose(jnp.broadcast_to(qrow, (page, SEQ_WIDTH)))

    tri = tri_ref[...]
    base = csum_scr[...]
    head_row = lax.broadcasted_iota(jnp.int32, (hp, page), 0)
    s_parts = []
    for i in range(n_pages):
        hi, mid, lo = _split3(lfs[i][0])
        cum = (_dot(hi, tri) + _dot(mid, tri)) + _dot(lo, tri) + base
        base = jnp.broadcast_to(cum[:, page - 1:page], cum.shape)
        qk = jnp.zeros((hp, page), F32)
        for h in range(FOX_HEADS):
            rows = slice(h * HEAD_DIM, (h + 1) * HEAD_DIM)
            r = jnp.sum(kts[i][0, rows, :] * qcol_scr[rows, :], axis=0, keepdims=True)
            qk = jnp.where(head_row == h, r, qk)
        s_parts.append(qk - cum)
    csum_scr[...] = base
    s = jnp.concatenate(s_parts, axis=1)

    m_old = m_scr[...]
    m_new = jnp.maximum(m_old, jnp.max(s, axis=-1, keepdims=True))
    alpha = jnp.exp(m_old - m_new)
    p = jnp.exp(s - m_new)
    l_scr[...] = alpha * l_scr[...] + jnp.sum(p, axis=-1, keepdims=True)
    m_scr[...] = m_new
    for h in range(FOX_HEADS):
        rows = slice(h * HEAD_DIM, (h + 1) * HEAD_DIM)
        acc = acc_scr[rows, :] * alpha[h:h + 1, :]
        for i in range(n_pages):
            acc = acc + vts[i][0, rows, :] * p[h:h + 1, i * page:(i + 1) * page]
        acc_scr[rows, :] = acc

    @pl.when(g == pl.num_programs(1) - 1)
    def _():
        lane = lax.broadcasted_iota(jnp.int32, (hp, SEQ_WIDTH), 1)
        head = lax.broadcasted_iota(jnp.int32, (hp, SEQ_WIDTH), 0)
        own = (lane >= head * HEAD_DIM) & (lane < (head + 1) * HEAD_DIM)
        spread = lambda col: jnp.sum(jnp.where(own, col, 0.0), axis=0, keepdims=True)
        c_new = csum_scr[:, 0:1] + lfn_ref[0]
        s_new = jnp.sum(jnp.where(own, qrow * kn_ref[0], 0.0), axis=-1, keepdims=True) - c_new
        m_fin = jnp.maximum(m_scr[...], s_new)
        a_fin = jnp.exp(m_scr[...] - m_fin)
        p_new = jnp.exp(s_new - m_fin)
        l_fin = a_fin * l_scr[...] + p_new
        acc_row = jnp.sum(jnp.transpose(acc_scr[...]), axis=0, keepdims=True)
        o_ref[0] = (spread(a_fin) * acc_row + spread(p_new) * vn_ref[0]) / spread(l_fin)


def fox_attention_decode(q, k_new, v_new, logf_new, cache_k, cache_v, cache_logf, page_table):
    s, w = q.shape
    n_pool, page = cache_k.shape[:2]
    assert page == LANES
    pages_per_seq = page_table.shape[1]
    n_pages = min(DEC_PAGES, pages_per_seq)
    assert pages_per_seq % n_pages == 0
    hp = FOX_HEADS
    ckt = jnp.transpose(cache_k, (0, 2, 3, 1)).reshape(n_pool, w, page)
    cvt = jnp.transpose(cache_v, (0, 2, 3, 1)).reshape(n_pool, w, page)
    clf = jnp.transpose(cache_logf, (0, 2, 1))
    lfn = logf_new.reshape(s, hp, 1)
    tri = jnp.triu(jnp.ones((page, page), F32)).astype(BF16)
    row = pl.BlockSpec((1, 1, w), lambda b, g, pt: (b, 0, 0))

    def paged(shape, i):
        return pl.BlockSpec((1,) + shape, lambda b, g, pt, i=i: (pt[b, g * n_pages + i], 0, 0))

    grid_spec = pltpu.PrefetchScalarGridSpec(
        num_scalar_prefetch=1,
        grid=(s, pages_per_seq // n_pages),
        in_specs=([row, row, row, pl.BlockSpec((1, hp, 1), lambda b, g, pt: (b, 0, 0))]
                  + [paged((w, page), i) for i in range(n_pages)]
                  + [paged((w, page), i) for i in range(n_pages)]
                  + [paged((hp, page), i) for i in range(n_pages)]
                  + [pl.BlockSpec((page, page), lambda b, g, pt: (0, 0))]),
        out_specs=row,
        scratch_shapes=[pltpu.VMEM((hp, 1), F32), pltpu.VMEM((hp, 1), F32),
                        pltpu.VMEM((hp, page), F32), pltpu.VMEM((w, page), F32),
                        pltpu.VMEM((w, page), F32)],
    )
    out = pl.pallas_call(
        functools.partial(_fox_dec_kernel, page=page, n_pages=n_pages),
        grid_spec=grid_spec,
        out_shape=jax.ShapeDtypeStruct((s, 1, w), F32),
        compiler_params=_params(("parallel", "arbitrary"), VMEM_LIMIT),
        name="fox_attention_decode",
    )(page_table, q.reshape(s, 1, w), k_new.reshape(s, 1, w), v_new.reshape(s, 1, w), lfn,
      *([ckt] * n_pages), *([cvt] * n_pages), *([clf] * n_pages), tri)
    return out.reshape(s, w)


def _trunk(x, mem_k, mem_v, conv_prev, ssm_state, fox_attend, p, s5_prep, sequential):
    assert N_A == 1
    b, t, d = x.shape
    m = b * t
    new_conv, ssm_out, kv = [], None, None
    for l in range(DEPTH):
        if sequential and l == N_A:
            kv = shared_kv_proj_seq(x, p["kv_norm"], p["w_kv"], p["b_f"],
                                    p["norm_mix_pre"][l], p["w_in"][l])
            z_seq, q_mem = kv[5], kv[6]
        elif not sequential:
            if l == N_A:
                kv = shared_kv_proj_step(x.reshape(m, d), p["kv_norm"], p["w_kv"], p["b_f"])
            z_seq, q_mem = norm_linear(x.reshape(m, d), p["norm_mix_pre"][l], p["w_in"][l],
                                       (SEQ_WIDTH, MEM_WIDTH))
        if l < N_A:
            if sequential:
                seq_out, q_mem, hr, hi = s5_mix_seq(x, p["norm_mix_pre"][l], p["w_in"][l],
                                                    s5_prep[l], p["d_skip"][l], p["w_glu"][l],
                                                    p["b_glu"][l])
            else:
                seq_out, hr, hi = s5_mix_step(z_seq, ssm_state[0][l], ssm_state[1][l], s5_prep[l],
                                              p["d_skip"][l], p["w_glu"][l], p["b_glu"][l])
            ssm_out = (hr.reshape(b, SSM_GROUPS, SSM_STATE), hi.reshape(b, SSM_GROUPS, SSM_STATE))
        else:
            seq_out = fox_attend(z_seq, kv)
        ffn_args = (p["norm_ffn_pre"][l], p["norm_ffn_post"][l], p["w_up"], p["conv_w"][l],
                    p["conv_b"][l], p["w_down"], l)
        if sequential:
            x3, cp = layer_tail_seq(x, seq_out.reshape(b, t, SEQ_WIDTH),
                                    q_mem.reshape(b, t, MEM_WIDTH), mem_k[l], mem_v[l],
                                    p["w_out"][l], p["norm_mix_post"][l], conv_prev[l], *ffn_args)
        else:
            mem_out = mem_attention(q_mem.reshape(b, t, MEM_WIDTH), mem_k, mem_v, l)
            x2 = mix_out(x.reshape(m, d), seq_out, mem_out.reshape(m, MEM_WIDTH),
                         p["w_out"][l], p["norm_mix_post"][l])
            x3, u_new = conv_ffn_step(x2, conv_prev[l], *ffn_args)
            cp = jnp.stack([conv_prev[l][:, 1], u_new], axis=1)
        new_conv.append(cp)
        x = x3.reshape(b, t, d)
    return x, ssm_out, jnp.stack(new_conv), kv


def kernel(x_prompt, x_sample, state_ssm_re, state_ssm_im, cache_k, cache_v, cache_logf,
           cache_mem_k, cache_mem_v, state_ffn_conv, page_table, mem_prompt,
           w_in, w_out, norm_mix_pre, norm_mix_post, norm_ffn_pre, norm_ffn_post,
           mem_norm, w_mem_kv, lam_re, lam_im, log_dt, b_re, b_im, c_re, c_im, d_skip,
           w_glu, b_glu, kv_norm, w_kv, b_f, w_up, conv_w, conv_b, w_down):
    per_layer_bf16 = lambda w: [w[l].astype(BF16) for l in range(w.shape[0])]
    p = dict(w_in=per_layer_bf16(w_in), w_out=per_layer_bf16(w_out), norm_mix_pre=norm_mix_pre,
             norm_mix_post=norm_mix_post, norm_ffn_pre=norm_ffn_pre, norm_ffn_post=norm_ffn_post,
             d_skip=d_skip, w_glu=per_layer_bf16(w_glu), b_glu=b_glu, kv_norm=kv_norm, w_kv=w_kv,
             b_f=b_f, w_up=w_up.astype(BF16), conv_w=conv_w, conv_b=conv_b,
             w_down=w_down.astype(BF16))
    s5_prep = [s5_prepare(lam_re[l], lam_im[l], log_dt[l], b_re[l], b_im[l], c_re[l], c_im[l])
               for l in range(N_A)]

    bp, tp, d = x_prompt.shape
    n_mem = mem_prompt.shape[1]
    mem_pairs = [mem_kv_proj(mem_prompt, mem_norm[l], w_mem_kv[l]) for l in range(DEPTH)]
    p_mem_kt = [mkt for mkt, _ in mem_pairs]
    p_mem_vt = [mvt for _, mvt in mem_pairs]
    zeros_conv = jnp.zeros((DEPTH, bp, CONV_W - 1, 2 * D_FF), F32)

    def fox_prompt(qt, kv):
        return fox_attention_seq(qt, kv[2], kv[3])

    y_prompt, p_ssm, p_conv, p_kv = _trunk(x_prompt, p_mem_kt, p_mem_vt, zeros_conv, None,
                                           fox_prompt, p, s5_prep, sequential=True)
    untr = lambda a, n: jnp.transpose(a.reshape(a.shape[0], n, HEAD_DIM, a.shape[2]), (0, 3, 1, 2))
    mem5 = lambda ms: jnp.stack([untr(a, MEM_HEADS) for a in ms])

    bs = x_sample.shape[0]
    tr_mem = lambda a: jnp.transpose(a, (0, 1, 3, 4, 2)).reshape(DEPTH, bs, MEM_WIDTH, n_mem)
    s_mem_kt = tr_mem(cache_mem_k)
    s_mem_vt = tr_mem(cache_mem_v)
    ssm0 = (state_ssm_re.reshape(N_A, bs, SSM_WIDTH), state_ssm_im.reshape(N_A, bs, SSM_WIDTH))

    def fox_sample(q, kv):
        k, v, logf = kv
        return fox_attention_decode(q, k, v, logf, cache_k, cache_v, cache_logf, page_table)

    y_sample, s_ssm, s_conv, s_kv = _trunk(x_sample, s_mem_kt, s_mem_vt, state_ffn_conv, ssm0,
                                           fox_sample, p, s5_prep, sequential=False)
    head4 = lambda a: a.reshape(bs, 1, FOX_HEADS, HEAD_DIM)

    return (y_prompt, y_sample, p_ssm[0][None], p_ssm[1][None],
            untr(p_kv[0], FOX_HEADS), untr(p_kv[1], FOX_HEADS), p_kv[4],
            mem5(p_mem_kt), mem5(p_mem_vt), p_conv,
            s_ssm[0][None], s_ssm[1][None],
            head4(s_kv[0]), head4(s_kv[1]), s_kv[2].reshape(bs, 1, FOX_HEADS), s_conv)
```

```python
import functools
import math

import jax
import jax.numpy as jnp
import numpy as np
from jax import lax
from jax.experimental import pallas as pl
from jax.experimental.pallas import tpu as pltpu

F32 = jnp.float32
BF16 = jnp.bfloat16

D_MODEL = 1024
DEPTH = 2
N_A = DEPTH // 2
HEAD_DIM = 64
MEM_HEADS = 4
MEM_WIDTH = MEM_HEADS * HEAD_DIM
SEQ_WIDTH = D_MODEL - MEM_WIDTH
SSM_GROUP = 16
SSM_GROUPS = SEQ_WIDTH // SSM_GROUP
SSM_STATE = 64
SSM_WIDTH = SSM_GROUPS * SSM_STATE
FOX_HEADS = SEQ_WIDTH // HEAD_DIM
D_FF = (11 * D_MODEL) // 4
CONV_W = 3
EPS = 1e-6
NEG_INF = -1e30
EIG_CLIP = -1e-4
ATTN_SCALE = HEAD_DIM ** -0.5
LOG2E = math.log2(math.e)

LANES = 128
SUBLANES = 8
MXU_DIM = 256
VMEM_BYTES_V7X = 64 * 1024 * 1024
VMEM_LIMIT = (VMEM_BYTES_V7X * 7) // 8

SSM_BLOCKS = SEQ_WIDTH // MXU_DIM
SSM_BLOCK_STATES = SSM_WIDTH // SSM_BLOCKS
HEAD_PAIRS = FOX_HEADS // 2


def _params(semantics, vmem=None):
    return pltpu.CompilerParams(dimension_semantics=semantics, vmem_limit_bytes=vmem)


def _row_tile(m, cap):
    t = min(m, cap)
    assert m % t == 0, (m, t)
    return t


def _rms(x, g):
    return x * lax.rsqrt(jnp.mean(x * x, axis=-1, keepdims=True) + EPS) * g


def _sigmoid(x):
    return 1.0 / (1.0 + jnp.exp(-x))


def _log_sigmoid(x):
    return -(jnp.maximum(-x, 0.0) + jnp.log1p(jnp.exp(-jnp.abs(x))))


def _split3(x):
    hi = x.astype(BF16)
    r1 = x - hi.astype(F32)
    mid = r1.astype(BF16)
    lo = (r1 - mid.astype(F32)).astype(BF16)
    return hi, mid, lo


def _dot(a, b):
    return jnp.dot(a, b, preferred_element_type=F32)


def _dot_nt(a, b):
    return lax.dot_general(a, b, (((1,), (1,)), ((), ())), preferred_element_type=F32)


def _norm_linear_kernel(x_ref, g_ref, w_ref, *out_refs, splits):
    h = _rms(x_ref[...], g_ref[...]).astype(BF16)
    c0 = 0
    for o_ref, n in zip(out_refs, splits):
        o_ref[...] = _dot(h, w_ref[:, c0:c0 + n])
        c0 += n


def norm_linear(x, g, w, splits, tm_cap=512):
    m, d = x.shape
    tm = _row_tile(m, tm_cap)
    n_tot = sum(splits)
    assert w.shape == (d, n_tot)
    return pl.pallas_call(
        functools.partial(_norm_linear_kernel, splits=tuple(splits)),
        grid=(m // tm,),
        in_specs=[pl.BlockSpec((tm, d), lambda i: (i, 0)),
                  pl.BlockSpec((1, d), lambda i: (0, 0)),
                  pl.BlockSpec((d, n_tot), lambda i: (0, 0))],
        out_specs=[pl.BlockSpec((tm, n), lambda i: (i, 0)) for n in splits],
        out_shape=[jax.ShapeDtypeStruct((m, n), F32) for n in splits],
        compiler_params=_params(("parallel",)),
        name="norm_linear",
    )(x, g.reshape(1, d), w)


KV_TILE = 512
KAUG_WIDTH = 2 * LANES
BIAS_PIECES = 3


def _kv_weights(w_kv, b_f):
    wk = w_kv[:, :SEQ_WIDTH].astype(BF16)
    wf = jnp.pad(w_kv[:, 2 * SEQ_WIDTH:], ((0, 0), (0, LANES - FOX_HEADS))).astype(BF16)
    bf = jnp.pad(b_f, (0, LANES - FOX_HEADS)).reshape(1, LANES)
    return wk, wf, bf


def _kv_seq_kernel(x_ref, g_ref, wk_ref, wvt_ref, wf_ref, bf_ref, tri_ref, place_ref,
                   gin_ref, wqt_ref, wm_ref,
                   kt_ref, vt_ref, kaug_ref, vtb_ref, lf_ref, qt_ref, qm_ref, carry_ref):
    @pl.when(pl.program_id(1) == 0)
    def _():
        carry_ref[...] = jnp.zeros_like(carry_ref)

    x = x_ref[0]
    xhat = x * lax.rsqrt(jnp.mean(x * x, axis=-1, keepdims=True) + EPS)
    hin = (xhat * gin_ref[...]).astype(BF16)
    qt_ref[0] = (_dot_nt(wqt_ref[...], hin) * (ATTN_SCALE * LOG2E)).astype(BF16)
    qm_ref[0] = _dot(hin, wm_ref[...])
    h = (xhat * g_ref[...]).astype(BF16)
    k = _dot(h, wk_ref[...])
    kt_ref[0] = jnp.transpose(k)
    vt = _dot_nt(wvt_ref[...], h)
    vt_ref[0] = vt
    vtb_ref[0, 0] = vt.astype(BF16)
    logf = _log_sigmoid(_dot(h, wf_ref[...]) + bf_ref[...])
    lf_ref[0] = logf[:, :FOX_HEADS]
    c3 = _dot(tri_ref[...], jnp.concatenate(_split3(logf), axis=1))
    cum = (c3[:, :LANES] + c3[:, LANES:2 * LANES]) + c3[:, 2 * LANES:] + carry_ref[...]
    carry_ref[...] = cum[cum.shape[0] - 1:, :]
    pieces = jnp.concatenate(_split3(cum * -LOG2E), axis=1)
    bias = _dot(pieces, place_ref[...]).astype(BF16)
    kb = k.astype(BF16)
    for p in range(HEAD_PAIRS):
        kaug_ref[0, :, p * KAUG_WIDTH:p * KAUG_WIDTH + LANES] = kb[:, p * LANES:(p + 1) * LANES]
        kaug_ref[0, :, p * KAUG_WIDTH + LANES:(p + 1) * KAUG_WIDTH] = bias


def shared_kv_proj_seq(x, g, w_kv, b_f, g_in, w_in):
    b, t, d = x.shape
    wqt = jnp.transpose(w_in[:, :SEQ_WIDTH])
    wm = w_in[:, SEQ_WIDTH:]
    tm = _row_tile(t, KV_TILE)
    wk, wf, bf = _kv_weights(w_kv, b_f)
    wvt = jnp.transpose(w_kv)[SEQ_WIDTH:2 * SEQ_WIDTH].astype(BF16)
    tri = jnp.asarray(np.tril(np.ones((tm, tm), np.float32)), BF16)
    hh = np.arange(FOX_HEADS)
    place = np.zeros((BIAS_PIECES * LANES, LANES), np.float32)
    for j in range(BIAS_PIECES):
        place[LANES * j + hh, BIAS_PIECES * hh + j] = 1.0
    place = jnp.asarray(place, BF16)
    tok = lambda n: pl.BlockSpec((1, tm, n), lambda i, j: (i, j, 0))
    tr = pl.BlockSpec((1, SEQ_WIDTH, tm), lambda i, j: (i, 0, j))
    full = lambda s: pl.BlockSpec(s, lambda i, j: (0,) * len(s))
    return pl.pallas_call(
        _kv_seq_kernel,
        grid=(b, t // tm),
        in_specs=[tok(d), full((1, d)), full((d, SEQ_WIDTH)), full((SEQ_WIDTH, d)),
                  full((d, LANES)), full((1, LANES)), full((tm, tm)),
                  full((BIAS_PIECES * LANES, LANES)),
                  full((1, d)), full((SEQ_WIDTH, d)), full((d, MEM_WIDTH))],
        out_specs=[tr, tr, tok(HEAD_PAIRS * KAUG_WIDTH),
                   pl.BlockSpec((1, 1, SEQ_WIDTH, tm), lambda i, j: (i, j, 0, 0)),
                   tok(FOX_HEADS), tr, tok(MEM_WIDTH)],
        out_shape=[jax.ShapeDtypeStruct((b, SEQ_WIDTH, t), F32),
                   jax.ShapeDtypeStruct((b, SEQ_WIDTH, t), F32),
                   jax.ShapeDtypeStruct((b, t, HEAD_PAIRS * KAUG_WIDTH), BF16),
                   jax.ShapeDtypeStruct((b, t // tm, SEQ_WIDTH, tm), BF16),
                   jax.ShapeDtypeStruct((b, t, FOX_HEADS), F32),
                   jax.ShapeDtypeStruct((b, SEQ_WIDTH, t), BF16),
                   jax.ShapeDtypeStruct((b, t, MEM_WIDTH), F32)],
        scratch_shapes=[pltpu.VMEM((1, LANES), F32)],
        compiler_params=_params(("parallel", "arbitrary")),
        name="shared_kv_proj_seq",
    )(x, g.reshape(1, d), wk, wvt, wf, bf, tri, place, g_in.reshape(1, d), wqt, wm)


def _kv_step_kernel(x_ref, g_ref, wk_ref, wv_ref, wf_ref, bf_ref, k_ref, v_ref, lf_ref):
    h = _rms(x_ref[...], g_ref[...]).astype(BF16)
    k_ref[...] = _dot(h, wk_ref[...])
    v_ref[...] = _dot(h, wv_ref[...])
    lf_ref[...] = _log_sigmoid(_dot(h, wf_ref[...]) + bf_ref[...])[:, :FOX_HEADS]


def shared_kv_proj_step(x, g, w_kv, b_f):
    s, d = x.shape
    wk, wf, bf = _kv_weights(w_kv, b_f)
    wv = w_kv[:, SEQ_WIDTH:2 * SEQ_WIDTH].astype(BF16)
    kv = jax.ShapeDtypeStruct((s, SEQ_WIDTH), F32)
    return pl.pallas_call(
        _kv_step_kernel,
        out_shape=[kv, kv, jax.ShapeDtypeStruct((s, FOX_HEADS), F32)],
        name="shared_kv_proj_step",
    )(x, g.reshape(1, d), wk, wv, wf, bf)


def _mem_kv_kernel(x_ref, g_ref, wt_ref, kt_ref, vt_ref):
    h = _rms(x_ref[0], g_ref[...]).astype(BF16)
    kvt = _dot_nt(wt_ref[...], h)
    kt_ref[0] = kvt[:MEM_WIDTH]
    vt_ref[0] = kvt[MEM_WIDTH:]


def mem_kv_proj(mem, g, w):
    b, n_mem, d = mem.shape
    out = jax.ShapeDtypeStruct((b, MEM_WIDTH, n_mem), F32)
    blk = pl.BlockSpec((1, MEM_WIDTH, n_mem), lambda i: (i, 0, 0))
    return pl.pallas_call(
        _mem_kv_kernel,
        grid=(b,),
        in_specs=[pl.BlockSpec((1, n_mem, d), lambda i: (i, 0, 0)),
                  pl.BlockSpec((1, d), lambda i: (0, 0)),
                  pl.BlockSpec((2 * MEM_WIDTH, d), lambda i: (0, 0))],
        out_specs=[blk, blk],
        out_shape=[out, out],
        compiler_params=_params(("parallel",)),
        name="mem_kv_proj",
    )(mem, g.reshape(1, d), jnp.transpose(w).astype(BF16))


def _mem_attn_tile(q, mkt, mvt):
    q = q * ATTN_SCALE
    mkt = mkt.astype(BF16)
    mvt = mvt.astype(BF16)
    lane = lax.broadcasted_iota(jnp.int32, (1, MEM_WIDTH), 1)
    out = jnp.zeros(q.shape, F32)
    for h in range(MEM_HEADS):
        in_head = (lane >= h * HEAD_DIM) & (lane < (h + 1) * HEAD_DIM)
        s = _dot(jnp.where(in_head, q, 0.0).astype(BF16), mkt)
        p = jnp.exp(s - jnp.max(s, axis=-1, keepdims=True))
        p = p / jnp.sum(p, axis=-1, keepdims=True)
        out = out + jnp.where(in_head, _dot_nt(p.astype(BF16), mvt), 0.0)
    return out


def _mem_attn_row(q, mkt, mvt):
    lane = lax.broadcasted_iota(jnp.int32, (MEM_HEADS, MEM_WIDTH), 1)
    head = lax.broadcasted_iota(jnp.int32, (MEM_HEADS, MEM_WIDTH), 0)
    own = (lane >= head * HEAD_DIM) & (lane < (head + 1) * HEAD_DIM)
    s = _dot(jnp.where(own, q * ATTN_SCALE, 0.0).astype(BF16), mkt.astype(BF16))
    p = jnp.exp(s - jnp.max(s, axis=-1, keepdims=True))
    p = p / jnp.sum(p, axis=-1, keepdims=True)
    o = _dot_nt(p.astype(BF16), mvt.astype(BF16))
    return jnp.sum(jnp.where(own, o, 0.0), axis=0, keepdims=True)


def _mem_attn_kernel(q_ref, mkt_ref, mvt_ref, o_ref):
    attend = _mem_attn_row if q_ref.shape[1] == 1 else _mem_attn_tile
    for s in range(q_ref.shape[0]):
        o_ref[s] = attend(q_ref[s], mkt_ref[s], mvt_ref[s])


def mem_attention(q_mem, mkt, mvt, layer, seqs_per_step=SUBLANES):
    b, t, w = q_mem.shape
    n_mem = mkt.shape[3]
    bs = _row_tile(b, seqs_per_step)
    mem = pl.BlockSpec((None, bs, w, n_mem), lambda i: (layer, i, 0, 0))
    return pl.pallas_call(
        _mem_attn_kernel,
        grid=(b // bs,),
        in_specs=[pl.BlockSpec((bs, t, w), lambda i: (i, 0, 0)), mem, mem],
        out_specs=pl.BlockSpec((bs, t, w), lambda i: (i, 0, 0)),
        out_shape=jax.ShapeDtypeStruct((b, t, w), F32),
        compiler_params=_params(("parallel",)),
        name="mem_attention",
    )(q_mem, mkt, mvt)


def _mix_out_tile(x, seq, mem, w_ref, g):
    o = (_dot(seq.astype(BF16), w_ref[:SEQ_WIDTH, :]) + _dot(mem.astype(BF16), w_ref[SEQ_WIDTH:, :]))
    return x + _rms(o, g)


def _mix_out_kernel(x_ref, s_ref, m_ref, w_ref, g_ref, o_ref):
    o_ref[...] = _mix_out_tile(x_ref[...], s_ref[...], m_ref[...], w_ref, g_ref[...])


def mix_out(x, seq_out, mem_out, w_out, g, tm_cap=512):
    m, d = x.shape
    tm = _row_tile(m, tm_cap)
    row = lambda n: pl.BlockSpec((tm, n), lambda i: (i, 0))
    return pl.pallas_call(
        _mix_out_kernel,
        grid=(m // tm,),
        in_specs=[row(d), row(SEQ_WIDTH), row(MEM_WIDTH),
                  pl.BlockSpec((d, d), lambda i: (0, 0)),
                  pl.BlockSpec((1, d), lambda i: (0, 0))],
        out_specs=row(d),
        out_shape=jax.ShapeDtypeStruct((m, d), F32),
        compiler_params=_params(("parallel",)),
        name="mix_out",
    )(x, seq_out, mem_out, w_out, g.reshape(1, d))


FFN_CHUNK = MXU_DIM
FFN_STEP_CHUNK = D_FF // 2
assert FFN_STEP_CHUNK % LANES == 0


def _tail_seq_kernel(x_ref, seq_ref, qm_ref, mkt_ref, mvt_ref, wout_ref, gmix_ref, prev_ref,
                     gpre_ref, gpost_ref, wup_ref, cw_ref, cb_ref, wdn_ref,
                     o_ref, conv_ref, h_scr, carry_scr, *, tm):
    @pl.when(pl.program_id(1) == 0)
    def _():
        carry_scr[...] = prev_ref[0]

    mem = _mem_attn_tile(qm_ref[0], mkt_ref[0], mvt_ref[0])
    x = _mix_out_tile(x_ref[0], seq_ref[0], mem, wout_ref, gmix_ref[...])
    xn = _rms(x, gpre_ref[...]).astype(BF16)
    row = lax.broadcasted_iota(jnp.int32, (SUBLANES, 1), 0)

    def up(col):
        return _dot(xn, wup_ref[:, col:col + FFN_CHUNK])

    def conv(u, col):
        c0 = carry_scr[0:1, col:col + FFN_CHUNK]
        c1 = carry_scr[1:2, col:col + FFN_CHUNK]
        u1 = pltpu.roll(u, 1, 0)
        u2 = pltpu.roll(u, 2, 0)
        u1 = jnp.concatenate([jnp.where(row == 0, c1, u1[:SUBLANES]), u1[SUBLANES:]], axis=0)
        u2 = jnp.concatenate(
            [jnp.where(row == 0, c0, jnp.where(row == 1, c1, u2[:SUBLANES])), u2[SUBLANES:]],
            axis=0)
        carry_scr[:, col:col + FFN_CHUNK] = u[tm - 2:, :]
        w = cw_ref[:, col:col + FFN_CHUNK]
        return u2 * w[0:1] + u1 * w[1:2] + u * w[2:3] + cb_ref[:, col:col + FFN_CHUNK]

    for c in range(D_FF // FFN_CHUNK):
        gate = conv(up(c * FFN_CHUNK), c * FFN_CHUNK)
        val = conv(up(D_FF + c * FFN_CHUNK), D_FF + c * FFN_CHUNK)
        h_scr[:, c * FFN_CHUNK:(c + 1) * FFN_CHUNK] = (gate * _sigmoid(gate) * val).astype(BF16)

    f = _dot(h_scr[...], wdn_ref[...])
    o_ref[0] = x + _rms(f, gpost_ref[...])
    conv_ref[0] = carry_scr[...]


def layer_tail_seq(x, seq_out, q_mem, mkt, mvt, w_out, g_mix, prev, g_pre, g_post, w_up, conv_w,
                   conv_b, w_down, layer, tm_cap=512):
    b, t, d = x.shape
    n_mem = mkt.shape[2]
    tm = _row_tile(t, tm_cap)
    assert tm >= 2 * SUBLANES
    f2 = 2 * D_FF
    full = lambda s: pl.BlockSpec(s, lambda i, j: (0,) * len(s), pipeline_mode=pl.Buffered(1))
    of_layer = lambda s: pl.BlockSpec((None,) + s, lambda i, j: (layer,) + (0,) * len(s),
                                      pipeline_mode=pl.Buffered(1))
    tok = lambda n: pl.BlockSpec((1, tm, n), lambda i, j: (i, j, 0))
    per_seq = lambda r, c: pl.BlockSpec((1, r, c), lambda i, j: (i, 0, 0))
    return pl.pallas_call(
        functools.partial(_tail_seq_kernel, tm=tm),
        grid=(b, t // tm),
        in_specs=[tok(d), tok(SEQ_WIDTH), tok(MEM_WIDTH),
                  per_seq(MEM_WIDTH, n_mem), per_seq(MEM_WIDTH, n_mem),
                  full((d, d)), full((1, d)), per_seq(CONV_W - 1, f2),
                  full((1, d)), full((1, d)), of_layer((d, f2)), full((CONV_W, f2)), full((1, f2)),
                  of_layer((D_FF, d))],
        out_specs=[tok(d), per_seq(CONV_W - 1, f2)],
        out_shape=[jax.ShapeDtypeStruct((b, t, d), F32),
                   jax.ShapeDtypeStruct((b, CONV_W - 1, f2), F32)],
        scratch_shapes=[pltpu.VMEM((tm, D_FF), BF16), pltpu.VMEM((CONV_W - 1, f2), F32)],
        compiler_params=_params(("parallel", "arbitrary"), VMEM_LIMIT),
        name="layer_tail_seq",
    )(x, seq_out, q_mem, mkt, mvt, w_out, g_mix.reshape(1, d), prev, g_pre.reshape(1, d),
      g_post.reshape(1, d), w_up, conv_w, conv_b.reshape(1, f2), w_down)


def _ffn_step_kernel(x_ref, p0g_ref, p0v_ref, p1g_ref, p1v_ref, gpre_ref, gpost_ref,
                     wg_ref, wv_ref, cwg_ref, cwv_ref, cbg_ref, cbv_ref, wdn_ref,
                     o_ref, ug_ref, uv_ref, acc_scr):
    c = pl.program_id(0)

    @pl.when(c == 0)
    def _():
        acc_scr[...] = jnp.zeros_like(acc_scr)

    x = x_ref[...]
    xn = _rms(x, gpre_ref[...]).astype(BF16)

    def conv(w_ref, p0_ref, p1_ref, cw_ref, cb_ref, u_ref):
        u = _dot(xn, w_ref[...])
        u_ref[...] = u
        w = cw_ref[...]
        return p0_ref[...] * w[0:1] + p1_ref[...] * w[1:2] + u * w[2:3] + cb_ref[...]

    gate = conv(wg_ref, p0g_ref, p1g_ref, cwg_ref, cbg_ref, ug_ref)
    val = conv(wv_ref, p0v_ref, p1v_ref, cwv_ref, cbv_ref, uv_ref)
    acc_scr[...] += _dot((gate * _sigmoid(gate) * val).astype(BF16), wdn_ref[...])

    @pl.when(c == pl.num_programs(0) - 1)
    def _():
        o_ref[...] = x + _rms(acc_scr[...], gpost_ref[...])


def conv_ffn_step(x, prev, g_pre, g_post, w_up, conv_w, conv_b, w_down, layer):
    s, d = x.shape
    f2 = 2 * D_FF
    fc = FFN_STEP_CHUNK
    nc = D_FF // fc
    prev2 = prev.reshape(s, (CONV_W - 1) * f2)
    cb = conv_b.reshape(1, f2)
    const = lambda shp: pl.BlockSpec(shp, lambda c: (0, 0))
    col = lambda rows, off: pl.BlockSpec((rows, fc), lambda c, off=off: (0, c + off))
    up_col = lambda off: pl.BlockSpec((None, d, fc), lambda c, off=off: (layer, 0, c + off))
    out, ug, uv = pl.pallas_call(
        _ffn_step_kernel,
        grid=(nc,),
        in_specs=[const((s, d)),
                  col(s, 0), col(s, nc), col(s, 2 * nc), col(s, 3 * nc),
                  const((1, d)), const((1, d)),
                  up_col(0), up_col(nc),
                  col(CONV_W, 0), col(CONV_W, nc),
                  col(1, 0), col(1, nc),
                  pl.BlockSpec((None, fc, d), lambda c: (layer, c, 0))],
        out_specs=[const((s, d)), col(s, 0), col(s, 0)],
        out_shape=[jax.ShapeDtypeStruct((s, d), F32),
                   jax.ShapeDtypeStruct((s, D_FF), F32),
                   jax.ShapeDtypeStruct((s, D_FF), F32)],
        scratch_shapes=[pltpu.VMEM((s, d), F32)],
        compiler_params=_params(("arbitrary",)),
        name="conv_ffn_step",
    )(x, prev2, prev2, prev2, prev2, g_pre.reshape(1, d), g_post.reshape(1, d),
      w_up, w_up, conv_w, conv_w, cb, cb, w_down)
    return out, jnp.concatenate([ug, uv], axis=-1)


def _s5_prep_kernel(lr_ref, li_ref, ldt_ref, br_ref, bi_ref, crt_ref, cit_ref, tile_b_ref,
                    tile_c_ref, pr_ref, pi_ref, bre_ref, bim_ref, bre2_ref, bim2_ref, cre_ref,
                    cim_ref):
    lr = jnp.minimum(lr_ref[...], EIG_CLIP)
    li = li_ref[...]
    dt = jnp.exp(ldt_ref[...])
    mag = jnp.exp(lr * dt)
    ar = mag * jnp.cos(li * dt)
    ai = mag * jnp.sin(li * dt)
    den = lr * lr + li * li
    nr = ar - 1.0
    fr = (nr * lr + ai * li) / den
    fi = (ai * lr - nr * li) / den
    br = br_ref[...]
    bi = bi_ref[...]
    bbr = fr * br - fi * bi
    bbi = fr * bi + fi * br

    def block_diag(rows, tile_ref, group_rows, group_cols):
        wide = _dot(rows.astype(BF16), tile_ref[...])
        r = lax.broadcasted_iota(jnp.int32, wide.shape, 0) // group_rows
        c = lax.broadcasted_iota(jnp.int32, wide.shape, 1) // group_cols
        return jnp.where(r == c, wide, 0.0).astype(BF16)

    nrow = MXU_DIM
    for k in range(SSM_BLOCKS):
        rows = slice(k * nrow, (k + 1) * nrow)
        b_maps = [(bbr, bre_ref, bre2_ref, 0), (bbi, bim_ref, bim2_ref, 0),
                  (ar * bbr - ai * bbi, None, bre2_ref, nrow),
                  (ar * bbi + ai * bbr, None, bim2_ref, nrow)]
        for m, plain_ref, stacked_ref, off in b_maps:
            blk = block_diag(m[rows], tile_b_ref, SSM_GROUP, SSM_STATE)
            if plain_ref is not None:
                plain_ref[k] = blk
            stacked_ref[k, off:off + nrow, :] = blk
        crow = slice(k * SSM_BLOCK_STATES, (k + 1) * SSM_BLOCK_STATES)
        cre_ref[k] = block_diag(crt_ref[crow, :], tile_c_ref, SSM_STATE, SSM_GROUP)
        cim_ref[k] = block_diag(cit_ref[crow, :], tile_c_ref, SSM_STATE, SSM_GROUP)
    pr, pi = ar, ai
    pr_ref[0] = pr
    pi_ref[0] = pi
    for n in range(1, SUBLANES):
        pr, pi = pr * ar - pi * ai, pr * ai + pi * ar
        pr_ref[n] = pr
        pi_ref[n] = pi


def s5_prepare(lam_re, lam_im, log_dt, b_re, b_im, c_re, c_im):
    g, p, c = b_re.shape
    assert (g, p, c) == (SSM_GROUPS, SSM_STATE, SSM_GROUP)
    rep = lambda a: jnp.repeat(a, c, axis=0)
    tr = lambda b: jnp.transpose(b, (0, 2, 1)).reshape(g * c, p)
    trc = lambda m: jnp.transpose(m, (0, 2, 1)).reshape(g * p, c)
    tile_b = jnp.asarray(np.arange(SSM_BLOCK_STATES)[None, :] % p == np.arange(p)[:, None], BF16)
    tile_c = jnp.asarray(np.arange(MXU_DIM)[None, :] % c == np.arange(c)[:, None], BF16)
    pw = jax.ShapeDtypeStruct((SUBLANES, g * c, p), F32)
    b1 = jax.ShapeDtypeStruct((SSM_BLOCKS, MXU_DIM, SSM_BLOCK_STATES), BF16)
    b2 = jax.ShapeDtypeStruct((SSM_BLOCKS, 2 * MXU_DIM, SSM_BLOCK_STATES), BF16)
    c1 = jax.ShapeDtypeStruct((SSM_BLOCKS, SSM_BLOCK_STATES, MXU_DIM), BF16)
    pr, pi, bre, bim, bre2, bim2, cre, cim = pl.pallas_call(
        _s5_prep_kernel, out_shape=[pw, pw, b1, b1, b2, b2, c1, c1], name="s5_prepare",
    )(rep(lam_re), rep(lam_im), jnp.broadcast_to(rep(log_dt[:, None]), (g * c, p)),
      tr(b_re), tr(b_im), trc(c_re), trc(c_im), tile_b, tile_c)
    pr = pr[:, ::c, :].reshape(SUBLANES, g * p)
    pi = pi[:, ::c, :].reshape(SUBLANES, g * p)
    return pr, pi, bre, bim, bre2, bim2, cre, cim


def _s5_finish(y, u, dskip_ref, wglu_ref, bglu_ref):
    y = y + dskip_ref[...] * u
    y = jax.nn.gelu(y)
    return y * _sigmoid(_dot(y.astype(BF16), wglu_ref[...]) + bglu_ref[...])


def _s5_seq_kernel(x_ref, gin_ref, win_ref, pr_ref, pi_ref, lvr_ref, lvi_ref, bre_ref, bim_ref,
                   cre_ref, cim_ref, dskip_ref, wglu_ref, bglu_ref, y_ref, qm_ref, hr_ref, hi_ref,
                   xr_scr, xi_scr, y_scr, *, tc):
    @pl.when(pl.program_id(1) == 0)
    def _():
        hr_ref[...] = jnp.zeros_like(hr_ref)
        hi_ref[...] = jnp.zeros_like(hi_ref)

    hin = _rms(x_ref[0], gin_ref[...]).astype(BF16)
    u = _dot(hin, win_ref[:, :SEQ_WIDTH])
    qm_ref[0] = _dot(hin, win_ref[:, SEQ_WIDTH:])
    ub = u.astype(BF16)
    first = lax.broadcasted_iota(jnp.int32, (tc, 1), 0) % SUBLANES == 0
    ub_prev = jnp.where(first, 0.0, pltpu.roll(u, 1, 0)).astype(BF16)
    nb = SSM_BLOCK_STATES
    last = SUBLANES - 1
    bc = lambda h: jnp.broadcast_to(h[last:last + 1, :], h.shape)
    block = lambda k: slice(k * nb, (k + 1) * nb)

    def project_in(k):
        cols = slice(k * MXU_DIM, (k + 1) * MXU_DIM)
        uk = jnp.concatenate([ub[:, cols], ub_prev[:, cols]], axis=1)
        xr_scr[k] = _dot(uk, bre_ref[k])
        xi_scr[k] = _dot(uk, bim_ref[k])

    def scan(k):
        sl = block(k)
        pr = pr_ref[:, sl]
        pi = pi_ref[:, sl]
        cr, ci = bc(hr_ref[0, :, sl]), bc(hi_ref[0, :, sl])
        for g in range(tc // SUBLANES):
            rows = slice(g * SUBLANES, (g + 1) * SUBLANES)
            xr = xr_scr[k, rows, :]
            xi = xi_scr[k, rows, :]
            for lv in range(1, 3):
                sr = pltpu.roll(xr, 1 << lv, 0)
                si = pltpu.roll(xi, 1 << lv, 0)
                ar = lvr_ref[lv, :, sl]
                ai = lvi_ref[lv, :, sl]
                xr, xi = xr + ar * sr - ai * si, xi + ar * si + ai * sr
            hr = xr + pr * cr - pi * ci
            hi = xi + pr * ci + pi * cr
            xr_scr[k, rows, :] = hr
            xi_scr[k, rows, :] = hi
            cr, ci = bc(hr), bc(hi)
        hr_ref[0, :, sl] = cr
        hi_ref[0, :, sl] = ci

    def project_out(k):
        y_scr[:, k * MXU_DIM:(k + 1) * MXU_DIM] = (
            _dot(xr_scr[k].astype(BF16), cre_ref[k]) - _dot(xi_scr[k].astype(BF16), cim_ref[k]))

    project_in(0)
    for k in range(SSM_BLOCKS):
        if k + 1 < SSM_BLOCKS:
            project_in(k + 1)
        scan(k)
        project_out(k)

    y_ref[0] = _s5_finish(y_scr[...], u, dskip_ref, wglu_ref, bglu_ref)


def s5_mix_seq(x, g_in, w_in, prep, d_skip, w_glu, b_glu, tc_cap=512):
    b, t, d = x.shape
    w = SEQ_WIDTH
    tc = _row_tile(t, tc_cap)
    pr, pi, _, _, bre, bim, cre, cim = prep
    rows = jnp.arange(SUBLANES)[:, None]
    lvr = jnp.stack([jnp.where(rows >= (1 << l), pr[(1 << l) - 1][None], 0.0) for l in range(3)])
    lvi = jnp.stack([jnp.where(rows >= (1 << l), pi[(1 << l) - 1][None], 0.0) for l in range(3)])
    full = lambda a: pl.BlockSpec(a.shape, lambda i, j: (0,) * a.ndim)
    consts = [g_in.reshape(1, d), w_in, pr, pi, lvr, lvi, bre, bim, cre, cim, d_skip.reshape(1, w),
              w_glu, b_glu.reshape(1, w)]
    tok = lambda n: pl.BlockSpec((1, tc, n), lambda i, j: (i, j, 0))
    y, qm, hr, hi = pl.pallas_call(
        functools.partial(_s5_seq_kernel, tc=tc),
        grid=(b, t // tc),
        in_specs=[tok(d)] + [full(a) for a in consts],
        out_specs=[tok(w), tok(MEM_WIDTH),
                   pl.BlockSpec((1, SUBLANES, SSM_WIDTH), lambda i, j: (i, 0, 0)),
                   pl.BlockSpec((1, SUBLANES, SSM_WIDTH), lambda i, j: (i, 0, 0))],
        out_shape=[jax.ShapeDtypeStruct((b, t, w), F32),
                   jax.ShapeDtypeStruct((b, t, MEM_WIDTH), F32),
                   jax.ShapeDtypeStruct((b, SUBLANES, SSM_WIDTH), F32),
                   jax.ShapeDtypeStruct((b, SUBLANES, SSM_WIDTH), F32)],
        scratch_shapes=[pltpu.VMEM((SSM_BLOCKS, tc, SSM_BLOCK_STATES), F32),
                        pltpu.VMEM((SSM_BLOCKS, tc, SSM_BLOCK_STATES), F32),
                        pltpu.VMEM((tc, w), F32)],
        compiler_params=_params(("parallel", "arbitrary")),
        name="s5_mix_seq",
    )(x, *consts)
    return y, qm, hr[:, SUBLANES - 1], hi[:, SUBLANES - 1]


def _s5_step_kernel(u_ref, h0r_ref, h0i_ref, ar_ref, ai_ref, bre_ref, bim_ref, cre_ref, cim_ref,
                    dskip_ref, wglu_ref, bglu_ref, y_ref, hr_ref, hi_ref, y_scr):
    u = u_ref[...]
    ub = u.astype(BF16)
    nb = SSM_BLOCK_STATES
    for k in range(SSM_BLOCKS):
        sl = slice(k * nb, (k + 1) * nb)
        uk = ub[:, k * MXU_DIM:(k + 1) * MXU_DIM]
        ar, ai = ar_ref[:, sl], ai_ref[:, sl]
        h0r, h0i = h0r_ref[:, sl], h0i_ref[:, sl]
        hr = _dot(uk, bre_ref[k]) + ar * h0r - ai * h0i
        hi = _dot(uk, bim_ref[k]) + ar * h0i + ai * h0r
        hr_ref[:, sl] = hr
        hi_ref[:, sl] = hi
        y_scr[:, k * MXU_DIM:(k + 1) * MXU_DIM] = (
            _dot(hr.astype(BF16), cre_ref[k]) - _dot(hi.astype(BF16), cim_ref[k]))
    y_ref[...] = _s5_finish(y_scr[...], u, dskip_ref, wglu_ref, bglu_ref)


def s5_mix_step(u, h0r, h0i, prep, d_skip, w_glu, b_glu):
    s, w = u.shape
    pr, pi, bre, bim, _, _, cre, cim = prep
    st = jax.ShapeDtypeStruct((s, SSM_WIDTH), F32)
    return pl.pallas_call(
        _s5_step_kernel,
        out_shape=[jax.ShapeDtypeStruct((s, w), F32), st, st],
        scratch_shapes=[pltpu.VMEM((s, w), F32)],
        name="s5_mix_step",
    )(u, h0r, h0i, pr[0:1], pi[0:1], bre, bim, cre, cim, d_skip.reshape(1, w), w_glu,
      b_glu.reshape(1, w))


FOX_TQ = 2 * KV_TILE
FOX_STRIP = MXU_DIM


def _fox_seq_kernel(q_ref, kaug_ref, vt_ref, tri_ref, o_ref, s_scr, p_scr, acc_scr, m_scr, l_scr,
                    smax_scr, *, tq, tk):
    i = pl.program_id(2)
    qt = q_ref[0].astype(F32)
    row = lax.broadcasted_iota(jnp.int32, (LANES, 1), 0)
    qa = []
    for h in range(2):
        own = (row >= h * HEAD_DIM) & (row < (h + 1) * HEAD_DIM)
        head = 2 * pl.program_id(1) + h
        ones = (row >= head * BIAS_PIECES) & (row < (head + 1) * BIAS_PIECES)
        qa.append(jnp.concatenate(
            [jnp.where(own, qt, 0.0), jnp.broadcast_to(jnp.where(ones, 1.0, 0.0), qt.shape)],
            axis=0).astype(BF16))
    acc_scr[...] = jnp.zeros_like(acc_scr)
    m_scr[...] = jnp.full_like(m_scr, NEG_INF)
    l_scr[...] = jnp.zeros_like(l_scr)
    n_sub = tq // FOX_STRIP
    strips = [(h, qs) for h in range(2) for qs in range(n_sub)]
    lanes = lambda qs: slice(qs * FOX_STRIP, (qs + 1) * FOX_STRIP)
    ones_rows = jnp.ones((2 * SUBLANES, tk), BF16)

    def visible_keys(qs, c):
        if c is None:
            return tk, 0
        first = qs * FOX_STRIP - c * tk
        if first < 0:
            return 0, 0
        return (tk, 0) if first >= tk else (first, FOX_STRIP)

    def scores(j, slot, c=None):
        ka = kaug_ref[0, pl.ds(pl.multiple_of(j * tk, tk), tk), :]
        for n, (h, qs) in enumerate(strips):
            if sum(visible_keys(qs, c)):
                s = _dot(ka, qa[h][:, lanes(qs)])
                s_scr[slot, n] = s
                if c is None:
                    smax_scr[slot, n] = jnp.max(s, axis=0, keepdims=True)

    def absorb(j, slot, c=None):
        alphas, n_keys = [], []
        for n, (h, qs) in enumerate(strips):
            n_full_keys, n_tri = visible_keys(qs, c)
            n_keys.append(n_full_keys + n_tri)
            if not n_keys[n]:
                alphas.append(None)
                continue
            m = m_scr[n]
            full_rows = slice(0, n_full_keys)
            tri_rows = slice(n_full_keys, n_full_keys + n_tri)
            m_new = m
            if n_tri:
                tri = s_scr[slot, n, tri_rows, :] + tri_ref[...]
                m_new = jnp.maximum(m_new, jnp.max(tri, axis=0, keepdims=True))
            if n_full_keys:
                m_new = jnp.maximum(m_new, smax_scr[slot, n] if c is None else jnp.max(
                    s_scr[slot, n, full_rows, :], axis=0, keepdims=True))
                p_scr[n, full_rows, :] = jnp.exp2(
                    s_scr[slot, n, full_rows, :] - m_new).astype(BF16)
            if n_tri:
                p_scr[n, tri_rows, :] = jnp.exp2(tri - m_new).astype(BF16)
            m_scr[n] = m_new
            alphas.append(jnp.exp2(m - m_new))
        for n, (h, qs) in enumerate(strips):
            if not n_keys[n]:
                continue
            vt = jnp.concatenate(
                [vt_ref[0, j, h * HEAD_DIM:(h + 1) * HEAD_DIM, :n_keys[n]],
                 ones_rows[:, :n_keys[n]]], axis=0)
            pv = _dot(vt, p_scr[n, :n_keys[n], :])
            acc_scr[h, :, lanes(qs)] = alphas[n] * acc_scr[h, :, lanes(qs)] + pv[:HEAD_DIM]
            l_scr[n] = alphas[n] * l_scr[n] + pv[HEAD_DIM:HEAD_DIM + 1]

    n_before = 2 * i
    scores(0, 0)

    def pair(jj, _):
        j = 2 * jj
        scores(j + 1, 1)
        absorb(j, 0)
        scores(j + 2, 0)
        absorb(j + 1, 1)
        return 0

    lax.fori_loop(0, i, pair, 0)
    if tq == tk:
        absorb(0, 0, c=0)
    else:
        scores(n_before + 1, 1, c=1)
        absorb(n_before, 0, c=0)
        absorb(n_before + 1, 1, c=1)

    l_head = lambda h: jnp.concatenate(
        [l_scr[h * n_sub + qs] for qs in range(n_sub)], axis=1)
    ot = jnp.concatenate([acc_scr[0] / l_head(0), acc_scr[1] / l_head(1)], axis=0)
    o_ref[0] = jnp.transpose(ot)


def fox_attention_seq(qt, kaug, vtb):
    b, w, t = qt.shape
    n_chunks, tk = vtb.shape[1], vtb.shape[3]
    tq = _row_tile(t, FOX_TQ)
    assert tq in (tk, 2 * tk) and n_chunks * tk == t and tk % FOX_STRIP == 0
    n_strips = 2 * tq // FOX_STRIP
    idx = np.arange(FOX_STRIP)
    tri = jnp.asarray(np.where(idx[:, None] <= idx[None, :], 0.0, NEG_INF), F32)
    return pl.pallas_call(
        functools.partial(_fox_seq_kernel, tq=tq, tk=tk),
        grid=(b, HEAD_PAIRS, t // tq),
        in_specs=[pl.BlockSpec((1, LANES, tq), lambda bi, p, i: (bi, p, i)),
                  pl.BlockSpec((1, t, KAUG_WIDTH), lambda bi, p, i: (bi, 0, p)),
                  pl.BlockSpec((1, n_chunks, LANES, tk), lambda bi, p, i: (bi, 0, p, 0)),
                  pl.BlockSpec((FOX_STRIP, FOX_STRIP), lambda bi, p, i: (0, 0))],
        out_specs=pl.BlockSpec((1, tq, LANES), lambda bi, p, i: (bi, i, p)),
        out_shape=jax.ShapeDtypeStruct((b, t, w), F32),
        scratch_shapes=[pltpu.VMEM((2, n_strips, tk, FOX_STRIP), F32),
                        pltpu.VMEM((n_strips, tk, FOX_STRIP), BF16),
                        pltpu.VMEM((2, HEAD_DIM, tq), F32),
                        pltpu.VMEM((n_strips, 1, FOX_STRIP), F32),
                        pltpu.VMEM((n_strips, 1, FOX_STRIP), F32),
                        pltpu.VMEM((2, n_strips, 1, FOX_STRIP), F32)],
        compiler_params=_params(("parallel", "parallel", "arbitrary")),
        name="fox_attention_seq",
    )(qt, kaug, vtb, tri)


DEC_PAGES = 16


def _fox_dec_kernel(pt_ref, q_ref, kn_ref, vn_ref, lfn_ref, *rest, page, n_pages):
    kts = rest[:n_pages]
    vts = rest[n_pages:2 * n_pages]
    lfs = rest[2 * n_pages:3 * n_pages]
    tri_ref = rest[3 * n_pages]
    o_ref = rest[3 * n_pages + 1]
    m_scr, l_scr, csum_scr, acc_scr, qcol_scr = rest[3 * n_pages + 2:]
    g = pl.program_id(1)
    hp = FOX_HEADS
    qrow = q_ref[0] * ATTN_SCALE

    @pl.when(g == 0)
    def _():
        m_scr[...] = jnp.full_like(m_scr, NEG_INF)
        l_scr[...] = jnp.zeros_like(l_scr)
        csum_scr[...] = jnp.zeros_like(csum_scr)
        acc_scr[...] = jnp.zeros_like(acc_scr)
        qcol_scr[...] = jnp.transpose(jnp.broadcast_to(qrow, (page, SEQ_WIDTH)))

    tri = tri_ref[...]
    base = csum_scr[...]
    head_row = lax.broadcasted_iota(jnp.int32, (hp, page), 0)
    s_parts = []
    for i in range(n_pages):
        hi, mid, lo = _split3(lfs[i][0])
        cum = (_dot(hi, tri) + _dot(mid, tri)) + _dot(lo, tri) + base
        base = jnp.broadcast_to(cum[:, page - 1:page], cum.shape)
        qk = jnp.zeros((hp, page), F32)
        for h in range(FOX_HEADS):
            rows = slice(h * HEAD_DIM, (h + 1) * HEAD_DIM)
            r = jnp.sum(kts[i][0, rows, :] * qcol_scr[rows, :], axis=0, keepdims=True)
            qk = jnp.where(head_row == h, r, qk)
        s_parts.append(qk - cum)
    csum_scr[...] = base
    s = jnp.concatenate(s_parts, axis=1)

    m_old = m_scr[...]
    m_new = jnp.maximum(m_old, jnp.max(s, axis=-1, keepdims=True))
    alpha = jnp.exp(m_old - m_new)
    p = jnp.exp(s - m_new)
    l_scr[...] = alpha * l_scr[...] + jnp.sum(p, axis=-1, keepdims=True)
    m_scr[...] = m_new
    for h in range(FOX_HEADS):
        rows = slice(h * HEAD_DIM, (h + 1) * HEAD_DIM)
        acc = acc_scr[rows, :] * alpha[h:h + 1, :]
        for i in range(n_pages):
            acc = acc + vts[i][0, rows, :] * p[h:h + 1, i * page:(i + 1) * page]
        acc_scr[rows, :] = acc

    @pl.when(g == pl.num_programs(1) - 1)
    def _():
        lane = lax.broadcasted_iota(jnp.int32, (hp, SEQ_WIDTH), 1)
        head = lax.broadcasted_iota(jnp.int32, (hp, SEQ_WIDTH), 0)
        own = (lane >= head * HEAD_DIM) & (lane < (head + 1) * HEAD_DIM)
        spread = lambda col: jnp.sum(jnp.where(own, col, 0.0), axis=0, keepdims=True)
        c_new = csum_scr[:, 0:1] + lfn_ref[0]
        s_new = jnp.sum(jnp.where(own, qrow * kn_ref[0], 0.0), axis=-1, keepdims=True) - c_new
        m_fin = jnp.maximum(m_scr[...], s_new)
        a_fin = jnp.exp(m_scr[...] - m_fin)
        p_new = jnp.exp(s_new - m_fin)
        l_fin = a_fin * l_scr[...] + p_new
        acc_row = jnp.sum(jnp.transpose(acc_scr[...]), axis=0, keepdims=True)
        o_ref[0] = (spread(a_fin) * acc_row + spread(p_new) * vn_ref[0]) / spread(l_fin)


def fox_attention_decode(q, k_new, v_new, logf_new, cache_k, cache_v, cache_logf, page_table):
    s, w = q.shape
    n_pool, page = cache_k.shape[:2]
    assert page == LANES
    pages_per_seq = page_table.shape[1]
    n_pages = min(DEC_PAGES, pages_per_seq)
    assert pages_per_seq % n_pages == 0
    hp = FOX_HEADS
    ckt = jnp.transpose(cache_k, (0, 2, 3, 1)).reshape(n_pool, w, page)
    cvt = jnp.transpose(cache_v, (0, 2, 3, 1)).reshape(n_pool, w, page)
    clf = jnp.transpose(cache_logf, (0, 2, 1))
    lfn = logf_new.reshape(s, hp, 1)
    tri = jnp.asarray(np.triu(np.ones((page, page), np.float32)), BF16)
    row = pl.BlockSpec((1, 1, w), lambda b, g, pt: (b, 0, 0))

    def paged(shape, i):
        return pl.BlockSpec((1,) + shape, lambda b, g, pt, i=i: (pt[b, g * n_pages + i], 0, 0))

    grid_spec = pltpu.PrefetchScalarGridSpec(
        num_scalar_prefetch=1,
        grid=(s, pages_per_seq // n_pages),
        in_specs=([row, row, row, pl.BlockSpec((1, hp, 1), lambda b, g, pt: (b, 0, 0))]
                  + [paged((w, page), i) for i in range(n_pages)]
                  + [paged((w, page), i) for i in range(n_pages)]
                  + [paged((hp, page), i) for i in range(n_pages)]
                  + [pl.BlockSpec((page, page), lambda b, g, pt: (0, 0))]),
        out_specs=row,
        scratch_shapes=[pltpu.VMEM((hp, 1), F32), pltpu.VMEM((hp, 1), F32),
                        pltpu.VMEM((hp, page), F32), pltpu.VMEM((w, page), F32),
                        pltpu.VMEM((w, page), F32)],
    )
    out = pl.pallas_call(
        functools.partial(_fox_dec_kernel, page=page, n_pages=n_pages),
        grid_spec=grid_spec,
        out_shape=jax.ShapeDtypeStruct((s, 1, w), F32),
        compiler_params=_params(("parallel", "arbitrary"), VMEM_LIMIT),
        name="fox_attention_decode",
    )(page_table, q.reshape(s, 1, w), k_new.reshape(s, 1, w), v_new.reshape(s, 1, w), lfn,
      *([ckt] * n_pages), *([cvt] * n_pages), *([clf] * n_pages), tri)
    return out.reshape(s, w)


def _trunk(x, mem_k, mem_v, conv_prev, ssm_state, fox_attend, p, s5_prep, sequential):
    assert N_A == 1
    b, t, d = x.shape
    m = b * t
    new_conv, ssm_out, kv = [], None, None
    for l in range(DEPTH):
        if sequential and l == N_A:
            kv = shared_kv_proj_seq(x, p["kv_norm"], p["w_kv"], p["b_f"],
                                    p["norm_mix_pre"][l], p["w_in"][l])
            z_seq, q_mem = kv[5], kv[6]
        elif not sequential:
            if l == N_A:
                kv = shared_kv_proj_step(x.reshape(m, d), p["kv_norm"], p["w_kv"], p["b_f"])
            z_seq, q_mem = norm_linear(x.reshape(m, d), p["norm_mix_pre"][l], p["w_in"][l],
                                       (SEQ_WIDTH, MEM_WIDTH))
        if l < N_A:
            if sequential:
                seq_out, q_mem, hr, hi = s5_mix_seq(x, p["norm_mix_pre"][l], p["w_in"][l],
                                                    s5_prep[l], p["d_skip"][l], p["w_glu"][l],
                                                    p["b_glu"][l])
            else:
                seq_out, hr, hi = s5_mix_step(z_seq, ssm_state[0][l], ssm_state[1][l], s5_prep[l],
                                              p["d_skip"][l], p["w_glu"][l], p["b_glu"][l])
            ssm_out = (hr.reshape(b, SSM_GROUPS, SSM_STATE), hi.reshape(b, SSM_GROUPS, SSM_STATE))
        else:
            seq_out = fox_attend(z_seq, kv)
        ffn_args = (p["norm_ffn_pre"][l], p["norm_ffn_post"][l], p["w_up"], p["conv_w"][l],
                    p["conv_b"][l], p["w_down"], l)
        if sequential:
            x3, cp = layer_tail_seq(x, seq_out.reshape(b, t, SEQ_WIDTH),
                                    q_mem.reshape(b, t, MEM_WIDTH), mem_k[l], mem_v[l],
                                    p["w_out"][l], p["norm_mix_post"][l], conv_prev[l], *ffn_args)
        else:
            mem_out = mem_attention(q_mem.reshape(b, t, MEM_WIDTH), mem_k, mem_v, l)
            x2 = mix_out(x.reshape(m, d), seq_out, mem_out.reshape(m, MEM_WIDTH),
                         p["w_out"][l], p["norm_mix_post"][l])
            x3, u_new = conv_ffn_step(x2, conv_prev[l], *ffn_args)
            cp = jnp.stack([conv_prev[l][:, 1], u_new], axis=1)
        new_conv.append(cp)
        x = x3.reshape(b, t, d)
    return x, ssm_out, jnp.stack(new_conv), kv


def kernel(x_prompt, x_sample, state_ssm_re, state_ssm_im, cache_k, cache_v, cache_logf,
           cache_mem_k, cache_mem_v, state_ffn_conv, page_table, mem_prompt,
           w_in, w_out, norm_mix_pre, norm_mix_post, norm_ffn_pre, norm_ffn_post,
           mem_norm, w_mem_kv, lam_re, lam_im, log_dt, b_re, b_im, c_re, c_im, d_skip,
           w_glu, b_glu, kv_norm, w_kv, b_f, w_up, conv_w, conv_b, w_down):
    per_layer_bf16 = lambda w: [w[l].astype(BF16) for l in range(w.shape[0])]
    p = dict(w_in=per_layer_bf16(w_in), w_out=per_layer_bf16(w_out), norm_mix_pre=norm_mix_pre,
             norm_mix_post=norm_mix_post, norm_ffn_pre=norm_ffn_pre, norm_ffn_post=norm_ffn_post,
             d_skip=d_skip, w_glu=per_layer_bf16(w_glu), b_glu=b_glu, kv_norm=kv_norm, w_kv=w_kv,
             b_f=b_f, w_up=w_up.astype(BF16), conv_w=conv_w, conv_b=conv_b,
             w_down=w_down.astype(BF16))
    s5_prep = [s5_prepare(lam_re[l], lam_im[l], log_dt[l], b_re[l], b_im[l], c_re[l], c_im[l])
               for l in range(N_A)]

    bp, tp, d = x_prompt.shape
    n_mem = mem_prompt.shape[1]
    mem_pairs = [mem_kv_proj(mem_prompt, mem_norm[l], w_mem_kv[l]) for l in range(DEPTH)]
    p_mem_kt = [mkt for mkt, _ in mem_pairs]
    p_mem_vt = [mvt for _, mvt in mem_pairs]
    zeros_conv = jnp.zeros((DEPTH, bp, CONV_W - 1, 2 * D_FF), F32)

    def fox_prompt(qt, kv):
        return fox_attention_seq(qt, kv[2], kv[3])

    y_prompt, p_ssm, p_conv, p_kv = _trunk(x_prompt, p_mem_kt, p_mem_vt, zeros_conv, None,
                                           fox_prompt, p, s5_prep, sequential=True)
    untr = lambda a, n: jnp.transpose(a.reshape(a.shape[0], n, HEAD_DIM, a.shape[2]), (0, 3, 1, 2))
    mem5 = lambda ms: jnp.stack([untr(a, MEM_HEADS) for a in ms])

    bs = x_sample.shape[0]
    tr_mem = lambda a: jnp.transpose(a, (0, 1, 3, 4, 2)).reshape(DEPTH, bs, MEM_WIDTH, n_mem)
    s_mem_kt = tr_mem(cache_mem_k)
    s_mem_vt = tr_mem(cache_mem_v)
    ssm0 = (state_ssm_re.reshape(N_A, bs, SSM_WIDTH), state_ssm_im.reshape(N_A, bs, SSM_WIDTH))

    def fox_sample(q, kv):
        k, v, logf = kv
        return fox_attention_decode(q, k, v, logf, cache_k, cache_v, cache_logf, page_table)

    y_sample, s_ssm, s_conv, s_kv = _trunk(x_sample, s_mem_kt, s_mem_vt, state_ffn_conv, ssm0,
                                           fox_sample, p, s5_prep, sequential=False)
    head4 = lambda a: a.reshape(bs, 1, FOX_HEADS, HEAD_DIM)

    return (y_prompt, y_sample, p_ssm[0][None], p_ssm[1][None],
            untr(p_kv[0], FOX_HEADS), untr(p_kv[1], FOX_HEADS), p_kv[4],
            mem5(p_mem_kt), mem5(p_mem_vt), p_conv,
            s_ssm[0][None], s_ssm[1][None],
            head4(s_kv[0]), head4(s_kv[1]), s_kv[2].reshape(bs, 1, FOX_HEADS), s_conv)
```

```python
import functools
import math

import jax
import jax.numpy as jnp
import numpy as np
from jax import lax
from jax.experimental import pallas as pl
from jax.experimental.pallas import tpu as pltpu

F32 = jnp.float32
BF16 = jnp.bfloat16

D_MODEL = 1024
DEPTH = 2
N_A = DEPTH // 2
HEAD_DIM = 64
MEM_HEADS = 4
MEM_WIDTH = MEM_HEADS * HEAD_DIM
SEQ_WIDTH = D_MODEL - MEM_WIDTH
SSM_GROUP = 16
SSM_GROUPS = SEQ_WIDTH // SSM_GROUP
SSM_STATE = 64
SSM_WIDTH = SSM_GROUPS * SSM_STATE
FOX_HEADS = SEQ_WIDTH // HEAD_DIM
D_FF = (11 * D_MODEL) // 4
CONV_W = 3
EPS = 1e-6
NEG_INF = -1e30
EIG_CLIP = -1e-4
ATTN_SCALE = HEAD_DIM ** -0.5
LOG2E = math.log2(math.e)

LANES = 128
SUBLANES = 8
MXU_DIM = 256
VMEM_BYTES_V7X = 64 * 1024 * 1024
VMEM_LIMIT = (VMEM_BYTES_V7X * 7) // 8

SSM_BLOCKS = SEQ_WIDTH // MXU_DIM
SSM_BLOCK_STATES = SSM_WIDTH // SSM_BLOCKS
S5_WINDOW = 4
S5_BLOCKS = SEQ_WIDTH // LANES
S5_BLOCK_STATES = SSM_WIDTH // S5_BLOCKS
HEAD_PAIRS = FOX_HEADS // 2


def _params(semantics, vmem=None):
    return pltpu.CompilerParams(dimension_semantics=semantics, vmem_limit_bytes=vmem)


def _row_tile(m, cap):
    t = min(m, cap)
    assert m % t == 0, (m, t)
    return t


def _rms(x, g):
    return x * lax.rsqrt(jnp.mean(x * x, axis=-1, keepdims=True) + EPS) * g


def _sigmoid(x):
    return 1.0 / (1.0 + jnp.exp(-x))


def _log_sigmoid(x):
    return -(jnp.maximum(-x, 0.0) + jnp.log1p(jnp.exp(-jnp.abs(x))))


def _split3(x):
    hi = x.astype(BF16)
    r1 = x - hi.astype(F32)
    mid = r1.astype(BF16)
    lo = (r1 - mid.astype(F32)).astype(BF16)
    return hi, mid, lo


def _dot(a, b):
    return jnp.dot(a, b, preferred_element_type=F32)


def _dot_nt(a, b):
    return lax.dot_general(a, b, (((1,), (1,)), ((), ())), preferred_element_type=F32)


def _norm_linear_kernel(x_ref, g_ref, w_ref, *out_refs, splits):
    h = _rms(x_ref[...], g_ref[...]).astype(BF16)
    c0 = 0
    for o_ref, n in zip(out_refs, splits):
        o_ref[...] = _dot(h, w_ref[:, c0:c0 + n])
        c0 += n


def norm_linear(x, g, w, splits, tm_cap=512):
    m, d = x.shape
    tm = _row_tile(m, tm_cap)
    n_tot = sum(splits)
    assert w.shape == (d, n_tot)
    return pl.pallas_call(
        functools.partial(_norm_linear_kernel, splits=tuple(splits)),
        grid=(m // tm,),
        in_specs=[pl.BlockSpec((tm, d), lambda i: (i, 0)),
                  pl.BlockSpec((1, d), lambda i: (0, 0)),
                  pl.BlockSpec((d, n_tot), lambda i: (0, 0))],
        out_specs=[pl.BlockSpec((tm, n), lambda i: (i, 0)) for n in splits],
        out_shape=[jax.ShapeDtypeStruct((m, n), F32) for n in splits],
        compiler_params=_params(("parallel",)),
        name="norm_linear",
    )(x, g.reshape(1, d), w)


KV_TILE = 512
KAUG_WIDTH = 2 * LANES
BIAS_PIECES = 3


def _kv_weights(w_kv, b_f):
    wk = w_kv[:, :SEQ_WIDTH].astype(BF16)
    wf = jnp.pad(w_kv[:, 2 * SEQ_WIDTH:], ((0, 0), (0, LANES - FOX_HEADS))).astype(BF16)
    bf = jnp.pad(b_f, (0, LANES - FOX_HEADS)).reshape(1, LANES)
    return wk, wf, bf


def _kv_seq_kernel(x_ref, g_ref, wk_ref, wvt_ref, wf_ref, bf_ref, tri_ref, place_ref,
                   gin_ref, wqt_ref, wm_ref,
                   kt_ref, vt_ref, kaug_ref, vtb_ref, lf_ref, qt_ref, qm_ref, carry_ref):
    @pl.when(pl.program_id(1) == 0)
    def _():
        carry_ref[...] = jnp.zeros_like(carry_ref)

    x = x_ref[0]
    xhat = x * lax.rsqrt(jnp.mean(x * x, axis=-1, keepdims=True) + EPS)
    hin = (xhat * gin_ref[...]).astype(BF16)
    qt_ref[0] = (_dot_nt(wqt_ref[...], hin) * (ATTN_SCALE * LOG2E)).astype(BF16)
    qm_ref[0] = _dot(hin, wm_ref[...])
    h = (xhat * g_ref[...]).astype(BF16)
    k = _dot(h, wk_ref[...])
    kt_ref[0] = jnp.transpose(k)
    vt = _dot_nt(wvt_ref[...], h)
    vt_ref[0] = vt
    vtb_ref[0, 0] = vt.astype(BF16)
    logf = _log_sigmoid(_dot(h, wf_ref[...]) + bf_ref[...])
    lf_ref[0] = logf[:, :FOX_HEADS]
    c3 = _dot(tri_ref[...], jnp.concatenate(_split3(logf), axis=1))
    cum = (c3[:, :LANES] + c3[:, LANES:2 * LANES]) + c3[:, 2 * LANES:] + carry_ref[...]
    carry_ref[...] = cum[cum.shape[0] - 1:, :]
    pieces = jnp.concatenate(_split3(cum * -LOG2E), axis=1)
    bias = _dot(pieces, place_ref[...]).astype(BF16)
    kb = k.astype(BF16)
    for p in range(HEAD_PAIRS):
        kaug_ref[0, :, p * KAUG_WIDTH:p * KAUG_WIDTH + LANES] = kb[:, p * LANES:(p + 1) * LANES]
        kaug_ref[0, :, p * KAUG_WIDTH + LANES:(p + 1) * KAUG_WIDTH] = bias


def shared_kv_proj_seq(x, g, w_kv, b_f, g_in, w_in):
    b, t, d = x.shape
    wqt = jnp.transpose(w_in[:, :SEQ_WIDTH])
    wm = w_in[:, SEQ_WIDTH:]
    tm = _row_tile(t, KV_TILE)
    wk, wf, bf = _kv_weights(w_kv, b_f)
    wvt = jnp.transpose(w_kv)[SEQ_WIDTH:2 * SEQ_WIDTH].astype(BF16)
    tri = jnp.asarray(np.tril(np.ones((tm, tm), np.float32)), BF16)
    hh = np.arange(FOX_HEADS)
    place = np.zeros((BIAS_PIECES * LANES, LANES), np.float32)
    for j in range(BIAS_PIECES):
        place[LANES * j + hh, BIAS_PIECES * hh + j] = 1.0
    place = jnp.asarray(place, BF16)
    tok = lambda n: pl.BlockSpec((1, tm, n), lambda i, j: (i, j, 0))
    tr = pl.BlockSpec((1, SEQ_WIDTH, tm), lambda i, j: (i, 0, j))
    full = lambda s: pl.BlockSpec(s, lambda i, j: (0,) * len(s))
    return pl.pallas_call(
        _kv_seq_kernel,
        grid=(b, t // tm),
        in_specs=[tok(d), full((1, d)), full((d, SEQ_WIDTH)), full((SEQ_WIDTH, d)),
                  full((d, LANES)), full((1, LANES)), full((tm, tm)),
                  full((BIAS_PIECES * LANES, LANES)),
                  full((1, d)), full((SEQ_WIDTH, d)), full((d, MEM_WIDTH))],
        out_specs=[tr, tr, tok(HEAD_PAIRS * KAUG_WIDTH),
                   pl.BlockSpec((1, 1, SEQ_WIDTH, tm), lambda i, j: (i, j, 0, 0)),
                   tok(FOX_HEADS), tr, tok(MEM_WIDTH)],
        out_shape=[jax.ShapeDtypeStruct((b, SEQ_WIDTH, t), F32),
                   jax.ShapeDtypeStruct((b, SEQ_WIDTH, t), F32),
                   jax.ShapeDtypeStruct((b, t, HEAD_PAIRS * KAUG_WIDTH), BF16),
                   jax.ShapeDtypeStruct((b, t // tm, SEQ_WIDTH, tm), BF16),
                   jax.ShapeDtypeStruct((b, t, FOX_HEADS), F32),
                   jax.ShapeDtypeStruct((b, SEQ_WIDTH, t), BF16),
                   jax.ShapeDtypeStruct((b, t, MEM_WIDTH), F32)],
        scratch_shapes=[pltpu.VMEM((1, LANES), F32)],
        compiler_params=_params(("parallel", "arbitrary")),
        name="shared_kv_proj_seq",
    )(x, g.reshape(1, d), wk, wvt, wf, bf, tri, place, g_in.reshape(1, d), wqt, wm)


def _kv_step_kernel(x_ref, g_ref, wk_ref, wv_ref, wf_ref, bf_ref, k_ref, v_ref, lf_ref):
    h = _rms(x_ref[...], g_ref[...]).astype(BF16)
    k_ref[...] = _dot(h, wk_ref[...])
    v_ref[...] = _dot(h, wv_ref[...])
    lf_ref[...] = _log_sigmoid(_dot(h, wf_ref[...]) + bf_ref[...])[:, :FOX_HEADS]


def shared_kv_proj_step(x, g, w_kv, b_f):
    s, d = x.shape
    wk, wf, bf = _kv_weights(w_kv, b_f)
    wv = w_kv[:, SEQ_WIDTH:2 * SEQ_WIDTH].astype(BF16)
    kv = jax.ShapeDtypeStruct((s, SEQ_WIDTH), F32)
    return pl.pallas_call(
        _kv_step_kernel,
        out_shape=[kv, kv, jax.ShapeDtypeStruct((s, FOX_HEADS), F32)],
        name="shared_kv_proj_step",
    )(x, g.reshape(1, d), wk, wv, wf, bf)


def _mem_kv_kernel(x_ref, g_ref, wt_ref, kt_ref, vt_ref):
    h = _rms(x_ref[0], g_ref[...]).astype(BF16)
    kvt = _dot_nt(wt_ref[...], h)
    kt_ref[0] = kvt[:MEM_WIDTH]
    vt_ref[0] = kvt[MEM_WIDTH:]


def mem_kv_proj(mem, g, w):
    b, n_mem, d = mem.shape
    out = jax.ShapeDtypeStruct((b, MEM_WIDTH, n_mem), F32)
    blk = pl.BlockSpec((1, MEM_WIDTH, n_mem), lambda i: (i, 0, 0))
    return pl.pallas_call(
        _mem_kv_kernel,
        grid=(b,),
        in_specs=[pl.BlockSpec((1, n_mem, d), lambda i: (i, 0, 0)),
                  pl.BlockSpec((1, d), lambda i: (0, 0)),
                  pl.BlockSpec((2 * MEM_WIDTH, d), lambda i: (0, 0))],
        out_specs=[blk, blk],
        out_shape=[out, out],
        compiler_params=_params(("parallel",)),
        name="mem_kv_proj",
    )(mem, g.reshape(1, d), jnp.transpose(w).astype(BF16))


def _mem_attn_tile(q, mkt, mvt):
    q = q * ATTN_SCALE
    mkt = mkt.astype(BF16)
    mvt = mvt.astype(BF16)
    lane = lax.broadcasted_iota(jnp.int32, (1, MEM_WIDTH), 1)
    out = jnp.zeros(q.shape, F32)
    for h in range(MEM_HEADS):
        in_head = (lane >= h * HEAD_DIM) & (lane < (h + 1) * HEAD_DIM)
        s = _dot(jnp.where(in_head, q, 0.0).astype(BF16), mkt)
        p = jnp.exp(s - jnp.max(s, axis=-1, keepdims=True))
        p = p / jnp.sum(p, axis=-1, keepdims=True)
        out = out + jnp.where(in_head, _dot_nt(p.astype(BF16), mvt), 0.0)
    return out


def _mem_attn_row(q, mkt, mvt):
    lane = lax.broadcasted_iota(jnp.int32, (MEM_HEADS, MEM_WIDTH), 1)
    head = lax.broadcasted_iota(jnp.int32, (MEM_HEADS, MEM_WIDTH), 0)
    own = (lane >= head * HEAD_DIM) & (lane < (head + 1) * HEAD_DIM)
    s = _dot(jnp.where(own, q * ATTN_SCALE, 0.0).astype(BF16), mkt.astype(BF16))
    p = jnp.exp(s - jnp.max(s, axis=-1, keepdims=True))
    p = p / jnp.sum(p, axis=-1, keepdims=True)
    o = _dot_nt(p.astype(BF16), mvt.astype(BF16))
    return jnp.sum(jnp.where(own, o, 0.0), axis=0, keepdims=True)


def _mem_attn_kernel(q_ref, mkt_ref, mvt_ref, o_ref):
    attend = _mem_attn_row if q_ref.shape[1] == 1 else _mem_attn_tile
    for s in range(q_ref.shape[0]):
        o_ref[s] = attend(q_ref[s], mkt_ref[s], mvt_ref[s])


def mem_attention(q_mem, mkt, mvt, layer, seqs_per_step=SUBLANES):
    b, t, w = q_mem.shape
    n_mem = mkt.shape[3]
    bs = _row_tile(b, seqs_per_step)
    mem = pl.BlockSpec((None, bs, w, n_mem), lambda i: (layer, i, 0, 0))
    return pl.pallas_call(
        _mem_attn_kernel,
        grid=(b // bs,),
        in_specs=[pl.BlockSpec((bs, t, w), lambda i: (i, 0, 0)), mem, mem],
        out_specs=pl.BlockSpec((bs, t, w), lambda i: (i, 0, 0)),
        out_shape=jax.ShapeDtypeStruct((b, t, w), F32),
        compiler_params=_params(("parallel",)),
        name="mem_attention",
    )(q_mem, mkt, mvt)


def _mix_out_tile(x, seq, mem, w_ref, g):
    o = (_dot(seq.astype(BF16), w_ref[:SEQ_WIDTH, :]) + _dot(mem.astype(BF16), w_ref[SEQ_WIDTH:, :]))
    return x + _rms(o, g)


def _mix_out_kernel(x_ref, s_ref, m_ref, w_ref, g_ref, o_ref):
    o_ref[...] = _mix_out_tile(x_ref[...], s_ref[...], m_ref[...], w_ref, g_ref[...])


def mix_out(x, seq_out, mem_out, w_out, g, tm_cap=512):
    m, d = x.shape
    tm = _row_tile(m, tm_cap)
    row = lambda n: pl.BlockSpec((tm, n), lambda i: (i, 0))
    return pl.pallas_call(
        _mix_out_kernel,
        grid=(m // tm,),
        in_specs=[row(d), row(SEQ_WIDTH), row(MEM_WIDTH),
                  pl.BlockSpec((d, d), lambda i: (0, 0)),
                  pl.BlockSpec((1, d), lambda i: (0, 0))],
        out_specs=row(d),
        out_shape=jax.ShapeDtypeStruct((m, d), F32),
        compiler_params=_params(("parallel",)),
        name="mix_out",
    )(x, seq_out, mem_out, w_out, g.reshape(1, d))


FFN_CHUNK = MXU_DIM
FFN_STEP_CHUNK = D_FF // 2
assert FFN_STEP_CHUNK % LANES == 0


def _tail_seq_kernel(x_ref, seq_ref, qm_ref, mkt_ref, mvt_ref, wout_ref, gmix_ref, prev_ref,
                     gpre_ref, gpost_ref, wup_ref, cw_ref, cb_ref, wdn_ref,
                     o_ref, conv_ref, h_scr, carry_scr, *, tm):
    @pl.when(pl.program_id(1) == 0)
    def _():
        carry_scr[...] = prev_ref[0]

    mem = _mem_attn_tile(qm_ref[0], mkt_ref[0], mvt_ref[0])
    x = _mix_out_tile(x_ref[0], seq_ref[0], mem, wout_ref, gmix_ref[...])
    xn = _rms(x, gpre_ref[...]).astype(BF16)
    row = lax.broadcasted_iota(jnp.int32, (SUBLANES, 1), 0)

    def up(col):
        return _dot(xn, wup_ref[:, col:col + FFN_CHUNK])

    def conv(u, col):
        c0 = carry_scr[0:1, col:col + FFN_CHUNK]
        c1 = carry_scr[1:2, col:col + FFN_CHUNK]
        u1 = pltpu.roll(u, 1, 0)
        u2 = pltpu.roll(u, 2, 0)
        u1 = jnp.concatenate([jnp.where(row == 0, c1, u1[:SUBLANES]), u1[SUBLANES:]], axis=0)
        u2 = jnp.concatenate(
            [jnp.where(row == 0, c0, jnp.where(row == 1, c1, u2[:SUBLANES])), u2[SUBLANES:]],
            axis=0)
        carry_scr[:, col:col + FFN_CHUNK] = u[tm - 2:, :]
        w = cw_ref[:, col:col + FFN_CHUNK]
        return u2 * w[0:1] + u1 * w[1:2] + u * w[2:3] + cb_ref[:, col:col + FFN_CHUNK]

    for c in range(D_FF // FFN_CHUNK):
        gate = conv(up(c * FFN_CHUNK), c * FFN_CHUNK)
        val = conv(up(D_FF + c * FFN_CHUNK), D_FF + c * FFN_CHUNK)
        h_scr[:, c * FFN_CHUNK:(c + 1) * FFN_CHUNK] = (gate * _sigmoid(gate) * val).astype(BF16)

    f = _dot(h_scr[...], wdn_ref[...])
    o_ref[0] = x + _rms(f, gpost_ref[...])
    conv_ref[0] = carry_scr[...]


def layer_tail_seq(x, seq_out, q_mem, mkt, mvt, w_out, g_mix, prev, g_pre, g_post, w_up, conv_w,
                   conv_b, w_down, layer, tm_cap=512):
    b, t, d = x.shape
    n_mem = mkt.shape[2]
    tm = _row_tile(t, tm_cap)
    assert tm >= 2 * SUBLANES
    f2 = 2 * D_FF
    full = lambda s: pl.BlockSpec(s, lambda i, j: (0,) * len(s), pipeline_mode=pl.Buffered(1))
    of_layer = lambda s: pl.BlockSpec((None,) + s, lambda i, j: (layer,) + (0,) * len(s),
                                      pipeline_mode=pl.Buffered(1))
    tok = lambda n: pl.BlockSpec((1, tm, n), lambda i, j: (i, j, 0))
    per_seq = lambda r, c: pl.BlockSpec((1, r, c), lambda i, j: (i, 0, 0))
    return pl.pallas_call(
        functools.partial(_tail_seq_kernel, tm=tm),
        grid=(b, t // tm),
        in_specs=[tok(d), tok(SEQ_WIDTH), tok(MEM_WIDTH),
                  per_seq(MEM_WIDTH, n_mem), per_seq(MEM_WIDTH, n_mem),
                  full((d, d)), full((1, d)), per_seq(CONV_W - 1, f2),
                  full((1, d)), full((1, d)), of_layer((d, f2)), full((CONV_W, f2)), full((1, f2)),
                  of_layer((D_FF, d))],
        out_specs=[tok(d), per_seq(CONV_W - 1, f2)],
        out_shape=[jax.ShapeDtypeStruct((b, t, d), F32),
                   jax.ShapeDtypeStruct((b, CONV_W - 1, f2), F32)],
        scratch_shapes=[pltpu.VMEM((tm, D_FF), BF16), pltpu.VMEM((CONV_W - 1, f2), F32)],
        compiler_params=_params(("parallel", "arbitrary"), VMEM_LIMIT),
        name="layer_tail_seq",
    )(x, seq_out, q_mem, mkt, mvt, w_out, g_mix.reshape(1, d), prev, g_pre.reshape(1, d),
      g_post.reshape(1, d), w_up, conv_w, conv_b.reshape(1, f2), w_down)


def _ffn_step_kernel(x_ref, p0g_ref, p0v_ref, p1g_ref, p1v_ref, gpre_ref, gpost_ref,
                     wg_ref, wv_ref, cwg_ref, cwv_ref, cbg_ref, cbv_ref, wdn_ref,
                     o_ref, ug_ref, uv_ref, acc_scr):
    c = pl.program_id(0)

    @pl.when(c == 0)
    def _():
        acc_scr[...] = jnp.zeros_like(acc_scr)

    x = x_ref[...]
    xn = _rms(x, gpre_ref[...]).astype(BF16)

    def conv(w_ref, p0_ref, p1_ref, cw_ref, cb_ref, u_ref):
        u = _dot(xn, w_ref[...])
        u_ref[...] = u
        w = cw_ref[...]
        return p0_ref[...] * w[0:1] + p1_ref[...] * w[1:2] + u * w[2:3] + cb_ref[...]

    gate = conv(wg_ref, p0g_ref, p1g_ref, cwg_ref, cbg_ref, ug_ref)
    val = conv(wv_ref, p0v_ref, p1v_ref, cwv_ref, cbv_ref, uv_ref)
    acc_scr[...] += _dot((gate * _sigmoid(gate) * val).astype(BF16), wdn_ref[...])

    @pl.when(c == pl.num_programs(0) - 1)
    def _():
        o_ref[...] = x + _rms(acc_scr[...], gpost_ref[...])


def conv_ffn_step(x, prev, g_pre, g_post, w_up, conv_w, conv_b, w_down, layer):
    s, d = x.shape
    f2 = 2 * D_FF
    fc = FFN_STEP_CHUNK
    nc = D_FF // fc
    prev2 = prev.reshape(s, (CONV_W - 1) * f2)
    cb = conv_b.reshape(1, f2)
    const = lambda shp: pl.BlockSpec(shp, lambda c: (0, 0))
    col = lambda rows, off: pl.BlockSpec((rows, fc), lambda c, off=off: (0, c + off))
    up_col = lambda off: pl.BlockSpec((None, d, fc), lambda c, off=off: (layer, 0, c + off))
    out, ug, uv = pl.pallas_call(
        _ffn_step_kernel,
        grid=(nc,),
        in_specs=[const((s, d)),
                  col(s, 0), col(s, nc), col(s, 2 * nc), col(s, 3 * nc),
                  const((1, d)), const((1, d)),
                  up_col(0), up_col(nc),
                  col(CONV_W, 0), col(CONV_W, nc),
                  col(1, 0), col(1, nc),
                  pl.BlockSpec((None, fc, d), lambda c: (layer, c, 0))],
        out_specs=[const((s, d)), col(s, 0), col(s, 0)],
        out_shape=[jax.ShapeDtypeStruct((s, d), F32),
                   jax.ShapeDtypeStruct((s, D_FF), F32),
                   jax.ShapeDtypeStruct((s, D_FF), F32)],
        scratch_shapes=[pltpu.VMEM((s, d), F32)],
        compiler_params=_params(("arbitrary",)),
        name="conv_ffn_step",
    )(x, prev2, prev2, prev2, prev2, g_pre.reshape(1, d), g_post.reshape(1, d),
      w_up, w_up, conv_w, conv_w, cb, cb, w_down)
    return out, jnp.concatenate([ug, uv], axis=-1)


def _s5_prep_kernel(lr_ref, li_ref, ldt_ref, br_ref, bi_ref, crt_ref, cit_ref, tile_b_ref,
                    tile_c_ref, pr_ref, pi_ref, bre_ref, bim_ref, cre_ref, cim_ref,
                    brew_ref, bimw_ref, crew_ref, cimw_ref):
    lr = jnp.minimum(lr_ref[...], EIG_CLIP)
    li = li_ref[...]
    dt = jnp.exp(ldt_ref[...])
    mag = jnp.exp(lr * dt)
    ar = mag * jnp.cos(li * dt)
    ai = mag * jnp.sin(li * dt)
    den = lr * lr + li * li
    nr = ar - 1.0
    fr = (nr * lr + ai * li) / den
    fi = (ai * lr - nr * li) / den
    br = br_ref[...]
    bi = bi_ref[...]
    bbr = fr * br - fi * bi
    bbi = fr * bi + fi * br

    def block_diag(rows, tile, group_rows, group_cols):
        wide = _dot(rows.astype(BF16), tile)
        r = lax.broadcasted_iota(jnp.int32, wide.shape, 0) // group_rows
        c = lax.broadcasted_iota(jnp.int32, wide.shape, 1) // group_cols
        return jnp.where(r == c, wide, 0.0).astype(BF16)

    for k in range(SSM_BLOCKS):
        rows = slice(k * MXU_DIM, (k + 1) * MXU_DIM)
        bre_ref[k] = block_diag(bbr[rows], tile_b_ref[...], SSM_GROUP, SSM_STATE)
        bim_ref[k] = block_diag(bbi[rows], tile_b_ref[...], SSM_GROUP, SSM_STATE)
        crow = slice(k * SSM_BLOCK_STATES, (k + 1) * SSM_BLOCK_STATES)
        cre_ref[k] = block_diag(crt_ref[crow, :], tile_c_ref[...], SSM_STATE, SSM_GROUP)
        cim_ref[k] = block_diag(cit_ref[crow, :], tile_c_ref[...], SSM_STATE, SSM_GROUP)
    window = [(bbr, bbi)]
    for _ in range(1, S5_WINDOW):
        wr, wi = window[-1]
        window.append((ar * wr - ai * wi, ar * wi + ai * wr))
    tile_b = tile_b_ref[:, :S5_BLOCK_STATES]
    tile_c = tile_c_ref[:, :LANES]
    for k in range(S5_BLOCKS):
        rows = slice(k * LANES, (k + 1) * LANES)
        for j, (wr, wi) in enumerate(window):
            stack = slice(j * LANES, (j + 1) * LANES)
            brew_ref[k, stack, :] = block_diag(wr[rows], tile_b, SSM_GROUP, SSM_STATE)
            bimw_ref[k, stack, :] = block_diag(wi[rows], tile_b, SSM_GROUP, SSM_STATE)
        crow = slice(k * S5_BLOCK_STATES, (k + 1) * S5_BLOCK_STATES)
        crew_ref[k] = block_diag(crt_ref[crow, :], tile_c, SSM_STATE, SSM_GROUP)
        cimw_ref[k] = block_diag(cit_ref[crow, :], tile_c, SSM_STATE, SSM_GROUP)
    pr, pi = ar, ai
    pr_ref[0] = pr
    pi_ref[0] = pi
    for n in range(1, SUBLANES):
        pr, pi = pr * ar - pi * ai, pr * ai + pi * ar
        pr_ref[n] = pr
        pi_ref[n] = pi


def s5_prepare(lam_re, lam_im, log_dt, b_re, b_im, c_re, c_im):
    g, p, c = b_re.shape
    assert (g, p, c) == (SSM_GROUPS, SSM_STATE, SSM_GROUP)
    rep = lambda a: jnp.repeat(a, c, axis=0)
    tr = lambda b: jnp.transpose(b, (0, 2, 1)).reshape(g * c, p)
    trc = lambda m: jnp.transpose(m, (0, 2, 1)).reshape(g * p, c)
    tile_b = jnp.asarray(np.arange(SSM_BLOCK_STATES)[None, :] % p == np.arange(p)[:, None], BF16)
    tile_c = jnp.asarray(np.arange(MXU_DIM)[None, :] % c == np.arange(c)[:, None], BF16)
    pw = jax.ShapeDtypeStruct((SUBLANES, g * c, p), F32)
    b1 = jax.ShapeDtypeStruct((SSM_BLOCKS, MXU_DIM, SSM_BLOCK_STATES), BF16)
    c1 = jax.ShapeDtypeStruct((SSM_BLOCKS, SSM_BLOCK_STATES, MXU_DIM), BF16)
    bw = jax.ShapeDtypeStruct((S5_BLOCKS, S5_WINDOW * LANES, S5_BLOCK_STATES), BF16)
    cw = jax.ShapeDtypeStruct((S5_BLOCKS, S5_BLOCK_STATES, LANES), BF16)
    pr, pi, bre, bim, cre, cim, brew, bimw, crew, cimw = pl.pallas_call(
        _s5_prep_kernel, out_shape=[pw, pw, b1, b1, c1, c1, bw, bw, cw, cw], name="s5_prepare",
    )(rep(lam_re), rep(lam_im), jnp.broadcast_to(rep(log_dt[:, None]), (g * c, p)),
      tr(b_re), tr(b_im), trc(c_re), trc(c_im), tile_b, tile_c)
    pr = pr[:, ::c, :].reshape(SUBLANES, g * p)
    pi = pi[:, ::c, :].reshape(SUBLANES, g * p)
    return dict(pr=pr, pi=pi, step=(bre, bim, cre, cim), seq=(brew, bimw, crew, cimw))


def _s5_finish(y, u, dskip_ref, wglu_ref, bglu_ref):
    y = y + dskip_ref[...] * u
    y = jax.nn.gelu(y)
    return y * _sigmoid(_dot(y.astype(BF16), wglu_ref[...]) + bglu_ref[...])


def _s5_seq_kernel(x_ref, gin_ref, win_ref, pr_ref, pi_ref, lvr_ref, lvi_ref, bre_ref, bim_ref,
                   cre_ref, cim_ref, dskip_ref, wglu_ref, bglu_ref, y_ref, qm_ref, hr_ref, hi_ref,
                   xr_scr, xi_scr, y_scr, *, tc):
    @pl.when(pl.program_id(1) == 0)
    def _():
        hr_ref[...] = jnp.zeros_like(hr_ref)
        hi_ref[...] = jnp.zeros_like(hi_ref)

    hin = _rms(x_ref[0], gin_ref[...]).astype(BF16)
    u = _dot(hin, win_ref[:, :SEQ_WIDTH])
    qm_ref[0] = _dot(hin, win_ref[:, SEQ_WIDTH:])
    step = lax.broadcasted_iota(jnp.int32, (tc, 1), 0) % SUBLANES
    window = [u.astype(BF16)] + [
        jnp.where(step >= j, pltpu.roll(u, j, 0), 0.0).astype(BF16) for j in range(1, S5_WINDOW)]
    nb = S5_BLOCK_STATES
    last = SUBLANES - 1
    bc = lambda h: jnp.broadcast_to(h[last:last + 1, :], h.shape)
    block = lambda k: slice(k * nb, (k + 1) * nb)

    def project_in(k):
        cols = slice(k * LANES, (k + 1) * LANES)
        uk = jnp.concatenate([uj[:, cols] for uj in window], axis=1)
        xr_scr[k] = _dot(uk, bre_ref[k])
        xi_scr[k] = _dot(uk, bim_ref[k])

    def scan(k):
        sl = block(k)
        pr = pr_ref[:, sl]
        pi = pi_ref[:, sl]
        cr, ci = bc(hr_ref[0, :, sl]), bc(hi_ref[0, :, sl])
        for g in range(tc // SUBLANES):
            rows = slice(g * SUBLANES, (g + 1) * SUBLANES)
            xr = xr_scr[k, rows, :]
            xi = xi_scr[k, rows, :]
            sr = pltpu.roll(xr, 4, 0)
            si = pltpu.roll(xi, 4, 0)
            ar = lvr_ref[:, sl]
            ai = lvi_ref[:, sl]
            xr, xi = xr + ar * sr - ai * si, xi + ar * si + ai * sr
            hr = xr + pr * cr - pi * ci
            hi = xi + pr * ci + pi * cr
            xr_scr[k, rows, :] = hr
            xi_scr[k, rows, :] = hi
            cr, ci = bc(hr), bc(hi)
        hr_ref[0, :, sl] = cr
        hi_ref[0, :, sl] = ci

    def project_out(k):
        y_scr[:, k * LANES:(k + 1) * LANES] = (
            _dot(xr_scr[k].astype(BF16), cre_ref[k]) - _dot(xi_scr[k].astype(BF16), cim_ref[k]))

    project_in(0)
    for k in range(S5_BLOCKS):
        if k + 1 < S5_BLOCKS:
            project_in(k + 1)
        scan(k)
        project_out(k)

    y_ref[0] = _s5_finish(y_scr[...], u, dskip_ref, wglu_ref, bglu_ref)


def s5_mix_seq(x, g_in, w_in, prep, d_skip, w_glu, b_glu, tc_cap=512):
    b, t, d = x.shape
    w = SEQ_WIDTH
    tc = _row_tile(t, tc_cap)
    pr, pi = prep["pr"], prep["pi"]
    bre, bim, cre, cim = prep["seq"]
    late = np.arange(SUBLANES)[:, None] >= S5_WINDOW
    lvr = jnp.where(late, pr[S5_WINDOW - 1][None], 0.0)
    lvi = jnp.where(late, pi[S5_WINDOW - 1][None], 0.0)
    full = lambda a: pl.BlockSpec(a.shape, lambda i, j: (0,) * a.ndim)
    consts = [g_in.reshape(1, d), w_in, pr, pi, lvr, lvi, bre, bim, cre, cim, d_skip.reshape(1, w),
              w_glu, b_glu.reshape(1, w)]
    tok = lambda n: pl.BlockSpec((1, tc, n), lambda i, j: (i, j, 0))
    y, qm, hr, hi = pl.pallas_call(
        functools.partial(_s5_seq_kernel, tc=tc),
        grid=(b, t // tc),
        in_specs=[tok(d)] + [full(a) for a in consts],
        out_specs=[tok(w), tok(MEM_WIDTH),
                   pl.BlockSpec((1, SUBLANES, SSM_WIDTH), lambda i, j: (i, 0, 0)),
                   pl.BlockSpec((1, SUBLANES, SSM_WIDTH), lambda i, j: (i, 0, 0))],
        out_shape=[jax.ShapeDtypeStruct((b, t, w), F32),
                   jax.ShapeDtypeStruct((b, t, MEM_WIDTH), F32),
                   jax.ShapeDtypeStruct((b, SUBLANES, SSM_WIDTH), F32),
                   jax.ShapeDtypeStruct((b, SUBLANES, SSM_WIDTH), F32)],
        scratch_shapes=[pltpu.VMEM((S5_BLOCKS, tc, S5_BLOCK_STATES), F32),
                        pltpu.VMEM((S5_BLOCKS, tc, S5_BLOCK_STATES), F32),
                        pltpu.VMEM((tc, w), F32)],
        compiler_params=_params(("parallel", "arbitrary")),
        name="s5_mix_seq",
    )(x, *consts)
    return y, qm, hr[:, SUBLANES - 1], hi[:, SUBLANES - 1]


def _s5_step_kernel(u_ref, h0r_ref, h0i_ref, ar_ref, ai_ref, bre_ref, bim_ref, cre_ref, cim_ref,
                    dskip_ref, wglu_ref, bglu_ref, y_ref, hr_ref, hi_ref, y_scr):
    u = u_ref[...]
    ub = u.astype(BF16)
    nb = SSM_BLOCK_STATES
    for k in range(SSM_BLOCKS):
        sl = slice(k * nb, (k + 1) * nb)
        uk = ub[:, k * MXU_DIM:(k + 1) * MXU_DIM]
        ar, ai = ar_ref[:, sl], ai_ref[:, sl]
        h0r, h0i = h0r_ref[:, sl], h0i_ref[:, sl]
        hr = _dot(uk, bre_ref[k]) + ar * h0r - ai * h0i
        hi = _dot(uk, bim_ref[k]) + ar * h0i + ai * h0r
        hr_ref[:, sl] = hr
        hi_ref[:, sl] = hi
        y_scr[:, k * MXU_DIM:(k + 1) * MXU_DIM] = (
            _dot(hr.astype(BF16), cre_ref[k]) - _dot(hi.astype(BF16), cim_ref[k]))
    y_ref[...] = _s5_finish(y_scr[...], u, dskip_ref, wglu_ref, bglu_ref)


def s5_mix_step(u, h0r, h0i, prep, d_skip, w_glu, b_glu):
    s, w = u.shape
    pr, pi = prep["pr"], prep["pi"]
    bre, bim, cre, cim = prep["step"]
    st = jax.ShapeDtypeStruct((s, SSM_WIDTH), F32)
    return pl.pallas_call(
        _s5_step_kernel,
        out_shape=[jax.ShapeDtypeStruct((s, w), F32), st, st],
        scratch_shapes=[pltpu.VMEM((s, w), F32)],
        name="s5_mix_step",
    )(u, h0r, h0i, pr[0:1], pi[0:1], bre, bim, cre, cim, d_skip.reshape(1, w), w_glu,
      b_glu.reshape(1, w))


FOX_TQ = 2 * KV_TILE
FOX_STRIP = MXU_DIM


def _fox_seq_kernel(q_ref, kaug_ref, vt_ref, tri_ref, o_ref, s_scr, p_scr, acc_scr, m_scr, l_scr,
                    smax_scr, *, tq, tk):
    i = pl.program_id(2)
    qt = q_ref[0].astype(F32)
    row = lax.broadcasted_iota(jnp.int32, (LANES, 1), 0)
    qa = []
    for h in range(2):
        own = (row >= h * HEAD_DIM) & (row < (h + 1) * HEAD_DIM)
        head = 2 * pl.program_id(1) + h
        ones = (row >= head * BIAS_PIECES) & (row < (head + 1) * BIAS_PIECES)
        qa.append(jnp.concatenate(
            [jnp.where(own, qt, 0.0), jnp.broadcast_to(jnp.where(ones, 1.0, 0.0), qt.shape)],
            axis=0).astype(BF16))
    acc_scr[...] = jnp.zeros_like(acc_scr)
    m_scr[...] = jnp.full_like(m_scr, NEG_INF)
    l_scr[...] = jnp.zeros_like(l_scr)
    n_sub = tq // FOX_STRIP
    strips = [(h, qs) for h in range(2) for qs in range(n_sub)]
    lanes = lambda qs: slice(qs * FOX_STRIP, (qs + 1) * FOX_STRIP)
    ones_rows = jnp.ones((2 * SUBLANES, tk), BF16)

    def visible_keys(qs, c):
        if c is None:
            return tk, 0
        first = qs * FOX_STRIP - c * tk
        if first < 0:
            return 0, 0
        return (tk, 0) if first >= tk else (first, FOX_STRIP)

    def scores(j, slot, c=None):
        ka = kaug_ref[0, pl.ds(pl.multiple_of(j * tk, tk), tk), :]
        for n, (h, qs) in enumerate(strips):
            if sum(visible_keys(qs, c)):
                s = _dot(ka, qa[h][:, lanes(qs)])
                s_scr[slot, n] = s
                if c is None:
                    smax_scr[slot, n] = jnp.max(s, axis=0, keepdims=True)

    def absorb(j, slot, c=None):
        alphas, n_keys = [], []
        for n, (h, qs) in enumerate(strips):
            n_full_keys, n_tri = visible_keys(qs, c)
            n_keys.append(n_full_keys + n_tri)
            if not n_keys[n]:
                alphas.append(None)
                continue
            m = m_scr[n]
            full_rows = slice(0, n_full_keys)
            tri_rows = slice(n_full_keys, n_full_keys + n_tri)
            m_new = m
            if n_tri:
                tri = s_scr[slot, n, tri_rows, :] + tri_ref[...]
                m_new = jnp.maximum(m_new, jnp.max(tri, axis=0, keepdims=True))
            if n_full_keys:
                m_new = jnp.maximum(m_new, smax_scr[slot, n] if c is None else jnp.max(
                    s_scr[slot, n, full_rows, :], axis=0, keepdims=True))
                p_scr[n, full_rows, :] = jnp.exp2(
                    s_scr[slot, n, full_rows, :] - m_new).astype(BF16)
            if n_tri:
                p_scr[n, tri_rows, :] = jnp.exp2(tri - m_new).astype(BF16)
            m_scr[n] = m_new
            alphas.append(jnp.exp2(m - m_new))
        for n, (h, qs) in enumerate(strips):
            if not n_keys[n]:
                continue
            vt = jnp.concatenate(
                [vt_ref[0, j, h * HEAD_DIM:(h + 1) * HEAD_DIM, :n_keys[n]],
                 ones_rows[:, :n_keys[n]]], axis=0)
            pv = _dot(vt, p_scr[n, :n_keys[n], :])
            acc_scr[h, :, lanes(qs)] = alphas[n] * acc_scr[h, :, lanes(qs)] + pv[:HEAD_DIM]
            l_scr[n] = alphas[n] * l_scr[n] + pv[HEAD_DIM:HEAD_DIM + 1]

    n_before = 2 * i
    scores(0, 0)

    def pair(jj, _):
        j = 2 * jj
        scores(j + 1, 1)
        absorb(j, 0)
        scores(j + 2, 0)
        absorb(j + 1, 1)
        return 0

    lax.fori_loop(0, i, pair, 0)
    if tq == tk:
        absorb(0, 0, c=0)
    else:
        scores(n_before + 1, 1, c=1)
        absorb(n_before, 0, c=0)
        absorb(n_before + 1, 1, c=1)

    l_head = lambda h: jnp.concatenate(
        [l_scr[h * n_sub + qs] for qs in range(n_sub)], axis=1)
    ot = jnp.concatenate([acc_scr[0] / l_head(0), acc_scr[1] / l_head(1)], axis=0)
    o_ref[0] = jnp.transpose(ot)


def fox_attention_seq(qt, kaug, vtb):
    b, w, t = qt.shape
    n_chunks, tk = vtb.shape[1], vtb.shape[3]
    tq = _row_tile(t, FOX_TQ)
    assert tq in (tk, 2 * tk) and n_chunks * tk == t and tk % FOX_STRIP == 0
    n_strips = 2 * tq // FOX_STRIP
    idx = np.arange(FOX_STRIP)
    tri = jnp.asarray(np.where(idx[:, None] <= idx[None, :], 0.0, NEG_INF), F32)
    return pl.pallas_call(
        functools.partial(_fox_seq_kernel, tq=tq, tk=tk),
        grid=(b, HEAD_PAIRS, t // tq),
        in_specs=[pl.BlockSpec((1, LANES, tq), lambda bi, p, i: (bi, p, i)),
                  pl.BlockSpec((1, t, KAUG_WIDTH), lambda bi, p, i: (bi, 0, p)),
                  pl.BlockSpec((1, n_chunks, LANES, tk), lambda bi, p, i: (bi, 0, p, 0)),
                  pl.BlockSpec((FOX_STRIP, FOX_STRIP), lambda bi, p, i: (0, 0))],
        out_specs=pl.BlockSpec((1, tq, LANES), lambda bi, p, i: (bi, i, p)),
        out_shape=jax.ShapeDtypeStruct((b, t, w), F32),
        scratch_shapes=[pltpu.VMEM((2, n_strips, tk, FOX_STRIP), F32),
                        pltpu.VMEM((n_strips, tk, FOX_STRIP), BF16),
                        pltpu.VMEM((2, HEAD_DIM, tq), F32),
                        pltpu.VMEM((n_strips, 1, FOX_STRIP), F32),
                        pltpu.VMEM((n_strips, 1, FOX_STRIP), F32),
                        pltpu.VMEM((2, n_strips, 1, FOX_STRIP), F32)],
        compiler_params=_params(("parallel", "parallel", "arbitrary")),
        name="fox_attention_seq",
    )(qt, kaug, vtb, tri)


DEC_PAGES = 16


def _fox_dec_kernel(pt_ref, q_ref, kn_ref, vn_ref, lfn_ref, *rest, page, n_pages):
    kts = rest[:n_pages]
    vts = rest[n_pages:2 * n_pages]
    lfs = rest[2 * n_pages:3 * n_pages]
    tri_ref = rest[3 * n_pages]
    o_ref = rest[3 * n_pages + 1]
    m_scr, l_scr, csum_scr, acc_scr, qcol_scr = rest[3 * n_pages + 2:]
    g = pl.program_id(1)
    hp = FOX_HEADS
    qrow = q_ref[0] * ATTN_SCALE

    @pl.when(g == 0)
    def _():
        m_scr[...] = jnp.full_like(m_scr, NEG_INF)
        l_scr[...] = jnp.zeros_like(l_scr)
        csum_scr[...] = jnp.zeros_like(csum_scr)
        acc_scr[...] = jnp.zeros_like(acc_scr)
        qcol_scr[...] = jnp.transpose(jnp.broadcast_to(qrow, (page, SEQ_WIDTH)))

    tri = tri_ref[...]
    base = csum_scr[...]
    head_row = lax.broadcasted_iota(jnp.int32, (hp, page), 0)
    s_parts = []
    for i in range(n_pages):
        hi, mid, lo = _split3(lfs[i][0])
        cum = (_dot(hi, tri) + _dot(mid, tri)) + _dot(lo, tri) + base
        base = jnp.broadcast_to(cum[:, page - 1:page], cum.shape)
        qk = jnp.zeros((hp, page), F32)
        for h in range(FOX_HEADS):
            rows = slice(h * HEAD_DIM, (h + 1) * HEAD_DIM)
            r = jnp.sum(kts[i][0, rows, :] * qcol_scr[rows, :], axis=0, keepdims=True)
            qk = jnp.where(head_row == h, r, qk)
        s_parts.append(qk - cum)
    csum_scr[...] = base
    s = jnp.concatenate(s_parts, axis=1)

    m_old = m_scr[...]
    m_new = jnp.maximum(m_old, jnp.max(s, axis=-1, keepdims=True))
    alpha = jnp.exp(m_old - m_new)
    p = jnp.exp(s - m_new)
    l_scr[...] = alpha * l_scr[...] + jnp.sum(p, axis=-1, keepdims=True)
    m_scr[...] = m_new
    for h in range(FOX_HEADS):
        rows = slice(h * HEAD_DIM, (h + 1) * HEAD_DIM)
        acc = acc_scr[rows, :] * alpha[h:h + 1, :]
        for i in range(n_pages):
            acc = acc + vts[i][0, rows, :] * p[h:h + 1, i * page:(i + 1) * page]
        acc_scr[rows, :] = acc

    @pl.when(g == pl.num_programs(1) - 1)
    def _():
        lane = lax.broadcasted_iota(jnp.int32, (hp, SEQ_WIDTH), 1)
        head = lax.broadcasted_iota(jnp.int32, (hp, SEQ_WIDTH), 0)
        own = (lane >= head * HEAD_DIM) & (lane < (head + 1) * HEAD_DIM)
        spread = lambda col: jnp.sum(jnp.where(own, col, 0.0), axis=0, keepdims=True)
        c_new = csum_scr[:, 0:1] + lfn_ref[0]
        s_new = jnp.sum(jnp.where(own, qrow * kn_ref[0], 0.0), axis=-1, keepdims=True) - c_new
        m_fin = jnp.maximum(m_scr[...], s_new)
        a_fin = jnp.exp(m_scr[...] - m_fin)
        p_new = jnp.exp(s_new - m_fin)
        l_fin = a_fin * l_scr[...] + p_new
        acc_row = jnp.sum(jnp.transpose(acc_scr[...]), axis=0, keepdims=True)
        o_ref[0] = (spread(a_fin) * acc_row + spread(p_new) * vn_ref[0]) / spread(l_fin)


def fox_attention_decode(q, k_new, v_new, logf_new, cache_k, cache_v, cache_logf, page_table):
    s, w = q.shape
    n_pool, page = cache_k.shape[:2]
    assert page == LANES
    pages_per_seq = page_table.shape[1]
    n_pages = min(DEC_PAGES, pages_per_seq)
    assert pages_per_seq % n_pages == 0
    hp = FOX_HEADS
    ckt = jnp.transpose(cache_k, (0, 2, 3, 1)).reshape(n_pool, w, page)
    cvt = jnp.transpose(cache_v, (0, 2, 3, 1)).reshape(n_pool, w, page)
    clf = jnp.transpose(cache_logf, (0, 2, 1))
    lfn = logf_new.reshape(s, hp, 1)
    tri = jnp.asarray(np.triu(np.ones((page, page), np.float32)), BF16)
    row = pl.BlockSpec((1, 1, w), lambda b, g, pt: (b, 0, 0))

    def paged(shape, i):
        return pl.BlockSpec((1,) + shape, lambda b, g, pt, i=i: (pt[b, g * n_pages + i], 0, 0))

    grid_spec = pltpu.PrefetchScalarGridSpec(
        num_scalar_prefetch=1,
        grid=(s, pages_per_seq // n_pages),
        in_specs=([row, row, row, pl.BlockSpec((1, hp, 1), lambda b, g, pt: (b, 0, 0))]
                  + [paged((w, page), i) for i in range(n_pages)]
                  + [paged((w, page), i) for i in range(n_pages)]
                  + [paged((hp, page), i) for i in range(n_pages)]
                  + [pl.BlockSpec((page, page), lambda b, g, pt: (0, 0))]),
        out_specs=row,
        scratch_shapes=[pltpu.VMEM((hp, 1), F32), pltpu.VMEM((hp, 1), F32),
                        pltpu.VMEM((hp, page), F32), pltpu.VMEM((w, page), F32),
                        pltpu.VMEM((w, page), F32)],
    )
    out = pl.pallas_call(
        functools.partial(_fox_dec_kernel, page=page, n_pages=n_pages),
        grid_spec=grid_spec,
        out_shape=jax.ShapeDtypeStruct((s, 1, w), F32),
        compiler_params=_params(("parallel", "arbitrary"), VMEM_LIMIT),
        name="fox_attention_decode",
    )(page_table, q.reshape(s, 1, w), k_new.reshape(s, 1, w), v_new.reshape(s, 1, w), lfn,
      *([ckt] * n_pages), *([cvt] * n_pages), *([clf] * n_pages), tri)
    return out.reshape(s, w)


def _trunk(x, mem_k, mem_v, conv_prev, ssm_state, fox_attend, p, s5_prep, sequential):
    assert N_A == 1
    b, t, d = x.shape
    m = b * t
    new_conv, ssm_out, kv = [], None, None
    for l in range(DEPTH):
        if sequential and l == N_A:
            kv = shared_kv_proj_seq(x, p["kv_norm"], p["w_kv"], p["b_f"],
                                    p["norm_mix_pre"][l], p["w_in"][l])
            z_seq, q_mem = kv[5], kv[6]
        elif not sequential:
            if l == N_A:
                kv = shared_kv_proj_step(x.reshape(m, d), p["kv_norm"], p["w_kv"], p["b_f"])
            z_seq, q_mem = norm_linear(x.reshape(m, d), p["norm_mix_pre"][l], p["w_in"][l],
                                       (SEQ_WIDTH, MEM_WIDTH))
        if l < N_A:
            if sequential:
                seq_out, q_mem, hr, hi = s5_mix_seq(x, p["norm_mix_pre"][l], p["w_in"][l],
                                                    s5_prep[l], p["d_skip"][l], p["w_glu"][l],
                                                    p["b_glu"][l])
            else:
                seq_out, hr, hi = s5_mix_step(z_seq, ssm_state[0][l], ssm_state[1][l], s5_prep[l],
                                              p["d_skip"][l], p["w_glu"][l], p["b_glu"][l])
            ssm_out = (hr.reshape(b, SSM_GROUPS, SSM_STATE), hi.reshape(b, SSM_GROUPS, SSM_STATE))
        else:
            seq_out = fox_attend(z_seq, kv)
        ffn_args = (p["norm_ffn_pre"][l], p["norm_ffn_post"][l], p["w_up"], p["conv_w"][l],
                    p["conv_b"][l], p["w_down"], l)
        if sequential:
            x3, cp = layer_tail_seq(x, seq_out.reshape(b, t, SEQ_WIDTH),
                                    q_mem.reshape(b, t, MEM_WIDTH), mem_k[l], mem_v[l],
                                    p["w_out"][l], p["norm_mix_post"][l], conv_prev[l], *ffn_args)
        else:
            mem_out = mem_attention(q_mem.reshape(b, t, MEM_WIDTH), mem_k, mem_v, l)
            x2 = mix_out(x.reshape(m, d), seq_out, mem_out.reshape(m, MEM_WIDTH),
                         p["w_out"][l], p["norm_mix_post"][l])
            x3, u_new = conv_ffn_step(x2, conv_prev[l], *ffn_args)
            cp = jnp.stack([conv_prev[l][:, 1], u_new], axis=1)
        new_conv.append(cp)
        x = x3.reshape(b, t, d)
    return x, ssm_out, jnp.stack(new_conv), kv


def kernel(x_prompt, x_sample, state_ssm_re, state_ssm_im, cache_k, cache_v, cache_logf,
           cache_mem_k, cache_mem_v, state_ffn_conv, page_table, mem_prompt,
           w_in, w_out, norm_mix_pre, norm_mix_post, norm_ffn_pre, norm_ffn_post,
           mem_norm, w_mem_kv, lam_re, lam_im, log_dt, b_re, b_im, c_re, c_im, d_skip,
           w_glu, b_glu, kv_norm, w_kv, b_f, w_up, conv_w, conv_b, w_down):
    per_layer_bf16 = lambda w: [w[l].astype(BF16) for l in range(w.shape[0])]
    p = dict(w_in=per_layer_bf16(w_in), w_out=per_layer_bf16(w_out), norm_mix_pre=norm_mix_pre,
             norm_mix_post=norm_mix_post, norm_ffn_pre=norm_ffn_pre, norm_ffn_post=norm_ffn_post,
             d_skip=d_skip, w_glu=per_layer_bf16(w_glu), b_glu=b_glu, kv_norm=kv_norm, w_kv=w_kv,
             b_f=b_f, w_up=w_up.astype(BF16), conv_w=conv_w, conv_b=conv_b,
             w_down=w_down.astype(BF16))
    s5_prep = [s5_prepare(lam_re[l], lam_im[l], log_dt[l], b_re[l], b_im[l], c_re[l], c_im[l])
               for l in range(N_A)]

    bp, tp, d = x_prompt.shape
    n_mem = mem_prompt.shape[1]
    mem_pairs = [mem_kv_proj(mem_prompt, mem_norm[l], w_mem_kv[l]) for l in range(DEPTH)]
    p_mem_kt = [mkt for mkt, _ in mem_pairs]
    p_mem_vt = [mvt for _, mvt in mem_pairs]
    zeros_conv = jnp.zeros((DEPTH, bp, CONV_W - 1, 2 * D_FF), F32)

    def fox_prompt(qt, kv):
        return fox_attention_seq(qt, kv[2], kv[3])

    y_prompt, p_ssm, p_conv, p_kv = _trunk(x_prompt, p_mem_kt, p_mem_vt, zeros_conv, None,
                                           fox_prompt, p, s5_prep, sequential=True)
    untr = lambda a, n: jnp.transpose(a.reshape(a.shape[0], n, HEAD_DIM, a.shape[2]), (0, 3, 1, 2))
    mem5 = lambda ms: jnp.stack([untr(a, MEM_HEADS) for a in ms])

    bs = x_sample.shape[0]
    tr_mem = lambda a: jnp.transpose(a, (0, 1, 3, 4, 2)).reshape(DEPTH, bs, MEM_WIDTH, n_mem)
    s_mem_kt = tr_mem(cache_mem_k)
    s_mem_vt = tr_mem(cache_mem_v)
    ssm0 = (state_ssm_re.reshape(N_A, bs, SSM_WIDTH), state_ssm_im.reshape(N_A, bs, SSM_WIDTH))

    def fox_sample(q, kv):
        k, v, logf = kv
        return fox_attention_decode(q, k, v, logf, cache_k, cache_v, cache_logf, page_table)

    y_sample, s_ssm, s_conv, s_kv = _trunk(x_sample, s_mem_kt, s_mem_vt, state_ffn_conv, ssm0,
                                           fox_sample, p, s5_prep, sequential=False)
    head4 = lambda a: a.reshape(bs, 1, FOX_HEADS, HEAD_DIM)

    return (y_prompt, y_sample, p_ssm[0][None], p_ssm[1][None],
            untr(p_kv[0], FOX_HEADS), untr(p_kv[1], FOX_HEADS), p_kv[4],
            mem5(p_mem_kt), mem5(p_mem_vt), p_conv,
            s_ssm[0][None], s_ssm[1][None],
            head4(s_kv[0]), head4(s_kv[1]), s_kv[2].reshape(bs, 1, FOX_HEADS), s_conv)
```

```python
import functools
import math

import jax
import jax.numpy as jnp
import numpy as np
from jax import lax
from jax.experimental import pallas as pl
from jax.experimental.pallas import tpu as pltpu

F32 = jnp.float32
BF16 = jnp.bfloat16

D_MODEL = 1024
DEPTH = 2
N_A = DEPTH // 2
HEAD_DIM = 64
MEM_HEADS = 4
MEM_WIDTH = MEM_HEADS * HEAD_DIM
SEQ_WIDTH = D_MODEL - MEM_WIDTH
SSM_GROUP = 16
SSM_GROUPS = SEQ_WIDTH // SSM_GROUP
SSM_STATE = 64
SSM_WIDTH = SSM_GROUPS * SSM_STATE
FOX_HEADS = SEQ_WIDTH // HEAD_DIM
D_FF = (11 * D_MODEL) // 4
CONV_W = 3
EPS = 1e-6
NEG_INF = -1e30
EIG_CLIP = -1e-4
ATTN_SCALE = HEAD_DIM ** -0.5
LOG2E = math.log2(math.e)

LANES = 128
SUBLANES = 8
MXU_DIM = 256
VMEM_BYTES_V7X = 64 * 1024 * 1024
VMEM_LIMIT = (VMEM_BYTES_V7X * 7) // 8

SSM_BLOCKS = SEQ_WIDTH // MXU_DIM
SSM_BLOCK_STATES = SSM_WIDTH // SSM_BLOCKS
S5_WINDOW = 4
S5_BLOCKS = SEQ_WIDTH // LANES
S5_BLOCK_STATES = SSM_WIDTH // S5_BLOCKS
HEAD_PAIRS = FOX_HEADS // 2


def _params(semantics, vmem=None):
    return pltpu.CompilerParams(dimension_semantics=semantics, vmem_limit_bytes=vmem)


def _row_tile(m, cap):
    t = min(m, cap)
    assert m % t == 0, (m, t)
    return t


def _rms(x, g):
    return x * lax.rsqrt(jnp.mean(x * x, axis=-1, keepdims=True) + EPS) * g


def _sigmoid(x):
    return 1.0 / (1.0 + jnp.exp(-x))


def _log_sigmoid(x):
    return -(jnp.maximum(-x, 0.0) + jnp.log1p(jnp.exp(-jnp.abs(x))))


def _split3(x):
    hi = x.astype(BF16)
    r1 = x - hi.astype(F32)
    mid = r1.astype(BF16)
    lo = (r1 - mid.astype(F32)).astype(BF16)
    return hi, mid, lo


def _dot(a, b):
    return jnp.dot(a, b, preferred_element_type=F32)


def _dot_nt(a, b):
    return lax.dot_general(a, b, (((1,), (1,)), ((), ())), preferred_element_type=F32)


def _norm_linear_kernel(x_ref, g_ref, w_ref, *out_refs, splits):
    h = _rms(x_ref[...], g_ref[...]).astype(BF16)
    c0 = 0
    for o_ref, n in zip(out_refs, splits):
        o_ref[...] = _dot(h, w_ref[:, c0:c0 + n])
        c0 += n


def norm_linear(x, g, w, splits, tm_cap=512):
    m, d = x.shape
    tm = _row_tile(m, tm_cap)
    n_tot = sum(splits)
    assert w.shape == (d, n_tot)
    return pl.pallas_call(
        functools.partial(_norm_linear_kernel, splits=tuple(splits)),
        grid=(m // tm,),
        in_specs=[pl.BlockSpec((tm, d), lambda i: (i, 0)),
                  pl.BlockSpec((1, d), lambda i: (0, 0)),
                  pl.BlockSpec((d, n_tot), lambda i: (0, 0))],
        out_specs=[pl.BlockSpec((tm, n), lambda i: (i, 0)) for n in splits],
        out_shape=[jax.ShapeDtypeStruct((m, n), F32) for n in splits],
        compiler_params=_params(("parallel",)),
        name="norm_linear",
    )(x, g.reshape(1, d), w)


KV_TILE = 512
KAUG_WIDTH = 2 * LANES
BIAS_PIECES = 3


def _kv_weights(w_kv, b_f):
    wk = w_kv[:, :SEQ_WIDTH].astype(BF16)
    wf = jnp.pad(w_kv[:, 2 * SEQ_WIDTH:], ((0, 0), (0, LANES - FOX_HEADS))).astype(BF16)
    bf = jnp.pad(b_f, (0, LANES - FOX_HEADS)).reshape(1, LANES)
    return wk, wf, bf


def _kv_seq_kernel(x_ref, g_ref, wk_ref, wvt_ref, wf_ref, bf_ref, tri_ref, place_ref,
                   gin_ref, wqt_ref, wm_ref,
                   kt_ref, vt_ref, kaug_ref, vtb_ref, lf_ref, qt_ref, qm_ref, carry_ref):
    @pl.when(pl.program_id(1) == 0)
    def _():
        carry_ref[...] = jnp.zeros_like(carry_ref)

    x = x_ref[0]
    xhat = x * lax.rsqrt(jnp.mean(x * x, axis=-1, keepdims=True) + EPS)
    hin = (xhat * gin_ref[...]).astype(BF16)
    qt_ref[0] = (_dot_nt(wqt_ref[...], hin) * (ATTN_SCALE * LOG2E)).astype(BF16)
    qm_ref[0] = _dot(hin, wm_ref[...])
    h = (xhat * g_ref[...]).astype(BF16)
    k = _dot(h, wk_ref[...])
    kt_ref[0] = jnp.transpose(k)
    vt = _dot_nt(wvt_ref[...], h)
    vt_ref[0] = vt
    vtb_ref[0, 0] = vt.astype(BF16)
    logf = _log_sigmoid(_dot(h, wf_ref[...]) + bf_ref[...])
    lf_ref[0] = logf[:, :FOX_HEADS]
    c3 = _dot(tri_ref[...], jnp.concatenate(_split3(logf), axis=1))
    cum = (c3[:, :LANES] + c3[:, LANES:2 * LANES]) + c3[:, 2 * LANES:] + carry_ref[...]
    carry_ref[...] = cum[cum.shape[0] - 1:, :]
    pieces = jnp.concatenate(_split3(cum * -LOG2E), axis=1)
    bias = _dot(pieces, place_ref[...]).astype(BF16)
    kb = k.astype(BF16)
    for p in range(HEAD_PAIRS):
        kaug_ref[0, :, p * KAUG_WIDTH:p * KAUG_WIDTH + LANES] = kb[:, p * LANES:(p + 1) * LANES]
        kaug_ref[0, :, p * KAUG_WIDTH + LANES:(p + 1) * KAUG_WIDTH] = bias


def shared_kv_proj_seq(x, g, w_kv, b_f, g_in, w_in):
    b, t, d = x.shape
    wqt = jnp.transpose(w_in[:, :SEQ_WIDTH])
    wm = w_in[:, SEQ_WIDTH:]
    tm = _row_tile(t, KV_TILE)
    wk, wf, bf = _kv_weights(w_kv, b_f)
    wvt = jnp.transpose(w_kv)[SEQ_WIDTH:2 * SEQ_WIDTH].astype(BF16)
    tri = jnp.asarray(np.tril(np.ones((tm, tm), np.float32)), BF16)
    hh = np.arange(FOX_HEADS)
    place = np.zeros((BIAS_PIECES * LANES, LANES), np.float32)
    for j in range(BIAS_PIECES):
        place[LANES * j + hh, BIAS_PIECES * hh + j] = 1.0
    place = jnp.asarray(place, BF16)
    tok = lambda n: pl.BlockSpec((1, tm, n), lambda i, j: (i, j, 0))
    tr = pl.BlockSpec((1, SEQ_WIDTH, tm), lambda i, j: (i, 0, j))
    full = lambda s: pl.BlockSpec(s, lambda i, j: (0,) * len(s))
    return pl.pallas_call(
        _kv_seq_kernel,
        grid=(b, t // tm),
        in_specs=[tok(d), full((1, d)), full((d, SEQ_WIDTH)), full((SEQ_WIDTH, d)),
                  full((d, LANES)), full((1, LANES)), full((tm, tm)),
                  full((BIAS_PIECES * LANES, LANES)),
                  full((1, d)), full((SEQ_WIDTH, d)), full((d, MEM_WIDTH))],
        out_specs=[tr, tr, tok(HEAD_PAIRS * KAUG_WIDTH),
                   pl.BlockSpec((1, 1, SEQ_WIDTH, tm), lambda i, j: (i, j, 0, 0)),
                   tok(FOX_HEADS), tr, tok(MEM_WIDTH)],
        out_shape=[jax.ShapeDtypeStruct((b, SEQ_WIDTH, t), F32),
                   jax.ShapeDtypeStruct((b, SEQ_WIDTH, t), F32),
                   jax.ShapeDtypeStruct((b, t, HEAD_PAIRS * KAUG_WIDTH), BF16),
                   jax.ShapeDtypeStruct((b, t // tm, SEQ_WIDTH, tm), BF16),
                   jax.ShapeDtypeStruct((b, t, FOX_HEADS), F32),
                   jax.ShapeDtypeStruct((b, SEQ_WIDTH, t), BF16),
                   jax.ShapeDtypeStruct((b, t, MEM_WIDTH), F32)],
        scratch_shapes=[pltpu.VMEM((1, LANES), F32)],
        compiler_params=_params(("parallel", "arbitrary")),
        name="shared_kv_proj_seq",
    )(x, g.reshape(1, d), wk, wvt, wf, bf, tri, place, g_in.reshape(1, d), wqt, wm)


def _kv_step_kernel(x_ref, g_ref, wk_ref, wv_ref, wf_ref, bf_ref, k_ref, v_ref, lf_ref):
    h = _rms(x_ref[...], g_ref[...]).astype(BF16)
    k_ref[...] = _dot(h, wk_ref[...])
    v_ref[...] = _dot(h, wv_ref[...])
    lf_ref[...] = _log_sigmoid(_dot(h, wf_ref[...]) + bf_ref[...])[:, :FOX_HEADS]


def shared_kv_proj_step(x, g, w_kv, b_f):
    s, d = x.shape
    wk, wf, bf = _kv_weights(w_kv, b_f)
    wv = w_kv[:, SEQ_WIDTH:2 * SEQ_WIDTH].astype(BF16)
    kv = jax.ShapeDtypeStruct((s, SEQ_WIDTH), F32)
    return pl.pallas_call(
        _kv_step_kernel,
        out_shape=[kv, kv, jax.ShapeDtypeStruct((s, FOX_HEADS), F32)],
        name="shared_kv_proj_step",
    )(x, g.reshape(1, d), wk, wv, wf, bf)


def _mem_kv_kernel(x_ref, g_ref, wt_ref, kt_ref, vt_ref):
    h = _rms(x_ref[0], g_ref[...]).astype(BF16)
    kvt = _dot_nt(wt_ref[...], h)
    kt_ref[0] = kvt[:MEM_WIDTH]
    vt_ref[0] = kvt[MEM_WIDTH:]


def mem_kv_proj(mem, g, w):
    b, n_mem, d = mem.shape
    out = jax.ShapeDtypeStruct((b, MEM_WIDTH, n_mem), F32)
    blk = pl.BlockSpec((1, MEM_WIDTH, n_mem), lambda i: (i, 0, 0))
    return pl.pallas_call(
        _mem_kv_kernel,
        grid=(b,),
        in_specs=[pl.BlockSpec((1, n_mem, d), lambda i: (i, 0, 0)),
                  pl.BlockSpec((1, d), lambda i: (0, 0)),
                  pl.BlockSpec((2 * MEM_WIDTH, d), lambda i: (0, 0))],
        out_specs=[blk, blk],
        out_shape=[out, out],
        compiler_params=_params(("parallel",)),
        name="mem_kv_proj",
    )(mem, g.reshape(1, d), jnp.transpose(w).astype(BF16))


def _mem_attn_tile(q, mkt, mvt):
    q = q * ATTN_SCALE
    mkt = mkt.astype(BF16)
    mvt = mvt.astype(BF16)
    lane = lax.broadcasted_iota(jnp.int32, (1, MEM_WIDTH), 1)
    out = jnp.zeros(q.shape, F32)
    for h in range(MEM_HEADS):
        in_head = (lane >= h * HEAD_DIM) & (lane < (h + 1) * HEAD_DIM)
        s = _dot(jnp.where(in_head, q, 0.0).astype(BF16), mkt)
        p = jnp.exp(s - jnp.max(s, axis=-1, keepdims=True))
        p = p / jnp.sum(p, axis=-1, keepdims=True)
        out = out + jnp.where(in_head, _dot_nt(p.astype(BF16), mvt), 0.0)
    return out


def _mem_attn_row(q, mkt, mvt):
    lane = lax.broadcasted_iota(jnp.int32, (MEM_HEADS, MEM_WIDTH), 1)
    head = lax.broadcasted_iota(jnp.int32, (MEM_HEADS, MEM_WIDTH), 0)
    own = (lane >= head * HEAD_DIM) & (lane < (head + 1) * HEAD_DIM)
    s = _dot(jnp.where(own, q * ATTN_SCALE, 0.0).astype(BF16), mkt.astype(BF16))
    p = jnp.exp(s - jnp.max(s, axis=-1, keepdims=True))
    p = p / jnp.sum(p, axis=-1, keepdims=True)
    o = _dot_nt(p.astype(BF16), mvt.astype(BF16))
    return jnp.sum(jnp.where(own, o, 0.0), axis=0, keepdims=True)


def _mem_attn_kernel(q_ref, mkt_ref, mvt_ref, o_ref):
    attend = _mem_attn_row if q_ref.shape[1] == 1 else _mem_attn_tile
    for s in range(q_ref.shape[0]):
        o_ref[s] = attend(q_ref[s], mkt_ref[s], mvt_ref[s])


def mem_attention(q_mem, mkt, mvt, layer, seqs_per_step=SUBLANES):
    b, t, w = q_mem.shape
    n_mem = mkt.shape[3]
    bs = _row_tile(b, seqs_per_step)
    mem = pl.BlockSpec((None, bs, w, n_mem), lambda i: (layer, i, 0, 0))
    return pl.pallas_call(
        _mem_attn_kernel,
        grid=(b // bs,),
        in_specs=[pl.BlockSpec((bs, t, w), lambda i: (i, 0, 0)), mem, mem],
        out_specs=pl.BlockSpec((bs, t, w), lambda i: (i, 0, 0)),
        out_shape=jax.ShapeDtypeStruct((b, t, w), F32),
        compiler_params=_params(("parallel",)),
        name="mem_attention",
    )(q_mem, mkt, mvt)


def _mix_out_tile(x, seq, mem, w_ref, g):
    o = (_dot(seq.astype(BF16), w_ref[:SEQ_WIDTH, :]) + _dot(mem.astype(BF16), w_ref[SEQ_WIDTH:, :]))
    return x + _rms(o, g)


def _mix_out_kernel(x_ref, s_ref, m_ref, w_ref, g_ref, o_ref):
    o_ref[...] = _mix_out_tile(x_ref[...], s_ref[...], m_ref[...], w_ref, g_ref[...])


def mix_out(x, seq_out, mem_out, w_out, g, tm_cap=512):
    m, d = x.shape
    tm = _row_tile(m, tm_cap)
    row = lambda n: pl.BlockSpec((tm, n), lambda i: (i, 0))
    return pl.pallas_call(
        _mix_out_kernel,
        grid=(m // tm,),
        in_specs=[row(d), row(SEQ_WIDTH), row(MEM_WIDTH),
                  pl.BlockSpec((d, d), lambda i: (0, 0)),
                  pl.BlockSpec((1, d), lambda i: (0, 0))],
        out_specs=row(d),
        out_shape=jax.ShapeDtypeStruct((m, d), F32),
        compiler_params=_params(("parallel",)),
        name="mix_out",
    )(x, seq_out, mem_out, w_out, g.reshape(1, d))


FFN_CHUNK = MXU_DIM
FFN_STEP_CHUNK = D_FF // 2
assert FFN_STEP_CHUNK % LANES == 0


def _tail_seq_kernel(x_ref, seq_ref, qm_ref, mkt_ref, mvt_ref, wout_ref, gmix_ref, prev_ref,
                     gpre_ref, gpost_ref, wup_ref, cw_ref, cb_ref, wdn_ref,
                     o_ref, conv_ref, h_scr, carry_scr, *, tm):
    @pl.when(pl.program_id(1) == 0)
    def _():
        carry_scr[...] = prev_ref[0]

    mem = _mem_attn_tile(qm_ref[0], mkt_ref[0], mvt_ref[0])
    x = _mix_out_tile(x_ref[0], seq_ref[0], mem, wout_ref, gmix_ref[...])
    xn = _rms(x, gpre_ref[...]).astype(BF16)
    row = lax.broadcasted_iota(jnp.int32, (SUBLANES, 1), 0)

    def up(col):
        return _dot(xn, wup_ref[:, col:col + FFN_CHUNK])

    def conv(u, col):
        c0 = carry_scr[0:1, col:col + FFN_CHUNK]
        c1 = carry_scr[1:2, col:col + FFN_CHUNK]
        u1 = pltpu.roll(u, 1, 0)
        u2 = pltpu.roll(u, 2, 0)
        u1 = jnp.concatenate([jnp.where(row == 0, c1, u1[:SUBLANES]), u1[SUBLANES:]], axis=0)
        u2 = jnp.concatenate(
            [jnp.where(row == 0, c0, jnp.where(row == 1, c1, u2[:SUBLANES])), u2[SUBLANES:]],
            axis=0)
        carry_scr[:, col:col + FFN_CHUNK] = u[tm - 2:, :]
        w = cw_ref[:, col:col + FFN_CHUNK]
        return u2 * w[0:1] + u1 * w[1:2] + u * w[2:3] + cb_ref[:, col:col + FFN_CHUNK]

    for c in range(D_FF // FFN_CHUNK):
        gate = conv(up(c * FFN_CHUNK), c * FFN_CHUNK)
        val = conv(up(D_FF + c * FFN_CHUNK), D_FF + c * FFN_CHUNK)
        h_scr[:, c * FFN_CHUNK:(c + 1) * FFN_CHUNK] = (gate * _sigmoid(gate) * val).astype(BF16)

    f = _dot(h_scr[...], wdn_ref[...])
    o_ref[0] = x + _rms(f, gpost_ref[...])
    conv_ref[0] = carry_scr[...]


def layer_tail_seq(x, seq_out, q_mem, mkt, mvt, w_out, g_mix, prev, g_pre, g_post, w_up, conv_w,
                   conv_b, w_down, layer, tm_cap=512):
    b, t, d = x.shape
    n_mem = mkt.shape[2]
    tm = _row_tile(t, tm_cap)
    assert tm >= 2 * SUBLANES
    f2 = 2 * D_FF
    full = lambda s: pl.BlockSpec(s, lambda i, j: (0,) * len(s), pipeline_mode=pl.Buffered(1))
    of_layer = lambda s: pl.BlockSpec((None,) + s, lambda i, j: (layer,) + (0,) * len(s),
                                      pipeline_mode=pl.Buffered(1))
    tok = lambda n: pl.BlockSpec((1, tm, n), lambda i, j: (i, j, 0))
    per_seq = lambda r, c: pl.BlockSpec((1, r, c), lambda i, j: (i, 0, 0))
    return pl.pallas_call(
        functools.partial(_tail_seq_kernel, tm=tm),
        grid=(b, t // tm),
        in_specs=[tok(d), tok(SEQ_WIDTH), tok(MEM_WIDTH),
                  per_seq(MEM_WIDTH, n_mem), per_seq(MEM_WIDTH, n_mem),
                  full((d, d)), full((1, d)), per_seq(CONV_W - 1, f2),
                  full((1, d)), full((1, d)), of_layer((d, f2)), full((CONV_W, f2)), full((1, f2)),
                  of_layer((D_FF, d))],
        out_specs=[tok(d), per_seq(CONV_W - 1, f2)],
        out_shape=[jax.ShapeDtypeStruct((b, t, d), F32),
                   jax.ShapeDtypeStruct((b, CONV_W - 1, f2), F32)],
        scratch_shapes=[pltpu.VMEM((tm, D_FF), BF16), pltpu.VMEM((CONV_W - 1, f2), F32)],
        compiler_params=_params(("parallel", "arbitrary"), VMEM_LIMIT),
        name="layer_tail_seq",
    )(x, seq_out, q_mem, mkt, mvt, w_out, g_mix.reshape(1, d), prev, g_pre.reshape(1, d),
      g_post.reshape(1, d), w_up, conv_w, conv_b.reshape(1, f2), w_down)


def _ffn_step_kernel(x_ref, p0g_ref, p0v_ref, p1g_ref, p1v_ref, gpre_ref, gpost_ref,
                     wg_ref, wv_ref, cwg_ref, cwv_ref, cbg_ref, cbv_ref, wdn_ref,
                     o_ref, ug_ref, uv_ref, acc_scr):
    c = pl.program_id(0)

    @pl.when(c == 0)
    def _():
        acc_scr[...] = jnp.zeros_like(acc_scr)

    x = x_ref[...]
    xn = _rms(x, gpre_ref[...]).astype(BF16)

    def conv(w_ref, p0_ref, p1_ref, cw_ref, cb_ref, u_ref):
        u = _dot(xn, w_ref[...])
        u_ref[...] = u
        w = cw_ref[...]
        return p0_ref[...] * w[0:1] + p1_ref[...] * w[1:2] + u * w[2:3] + cb_ref[...]

    gate = conv(wg_ref, p0g_ref, p1g_ref, cwg_ref, cbg_ref, ug_ref)
    val = conv(wv_ref, p0v_ref, p1v_ref, cwv_ref, cbv_ref, uv_ref)
    acc_scr[...] += _dot((gate * _sigmoid(gate) * val).astype(BF16), wdn_ref[...])

    @pl.when(c == pl.num_programs(0) - 1)
    def _():
        o_ref[...] = x + _rms(acc_scr[...], gpost_ref[...])


def conv_ffn_step(x, prev, g_pre, g_post, w_up, conv_w, conv_b, w_down, layer):
    s, d = x.shape
    f2 = 2 * D_FF
    fc = FFN_STEP_CHUNK
    nc = D_FF // fc
    prev2 = prev.reshape(s, (CONV_W - 1) * f2)
    cb = conv_b.reshape(1, f2)
    const = lambda shp: pl.BlockSpec(shp, lambda c: (0, 0))
    col = lambda rows, off: pl.BlockSpec((rows, fc), lambda c, off=off: (0, c + off))
    up_col = lambda off: pl.BlockSpec((None, d, fc), lambda c, off=off: (layer, 0, c + off))
    out, ug, uv = pl.pallas_call(
        _ffn_step_kernel,
        grid=(nc,),
        in_specs=[const((s, d)),
                  col(s, 0), col(s, nc), col(s, 2 * nc), col(s, 3 * nc),
                  const((1, d)), const((1, d)),
                  up_col(0), up_col(nc),
                  col(CONV_W, 0), col(CONV_W, nc),
                  col(1, 0), col(1, nc),
                  pl.BlockSpec((None, fc, d), lambda c: (layer, c, 0))],
        out_specs=[const((s, d)), col(s, 0), col(s, 0)],
        out_shape=[jax.ShapeDtypeStruct((s, d), F32),
                   jax.ShapeDtypeStruct((s, D_FF), F32),
                   jax.ShapeDtypeStruct((s, D_FF), F32)],
        scratch_shapes=[pltpu.VMEM((s, d), F32)],
        compiler_params=_params(("arbitrary",)),
        name="conv_ffn_step",
    )(x, prev2, prev2, prev2, prev2, g_pre.reshape(1, d), g_post.reshape(1, d),
      w_up, w_up, conv_w, conv_w, cb, cb, w_down)
    return out, jnp.concatenate([ug, uv], axis=-1)


def _s5_prep_kernel(lr_ref, li_ref, ldt_ref, br_ref, bi_ref, crt_ref, cit_ref, tile_b_ref,
                    tile_c_ref, pr_ref, pi_ref, bre_ref, bim_ref, cre_ref, cim_ref,
                    brew_ref, bimw_ref, crew_ref, cimw_ref):
    lr = jnp.minimum(lr_ref[...], EIG_CLIP)
    li = li_ref[...]
    dt = jnp.exp(ldt_ref[...])
    mag = jnp.exp(lr * dt)
    ar = mag * jnp.cos(li * dt)
    ai = mag * jnp.sin(li * dt)
    den = lr * lr + li * li
    nr = ar - 1.0
    fr = (nr * lr + ai * li) / den
    fi = (ai * lr - nr * li) / den
    br = br_ref[...]
    bi = bi_ref[...]
    bbr = fr * br - fi * bi
    bbi = fr * bi + fi * br

    def block_diag(rows, tile, group_rows, group_cols):
        wide = _dot(rows.astype(BF16), tile)
        r = lax.broadcasted_iota(jnp.int32, wide.shape, 0) // group_rows
        c = lax.broadcasted_iota(jnp.int32, wide.shape, 1) // group_cols
        return jnp.where(r == c, wide, 0.0).astype(BF16)

    for k in range(SSM_BLOCKS):
        rows = slice(k * MXU_DIM, (k + 1) * MXU_DIM)
        bre_ref[k] = block_diag(bbr[rows], tile_b_ref[...], SSM_GROUP, SSM_STATE)
        bim_ref[k] = block_diag(bbi[rows], tile_b_ref[...], SSM_GROUP, SSM_STATE)
        crow = slice(k * SSM_BLOCK_STATES, (k + 1) * SSM_BLOCK_STATES)
        cre_ref[k] = block_diag(crt_ref[crow, :], tile_c_ref[...], SSM_STATE, SSM_GROUP)
        cim_ref[k] = block_diag(cit_ref[crow, :], tile_c_ref[...], SSM_STATE, SSM_GROUP)
    window = [(bbr, bbi)]
    for _ in range(1, S5_WINDOW):
        wr, wi = window[-1]
        window.append((ar * wr - ai * wi, ar * wi + ai * wr))
    tile_b = tile_b_ref[:, :S5_BLOCK_STATES]
    tile_c = tile_c_ref[:, :LANES]
    for k in range(S5_BLOCKS):
        rows = slice(k * LANES, (k + 1) * LANES)
        for j, (wr, wi) in enumerate(window):
            stack = slice(j * LANES, (j + 1) * LANES)
            brew_ref[k, stack, :] = block_diag(wr[rows], tile_b, SSM_GROUP, SSM_STATE)
            bimw_ref[k, stack, :] = block_diag(wi[rows], tile_b, SSM_GROUP, SSM_STATE)
        crow = slice(k * S5_BLOCK_STATES, (k + 1) * S5_BLOCK_STATES)
        crew_ref[k] = block_diag(crt_ref[crow, :], tile_c, SSM_STATE, SSM_GROUP)
        cimw_ref[k] = block_diag(cit_ref[crow, :], tile_c, SSM_STATE, SSM_GROUP)
    pr, pi = ar, ai
    pr_ref[0] = pr
    pi_ref[0] = pi
    for n in range(1, SUBLANES):
        pr, pi = pr * ar - pi * ai, pr * ai + pi * ar
        pr_ref[n] = pr
        pi_ref[n] = pi


def s5_prepare(lam_re, lam_im, log_dt, b_re, b_im, c_re, c_im):
    g, p, c = b_re.shape
    assert (g, p, c) == (SSM_GROUPS, SSM_STATE, SSM_GROUP)
    rep = lambda a: jnp.repeat(a, c, axis=0)
    tr = lambda b: jnp.transpose(b, (0, 2, 1)).reshape(g * c, p)
    trc = lambda m: jnp.transpose(m, (0, 2, 1)).reshape(g * p, c)
    tile_b = jnp.asarray(np.arange(SSM_BLOCK_STATES)[None, :] % p == np.arange(p)[:, None], BF16)
    tile_c = jnp.asarray(np.arange(MXU_DIM)[None, :] % c == np.arange(c)[:, None], BF16)
    pw = jax.ShapeDtypeStruct((SUBLANES, g * c, p), F32)
    b1 = jax.ShapeDtypeStruct((SSM_BLOCKS, MXU_DIM, SSM_BLOCK_STATES), BF16)
    c1 = jax.ShapeDtypeStruct((SSM_BLOCKS, SSM_BLOCK_STATES, MXU_DIM), BF16)
    bw = jax.ShapeDtypeStruct((S5_BLOCKS, S5_WINDOW * LANES, S5_BLOCK_STATES), BF16)
    cw = jax.ShapeDtypeStruct((S5_BLOCKS, S5_BLOCK_STATES, LANES), BF16)
    pr, pi, bre, bim, cre, cim, brew, bimw, crew, cimw = pl.pallas_call(
        _s5_prep_kernel, out_shape=[pw, pw, b1, b1, c1, c1, bw, bw, cw, cw], name="s5_prepare",
    )(rep(lam_re), rep(lam_im), jnp.broadcast_to(rep(log_dt[:, None]), (g * c, p)),
      tr(b_re), tr(b_im), trc(c_re), trc(c_im), tile_b, tile_c)
    pr = pr[:, ::c, :].reshape(SUBLANES, g * p)
    pi = pi[:, ::c, :].reshape(SUBLANES, g * p)
    return dict(pr=pr, pi=pi, step=(bre, bim, cre, cim), seq=(brew, bimw, crew, cimw))


def _s5_finish(y, u, dskip_ref, wglu_ref, bglu_ref):
    y = y + dskip_ref[...] * u
    y = jax.nn.gelu(y)
    return y * _sigmoid(_dot(y.astype(BF16), wglu_ref[...]) + bglu_ref[...])


def _s5_seq_kernel(x_ref, gin_ref, win_ref, pr_ref, pi_ref, lvr_ref, lvi_ref, bre_ref, bim_ref,
                   cre_ref, cim_ref, dskip_ref, wglu_ref, bglu_ref, y_ref, qm_ref, hr_ref, hi_ref,
                   xr_scr, xi_scr, y_scr, *, tc):
    @pl.when(pl.program_id(1) == 0)
    def _():
        hr_ref[...] = jnp.zeros_like(hr_ref)
        hi_ref[...] = jnp.zeros_like(hi_ref)

    hin = _rms(x_ref[0], gin_ref[...]).astype(BF16)
    u = _dot(hin, win_ref[:, :SEQ_WIDTH])
    qm_ref[0] = _dot(hin, win_ref[:, SEQ_WIDTH:])
    step = lax.broadcasted_iota(jnp.int32, (tc, 1), 0) % SUBLANES
    window = [u.astype(BF16)] + [
        jnp.where(step >= j, pltpu.roll(u, j, 0), 0.0).astype(BF16) for j in range(1, S5_WINDOW)]
    nb = S5_BLOCK_STATES
    last = SUBLANES - 1
    bc = lambda h: jnp.broadcast_to(h[last:last + 1, :], h.shape)
    block = lambda k: slice(k * nb, (k + 1) * nb)

    def project_in(k):
        cols = slice(k * LANES, (k + 1) * LANES)
        uk = jnp.concatenate([uj[:, cols] for uj in window], axis=1)
        xr_scr[k] = _dot(uk, bre_ref[k])
        xi_scr[k] = _dot(uk, bim_ref[k])

    def scan(k):
        sl = block(k)
        pr = pr_ref[:, sl]
        pi = pi_ref[:, sl]
        cr, ci = bc(hr_ref[0, :, sl]), bc(hi_ref[0, :, sl])
        for g in range(tc // SUBLANES):
            rows = slice(g * SUBLANES, (g + 1) * SUBLANES)
            xr = xr_scr[k, rows, :]
            xi = xi_scr[k, rows, :]
            sr = pltpu.roll(xr, 4, 0)
            si = pltpu.roll(xi, 4, 0)
            ar = lvr_ref[:, sl]
            ai = lvi_ref[:, sl]
            xr, xi = xr + ar * sr - ai * si, xi + ar * si + ai * sr
            hr = xr + pr * cr - pi * ci
            hi = xi + pr * ci + pi * cr
            xr_scr[k, rows, :] = hr
            xi_scr[k, rows, :] = hi
            cr, ci = bc(hr), bc(hi)
        hr_ref[0, :, sl] = cr
        hi_ref[0, :, sl] = ci

    def project_out(k):
        y_scr[:, k * LANES:(k + 1) * LANES] = (
            _dot(xr_scr[k].astype(BF16), cre_ref[k]) - _dot(xi_scr[k].astype(BF16), cim_ref[k]))

    project_in(0)
    for k in range(S5_BLOCKS):
        if k + 1 < S5_BLOCKS:
            project_in(k + 1)
        scan(k)
        project_out(k)

    y_ref[0] = _s5_finish(y_scr[...], u, dskip_ref, wglu_ref, bglu_ref)


def s5_mix_seq(x, g_in, w_in, prep, d_skip, w_glu, b_glu, tc_cap=512):
    b, t, d = x.shape
    w = SEQ_WIDTH
    tc = _row_tile(t, tc_cap)
    pr, pi = prep["pr"], prep["pi"]
    bre, bim, cre, cim = prep["seq"]
    late = np.arange(SUBLANES)[:, None] >= S5_WINDOW
    lvr = jnp.where(late, pr[S5_WINDOW - 1][None], 0.0)
    lvi = jnp.where(late, pi[S5_WINDOW - 1][None], 0.0)
    full = lambda a: pl.BlockSpec(a.shape, lambda i, j: (0,) * a.ndim)
    consts = [g_in.reshape(1, d), w_in, pr, pi, lvr, lvi, bre, bim, cre, cim, d_skip.reshape(1, w),
              w_glu, b_glu.reshape(1, w)]
    tok = lambda n: pl.BlockSpec((1, tc, n), lambda i, j: (i, j, 0))
    y, qm, hr, hi = pl.pallas_call(
        functools.partial(_s5_seq_kernel, tc=tc),
        grid=(b, t // tc),
        in_specs=[tok(d)] + [full(a) for a in consts],
        out_specs=[tok(w), tok(MEM_WIDTH),
                   pl.BlockSpec((1, SUBLANES, SSM_WIDTH), lambda i, j: (i, 0, 0)),
                   pl.BlockSpec((1, SUBLANES, SSM_WIDTH), lambda i, j: (i, 0, 0))],
        out_shape=[jax.ShapeDtypeStruct((b, t, w), F32),
                   jax.ShapeDtypeStruct((b, t, MEM_WIDTH), F32),
                   jax.ShapeDtypeStruct((b, SUBLANES, SSM_WIDTH), F32),
                   jax.ShapeDtypeStruct((b, SUBLANES, SSM_WIDTH), F32)],
        scratch_shapes=[pltpu.VMEM((S5_BLOCKS, tc, S5_BLOCK_STATES), F32),
                        pltpu.VMEM((S5_BLOCKS, tc, S5_BLOCK_STATES), F32),
                        pltpu.VMEM((tc, w), F32)],
        compiler_params=_params(("parallel", "arbitrary")),
        name="s5_mix_seq",
    )(x, *consts)
    return y, qm, hr[:, SUBLANES - 1], hi[:, SUBLANES - 1]


def _s5_step_kernel(u_ref, h0r_ref, h0i_ref, ar_ref, ai_ref, bre_ref, bim_ref, cre_ref, cim_ref,
                    dskip_ref, wglu_ref, bglu_ref, y_ref, hr_ref, hi_ref, y_scr):
    u = u_ref[...]
    ub = u.astype(BF16)
    nb = SSM_BLOCK_STATES
    for k in range(SSM_BLOCKS):
        sl = slice(k * nb, (k + 1) * nb)
        uk = ub[:, k * MXU_DIM:(k + 1) * MXU_DIM]
        ar, ai = ar_ref[:, sl], ai_ref[:, sl]
        h0r, h0i = h0r_ref[:, sl], h0i_ref[:, sl]
        hr = _dot(uk, bre_ref[k]) + ar * h0r - ai * h0i
        hi = _dot(uk, bim_ref[k]) + ar * h0i + ai * h0r
        hr_ref[:, sl] = hr
        hi_ref[:, sl] = hi
        y_scr[:, k * MXU_DIM:(k + 1) * MXU_DIM] = (
            _dot(hr.astype(BF16), cre_ref[k]) - _dot(hi.astype(BF16), cim_ref[k]))
    y_ref[...] = _s5_finish(y_scr[...], u, dskip_ref, wglu_ref, bglu_ref)


def s5_mix_step(u, h0r, h0i, prep, d_skip, w_glu, b_glu):
    s, w = u.shape
    pr, pi = prep["pr"], prep["pi"]
    bre, bim, cre, cim = prep["step"]
    st = jax.ShapeDtypeStruct((s, SSM_WIDTH), F32)
    return pl.pallas_call(
        _s5_step_kernel,
        out_shape=[jax.ShapeDtypeStruct((s, w), F32), st, st],
        scratch_shapes=[pltpu.VMEM((s, w), F32)],
        name="s5_mix_step",
    )(u, h0r, h0i, pr[0:1], pi[0:1], bre, bim, cre, cim, d_skip.reshape(1, w), w_glu,
      b_glu.reshape(1, w))


FOX_TQ = 2 * KV_TILE
FOX_STRIP = MXU_DIM


def _fox_seq_kernel(q_ref, kaug_ref, vt_ref, tri_ref, o_ref, s_scr, p_scr, acc_scr, m_scr, l_scr,
                    smax_scr, *, tq, tk):
    i = pl.program_id(2)
    qt = q_ref[0].astype(F32)
    row = lax.broadcasted_iota(jnp.int32, (LANES, 1), 0)
    qa = []
    for h in range(2):
        own = (row >= h * HEAD_DIM) & (row < (h + 1) * HEAD_DIM)
        head = 2 * pl.program_id(1) + h
        ones = (row >= head * BIAS_PIECES) & (row < (head + 1) * BIAS_PIECES)
        qa.append(jnp.concatenate(
            [jnp.where(own, qt, 0.0), jnp.broadcast_to(jnp.where(ones, 1.0, 0.0), qt.shape)],
            axis=0).astype(BF16))
    acc_scr[...] = jnp.zeros_like(acc_scr)
    m_scr[...] = jnp.full_like(m_scr, NEG_INF)
    l_scr[...] = jnp.zeros_like(l_scr)
    n_sub = tq // FOX_STRIP
    strips = [(h, qs) for h in range(2) for qs in range(n_sub)]
    lanes = lambda qs: slice(qs * FOX_STRIP, (qs + 1) * FOX_STRIP)
    ones_rows = jnp.ones((2 * SUBLANES, tk), BF16)

    def visible_keys(qs, c):
        if c is None:
            return tk, 0
        first = qs * FOX_STRIP - c * tk
        if first < 0:
            return 0, 0
        return (tk, 0) if first >= tk else (first, FOX_STRIP)

    def scores(j, slot, c=None):
        ka = kaug_ref[0, pl.ds(pl.multiple_of(j * tk, tk), tk), :]
        for n, (h, qs) in enumerate(strips):
            if sum(visible_keys(qs, c)):
                s = _dot(ka, qa[h][:, lanes(qs)])
                s_scr[slot, n] = s
                if c is None:
                    smax_scr[slot, n] = jnp.max(s, axis=0, keepdims=True)

    def absorb(j, slot, c=None):
        alphas, n_keys = [], []
        for n, (h, qs) in enumerate(strips):
            n_full_keys, n_tri = visible_keys(qs, c)
            n_keys.append(n_full_keys + n_tri)
            if not n_keys[n]:
                alphas.append(None)
                continue
            m = m_scr[n]
            full_rows = slice(0, n_full_keys)
            tri_rows = slice(n_full_keys, n_full_keys + n_tri)
            m_new = m
            if n_tri:
                tri = s_scr[slot, n, tri_rows, :] + tri_ref[...]
                m_new = jnp.maximum(m_new, jnp.max(tri, axis=0, keepdims=True))
            if n_full_keys:
                m_new = jnp.maximum(m_new, smax_scr[slot, n] if c is None else jnp.max(
                    s_scr[slot, n, full_rows, :], axis=0, keepdims=True))
                p_scr[n, full_rows, :] = jnp.exp2(
                    s_scr[slot, n, full_rows, :] - m_new).astype(BF16)
            if n_tri:
                p_scr[n, tri_rows, :] = jnp.exp2(tri - m_new).astype(BF16)
            m_scr[n] = m_new
            alphas.append(jnp.exp2(m - m_new))
        for n, (h, qs) in enumerate(strips):
            if not n_keys[n]:
                continue
            vt = jnp.concatenate(
                [vt_ref[0, j, h * HEAD_DIM:(h + 1) * HEAD_DIM, :n_keys[n]],
                 ones_rows[:, :n_keys[n]]], axis=0)
            pv = _dot(vt, p_scr[n, :n_keys[n], :])
            acc_scr[h, :, lanes(qs)] = alphas[n] * acc_scr[h, :, lanes(qs)] + pv[:HEAD_DIM]
            l_scr[n] = alphas[n] * l_scr[n] + pv[HEAD_DIM:HEAD_DIM + 1]

    n_before = 2 * i
    scores(0, 0)

    def pair(jj, _):
        j = 2 * jj
        scores(j + 1, 1)
        absorb(j, 0)
        scores(j + 2, 0)
        absorb(j + 1, 1)
        return 0

    lax.fori_loop(0, i, pair, 0)
    if tq == tk:
        absorb(0, 0, c=0)
    else:
        scores(n_before + 1, 1, c=1)
        absorb(n_before, 0, c=0)
        absorb(n_before + 1, 1, c=1)

    l_head = lambda h: jnp.concatenate(
        [l_scr[h * n_sub + qs] for qs in range(n_sub)], axis=1)
    ot = jnp.concatenate([acc_scr[0] / l_head(0), acc_scr[1] / l_head(1)], axis=0)
    o_ref[0] = jnp.transpose(ot)


def fox_attention_seq(qt, kaug, vtb):
    b, w, t = qt.shape
    n_chunks, tk = vtb.shape[1], vtb.shape[3]
    tq = _row_tile(t, FOX_TQ)
    assert tq in (tk, 2 * tk) and n_chunks * tk == t and tk % FOX_STRIP == 0
    n_strips = 2 * tq // FOX_STRIP
    idx = np.arange(FOX_STRIP)
    tri = jnp.asarray(np.where(idx[:, None] <= idx[None, :], 0.0, NEG_INF), F32)
    return pl.pallas_call(
        functools.partial(_fox_seq_kernel, tq=tq, tk=tk),
        grid=(b, HEAD_PAIRS, t // tq),
        in_specs=[pl.BlockSpec((1, LANES, tq), lambda bi, p, i: (bi, p, i)),
                  pl.BlockSpec((1, t, KAUG_WIDTH), lambda bi, p, i: (bi, 0, p)),
                  pl.BlockSpec((1, n_chunks, LANES, tk), lambda bi, p, i: (bi, 0, p, 0)),
                  pl.BlockSpec((FOX_STRIP, FOX_STRIP), lambda bi, p, i: (0, 0))],
        out_specs=pl.BlockSpec((1, tq, LANES), lambda bi, p, i: (bi, i, p)),
        out_shape=jax.ShapeDtypeStruct((b, t, w), F32),
        scratch_shapes=[pltpu.VMEM((2, n_strips, tk, FOX_STRIP), F32),
                        pltpu.VMEM((n_strips, tk, FOX_STRIP), BF16),
                        pltpu.VMEM((2, HEAD_DIM, tq), F32),
                        pltpu.VMEM((n_strips, 1, FOX_STRIP), F32),
                        pltpu.VMEM((n_strips, 1, FOX_STRIP), F32),
                        pltpu.VMEM((2, n_strips, 1, FOX_STRIP), F32)],
        compiler_params=_params(("parallel", "parallel", "arbitrary")),
        name="fox_attention_seq",
    )(qt, kaug, vtb, tri)


DEC_PAGES = 16
DEC_SLOTS = 3


def _fox_dec_kernel(pt_ref, q_ref, kn_ref, vn_ref, lfn_ref, tri_ref, kt_hbm, vt_hbm, lf_hbm, o_ref,
                    m_scr, l_scr, csum_scr, acc_scr, qcol_scr, kbuf, vbuf, lbuf, sems,
                    *, page, n_pages):
    g = pl.program_id(1)
    n_g = pl.num_programs(1)
    t = pl.program_id(0) * n_g + g
    n_steps = pl.num_programs(0) * n_g
    hp = FOX_HEADS
    qrow = q_ref[0] * ATTN_SCALE

    def page_copies(step):
        slot = step % DEC_SLOTS
        seq, grp = step // n_g, step % n_g
        cps = []
        for i in range(n_pages):
            pid = pt_ref[seq, grp * n_pages + i]
            cps.append(pltpu.make_async_copy(kt_hbm.at[pid], kbuf.at[slot, i], sems.at[0, slot]))
            cps.append(pltpu.make_async_copy(vt_hbm.at[pid], vbuf.at[slot, i], sems.at[1, slot]))
            cps.append(pltpu.make_async_copy(lf_hbm.at[pid], lbuf.at[slot, i], sems.at[2, slot]))
        return cps

    @pl.when(t == 0)
    def _():
        for step in range(DEC_SLOTS - 1):
            @pl.when(step < n_steps)
            def _():
                for cp in page_copies(step):
                    cp.start()

    @pl.when(t + (DEC_SLOTS - 1) < n_steps)
    def _():
        for cp in page_copies(t + (DEC_SLOTS - 1)):
            cp.start()

    for cp in page_copies(t):
        cp.wait()
    slot = t % DEC_SLOTS

    @pl.when(g == 0)
    def _():
        m_scr[...] = jnp.full_like(m_scr, NEG_INF)
        l_scr[...] = jnp.zeros_like(l_scr)
        csum_scr[...] = jnp.zeros_like(csum_scr)
        acc_scr[...] = jnp.zeros_like(acc_scr)
        qcol_scr[...] = jnp.transpose(jnp.broadcast_to(qrow, (page, SEQ_WIDTH)))

    tri = tri_ref[...]
    base = csum_scr[...]
    head_row = lax.broadcasted_iota(jnp.int32, (hp, page), 0)
    s_parts = []
    for i in range(n_pages):
        hi, mid, lo = _split3(lbuf[slot, i])
        cum = (_dot(hi, tri) + _dot(mid, tri)) + _dot(lo, tri) + base
        base = jnp.broadcast_to(cum[:, page - 1:page], cum.shape)
        qk = jnp.zeros((hp, page), F32)
        for h in range(FOX_HEADS):
            rows = slice(h * HEAD_DIM, (h + 1) * HEAD_DIM)
            r = jnp.sum(kbuf[slot, i, rows, :] * qcol_scr[rows, :], axis=0, keepdims=True)
            qk = jnp.where(head_row == h, r, qk)
        s_parts.append(qk - cum)
    csum_scr[...] = base
    s = jnp.concatenate(s_parts, axis=1)

    m_old = m_scr[...]
    m_new = jnp.maximum(m_old, jnp.max(s, axis=-1, keepdims=True))
    alpha = jnp.exp(m_old - m_new)
    p = jnp.exp(s - m_new)
    l_scr[...] = alpha * l_scr[...] + jnp.sum(p, axis=-1, keepdims=True)
    m_scr[...] = m_new
    for h in range(FOX_HEADS):
        rows = slice(h * HEAD_DIM, (h + 1) * HEAD_DIM)
        acc = acc_scr[rows, :] * alpha[h:h + 1, :]
        for i in range(n_pages):
            acc = acc + vbuf[slot, i, rows, :] * p[h:h + 1, i * page:(i + 1) * page]
        acc_scr[rows, :] = acc

    @pl.when(g == pl.num_programs(1) - 1)
    def _():
        lane = lax.broadcasted_iota(jnp.int32, (hp, SEQ_WIDTH), 1)
        head = lax.broadcasted_iota(jnp.int32, (hp, SEQ_WIDTH), 0)
        own = (lane >= head * HEAD_DIM) & (lane < (head + 1) * HEAD_DIM)
        spread = lambda col: jnp.sum(jnp.where(own, col, 0.0), axis=0, keepdims=True)
        c_new = csum_scr[:, 0:1] + lfn_ref[0]
        s_new = jnp.sum(jnp.where(own, qrow * kn_ref[0], 0.0), axis=-1, keepdims=True) - c_new
        m_fin = jnp.maximum(m_scr[...], s_new)
        a_fin = jnp.exp(m_scr[...] - m_fin)
        p_new = jnp.exp(s_new - m_fin)
        l_fin = a_fin * l_scr[...] + p_new
        acc_row = jnp.sum(jnp.transpose(acc_scr[...]), axis=0, keepdims=True)
        o_ref[0] = (spread(a_fin) * acc_row + spread(p_new) * vn_ref[0]) / spread(l_fin)


def fox_attention_decode(q, k_new, v_new, logf_new, cache_k, cache_v, cache_logf, page_table):
    s, w = q.shape
    n_pool, page = cache_k.shape[:2]
    assert page == LANES
    pages_per_seq = page_table.shape[1]
    n_pages = min(DEC_PAGES, pages_per_seq)
    assert pages_per_seq % n_pages == 0
    hp = FOX_HEADS
    ckt = jnp.transpose(cache_k, (0, 2, 3, 1)).reshape(n_pool, w, page)
    cvt = jnp.transpose(cache_v, (0, 2, 3, 1)).reshape(n_pool, w, page)
    clf = jnp.transpose(cache_logf, (0, 2, 1))
    lfn = logf_new.reshape(s, hp, 1)
    tri = jnp.asarray(np.triu(np.ones((page, page), np.float32)), BF16)
    row = pl.BlockSpec((1, 1, w), lambda b, g, pt: (b, 0, 0))
    in_hbm = pl.BlockSpec(memory_space=pl.ANY)
    grid_spec = pltpu.PrefetchScalarGridSpec(
        num_scalar_prefetch=1,
        grid=(s, pages_per_seq // n_pages),
        in_specs=[row, row, row, pl.BlockSpec((1, hp, 1), lambda b, g, pt: (b, 0, 0)),
                  pl.BlockSpec((page, page), lambda b, g, pt: (0, 0)), in_hbm, in_hbm, in_hbm],
        out_specs=row,
        scratch_shapes=[pltpu.VMEM((hp, 1), F32), pltpu.VMEM((hp, 1), F32),
                        pltpu.VMEM((hp, page), F32), pltpu.VMEM((w, page), F32),
                        pltpu.VMEM((w, page), F32),
                        pltpu.VMEM((DEC_SLOTS, n_pages, w, page), F32),
                        pltpu.VMEM((DEC_SLOTS, n_pages, w, page), F32),
                        pltpu.VMEM((DEC_SLOTS, n_pages, hp, page), F32),
                        pltpu.SemaphoreType.DMA((3, DEC_SLOTS))],
    )
    out = pl.pallas_call(
        functools.partial(_fox_dec_kernel, page=page, n_pages=n_pages),
        grid_spec=grid_spec,
        out_shape=jax.ShapeDtypeStruct((s, 1, w), F32),
        compiler_params=_params(("arbitrary", "arbitrary"), VMEM_LIMIT),
        name="fox_attention_decode",
    )(page_table, q.reshape(s, 1, w), k_new.reshape(s, 1, w), v_new.reshape(s, 1, w), lfn, tri,
      ckt, cvt, clf)
    return out.reshape(s, w)


def _trunk(x, mem_k, mem_v, conv_prev, ssm_state, fox_attend, p, s5_prep, sequential):
    assert N_A == 1
    b, t, d = x.shape
    m = b * t
    new_conv, ssm_out, kv = [], None, None
    for l in range(DEPTH):
        if sequential and l == N_A:
            kv = shared_kv_proj_seq(x, p["kv_norm"], p["w_kv"], p["b_f"],
                                    p["norm_mix_pre"][l], p["w_in"][l])
            z_seq, q_mem = kv[5], kv[6]
        elif not sequential:
            if l == N_A:
                kv = shared_kv_proj_step(x.reshape(m, d), p["kv_norm"], p["w_kv"], p["b_f"])
            z_seq, q_mem = norm_linear(x.reshape(m, d), p["norm_mix_pre"][l], p["w_in"][l],
                                       (SEQ_WIDTH, MEM_WIDTH))
        if l < N_A:
            if sequential:
                seq_out, q_mem, hr, hi = s5_mix_seq(x, p["norm_mix_pre"][l], p["w_in"][l],
                                                    s5_prep[l], p["d_skip"][l], p["w_glu"][l],
                                                    p["b_glu"][l])
            else:
                seq_out, hr, hi = s5_mix_step(z_seq, ssm_state[0][l], ssm_state[1][l], s5_prep[l],
                                              p["d_skip"][l], p["w_glu"][l], p["b_glu"][l])
            ssm_out = (hr.reshape(b, SSM_GROUPS, SSM_STATE), hi.reshape(b, SSM_GROUPS, SSM_STATE))
        else:
            seq_out = fox_attend(z_seq, kv)
        ffn_args = (p["norm_ffn_pre"][l], p["norm_ffn_post"][l], p["w_up"], p["conv_w"][l],
                    p["conv_b"][l], p["w_down"], l)
        if sequential:
            x3, cp = layer_tail_seq(x, seq_out.reshape(b, t, SEQ_WIDTH),
                                    q_mem.reshape(b, t, MEM_WIDTH), mem_k[l], mem_v[l],
                                    p["w_out"][l], p["norm_mix_post"][l], conv_prev[l], *ffn_args)
        else:
            mem_out = mem_attention(q_mem.reshape(b, t, MEM_WIDTH), mem_k, mem_v, l)
            x2 = mix_out(x.reshape(m, d), seq_out, mem_out.reshape(m, MEM_WIDTH),
                         p["w_out"][l], p["norm_mix_post"][l])
            x3, u_new = conv_ffn_step(x2, conv_prev[l], *ffn_args)
            cp = jnp.stack([conv_prev[l][:, 1], u_new], axis=1)
        new_conv.append(cp)
        x = x3.reshape(b, t, d)
    return x, ssm_out, jnp.stack(new_conv), kv


def kernel(x_prompt, x_sample, state_ssm_re, state_ssm_im, cache_k, cache_v, cache_logf,
           cache_mem_k, cache_mem_v, state_ffn_conv, page_table, mem_prompt,
           w_in, w_out, norm_mix_pre, norm_mix_post, norm_ffn_pre, norm_ffn_post,
           mem_norm, w_mem_kv, lam_re, lam_im, log_dt, b_re, b_im, c_re, c_im, d_skip,
           w_glu, b_glu, kv_norm, w_kv, b_f, w_up, conv_w, conv_b, w_down):
    per_layer_bf16 = lambda w: [w[l].astype(BF16) for l in range(w.shape[0])]
    p = dict(w_in=per_layer_bf16(w_in), w_out=per_layer_bf16(w_out), norm_mix_pre=norm_mix_pre,
             norm_mix_post=norm_mix_post, norm_ffn_pre=norm_ffn_pre, norm_ffn_post=norm_ffn_post,
             d_skip=d_skip, w_glu=per_layer_bf16(w_glu), b_glu=b_glu, kv_norm=kv_norm, w_kv=w_kv,
             b_f=b_f, w_up=w_up.astype(BF16), conv_w=conv_w, conv_b=conv_b,
             w_down=w_down.astype(BF16))
    s5_prep = [s5_prepare(lam_re[l], lam_im[l], log_dt[l], b_re[l], b_im[l], c_re[l], c_im[l])
               for l in range(N_A)]

    bp, tp, d = x_prompt.shape
    n_mem = mem_prompt.shape[1]
    mem_pairs = [mem_kv_proj(mem_prompt, mem_norm[l], w_mem_kv[l]) for l in range(DEPTH)]
    p_mem_kt = [mkt for mkt, _ in mem_pairs]
    p_mem_vt = [mvt for _, mvt in mem_pairs]
    zeros_conv = jnp.zeros((DEPTH, bp, CONV_W - 1, 2 * D_FF), F32)

    def fox_prompt(qt, kv):
        return fox_attention_seq(qt, kv[2], kv[3])

    y_prompt, p_ssm, p_conv, p_kv = _trunk(x_prompt, p_mem_kt, p_mem_vt, zeros_conv, None,
                                           fox_prompt, p, s5_prep, sequential=True)
    untr = lambda a, n: jnp.transpose(a.reshape(a.shape[0], n, HEAD_DIM, a.shape[2]), (0, 3, 1, 2))
    mem5 = lambda ms: jnp.stack([untr(a, MEM_HEADS) for a in ms])

    bs = x_sample.shape[0]
    tr_mem = lambda a: jnp.transpose(a, (0, 1, 3, 4, 2)).reshape(DEPTH, bs, MEM_WIDTH, n_mem)
    s_mem_kt = tr_mem(cache_mem_k)
    s_mem_vt = tr_mem(cache_mem_v)
    ssm0 = (state_ssm_re.reshape(N_A, bs, SSM_WIDTH), state_ssm_im.reshape(N_A, bs, SSM_WIDTH))

    def fox_sample(q, kv):
        k, v, logf = kv
        return fox_attention_decode(q, k, v, logf, cache_k, cache_v, cache_logf, page_table)

    y_sample, s_ssm, s_conv, s_kv = _trunk(x_sample, s_mem_kt, s_mem_vt, state_ffn_conv, ssm0,
                                           fox_sample, p, s5_prep, sequential=False)
    head4 = lambda a: a.reshape(bs, 1, FOX_HEADS, HEAD_DIM)

    return (y_prompt, y_sample, p_ssm[0][None], p_ssm[1][None],
            untr(p_kv[0], FOX_HEADS), untr(p_kv[1], FOX_HEADS), p_kv[4],
            mem5(p_mem_kt), mem5(p_mem_vt), p_conv,
            s_ssm[0][None], s_ssm[1][None],
            head4(s_kv[0]), head4(s_kv[1]), s_kv[2].reshape(bs, 1, FOX_HEADS), s_conv)
```

```python
import functools
import math

import jax
import jax.numpy as jnp
import numpy as np
from jax import lax
from jax.experimental import pallas as pl
from jax.experimental.pallas import tpu as pltpu

F32 = jnp.float32
BF16 = jnp.bfloat16

D_MODEL = 1024
DEPTH = 2
N_A = DEPTH // 2
HEAD_DIM = 64
MEM_HEADS = 4
MEM_WIDTH = MEM_HEADS * HEAD_DIM
SEQ_WIDTH = D_MODEL - MEM_WIDTH
SSM_GROUP = 16
SSM_GROUPS = SEQ_WIDTH // SSM_GROUP
SSM_STATE = 64
SSM_WIDTH = SSM_GROUPS * SSM_STATE
FOX_HEADS = SEQ_WIDTH // HEAD_DIM
D_FF = (11 * D_MODEL) // 4
CONV_W = 3
EPS = 1e-6
NEG_INF = -1e30
EIG_CLIP = -1e-4
ATTN_SCALE = HEAD_DIM ** -0.5
LOG2E = math.log2(math.e)

LANES = 128
SUBLANES = 8
MXU_DIM = 256
VMEM_BYTES_V7X = 64 * 1024 * 1024
VMEM_LIMIT = (VMEM_BYTES_V7X * 7) // 8

SSM_BLOCKS = SEQ_WIDTH // MXU_DIM
SSM_BLOCK_STATES = SSM_WIDTH // SSM_BLOCKS
S5_WINDOW = 4
S5_BLOCKS = SEQ_WIDTH // LANES
S5_BLOCK_STATES = SSM_WIDTH // S5_BLOCKS
HEAD_PAIRS = FOX_HEADS // 2


def _params(semantics, vmem=None):
    return pltpu.CompilerParams(dimension_semantics=semantics, vmem_limit_bytes=vmem)


def _row_tile(m, cap):
    t = min(m, cap)
    assert m % t == 0, (m, t)
    return t


def _rms(x, g):
    return x * lax.rsqrt(jnp.mean(x * x, axis=-1, keepdims=True) + EPS) * g


def _sigmoid(x):
    return 1.0 / (1.0 + jnp.exp(-x))


def _log_sigmoid(x):
    return -(jnp.maximum(-x, 0.0) + jnp.log1p(jnp.exp(-jnp.abs(x))))


def _split3(x):
    hi = x.astype(BF16)
    r1 = x - hi.astype(F32)
    mid = r1.astype(BF16)
    lo = (r1 - mid.astype(F32)).astype(BF16)
    return hi, mid, lo


def _dot(a, b):
    return jnp.dot(a, b, preferred_element_type=F32)


def _dot_nt(a, b):
    return lax.dot_general(a, b, (((1,), (1,)), ((), ())), preferred_element_type=F32)


def _norm_linear_kernel(x_ref, g_ref, w_ref, *out_refs, splits):
    h = _rms(x_ref[...], g_ref[...]).astype(BF16)
    c0 = 0
    for o_ref, n in zip(out_refs, splits):
        o_ref[...] = _dot(h, w_ref[:, c0:c0 + n])
        c0 += n


def norm_linear(x, g, w, splits, tm_cap=512):
    m, d = x.shape
    tm = _row_tile(m, tm_cap)
    n_tot = sum(splits)
    assert w.shape == (d, n_tot)
    return pl.pallas_call(
        functools.partial(_norm_linear_kernel, splits=tuple(splits)),
        grid=(m // tm,),
        in_specs=[pl.BlockSpec((tm, d), lambda i: (i, 0)),
                  pl.BlockSpec((1, d), lambda i: (0, 0)),
                  pl.BlockSpec((d, n_tot), lambda i: (0, 0))],
        out_specs=[pl.BlockSpec((tm, n), lambda i: (i, 0)) for n in splits],
        out_shape=[jax.ShapeDtypeStruct((m, n), F32) for n in splits],
        compiler_params=_params(("parallel",)),
        name="norm_linear",
    )(x, g.reshape(1, d), w)


KV_TILE = 512
KAUG_WIDTH = 2 * LANES
BIAS_PIECES = 3


def _kv_weights(w_kv, b_f):
    wk = w_kv[:, :SEQ_WIDTH].astype(BF16)
    wf = jnp.pad(w_kv[:, 2 * SEQ_WIDTH:], ((0, 0), (0, LANES - FOX_HEADS))).astype(BF16)
    bf = jnp.pad(b_f, (0, LANES - FOX_HEADS)).reshape(1, LANES)
    return wk, wf, bf


def _kv_seq_kernel(x_ref, g_ref, wk_ref, wvt_ref, wf_ref, bf_ref, tri_ref, place_ref,
                   gin_ref, wqt_ref, wm_ref,
                   kt_ref, vt_ref, kaug_ref, vtb_ref, lf_ref, qt_ref, qm_ref, carry_ref):
    @pl.when(pl.program_id(1) == 0)
    def _():
        carry_ref[...] = jnp.zeros_like(carry_ref)

    x = x_ref[0]
    xhat = x * lax.rsqrt(jnp.mean(x * x, axis=-1, keepdims=True) + EPS)
    hin = (xhat * gin_ref[...]).astype(BF16)
    qt_ref[0] = (_dot_nt(wqt_ref[...], hin) * (ATTN_SCALE * LOG2E)).astype(BF16)
    qm_ref[0] = _dot(hin, wm_ref[...])
    h = (xhat * g_ref[...]).astype(BF16)
    k = _dot(h, wk_ref[...])
    kt_ref[0] = jnp.transpose(k)
    vt = _dot_nt(wvt_ref[...], h)
    vt_ref[0] = vt
    vtb_ref[0, 0] = vt.astype(BF16)
    logf = _log_sigmoid(_dot(h, wf_ref[...]) + bf_ref[...])
    lf_ref[0] = logf[:, :FOX_HEADS]
    c3 = _dot(tri_ref[...], jnp.concatenate(_split3(logf), axis=1))
    cum = (c3[:, :LANES] + c3[:, LANES:2 * LANES]) + c3[:, 2 * LANES:] + carry_ref[...]
    carry_ref[...] = cum[cum.shape[0] - 1:, :]
    pieces = jnp.concatenate(_split3(cum * -LOG2E), axis=1)
    bias = _dot(pieces, place_ref[...]).astype(BF16)
    kb = k.astype(BF16)
    for p in range(HEAD_PAIRS):
        kaug_ref[0, :, p * KAUG_WIDTH:p * KAUG_WIDTH + LANES] = kb[:, p * LANES:(p + 1) * LANES]
        kaug_ref[0, :, p * KAUG_WIDTH + LANES:(p + 1) * KAUG_WIDTH] = bias


def shared_kv_proj_seq(x, g, w_kv, b_f, g_in, w_in):
    b, t, d = x.shape
    wqt = jnp.transpose(w_in[:, :SEQ_WIDTH])
    wm = w_in[:, SEQ_WIDTH:]
    tm = _row_tile(t, KV_TILE)
    wk, wf, bf = _kv_weights(w_kv, b_f)
    wvt = jnp.transpose(w_kv)[SEQ_WIDTH:2 * SEQ_WIDTH].astype(BF16)
    tri = jnp.asarray(np.tril(np.ones((tm, tm), np.float32)), BF16)
    hh = np.arange(FOX_HEADS)
    place = np.zeros((BIAS_PIECES * LANES, LANES), np.float32)
    for j in range(BIAS_PIECES):
        place[LANES * j + hh, BIAS_PIECES * hh + j] = 1.0
    place = jnp.asarray(place, BF16)
    tok = lambda n: pl.BlockSpec((1, tm, n), lambda i, j: (i, j, 0))
    tr = pl.BlockSpec((1, SEQ_WIDTH, tm), lambda i, j: (i, 0, j))
    full = lambda s: pl.BlockSpec(s, lambda i, j: (0,) * len(s))
    return pl.pallas_call(
        _kv_seq_kernel,
        grid=(b, t // tm),
        in_specs=[tok(d), full((1, d)), full((d, SEQ_WIDTH)), full((SEQ_WIDTH, d)),
                  full((d, LANES)), full((1, LANES)), full((tm, tm)),
                  full((BIAS_PIECES * LANES, LANES)),
                  full((1, d)), full((SEQ_WIDTH, d)), full((d, MEM_WIDTH))],
        out_specs=[tr, tr, tok(HEAD_PAIRS * KAUG_WIDTH),
                   pl.BlockSpec((1, 1, SEQ_WIDTH, tm), lambda i, j: (i, j, 0, 0)),
                   tok(FOX_HEADS), tr, tok(MEM_WIDTH)],
        out_shape=[jax.ShapeDtypeStruct((b, SEQ_WIDTH, t), F32),
                   jax.ShapeDtypeStruct((b, SEQ_WIDTH, t), F32),
                   jax.ShapeDtypeStruct((b, t, HEAD_PAIRS * KAUG_WIDTH), BF16),
                   jax.ShapeDtypeStruct((b, t // tm, SEQ_WIDTH, tm), BF16),
                   jax.ShapeDtypeStruct((b, t, FOX_HEADS), F32),
                   jax.ShapeDtypeStruct((b, SEQ_WIDTH, t), BF16),
                   jax.ShapeDtypeStruct((b, t, MEM_WIDTH), F32)],
        scratch_shapes=[pltpu.VMEM((1, LANES), F32)],
        compiler_params=_params(("parallel", "arbitrary")),
        name="shared_kv_proj_seq",
    )(x, g.reshape(1, d), wk, wvt, wf, bf, tri, place, g_in.reshape(1, d), wqt, wm)


def _kv_step_kernel(x_ref, g_ref, wk_ref, wv_ref, wf_ref, bf_ref, k_ref, v_ref, lf_ref):
    h = _rms(x_ref[...], g_ref[...]).astype(BF16)
    k_ref[...] = _dot(h, wk_ref[...])
    v_ref[...] = _dot(h, wv_ref[...])
    lf_ref[...] = _log_sigmoid(_dot(h, wf_ref[...]) + bf_ref[...])[:, :FOX_HEADS]


def shared_kv_proj_step(x, g, w_kv, b_f):
    s, d = x.shape
    wk, wf, bf = _kv_weights(w_kv, b_f)
    wv = w_kv[:, SEQ_WIDTH:2 * SEQ_WIDTH].astype(BF16)
    kv = jax.ShapeDtypeStruct((s, SEQ_WIDTH), F32)
    return pl.pallas_call(
        _kv_step_kernel,
        out_shape=[kv, kv, jax.ShapeDtypeStruct((s, FOX_HEADS), F32)],
        name="shared_kv_proj_step",
    )(x, g.reshape(1, d), wk, wv, wf, bf)


def _mem_kv_kernel(x_ref, g_ref, wt_ref, kt_ref, vt_ref):
    h = _rms(x_ref[0], g_ref[...]).astype(BF16)
    kvt = _dot_nt(wt_ref[...], h)
    kt_ref[0] = kvt[:MEM_WIDTH]
    vt_ref[0] = kvt[MEM_WIDTH:]


def mem_kv_proj(mem, g, w):
    b, n_mem, d = mem.shape
    out = jax.ShapeDtypeStruct((b, MEM_WIDTH, n_mem), F32)
    blk = pl.BlockSpec((1, MEM_WIDTH, n_mem), lambda i: (i, 0, 0))
    return pl.pallas_call(
        _mem_kv_kernel,
        grid=(b,),
        in_specs=[pl.BlockSpec((1, n_mem, d), lambda i: (i, 0, 0)),
                  pl.BlockSpec((1, d), lambda i: (0, 0)),
                  pl.BlockSpec((2 * MEM_WIDTH, d), lambda i: (0, 0))],
        out_specs=[blk, blk],
        out_shape=[out, out],
        compiler_params=_params(("parallel",)),
        name="mem_kv_proj",
    )(mem, g.reshape(1, d), jnp.transpose(w).astype(BF16))


def _mem_attn_tile(q, mkt, mvt):
    q = q * ATTN_SCALE
    mkt = mkt.astype(BF16)
    mvt = mvt.astype(BF16)
    lane = lax.broadcasted_iota(jnp.int32, (1, MEM_WIDTH), 1)
    out = jnp.zeros(q.shape, F32)
    for h in range(MEM_HEADS):
        in_head = (lane >= h * HEAD_DIM) & (lane < (h + 1) * HEAD_DIM)
        s = _dot(jnp.where(in_head, q, 0.0).astype(BF16), mkt)
        p = jnp.exp(s - jnp.max(s, axis=-1, keepdims=True))
        p = p / jnp.sum(p, axis=-1, keepdims=True)
        out = out + jnp.where(in_head, _dot_nt(p.astype(BF16), mvt), 0.0)
    return out


def _mem_attn_row(q, mkt, mvt):
    lane = lax.broadcasted_iota(jnp.int32, (MEM_HEADS, MEM_WIDTH), 1)
    head = lax.broadcasted_iota(jnp.int32, (MEM_HEADS, MEM_WIDTH), 0)
    own = (lane >= head * HEAD_DIM) & (lane < (head + 1) * HEAD_DIM)
    s = _dot(jnp.where(own, q * ATTN_SCALE, 0.0).astype(BF16), mkt.astype(BF16))
    p = jnp.exp(s - jnp.max(s, axis=-1, keepdims=True))
    p = p / jnp.sum(p, axis=-1, keepdims=True)
    o = _dot_nt(p.astype(BF16), mvt.astype(BF16))
    return jnp.sum(jnp.where(own, o, 0.0), axis=0, keepdims=True)


def _mem_attn_kernel(q_ref, mkt_ref, mvt_ref, o_ref):
    attend = _mem_attn_row if q_ref.shape[1] == 1 else _mem_attn_tile
    for s in range(q_ref.shape[0]):
        o_ref[s] = attend(q_ref[s], mkt_ref[s], mvt_ref[s])


def mem_attention(q_mem, mkt, mvt, layer, seqs_per_step=SUBLANES):
    b, t, w = q_mem.shape
    n_mem = mkt.shape[3]
    bs = _row_tile(b, seqs_per_step)
    mem = pl.BlockSpec((None, bs, w, n_mem), lambda i: (layer, i, 0, 0))
    return pl.pallas_call(
        _mem_attn_kernel,
        grid=(b // bs,),
        in_specs=[pl.BlockSpec((bs, t, w), lambda i: (i, 0, 0)), mem, mem],
        out_specs=pl.BlockSpec((bs, t, w), lambda i: (i, 0, 0)),
        out_shape=jax.ShapeDtypeStruct((b, t, w), F32),
        compiler_params=_params(("parallel",)),
        name="mem_attention",
    )(q_mem, mkt, mvt)


def _mix_out_tile(x, seq, mem, w_ref, g):
    o = (_dot(seq.astype(BF16), w_ref[:SEQ_WIDTH, :]) + _dot(mem.astype(BF16), w_ref[SEQ_WIDTH:, :]))
    return x + _rms(o, g)


def _mix_out_kernel(x_ref, s_ref, m_ref, w_ref, g_ref, o_ref):
    o_ref[...] = _mix_out_tile(x_ref[...], s_ref[...], m_ref[...], w_ref, g_ref[...])


def mix_out(x, seq_out, mem_out, w_out, g, tm_cap=512):
    m, d = x.shape
    tm = _row_tile(m, tm_cap)
    row = lambda n: pl.BlockSpec((tm, n), lambda i: (i, 0))
    return pl.pallas_call(
        _mix_out_kernel,
        grid=(m // tm,),
        in_specs=[row(d), row(SEQ_WIDTH), row(MEM_WIDTH),
                  pl.BlockSpec((d, d), lambda i: (0, 0)),
                  pl.BlockSpec((1, d), lambda i: (0, 0))],
        out_specs=row(d),
        out_shape=jax.ShapeDtypeStruct((m, d), F32),
        compiler_params=_params(("parallel",)),
        name="mix_out",
    )(x, seq_out, mem_out, w_out, g.reshape(1, d))


FFN_CHUNK = MXU_DIM
FFN_STEP_CHUNK = D_FF // 2
assert FFN_STEP_CHUNK % LANES == 0


def _tail_seq_kernel(x_ref, seq_ref, qm_ref, mkt_ref, mvt_ref, wout_ref, gmix_ref, prev_ref,
                     gpre_ref, gpost_ref, wup_ref, cw_ref, cb_ref, wdn_ref,
                     o_ref, conv_ref, h_scr, carry_scr, *, tm):
    @pl.when(pl.program_id(1) == 0)
    def _():
        carry_scr[...] = prev_ref[0]

    mem = _mem_attn_tile(qm_ref[0], mkt_ref[0], mvt_ref[0])
    x = _mix_out_tile(x_ref[0], seq_ref[0], mem, wout_ref, gmix_ref[...])
    xn = _rms(x, gpre_ref[...]).astype(BF16)
    row = lax.broadcasted_iota(jnp.int32, (SUBLANES, 1), 0)

    def up(col):
        return _dot(xn, wup_ref[:, col:col + FFN_CHUNK])

    def conv(u, col):
        c0 = carry_scr[0:1, col:col + FFN_CHUNK]
        c1 = carry_scr[1:2, col:col + FFN_CHUNK]
        u1 = pltpu.roll(u, 1, 0)
        u2 = pltpu.roll(u, 2, 0)
        u1 = jnp.concatenate([jnp.where(row == 0, c1, u1[:SUBLANES]), u1[SUBLANES:]], axis=0)
        u2 = jnp.concatenate(
            [jnp.where(row == 0, c0, jnp.where(row == 1, c1, u2[:SUBLANES])), u2[SUBLANES:]],
            axis=0)
        carry_scr[:, col:col + FFN_CHUNK] = u[tm - 2:, :]
        w = cw_ref[:, col:col + FFN_CHUNK]
        return u2 * w[0:1] + u1 * w[1:2] + u * w[2:3] + cb_ref[:, col:col + FFN_CHUNK]

    for c in range(D_FF // FFN_CHUNK):
        gate = conv(up(c * FFN_CHUNK), c * FFN_CHUNK)
        val = conv(up(D_FF + c * FFN_CHUNK), D_FF + c * FFN_CHUNK)
        h_scr[:, c * FFN_CHUNK:(c + 1) * FFN_CHUNK] = (gate * _sigmoid(gate) * val).astype(BF16)

    f = _dot(h_scr[...], wdn_ref[...])
    o_ref[0] = x + _rms(f, gpost_ref[...])
    conv_ref[0] = carry_scr[...]


def layer_tail_seq(x, seq_out, q_mem, mkt, mvt, w_out, g_mix, prev, g_pre, g_post, w_up, conv_w,
                   conv_b, w_down, layer, tm_cap=512):
    b, t, d = x.shape
    n_mem = mkt.shape[2]
    tm = _row_tile(t, tm_cap)
    assert tm >= 2 * SUBLANES
    f2 = 2 * D_FF
    full = lambda s: pl.BlockSpec(s, lambda i, j: (0,) * len(s), pipeline_mode=pl.Buffered(1))
    of_layer = lambda s: pl.BlockSpec((None,) + s, lambda i, j: (layer,) + (0,) * len(s),
                                      pipeline_mode=pl.Buffered(1))
    tok = lambda n: pl.BlockSpec((1, tm, n), lambda i, j: (i, j, 0))
    per_seq = lambda r, c: pl.BlockSpec((1, r, c), lambda i, j: (i, 0, 0))
    return pl.pallas_call(
        functools.partial(_tail_seq_kernel, tm=tm),
        grid=(b, t // tm),
        in_specs=[tok(d), tok(SEQ_WIDTH), tok(MEM_WIDTH),
                  per_seq(MEM_WIDTH, n_mem), per_seq(MEM_WIDTH, n_mem),
                  full((d, d)), full((1, d)), per_seq(CONV_W - 1, f2),
                  full((1, d)), full((1, d)), of_layer((d, f2)), full((CONV_W, f2)), full((1, f2)),
                  of_layer((D_FF, d))],
        out_specs=[tok(d), per_seq(CONV_W - 1, f2)],
        out_shape=[jax.ShapeDtypeStruct((b, t, d), F32),
                   jax.ShapeDtypeStruct((b, CONV_W - 1, f2), F32)],
        scratch_shapes=[pltpu.VMEM((tm, D_FF), BF16), pltpu.VMEM((CONV_W - 1, f2), F32)],
        compiler_params=_params(("parallel", "arbitrary"), VMEM_LIMIT),
        name="layer_tail_seq",
    )(x, seq_out, q_mem, mkt, mvt, w_out, g_mix.reshape(1, d), prev, g_pre.reshape(1, d),
      g_post.reshape(1, d), w_up, conv_w, conv_b.reshape(1, f2), w_down)


def _ffn_step_kernel(x_ref, p0g_ref, p0v_ref, p1g_ref, p1v_ref, gpre_ref, gpost_ref,
                     wg_ref, wv_ref, cwg_ref, cwv_ref, cbg_ref, cbv_ref, wdn_ref,
                     o_ref, ug_ref, uv_ref, acc_scr):
    c = pl.program_id(0)

    @pl.when(c == 0)
    def _():
        acc_scr[...] = jnp.zeros_like(acc_scr)

    x = x_ref[...]
    xn = _rms(x, gpre_ref[...]).astype(BF16)

    def conv(w_ref, p0_ref, p1_ref, cw_ref, cb_ref, u_ref):
        u = _dot(xn, w_ref[...])
        u_ref[...] = u
        w = cw_ref[...]
        return p0_ref[...] * w[0:1] + p1_ref[...] * w[1:2] + u * w[2:3] + cb_ref[...]

    gate = conv(wg_ref, p0g_ref, p1g_ref, cwg_ref, cbg_ref, ug_ref)
    val = conv(wv_ref, p0v_ref, p1v_ref, cwv_ref, cbv_ref, uv_ref)
    acc_scr[...] += _dot((gate * _sigmoid(gate) * val).astype(BF16), wdn_ref[...])

    @pl.when(c == pl.num_programs(0) - 1)
    def _():
        o_ref[...] = x + _rms(acc_scr[...], gpost_ref[...])


def conv_ffn_step(x, prev, g_pre, g_post, w_up, conv_w, conv_b, w_down, layer):
    s, d = x.shape
    f2 = 2 * D_FF
    fc = FFN_STEP_CHUNK
    nc = D_FF // fc
    prev2 = prev.reshape(s, (CONV_W - 1) * f2)
    cb = conv_b.reshape(1, f2)
    const = lambda shp: pl.BlockSpec(shp, lambda c: (0, 0))
    col = lambda rows, off: pl.BlockSpec((rows, fc), lambda c, off=off: (0, c + off))
    up_col = lambda off: pl.BlockSpec((None, d, fc), lambda c, off=off: (layer, 0, c + off))
    out, ug, uv = pl.pallas_call(
        _ffn_step_kernel,
        grid=(nc,),
        in_specs=[const((s, d)),
                  col(s, 0), col(s, nc), col(s, 2 * nc), col(s, 3 * nc),
                  const((1, d)), const((1, d)),
                  up_col(0), up_col(nc),
                  col(CONV_W, 0), col(CONV_W, nc),
                  col(1, 0), col(1, nc),
                  pl.BlockSpec((None, fc, d), lambda c: (layer, c, 0))],
        out_specs=[const((s, d)), col(s, 0), col(s, 0)],
        out_shape=[jax.ShapeDtypeStruct((s, d), F32),
                   jax.ShapeDtypeStruct((s, D_FF), F32),
                   jax.ShapeDtypeStruct((s, D_FF), F32)],
        scratch_shapes=[pltpu.VMEM((s, d), F32)],
        compiler_params=_params(("arbitrary",)),
        name="conv_ffn_step",
    )(x, prev2, prev2, prev2, prev2, g_pre.reshape(1, d), g_post.reshape(1, d),
      w_up, w_up, conv_w, conv_w, cb, cb, w_down)
    return out, jnp.concatenate([ug, uv], axis=-1)


def _s5_prep_kernel(lr_ref, li_ref, ldt_ref, br_ref, bi_ref, crt_ref, cit_ref, tile_b_ref,
                    tile_c_ref, pr_ref, pi_ref, bre_ref, bim_ref, cre_ref, cim_ref,
                    brew_ref, bimw_ref, crew_ref, cimw_ref):
    lr = jnp.minimum(lr_ref[...], EIG_CLIP)
    li = li_ref[...]
    dt = jnp.exp(ldt_ref[...])
    mag = jnp.exp(lr * dt)
    ar = mag * jnp.cos(li * dt)
    ai = mag * jnp.sin(li * dt)
    den = lr * lr + li * li
    nr = ar - 1.0
    fr = (nr * lr + ai * li) / den
    fi = (ai * lr - nr * li) / den
    br = br_ref[...]
    bi = bi_ref[...]
    bbr = fr * br - fi * bi
    bbi = fr * bi + fi * br

    def block_diag(rows, tile, group_rows, group_cols):
        wide = _dot(rows.astype(BF16), tile)
        r = lax.broadcasted_iota(jnp.int32, wide.shape, 0) // group_rows
        c = lax.broadcasted_iota(jnp.int32, wide.shape, 1) // group_cols
        return jnp.where(r == c, wide, 0.0).astype(BF16)

    for k in range(SSM_BLOCKS):
        rows = slice(k * MXU_DIM, (k + 1) * MXU_DIM)
        bre_ref[k] = block_diag(bbr[rows], tile_b_ref[...], SSM_GROUP, SSM_STATE)
        bim_ref[k] = block_diag(bbi[rows], tile_b_ref[...], SSM_GROUP, SSM_STATE)
        crow = slice(k * SSM_BLOCK_STATES, (k + 1) * SSM_BLOCK_STATES)
        cre_ref[k] = block_diag(crt_ref[crow, :], tile_c_ref[...], SSM_STATE, SSM_GROUP)
        cim_ref[k] = block_diag(cit_ref[crow, :], tile_c_ref[...], SSM_STATE, SSM_GROUP)
    window = [(bbr, bbi)]
    for _ in range(1, S5_WINDOW):
        wr, wi = window[-1]
        window.append((ar * wr - ai * wi, ar * wi + ai * wr))
    tile_b = tile_b_ref[:, :S5_BLOCK_STATES]
    tile_c = tile_c_ref[:, :LANES]
    for k in range(S5_BLOCKS):
        rows = slice(k * LANES, (k + 1) * LANES)
        for j, (wr, wi) in enumerate(window):
            stack = slice(j * LANES, (j + 1) * LANES)
            brew_ref[k, stack, :] = block_diag(wr[rows], tile_b, SSM_GROUP, SSM_STATE)
            bimw_ref[k, stack, :] = block_diag(wi[rows], tile_b, SSM_GROUP, SSM_STATE)
        crow = slice(k * S5_BLOCK_STATES, (k + 1) * S5_BLOCK_STATES)
        crew_ref[k] = block_diag(crt_ref[crow, :], tile_c, SSM_STATE, SSM_GROUP)
        cimw_ref[k] = block_diag(cit_ref[crow, :], tile_c, SSM_STATE, SSM_GROUP)
    pr, pi = ar, ai
    pr_ref[0] = pr
    pi_ref[0] = pi
    for n in range(1, SUBLANES):
        pr, pi = pr * ar - pi * ai, pr * ai + pi * ar
        pr_ref[n] = pr
        pi_ref[n] = pi


def s5_prepare(lam_re, lam_im, log_dt, b_re, b_im, c_re, c_im):
    g, p, c = b_re.shape
    assert (g, p, c) == (SSM_GROUPS, SSM_STATE, SSM_GROUP)
    rep = lambda a: jnp.repeat(a, c, axis=0)
    tr = lambda b: jnp.transpose(b, (0, 2, 1)).reshape(g * c, p)
    trc = lambda m: jnp.transpose(m, (0, 2, 1)).reshape(g * p, c)
    tile_b = jnp.asarray(np.arange(SSM_BLOCK_STATES)[None, :] % p == np.arange(p)[:, None], BF16)
    tile_c = jnp.asarray(np.arange(MXU_DIM)[None, :] % c == np.arange(c)[:, None], BF16)
    pw = jax.ShapeDtypeStruct((SUBLANES, g * c, p), F32)
    b1 = jax.ShapeDtypeStruct((SSM_BLOCKS, MXU_DIM, SSM_BLOCK_STATES), BF16)
    c1 = jax.ShapeDtypeStruct((SSM_BLOCKS, SSM_BLOCK_STATES, MXU_DIM), BF16)
    bw = jax.ShapeDtypeStruct((S5_BLOCKS, S5_WINDOW * LANES, S5_BLOCK_STATES), BF16)
    cw = jax.ShapeDtypeStruct((S5_BLOCKS, S5_BLOCK_STATES, LANES), BF16)
    pr, pi, bre, bim, cre, cim, brew, bimw, crew, cimw = pl.pallas_call(
        _s5_prep_kernel, out_shape=[pw, pw, b1, b1, c1, c1, bw, bw, cw, cw], name="s5_prepare",
    )(rep(lam_re), rep(lam_im), jnp.broadcast_to(rep(log_dt[:, None]), (g * c, p)),
      tr(b_re), tr(b_im), trc(c_re), trc(c_im), tile_b, tile_c)
    pr = pr[:, ::c, :].reshape(SUBLANES, g * p)
    pi = pi[:, ::c, :].reshape(SUBLANES, g * p)
    return dict(pr=pr, pi=pi, step=(bre, bim, cre, cim), seq=(brew, bimw, crew, cimw))


def _s5_finish(y, u, dskip_ref, wglu_ref, bglu_ref):
    y = y + dskip_ref[...] * u
    y = jax.nn.gelu(y)
    return y * _sigmoid(_dot(y.astype(BF16), wglu_ref[...]) + bglu_ref[...])


def _s5_seq_kernel(x_ref, gin_ref, win_ref, pr_ref, pi_ref, lvr_ref, lvi_ref, bre_ref, bim_ref,
                   cre_ref, cim_ref, dskip_ref, wglu_ref, bglu_ref, y_ref, qm_ref, hr_ref, hi_ref,
                   xr_scr, xi_scr, y_scr, *, tc):
    @pl.when(pl.program_id(1) == 0)
    def _():
        hr_ref[...] = jnp.zeros_like(hr_ref)
        hi_ref[...] = jnp.zeros_like(hi_ref)

    hin = _rms(x_ref[0], gin_ref[...]).astype(BF16)
    u = _dot(hin, win_ref[:, :SEQ_WIDTH])
    qm_ref[0] = _dot(hin, win_ref[:, SEQ_WIDTH:])
    step = lax.broadcasted_iota(jnp.int32, (tc, 1), 0) % SUBLANES
    window = [u.astype(BF16)] + [
        jnp.where(step >= j, pltpu.roll(u, j, 0), 0.0).astype(BF16) for j in range(1, S5_WINDOW)]
    nb = S5_BLOCK_STATES
    last = SUBLANES - 1
    bc = lambda h: jnp.broadcast_to(h[last:last + 1, :], h.shape)
    block = lambda k: slice(k * nb, (k + 1) * nb)

    def project_in(k):
        cols = slice(k * LANES, (k + 1) * LANES)
        uk = jnp.concatenate([uj[:, cols] for uj in window], axis=1)
        xr_scr[k] = _dot(uk, bre_ref[k])
        xi_scr[k] = _dot(uk, bim_ref[k])

    def scan(k):
        sl = block(k)
        pr = pr_ref[:, sl]
        pi = pi_ref[:, sl]
        cr, ci = bc(hr_ref[0, :, sl]), bc(hi_ref[0, :, sl])
        for g in range(tc // SUBLANES):
            rows = slice(g * SUBLANES, (g + 1) * SUBLANES)
            xr = xr_scr[k, rows, :]
            xi = xi_scr[k, rows, :]
            sr = pltpu.roll(xr, 4, 0)
            si = pltpu.roll(xi, 4, 0)
            ar = lvr_ref[:, sl]
            ai = lvi_ref[:, sl]
            xr, xi = xr + ar * sr - ai * si, xi + ar * si + ai * sr
            hr = xr + pr * cr - pi * ci
            hi = xi + pr * ci + pi * cr
            xr_scr[k, rows, :] = hr
            xi_scr[k, rows, :] = hi
            cr, ci = bc(hr), bc(hi)
        hr_ref[0, :, sl] = cr
        hi_ref[0, :, sl] = ci

    def project_out(k):
        y_scr[:, k * LANES:(k + 1) * LANES] = (
            _dot(xr_scr[k].astype(BF16), cre_ref[k]) - _dot(xi_scr[k].astype(BF16), cim_ref[k]))

    project_in(0)
    for k in range(S5_BLOCKS):
        if k + 1 < S5_BLOCKS:
            project_in(k + 1)
        scan(k)
        project_out(k)

    y_ref[0] = _s5_finish(y_scr[...], u, dskip_ref, wglu_ref, bglu_ref)


def s5_mix_seq(x, g_in, w_in, prep, d_skip, w_glu, b_glu, tc_cap=512):
    b, t, d = x.shape
    w = SEQ_WIDTH
    tc = _row_tile(t, tc_cap)
    pr, pi = prep["pr"], prep["pi"]
    bre, bim, cre, cim = prep["seq"]
    late = np.arange(SUBLANES)[:, None] >= S5_WINDOW
    lvr = jnp.where(late, pr[S5_WINDOW - 1][None], 0.0)
    lvi = jnp.where(late, pi[S5_WINDOW - 1][None], 0.0)
    full = lambda a: pl.BlockSpec(a.shape, lambda i, j: (0,) * a.ndim)
    consts = [g_in.reshape(1, d), w_in, pr, pi, lvr, lvi, bre, bim, cre, cim, d_skip.reshape(1, w),
              w_glu, b_glu.reshape(1, w)]
    tok = lambda n: pl.BlockSpec((1, tc, n), lambda i, j: (i, j, 0))
    y, qm, hr, hi = pl.pallas_call(
        functools.partial(_s5_seq_kernel, tc=tc),
        grid=(b, t // tc),
        in_specs=[tok(d)] + [full(a) for a in consts],
        out_specs=[tok(w), tok(MEM_WIDTH),
                   pl.BlockSpec((1, SUBLANES, SSM_WIDTH), lambda i, j: (i, 0, 0)),
                   pl.BlockSpec((1, SUBLANES, SSM_WIDTH), lambda i, j: (i, 0, 0))],
        out_shape=[jax.ShapeDtypeStruct((b, t, w), F32),
                   jax.ShapeDtypeStruct((b, t, MEM_WIDTH), F32),
                   jax.ShapeDtypeStruct((b, SUBLANES, SSM_WIDTH), F32),
                   jax.ShapeDtypeStruct((b, SUBLANES, SSM_WIDTH), F32)],
        scratch_shapes=[pltpu.VMEM((S5_BLOCKS, tc, S5_BLOCK_STATES), F32),
                        pltpu.VMEM((S5_BLOCKS, tc, S5_BLOCK_STATES), F32),
                        pltpu.VMEM((tc, w), F32)],
        compiler_params=_params(("parallel", "arbitrary")),
        name="s5_mix_seq",
    )(x, *consts)
    return y, qm, hr[:, SUBLANES - 1], hi[:, SUBLANES - 1]


def _s5_step_kernel(u_ref, h0r_ref, h0i_ref, ar_ref, ai_ref, bre_ref, bim_ref, cre_ref, cim_ref,
                    dskip_ref, wglu_ref, bglu_ref, y_ref, hr_ref, hi_ref, y_scr):
    u = u_ref[...]
    ub = u.astype(BF16)
    nb = SSM_BLOCK_STATES
    for k in range(SSM_BLOCKS):
        sl = slice(k * nb, (k + 1) * nb)
        uk = ub[:, k * MXU_DIM:(k + 1) * MXU_DIM]
        ar, ai = ar_ref[:, sl], ai_ref[:, sl]
        h0r, h0i = h0r_ref[:, sl], h0i_ref[:, sl]
        hr = _dot(uk, bre_ref[k]) + ar * h0r - ai * h0i
        hi = _dot(uk, bim_ref[k]) + ar * h0i + ai * h0r
        hr_ref[:, sl] = hr
        hi_ref[:, sl] = hi
        y_scr[:, k * MXU_DIM:(k + 1) * MXU_DIM] = (
            _dot(hr.astype(BF16), cre_ref[k]) - _dot(hi.astype(BF16), cim_ref[k]))
    y_ref[...] = _s5_finish(y_scr[...], u, dskip_ref, wglu_ref, bglu_ref)


def s5_mix_step(u, h0r, h0i, prep, d_skip, w_glu, b_glu):
    s, w = u.shape
    pr, pi = prep["pr"], prep["pi"]
    bre, bim, cre, cim = prep["step"]
    st = jax.ShapeDtypeStruct((s, SSM_WIDTH), F32)
    return pl.pallas_call(
        _s5_step_kernel,
        out_shape=[jax.ShapeDtypeStruct((s, w), F32), st, st],
        scratch_shapes=[pltpu.VMEM((s, w), F32)],
        name="s5_mix_step",
    )(u, h0r, h0i, pr[0:1], pi[0:1], bre, bim, cre, cim, d_skip.reshape(1, w), w_glu,
      b_glu.reshape(1, w))


FOX_TQ = 2 * KV_TILE
FOX_STRIP = MXU_DIM


def _fox_seq_kernel(q_ref, kaug_ref, vt_ref, tri_ref, o_ref, s_scr, p_scr, acc_scr, m_scr, l_scr,
                    smax_scr, *, tq, tk):
    i = pl.program_id(2)
    qt = q_ref[0].astype(F32)
    row = lax.broadcasted_iota(jnp.int32, (LANES, 1), 0)
    qa = []
    for h in range(2):
        own = (row >= h * HEAD_DIM) & (row < (h + 1) * HEAD_DIM)
        head = 2 * pl.program_id(1) + h
        ones = (row >= head * BIAS_PIECES) & (row < (head + 1) * BIAS_PIECES)
        qa.append(jnp.concatenate(
            [jnp.where(own, qt, 0.0), jnp.broadcast_to(jnp.where(ones, 1.0, 0.0), qt.shape)],
            axis=0).astype(BF16))
    acc_scr[...] = jnp.zeros_like(acc_scr)
    m_scr[...] = jnp.full_like(m_scr, NEG_INF)
    l_scr[...] = jnp.zeros_like(l_scr)
    n_sub = tq // FOX_STRIP
    strips = [(h, qs) for h in range(2) for qs in range(n_sub)]
    lanes = lambda qs: slice(qs * FOX_STRIP, (qs + 1) * FOX_STRIP)
    ones_rows = jnp.ones((2 * SUBLANES, tk), BF16)

    def visible_keys(qs, c):
        if c is None:
            return tk, 0
        first = qs * FOX_STRIP - c * tk
        if first < 0:
            return 0, 0
        return (tk, 0) if first >= tk else (first, FOX_STRIP)

    def scores(j, slot, c=None):
        ka = kaug_ref[0, pl.ds(pl.multiple_of(j * tk, tk), tk), :]
        for n, (h, qs) in enumerate(strips):
            if sum(visible_keys(qs, c)):
                s = _dot(ka, qa[h][:, lanes(qs)])
                s_scr[slot, n] = s
                if c is None:
                    smax_scr[slot, n] = jnp.max(s, axis=0, keepdims=True)

    def absorb(j, slot, c=None):
        alphas, n_keys = [], []
        for n, (h, qs) in enumerate(strips):
            n_full_keys, n_tri = visible_keys(qs, c)
            n_keys.append(n_full_keys + n_tri)
            if not n_keys[n]:
                alphas.append(None)
                continue
            m = m_scr[n]
            full_rows = slice(0, n_full_keys)
            tri_rows = slice(n_full_keys, n_full_keys + n_tri)
            m_new = m
            if n_tri:
                tri = s_scr[slot, n, tri_rows, :] + tri_ref[...]
                m_new = jnp.maximum(m_new, jnp.max(tri, axis=0, keepdims=True))
            if n_full_keys:
                m_new = jnp.maximum(m_new, smax_scr[slot, n] if c is None else jnp.max(
                    s_scr[slot, n, full_rows, :], axis=0, keepdims=True))
                p_scr[n, full_rows, :] = jnp.exp2(
                    s_scr[slot, n, full_rows, :] - m_new).astype(BF16)
            if n_tri:
                p_scr[n, tri_rows, :] = jnp.exp2(tri - m_new).astype(BF16)
            m_scr[n] = m_new
            alphas.append(jnp.exp2(m - m_new))
        for n, (h, qs) in enumerate(strips):
            if not n_keys[n]:
                continue
            vt = jnp.concatenate(
                [vt_ref[0, j, h * HEAD_DIM:(h + 1) * HEAD_DIM, :n_keys[n]],
                 ones_rows[:, :n_keys[n]]], axis=0)
            pv = _dot(vt, p_scr[n, :n_keys[n], :])
            acc_scr[h, :, lanes(qs)] = alphas[n] * acc_scr[h, :, lanes(qs)] + pv[:HEAD_DIM]
            l_scr[n] = alphas[n] * l_scr[n] + pv[HEAD_DIM:HEAD_DIM + 1]

    n_before = 2 * i
    scores(0, 0)

    def pair(jj, _):
        j = 2 * jj
        scores(j + 1, 1)
        absorb(j, 0)
        scores(j + 2, 0)
        absorb(j + 1, 1)
        return 0

    lax.fori_loop(0, i, pair, 0)
    if tq == tk:
        absorb(0, 0, c=0)
    else:
        scores(n_before + 1, 1, c=1)
        absorb(n_before, 0, c=0)
        absorb(n_before + 1, 1, c=1)

    l_head = lambda h: jnp.concatenate(
        [l_scr[h * n_sub + qs] for qs in range(n_sub)], axis=1)
    ot = jnp.concatenate([acc_scr[0] / l_head(0), acc_scr[1] / l_head(1)], axis=0)
    o_ref[0] = jnp.transpose(ot)


def fox_attention_seq(qt, kaug, vtb):
    b, w, t = qt.shape
    n_chunks, tk = vtb.shape[1], vtb.shape[3]
    tq = _row_tile(t, FOX_TQ)
    assert tq in (tk, 2 * tk) and n_chunks * tk == t and tk % FOX_STRIP == 0
    n_strips = 2 * tq // FOX_STRIP
    idx = np.arange(FOX_STRIP)
    tri = jnp.asarray(np.where(idx[:, None] <= idx[None, :], 0.0, NEG_INF), F32)
    return pl.pallas_call(
        functools.partial(_fox_seq_kernel, tq=tq, tk=tk),
        grid=(b, HEAD_PAIRS, t // tq),
        in_specs=[pl.BlockSpec((1, LANES, tq), lambda bi, p, i: (bi, p, i)),
                  pl.BlockSpec((1, t, KAUG_WIDTH), lambda bi, p, i: (bi, 0, p)),
                  pl.BlockSpec((1, n_chunks, LANES, tk), lambda bi, p, i: (bi, 0, p, 0)),
                  pl.BlockSpec((FOX_STRIP, FOX_STRIP), lambda bi, p, i: (0, 0))],
        out_specs=pl.BlockSpec((1, tq, LANES), lambda bi, p, i: (bi, i, p)),
        out_shape=jax.ShapeDtypeStruct((b, t, w), F32),
        scratch_shapes=[pltpu.VMEM((2, n_strips, tk, FOX_STRIP), F32),
                        pltpu.VMEM((n_strips, tk, FOX_STRIP), BF16),
                        pltpu.VMEM((2, HEAD_DIM, tq), F32),
                        pltpu.VMEM((n_strips, 1, FOX_STRIP), F32),
                        pltpu.VMEM((n_strips, 1, FOX_STRIP), F32),
                        pltpu.VMEM((2, n_strips, 1, FOX_STRIP), F32)],
        compiler_params=_params(("parallel", "parallel", "arbitrary")),
        name="fox_attention_seq",
    )(qt, kaug, vtb, tri)


DEC_PAGES = 16
DEC_SLOTS = 4


def _fox_dec_kernel(pt_ref, q_ref, kn_ref, vn_ref, lfn_ref, tri_ref, kt_hbm, vt_hbm, lf_hbm, o_ref,
                    m_scr, l_scr, csum_scr, acc_scr, qcol_scr, kbuf, vbuf, lbuf, sems,
                    *, page, n_pages):
    g = pl.program_id(1)
    n_g = pl.num_programs(1)
    t = pl.program_id(0) * n_g + g
    n_steps = pl.num_programs(0) * n_g
    hp = FOX_HEADS
    qrow = q_ref[0] * ATTN_SCALE

    def page_copies(step):
        slot = step % DEC_SLOTS
        seq, grp = step // n_g, step % n_g
        cps = []
        for i in range(n_pages):
            pid = pt_ref[seq, grp * n_pages + i]
            cps.append(pltpu.make_async_copy(kt_hbm.at[pid], kbuf.at[slot, i], sems.at[0, slot]))
            cps.append(pltpu.make_async_copy(vt_hbm.at[pid], vbuf.at[slot, i], sems.at[1, slot]))
            cps.append(pltpu.make_async_copy(lf_hbm.at[pid], lbuf.at[slot, i], sems.at[2, slot]))
        return cps

    @pl.when(t == 0)
    def _():
        for step in range(DEC_SLOTS - 1):
            @pl.when(step < n_steps)
            def _():
                for cp in page_copies(step):
                    cp.start()

    @pl.when(t + (DEC_SLOTS - 1) < n_steps)
    def _():
        for cp in page_copies(t + (DEC_SLOTS - 1)):
            cp.start()

    for cp in page_copies(t):
        cp.wait()
    slot = t % DEC_SLOTS

    @pl.when(g == 0)
    def _():
        m_scr[...] = jnp.full_like(m_scr, NEG_INF)
        l_scr[...] = jnp.zeros_like(l_scr)
        csum_scr[...] = jnp.zeros_like(csum_scr)
        acc_scr[...] = jnp.zeros_like(acc_scr)
        qcol_scr[...] = jnp.transpose(jnp.broadcast_to(qrow, (page, SEQ_WIDTH)))

    tri = tri_ref[...]
    base = csum_scr[...]
    head_row = lax.broadcasted_iota(jnp.int32, (hp, page), 0)
    s_parts = []
    for i in range(n_pages):
        hi, mid, lo = _split3(lbuf[slot, i])
        cum = (_dot(hi, tri) + _dot(mid, tri)) + _dot(lo, tri) + base
        base = jnp.broadcast_to(cum[:, page - 1:page], cum.shape)
        qk = jnp.zeros((hp, page), F32)
        for h in range(FOX_HEADS):
            rows = slice(h * HEAD_DIM, (h + 1) * HEAD_DIM)
            r = jnp.sum(kbuf[slot, i, rows, :] * qcol_scr[rows, :], axis=0, keepdims=True)
            qk = jnp.where(head_row == h, r, qk)
        s_parts.append(qk - cum)
    csum_scr[...] = base
    s = jnp.concatenate(s_parts, axis=1)

    m_old = m_scr[...]
    m_new = jnp.maximum(m_old, jnp.max(s, axis=-1, keepdims=True))
    alpha = jnp.exp(m_old - m_new)
    p = jnp.exp(s - m_new)
    l_scr[...] = alpha * l_scr[...] + jnp.sum(p, axis=-1, keepdims=True)
    m_scr[...] = m_new
    for h in range(FOX_HEADS):
        rows = slice(h * HEAD_DIM, (h + 1) * HEAD_DIM)
        acc = acc_scr[rows, :] * alpha[h:h + 1, :]
        for i in range(n_pages):
            acc = acc + vbuf[slot, i, rows, :] * p[h:h + 1, i * page:(i + 1) * page]
        acc_scr[rows, :] = acc

    @pl.when(g == pl.num_programs(1) - 1)
    def _():
        lane = lax.broadcasted_iota(jnp.int32, (hp, SEQ_WIDTH), 1)
        head = lax.broadcasted_iota(jnp.int32, (hp, SEQ_WIDTH), 0)
        own = (lane >= head * HEAD_DIM) & (lane < (head + 1) * HEAD_DIM)
        spread = lambda col: jnp.sum(jnp.where(own, col, 0.0), axis=0, keepdims=True)
        c_new = csum_scr[:, 0:1] + lfn_ref[0]
        s_new = jnp.sum(jnp.where(own, qrow * kn_ref[0], 0.0), axis=-1, keepdims=True) - c_new
        m_fin = jnp.maximum(m_scr[...], s_new)
        a_fin = jnp.exp(m_scr[...] - m_fin)
        p_new = jnp.exp(s_new - m_fin)
        l_fin = a_fin * l_scr[...] + p_new
        acc_row = jnp.sum(jnp.transpose(acc_scr[...]), axis=0, keepdims=True)
        o_ref[0] = (spread(a_fin) * acc_row + spread(p_new) * vn_ref[0]) / spread(l_fin)


def fox_attention_decode(q, k_new, v_new, logf_new, cache_k, cache_v, cache_logf, page_table):
    s, w = q.shape
    n_pool, page = cache_k.shape[:2]
    assert page == LANES
    pages_per_seq = page_table.shape[1]
    n_pages = min(DEC_PAGES, pages_per_seq)
    assert pages_per_seq % n_pages == 0
    hp = FOX_HEADS
    ckt = jnp.transpose(cache_k, (0, 2, 3, 1)).reshape(n_pool, w, page)
    cvt = jnp.transpose(cache_v, (0, 2, 3, 1)).reshape(n_pool, w, page)
    clf = jnp.transpose(cache_logf, (0, 2, 1))
    lfn = logf_new.reshape(s, hp, 1)
    tri = jnp.asarray(np.triu(np.ones((page, page), np.float32)), BF16)
    row = pl.BlockSpec((1, 1, w), lambda b, g, pt: (b, 0, 0))
    in_hbm = pl.BlockSpec(memory_space=pl.ANY)
    grid_spec = pltpu.PrefetchScalarGridSpec(
        num_scalar_prefetch=1,
        grid=(s, pages_per_seq // n_pages),
        in_specs=[row, row, row, pl.BlockSpec((1, hp, 1), lambda b, g, pt: (b, 0, 0)),
                  pl.BlockSpec((page, page), lambda b, g, pt: (0, 0)), in_hbm, in_hbm, in_hbm],
        out_specs=row,
        scratch_shapes=[pltpu.VMEM((hp, 1), F32), pltpu.VMEM((hp, 1), F32),
                        pltpu.VMEM((hp, page), F32), pltpu.VMEM((w, page), F32),
                        pltpu.VMEM((w, page), F32),
                        pltpu.VMEM((DEC_SLOTS, n_pages, w, page), F32),
                        pltpu.VMEM((DEC_SLOTS, n_pages, w, page), F32),
                        pltpu.VMEM((DEC_SLOTS, n_pages, hp, page), F32),
                        pltpu.SemaphoreType.DMA((3, DEC_SLOTS))],
    )
    out = pl.pallas_call(
        functools.partial(_fox_dec_kernel, page=page, n_pages=n_pages),
        grid_spec=grid_spec,
        out_shape=jax.ShapeDtypeStruct((s, 1, w), F32),
        compiler_params=_params(("arbitrary", "arbitrary"), VMEM_LIMIT),
        name="fox_attention_decode",
    )(page_table, q.reshape(s, 1, w), k_new.reshape(s, 1, w), v_new.reshape(s, 1, w), lfn, tri,
      ckt, cvt, clf)
    return out.reshape(s, w)


def _trunk(x, mem_k, mem_v, conv_prev, ssm_state, fox_attend, p, s5_prep, sequential):
    assert N_A == 1
    b, t, d = x.shape
    m = b * t
    new_conv, ssm_out, kv = [], None, None
    for l in range(DEPTH):
        if sequential and l == N_A:
            kv = shared_kv_proj_seq(x, p["kv_norm"], p["w_kv"], p["b_f"],
                                    p["norm_mix_pre"][l], p["w_in"][l])
            z_seq, q_mem = kv[5], kv[6]
        elif not sequential:
            if l == N_A:
                kv = shared_kv_proj_step(x.reshape(m, d), p["kv_norm"], p["w_kv"], p["b_f"])
            z_seq, q_mem = norm_linear(x.reshape(m, d), p["norm_mix_pre"][l], p["w_in"][l],
                                       (SEQ_WIDTH, MEM_WIDTH))
        if l < N_A:
            if sequential:
                seq_out, q_mem, hr, hi = s5_mix_seq(x, p["norm_mix_pre"][l], p["w_in"][l],
                                                    s5_prep[l], p["d_skip"][l], p["w_glu"][l],
                                                    p["b_glu"][l])
            else:
                seq_out, hr, hi = s5_mix_step(z_seq, ssm_state[0][l], ssm_state[1][l], s5_prep[l],
                                              p["d_skip"][l], p["w_glu"][l], p["b_glu"][l])
            ssm_out = (hr.reshape(b, SSM_GROUPS, SSM_STATE), hi.reshape(b, SSM_GROUPS, SSM_STATE))
        else:
            seq_out = fox_attend(z_seq, kv)
        ffn_args = (p["norm_ffn_pre"][l], p["norm_ffn_post"][l], p["w_up"], p["conv_w"][l],
                    p["conv_b"][l], p["w_down"], l)
        if sequential:
            x3, cp = layer_tail_seq(x, seq_out.reshape(b, t, SEQ_WIDTH),
                                    q_mem.reshape(b, t, MEM_WIDTH), mem_k[l], mem_v[l],
                                    p["w_out"][l], p["norm_mix_post"][l], conv_prev[l], *ffn_args)
        else:
            mem_out = mem_attention(q_mem.reshape(b, t, MEM_WIDTH), mem_k, mem_v, l)
            x2 = mix_out(x.reshape(m, d), seq_out, mem_out.reshape(m, MEM_WIDTH),
                         p["w_out"][l], p["norm_mix_post"][l])
            x3, u_new = conv_ffn_step(x2, conv_prev[l], *ffn_args)
            cp = jnp.stack([conv_prev[l][:, 1], u_new], axis=1)
        new_conv.append(cp)
        x = x3.reshape(b, t, d)
    return x, ssm_out, jnp.stack(new_conv), kv


def kernel(x_prompt, x_sample, state_ssm_re, state_ssm_im, cache_k, cache_v, cache_logf,
           cache_mem_k, cache_mem_v, state_ffn_conv, page_table, mem_prompt,
           w_in, w_out, norm_mix_pre, norm_mix_post, norm_ffn_pre, norm_ffn_post,
           mem_norm, w_mem_kv, lam_re, lam_im, log_dt, b_re, b_im, c_re, c_im, d_skip,
           w_glu, b_glu, kv_norm, w_kv, b_f, w_up, conv_w, conv_b, w_down):
    per_layer_bf16 = lambda w: [w[l].astype(BF16) for l in range(w.shape[0])]
    p = dict(w_in=per_layer_bf16(w_in), w_out=per_layer_bf16(w_out), norm_mix_pre=norm_mix_pre,
             norm_mix_post=norm_mix_post, norm_ffn_pre=norm_ffn_pre, norm_ffn_post=norm_ffn_post,
             d_skip=d_skip, w_glu=per_layer_bf16(w_glu), b_glu=b_glu, kv_norm=kv_norm, w_kv=w_kv,
             b_f=b_f, w_up=w_up.astype(BF16), conv_w=conv_w, conv_b=conv_b,
             w_down=w_down.astype(BF16))
    s5_prep = [s5_prepare(lam_re[l], lam_im[l], log_dt[l], b_re[l], b_im[l], c_re[l], c_im[l])
               for l in range(N_A)]

    bp, tp, d = x_prompt.shape
    n_mem = mem_prompt.shape[1]
    mem_pairs = [mem_kv_proj(mem_prompt, mem_norm[l], w_mem_kv[l]) for l in range(DEPTH)]
    p_mem_kt = [mkt for mkt, _ in mem_pairs]
    p_mem_vt = [mvt for _, mvt in mem_pairs]
    zeros_conv = jnp.zeros((DEPTH, bp, CONV_W - 1, 2 * D_FF), F32)

    def fox_prompt(qt, kv):
        return fox_attention_seq(qt, kv[2], kv[3])

    y_prompt, p_ssm, p_conv, p_kv = _trunk(x_prompt, p_mem_kt, p_mem_vt, zeros_conv, None,
                                           fox_prompt, p, s5_prep, sequential=True)
    untr = lambda a, n: jnp.transpose(a.reshape(a.shape[0], n, HEAD_DIM, a.shape[2]), (0, 3, 1, 2))
    mem5 = lambda ms: jnp.stack([untr(a, MEM_HEADS) for a in ms])

    bs = x_sample.shape[0]
    tr_mem = lambda a: jnp.transpose(a, (0, 1, 3, 4, 2)).reshape(DEPTH, bs, MEM_WIDTH, n_mem)
    s_mem_kt = tr_mem(cache_mem_k)
    s_mem_vt = tr_mem(cache_mem_v)
    ssm0 = (state_ssm_re.reshape(N_A, bs, SSM_WIDTH), state_ssm_im.reshape(N_A, bs, SSM_WIDTH))

    def fox_sample(q, kv):
        k, v, logf = kv
        return fox_attention_decode(q, k, v, logf, cache_k, cache_v, cache_logf, page_table)

    y_sample, s_ssm, s_conv, s_kv = _trunk(x_sample, s_mem_kt, s_mem_vt, state_ffn_conv, ssm0,
                                           fox_sample, p, s5_prep, sequential=False)
    head4 = lambda a: a.reshape(bs, 1, FOX_HEADS, HEAD_DIM)

    return (y_prompt, y_sample, p_ssm[0][None], p_ssm[1][None],
            untr(p_kv[0], FOX_HEADS), untr(p_kv[1], FOX_HEADS), p_kv[4],
            mem5(p_mem_kt), mem5(p_mem_vt), p_conv,
            s_ssm[0][None], s_ssm[1][None],
            head4(s_kv[0]), head4(s_kv[1]), s_kv[2].reshape(bs, 1, FOX_HEADS), s_conv)
```

```python
import functools
import math

import jax
import jax.numpy as jnp
import numpy as np
from jax import lax
from jax.experimental import pallas as pl
from jax.experimental.pallas import tpu as pltpu

F32 = jnp.float32
BF16 = jnp.bfloat16

D_MODEL = 1024
DEPTH = 2
N_A = DEPTH // 2
HEAD_DIM = 64
MEM_HEADS = 4
MEM_WIDTH = MEM_HEADS * HEAD_DIM
SEQ_WIDTH = D_MODEL - MEM_WIDTH
SSM_GROUP = 16
SSM_GROUPS = SEQ_WIDTH // SSM_GROUP
SSM_STATE = 64
SSM_WIDTH = SSM_GROUPS * SSM_STATE
FOX_HEADS = SEQ_WIDTH // HEAD_DIM
D_FF = (11 * D_MODEL) // 4
CONV_W = 3
EPS = 1e-6
NEG_INF = -1e30
EIG_CLIP = -1e-4
ATTN_SCALE = HEAD_DIM ** -0.5
LOG2E = math.log2(math.e)

LANES = 128
SUBLANES = 8
MXU_DIM = 256
VMEM_BYTES_V7X = 64 * 1024 * 1024
VMEM_LIMIT = (VMEM_BYTES_V7X * 7) // 8

SSM_BLOCKS = SEQ_WIDTH // MXU_DIM
SSM_BLOCK_STATES = SSM_WIDTH // SSM_BLOCKS
S5_WINDOW = 4
S5_BLOCKS = SEQ_WIDTH // LANES
S5_BLOCK_STATES = SSM_WIDTH // S5_BLOCKS
HEAD_PAIRS = FOX_HEADS // 2


def _params(semantics, vmem=None):
    return pltpu.CompilerParams(dimension_semantics=semantics, vmem_limit_bytes=vmem)


def _row_tile(m, cap):
    t = min(m, cap)
    assert m % t == 0, (m, t)
    return t


def _rms(x, g):
    return x * lax.rsqrt(jnp.mean(x * x, axis=-1, keepdims=True) + EPS) * g


def _sigmoid(x):
    return 1.0 / (1.0 + jnp.exp(-x))


def _log_sigmoid(x):
    return -(jnp.maximum(-x, 0.0) + jnp.log1p(jnp.exp(-jnp.abs(x))))


def _split3(x):
    hi = x.astype(BF16)
    r1 = x - hi.astype(F32)
    mid = r1.astype(BF16)
    lo = (r1 - mid.astype(F32)).astype(BF16)
    return hi, mid, lo


def _dot(a, b):
    return jnp.dot(a, b, preferred_element_type=F32)


def _dot_nt(a, b):
    return lax.dot_general(a, b, (((1,), (1,)), ((), ())), preferred_element_type=F32)


def _norm_linear_kernel(x_ref, g_ref, w_ref, *out_refs, splits):
    h = _rms(x_ref[...], g_ref[...]).astype(BF16)
    c0 = 0
    for o_ref, n in zip(out_refs, splits):
        o_ref[...] = _dot(h, w_ref[:, c0:c0 + n])
        c0 += n


def norm_linear(x, g, w, splits, tm_cap=512):
    m, d = x.shape
    tm = _row_tile(m, tm_cap)
    n_tot = sum(splits)
    assert w.shape == (d, n_tot)
    return pl.pallas_call(
        functools.partial(_norm_linear_kernel, splits=tuple(splits)),
        grid=(m // tm,),
        in_specs=[pl.BlockSpec((tm, d), lambda i: (i, 0)),
                  pl.BlockSpec((1, d), lambda i: (0, 0)),
                  pl.BlockSpec((d, n_tot), lambda i: (0, 0))],
        out_specs=[pl.BlockSpec((tm, n), lambda i: (i, 0)) for n in splits],
        out_shape=[jax.ShapeDtypeStruct((m, n), F32) for n in splits],
        compiler_params=_params(("parallel",)),
        name="norm_linear",
    )(x, g.reshape(1, d), w)


KV_TILE = 512
KAUG_WIDTH = 2 * LANES
BIAS_PIECES = 3


def _kv_weights(w_kv, b_f):
    wk = w_kv[:, :SEQ_WIDTH].astype(BF16)
    wf = jnp.pad(w_kv[:, 2 * SEQ_WIDTH:], ((0, 0), (0, LANES - FOX_HEADS))).astype(BF16)
    bf = jnp.pad(b_f, (0, LANES - FOX_HEADS)).reshape(1, LANES)
    return wk, wf, bf


def _kv_seq_kernel(x_ref, g_ref, wk_ref, wvt_ref, wf_ref, bf_ref, tri_ref, place_ref,
                   gin_ref, wqt_ref, wm_ref,
                   kt_ref, vt_ref, kaug_ref, vtb_ref, lf_ref, qt_ref, qm_ref, carry_ref):
    @pl.when(pl.program_id(1) == 0)
    def _():
        carry_ref[...] = jnp.zeros_like(carry_ref)

    x = x_ref[0]
    xhat = x * lax.rsqrt(jnp.mean(x * x, axis=-1, keepdims=True) + EPS)
    hin = (xhat * gin_ref[...]).astype(BF16)
    qt_ref[0] = (_dot_nt(wqt_ref[...], hin) * (ATTN_SCALE * LOG2E)).astype(BF16)
    qm_ref[0] = _dot(hin, wm_ref[...])
    h = (xhat * g_ref[...]).astype(BF16)
    k = _dot(h, wk_ref[...])
    kt_ref[0] = jnp.transpose(k)
    vt = _dot_nt(wvt_ref[...], h)
    vt_ref[0] = vt
    vtb_ref[0, 0] = vt.astype(BF16)
    logf = _log_sigmoid(_dot(h, wf_ref[...]) + bf_ref[...])
    lf_ref[0] = logf[:, :FOX_HEADS]
    c3 = _dot(tri_ref[...], jnp.concatenate(_split3(logf), axis=1))
    cum = (c3[:, :LANES] + c3[:, LANES:2 * LANES]) + c3[:, 2 * LANES:] + carry_ref[...]
    carry_ref[...] = cum[cum.shape[0] - 1:, :]
    pieces = jnp.concatenate(_split3(cum * -LOG2E), axis=1)
    bias = _dot(pieces, place_ref[...]).astype(BF16)
    kb = k.astype(BF16)
    for p in range(HEAD_PAIRS):
        kaug_ref[0, :, p * KAUG_WIDTH:p * KAUG_WIDTH + LANES] = kb[:, p * LANES:(p + 1) * LANES]
        kaug_ref[0, :, p * KAUG_WIDTH + LANES:(p + 1) * KAUG_WIDTH] = bias


def shared_kv_proj_seq(x, g, w_kv, b_f, g_in, w_in):
    b, t, d = x.shape
    wqt = jnp.transpose(w_in[:, :SEQ_WIDTH])
    wm = w_in[:, SEQ_WIDTH:]
    tm = _row_tile(t, KV_TILE)
    wk, wf, bf = _kv_weights(w_kv, b_f)
    wvt = jnp.transpose(w_kv)[SEQ_WIDTH:2 * SEQ_WIDTH].astype(BF16)
    tri = jnp.asarray(np.tril(np.ones((tm, tm), np.float32)), BF16)
    hh = np.arange(FOX_HEADS)
    place = np.zeros((BIAS_PIECES * LANES, LANES), np.float32)
    for j in range(BIAS_PIECES):
        place[LANES * j + hh, BIAS_PIECES * hh + j] = 1.0
    place = jnp.asarray(place, BF16)
    tok = lambda n: pl.BlockSpec((1, tm, n), lambda i, j: (i, j, 0))
    tr = pl.BlockSpec((1, SEQ_WIDTH, tm), lambda i, j: (i, 0, j))
    full = lambda s: pl.BlockSpec(s, lambda i, j: (0,) * len(s))
    return pl.pallas_call(
        _kv_seq_kernel,
        grid=(b, t // tm),
        in_specs=[tok(d), full((1, d)), full((d, SEQ_WIDTH)), full((SEQ_WIDTH, d)),
                  full((d, LANES)), full((1, LANES)), full((tm, tm)),
                  full((BIAS_PIECES * LANES, LANES)),
                  full((1, d)), full((SEQ_WIDTH, d)), full((d, MEM_WIDTH))],
        out_specs=[tr, tr, tok(HEAD_PAIRS * KAUG_WIDTH),
                   pl.BlockSpec((1, 1, SEQ_WIDTH, tm), lambda i, j: (i, j, 0, 0)),
                   tok(FOX_HEADS), tr, tok(MEM_WIDTH)],
        out_shape=[jax.ShapeDtypeStruct((b, SEQ_WIDTH, t), F32),
                   jax.ShapeDtypeStruct((b, SEQ_WIDTH, t), F32),
                   jax.ShapeDtypeStruct((b, t, HEAD_PAIRS * KAUG_WIDTH), BF16),
                   jax.ShapeDtypeStruct((b, t // tm, SEQ_WIDTH, tm), BF16),
                   jax.ShapeDtypeStruct((b, t, FOX_HEADS), F32),
                   jax.ShapeDtypeStruct((b, SEQ_WIDTH, t), BF16),
                   jax.ShapeDtypeStruct((b, t, MEM_WIDTH), F32)],
        scratch_shapes=[pltpu.VMEM((1, LANES), F32)],
        compiler_params=_params(("parallel", "arbitrary")),
        name="shared_kv_proj_seq",
    )(x, g.reshape(1, d), wk, wvt, wf, bf, tri, place, g_in.reshape(1, d), wqt, wm)


def _kv_step_kernel(x_ref, g_ref, wk_ref, wv_ref, wf_ref, bf_ref, k_ref, v_ref, lf_ref):
    h = _rms(x_ref[...], g_ref[...]).astype(BF16)
    k_ref[...] = _dot(h, wk_ref[...])
    v_ref[...] = _dot(h, wv_ref[...])
    lf_ref[...] = _log_sigmoid(_dot(h, wf_ref[...]) + bf_ref[...])[:, :FOX_HEADS]


def shared_kv_proj_step(x, g, w_kv, b_f):
    s, d = x.shape
    wk, wf, bf = _kv_weights(w_kv, b_f)
    wv = w_kv[:, SEQ_WIDTH:2 * SEQ_WIDTH].astype(BF16)
    kv = jax.ShapeDtypeStruct((s, SEQ_WIDTH), F32)
    return pl.pallas_call(
        _kv_step_kernel,
        out_shape=[kv, kv, jax.ShapeDtypeStruct((s, FOX_HEADS), F32)],
        name="shared_kv_proj_step",
    )(x, g.reshape(1, d), wk, wv, wf, bf)


def _mem_kv_kernel(x_ref, g_ref, wt_ref, kt_ref, vt_ref):
    h = _rms(x_ref[0], g_ref[...]).astype(BF16)
    kvt = _dot_nt(wt_ref[...], h)
    kt_ref[0] = kvt[:MEM_WIDTH]
    vt_ref[0] = kvt[MEM_WIDTH:]


def mem_kv_proj(mem, g, w):
    b, n_mem, d = mem.shape
    out = jax.ShapeDtypeStruct((b, MEM_WIDTH, n_mem), F32)
    blk = pl.BlockSpec((1, MEM_WIDTH, n_mem), lambda i: (i, 0, 0))
    return pl.pallas_call(
        _mem_kv_kernel,
        grid=(b,),
        in_specs=[pl.BlockSpec((1, n_mem, d), lambda i: (i, 0, 0)),
                  pl.BlockSpec((1, d), lambda i: (0, 0)),
                  pl.BlockSpec((2 * MEM_WIDTH, d), lambda i: (0, 0))],
        out_specs=[blk, blk],
        out_shape=[out, out],
        compiler_params=_params(("parallel",)),
        name="mem_kv_proj",
    )(mem, g.reshape(1, d), jnp.transpose(w).astype(BF16))


def _mem_attn_tile(q, mkt, mvt):
    q = q * ATTN_SCALE
    mkt = mkt.astype(BF16)
    mvt = mvt.astype(BF16)
    lane = lax.broadcasted_iota(jnp.int32, (1, MEM_WIDTH), 1)
    out = jnp.zeros(q.shape, F32)
    for h in range(MEM_HEADS):
        in_head = (lane >= h * HEAD_DIM) & (lane < (h + 1) * HEAD_DIM)
        s = _dot(jnp.where(in_head, q, 0.0).astype(BF16), mkt)
        p = jnp.exp(s - jnp.max(s, axis=-1, keepdims=True))
        p = p / jnp.sum(p, axis=-1, keepdims=True)
        out = out + jnp.where(in_head, _dot_nt(p.astype(BF16), mvt), 0.0)
    return out


def _mem_attn_row(q, mkt, mvt):
    lane = lax.broadcasted_iota(jnp.int32, (MEM_HEADS, MEM_WIDTH), 1)
    head = lax.broadcasted_iota(jnp.int32, (MEM_HEADS, MEM_WIDTH), 0)
    own = (lane >= head * HEAD_DIM) & (lane < (head + 1) * HEAD_DIM)
    s = _dot(jnp.where(own, q * ATTN_SCALE, 0.0).astype(BF16), mkt.astype(BF16))
    p = jnp.exp(s - jnp.max(s, axis=-1, keepdims=True))
    p = p / jnp.sum(p, axis=-1, keepdims=True)
    o = _dot_nt(p.astype(BF16), mvt.astype(BF16))
    return jnp.sum(jnp.where(own, o, 0.0), axis=0, keepdims=True)


def _mem_attn_kernel(q_ref, mkt_ref, mvt_ref, o_ref):
    attend = _mem_attn_row if q_ref.shape[1] == 1 else _mem_attn_tile
    for s in range(q_ref.shape[0]):
        o_ref[s] = attend(q_ref[s], mkt_ref[s], mvt_ref[s])


def mem_attention(q_mem, mkt, mvt, layer, seqs_per_step=SUBLANES):
    b, t, w = q_mem.shape
    n_mem = mkt.shape[3]
    bs = _row_tile(b, seqs_per_step)
    mem = pl.BlockSpec((None, bs, w, n_mem), lambda i: (layer, i, 0, 0))
    return pl.pallas_call(
        _mem_attn_kernel,
        grid=(b // bs,),
        in_specs=[pl.BlockSpec((bs, t, w), lambda i: (i, 0, 0)), mem, mem],
        out_specs=pl.BlockSpec((bs, t, w), lambda i: (i, 0, 0)),
        out_shape=jax.ShapeDtypeStruct((b, t, w), F32),
        compiler_params=_params(("parallel",)),
        name="mem_attention",
    )(q_mem, mkt, mvt)


def _mix_out_tile(x, seq, mem, w_ref, g):
    o = (_dot(seq.astype(BF16), w_ref[:SEQ_WIDTH, :]) + _dot(mem.astype(BF16), w_ref[SEQ_WIDTH:, :]))
    return x + _rms(o, g)


def _mix_out_kernel(x_ref, s_ref, m_ref, w_ref, g_ref, o_ref):
    o_ref[...] = _mix_out_tile(x_ref[...], s_ref[...], m_ref[...], w_ref, g_ref[...])


def mix_out(x, seq_out, mem_out, w_out, g, tm_cap=512):
    m, d = x.shape
    tm = _row_tile(m, tm_cap)
    row = lambda n: pl.BlockSpec((tm, n), lambda i: (i, 0))
    return pl.pallas_call(
        _mix_out_kernel,
        grid=(m // tm,),
        in_specs=[row(d), row(SEQ_WIDTH), row(MEM_WIDTH),
                  pl.BlockSpec((d, d), lambda i: (0, 0)),
                  pl.BlockSpec((1, d), lambda i: (0, 0))],
        out_specs=row(d),
        out_shape=jax.ShapeDtypeStruct((m, d), F32),
        compiler_params=_params(("parallel",)),
        name="mix_out",
    )(x, seq_out, mem_out, w_out, g.reshape(1, d))


FFN_CHUNK = MXU_DIM
FFN_STEP_CHUNK = D_FF // 2
assert FFN_STEP_CHUNK % LANES == 0


def _tail_seq_kernel(x_ref, seq_ref, qm_ref, mkt_ref, mvt_ref, wout_ref, gmix_ref, prev_ref,
                     gpre_ref, gpost_ref, wup_ref, cw_ref, cb_ref, wdn_ref,
                     o_ref, conv_ref, h_scr, carry_scr, *, tm):
    @pl.when(pl.program_id(1) == 0)
    def _():
        carry_scr[...] = prev_ref[0]

    mem = _mem_attn_tile(qm_ref[0], mkt_ref[0], mvt_ref[0])
    x = _mix_out_tile(x_ref[0], seq_ref[0], mem, wout_ref, gmix_ref[...])
    xn = _rms(x, gpre_ref[...]).astype(BF16)
    row = lax.broadcasted_iota(jnp.int32, (SUBLANES, 1), 0)

    def up(col):
        return _dot(xn, wup_ref[:, col:col + FFN_CHUNK])

    def conv(u, col):
        c0 = carry_scr[0:1, col:col + FFN_CHUNK]
        c1 = carry_scr[1:2, col:col + FFN_CHUNK]
        u1 = pltpu.roll(u, 1, 0)
        u2 = pltpu.roll(u, 2, 0)
        u1 = jnp.concatenate([jnp.where(row == 0, c1, u1[:SUBLANES]), u1[SUBLANES:]], axis=0)
        u2 = jnp.concatenate(
            [jnp.where(row == 0, c0, jnp.where(row == 1, c1, u2[:SUBLANES])), u2[SUBLANES:]],
            axis=0)
        carry_scr[:, col:col + FFN_CHUNK] = u[tm - 2:, :]
        w = cw_ref[:, col:col + FFN_CHUNK]
        return u2 * w[0:1] + u1 * w[1:2] + u * w[2:3] + cb_ref[:, col:col + FFN_CHUNK]

    for c in range(D_FF // FFN_CHUNK):
        gate = conv(up(c * FFN_CHUNK), c * FFN_CHUNK)
        val = conv(up(D_FF + c * FFN_CHUNK), D_FF + c * FFN_CHUNK)
        h_scr[:, c * FFN_CHUNK:(c + 1) * FFN_CHUNK] = (gate * _sigmoid(gate) * val).astype(BF16)

    f = _dot(h_scr[...], wdn_ref[...])
    o_ref[0] = x + _rms(f, gpost_ref[...])
    conv_ref[0] = carry_scr[...]


def layer_tail_seq(x, seq_out, q_mem, mkt, mvt, w_out, g_mix, prev, g_pre, g_post, w_up, conv_w,
                   conv_b, w_down, layer, tm_cap=512):
    b, t, d = x.shape
    n_mem = mkt.shape[2]
    tm = _row_tile(t, tm_cap)
    assert tm >= 2 * SUBLANES
    f2 = 2 * D_FF
    full = lambda s: pl.BlockSpec(s, lambda i, j: (0,) * len(s), pipeline_mode=pl.Buffered(1))
    of_layer = lambda s: pl.BlockSpec((None,) + s, lambda i, j: (layer,) + (0,) * len(s),
                                      pipeline_mode=pl.Buffered(1))
    tok = lambda n: pl.BlockSpec((1, tm, n), lambda i, j: (i, j, 0))
    per_seq = lambda r, c: pl.BlockSpec((1, r, c), lambda i, j: (i, 0, 0))
    return pl.pallas_call(
        functools.partial(_tail_seq_kernel, tm=tm),
        grid=(b, t // tm),
        in_specs=[tok(d), tok(SEQ_WIDTH), tok(MEM_WIDTH),
                  per_seq(MEM_WIDTH, n_mem), per_seq(MEM_WIDTH, n_mem),
                  full((d, d)), full((1, d)), per_seq(CONV_W - 1, f2),
                  full((1, d)), full((1, d)), of_layer((d, f2)), full((CONV_W, f2)), full((1, f2)),
                  of_layer((D_FF, d))],
        out_specs=[tok(d), per_seq(CONV_W - 1, f2)],
        out_shape=[jax.ShapeDtypeStruct((b, t, d), F32),
                   jax.ShapeDtypeStruct((b, CONV_W - 1, f2), F32)],
        scratch_shapes=[pltpu.VMEM((tm, D_FF), BF16), pltpu.VMEM((CONV_W - 1, f2), F32)],
        compiler_params=_params(("parallel", "arbitrary"), VMEM_LIMIT),
        name="layer_tail_seq",
    )(x, seq_out, q_mem, mkt, mvt, w_out, g_mix.reshape(1, d), prev, g_pre.reshape(1, d),
      g_post.reshape(1, d), w_up, conv_w, conv_b.reshape(1, f2), w_down)


def _ffn_step_kernel(x_ref, p0g_ref, p0v_ref, p1g_ref, p1v_ref, gpre_ref, gpost_ref,
                     wg_ref, wv_ref, cwg_ref, cwv_ref, cbg_ref, cbv_ref, wdn_ref,
                     o_ref, ug_ref, uv_ref, acc_scr):
    c = pl.program_id(0)

    @pl.when(c == 0)
    def _():
        acc_scr[...] = jnp.zeros_like(acc_scr)

    x = x_ref[...]
    xn = _rms(x, gpre_ref[...]).astype(BF16)

    def conv(w_ref, p0_ref, p1_ref, cw_ref, cb_ref, u_ref):
        u = _dot(xn, w_ref[...])
        u_ref[...] = u
        w = cw_ref[...]
        return p0_ref[...] * w[0:1] + p1_ref[...] * w[1:2] + u * w[2:3] + cb_ref[...]

    gate = conv(wg_ref, p0g_ref, p1g_ref, cwg_ref, cbg_ref, ug_ref)
    val = conv(wv_ref, p0v_ref, p1v_ref, cwv_ref, cbv_ref, uv_ref)
    acc_scr[...] += _dot((gate * _sigmoid(gate) * val).astype(BF16), wdn_ref[...])

    @pl.when(c == pl.num_programs(0) - 1)
    def _():
        o_ref[...] = x + _rms(acc_scr[...], gpost_ref[...])


def conv_ffn_step(x, prev, g_pre, g_post, w_up, conv_w, conv_b, w_down, layer):
    s, d = x.shape
    f2 = 2 * D_FF
    fc = FFN_STEP_CHUNK
    nc = D_FF // fc
    prev2 = prev.reshape(s, (CONV_W - 1) * f2)
    cb = conv_b.reshape(1, f2)
    const = lambda shp: pl.BlockSpec(shp, lambda c: (0, 0))
    col = lambda rows, off: pl.BlockSpec((rows, fc), lambda c, off=off: (0, c + off))
    up_col = lambda off: pl.BlockSpec((None, d, fc), lambda c, off=off: (layer, 0, c + off))
    out, ug, uv = pl.pallas_call(
        _ffn_step_kernel,
        grid=(nc,),
        in_specs=[const((s, d)),
                  col(s, 0), col(s, nc), col(s, 2 * nc), col(s, 3 * nc),
                  const((1, d)), const((1, d)),
                  up_col(0), up_col(nc),
                  col(CONV_W, 0), col(CONV_W, nc),
                  col(1, 0), col(1, nc),
                  pl.BlockSpec((None, fc, d), lambda c: (layer, c, 0))],
        out_specs=[const((s, d)), col(s, 0), col(s, 0)],
        out_shape=[jax.ShapeDtypeStruct((s, d), F32),
                   jax.ShapeDtypeStruct((s, D_FF), F32),
                   jax.ShapeDtypeStruct((s, D_FF), F32)],
        scratch_shapes=[pltpu.VMEM((s, d), F32)],
        compiler_params=_params(("arbitrary",)),
        name="conv_ffn_step",
    )(x, prev2, prev2, prev2, prev2, g_pre.reshape(1, d), g_post.reshape(1, d),
      w_up, w_up, conv_w, conv_w, cb, cb, w_down)
    return out, jnp.concatenate([ug, uv], axis=-1)


def _s5_prep_kernel(lr_ref, li_ref, ldt_ref, br_ref, bi_ref, crt_ref, cit_ref, tile_b_ref,
                    tile_c_ref, pr_ref, pi_ref, bre_ref, bim_ref, cre_ref, cim_ref,
                    brew_ref, bimw_ref, crew_ref, cimw_ref):
    lr = jnp.minimum(lr_ref[...], EIG_CLIP)
    li = li_ref[...]
    dt = jnp.exp(ldt_ref[...])
    mag = jnp.exp(lr * dt)
    ar = mag * jnp.cos(li * dt)
    ai = mag * jnp.sin(li * dt)
    den = lr * lr + li * li
    nr = ar - 1.0
    fr = (nr * lr + ai * li) / den
    fi = (ai * lr - nr * li) / den
    br = br_ref[...]
    bi = bi_ref[...]
    bbr = fr * br - fi * bi
    bbi = fr * bi + fi * br

    def block_diag(rows, tile, group_rows, group_cols):
        wide = _dot(rows.astype(BF16), tile)
        r = lax.broadcasted_iota(jnp.int32, wide.shape, 0) // group_rows
        c = lax.broadcasted_iota(jnp.int32, wide.shape, 1) // group_cols
        return jnp.where(r == c, wide, 0.0).astype(BF16)

    for k in range(SSM_BLOCKS):
        rows = slice(k * MXU_DIM, (k + 1) * MXU_DIM)
        bre_ref[k] = block_diag(bbr[rows], tile_b_ref[...], SSM_GROUP, SSM_STATE)
        bim_ref[k] = block_diag(bbi[rows], tile_b_ref[...], SSM_GROUP, SSM_STATE)
        crow = slice(k * SSM_BLOCK_STATES, (k + 1) * SSM_BLOCK_STATES)
        cre_ref[k] = block_diag(crt_ref[crow, :], tile_c_ref[...], SSM_STATE, SSM_GROUP)
        cim_ref[k] = block_diag(cit_ref[crow, :], tile_c_ref[...], SSM_STATE, SSM_GROUP)
    window = [(bbr, bbi)]
    for _ in range(1, S5_WINDOW):
        wr, wi = window[-1]
        window.append((ar * wr - ai * wi, ar * wi + ai * wr))
    tile_b = tile_b_ref[:, :S5_BLOCK_STATES]
    tile_c = tile_c_ref[:, :LANES]
    for k in range(S5_BLOCKS):
        rows = slice(k * LANES, (k + 1) * LANES)
        for j, (wr, wi) in enumerate(window):
            stack = slice(j * LANES, (j + 1) * LANES)
            brew_ref[k, stack, :] = block_diag(wr[rows], tile_b, SSM_GROUP, SSM_STATE)
            bimw_ref[k, stack, :] = block_diag(wi[rows], tile_b, SSM_GROUP, SSM_STATE)
        crow = slice(k * S5_BLOCK_STATES, (k + 1) * S5_BLOCK_STATES)
        crew_ref[k] = block_diag(crt_ref[crow, :], tile_c, SSM_STATE, SSM_GROUP)
        cimw_ref[k] = block_diag(cit_ref[crow, :], tile_c, SSM_STATE, SSM_GROUP)
    pr, pi = ar, ai
    pr_ref[0] = pr
    pi_ref[0] = pi
    for n in range(1, SUBLANES):
        pr, pi = pr * ar - pi * ai, pr * ai + pi * ar
        pr_ref[n] = pr
        pi_ref[n] = pi


def s5_prepare(lam_re, lam_im, log_dt, b_re, b_im, c_re, c_im):
    g, p, c = b_re.shape
    assert (g, p, c) == (SSM_GROUPS, SSM_STATE, SSM_GROUP)
    rep = lambda a: jnp.repeat(a, c, axis=0)
    tr = lambda b: jnp.transpose(b, (0, 2, 1)).reshape(g * c, p)
    trc = lambda m: jnp.transpose(m, (0, 2, 1)).reshape(g * p, c)
    tile_b = jnp.asarray(np.arange(SSM_BLOCK_STATES)[None, :] % p == np.arange(p)[:, None], BF16)
    tile_c = jnp.asarray(np.arange(MXU_DIM)[None, :] % c == np.arange(c)[:, None], BF16)
    pw = jax.ShapeDtypeStruct((SUBLANES, g * c, p), F32)
    b1 = jax.ShapeDtypeStruct((SSM_BLOCKS, MXU_DIM, SSM_BLOCK_STATES), BF16)
    c1 = jax.ShapeDtypeStruct((SSM_BLOCKS, SSM_BLOCK_STATES, MXU_DIM), BF16)
    bw = jax.ShapeDtypeStruct((S5_BLOCKS, S5_WINDOW * LANES, S5_BLOCK_STATES), BF16)
    cw = jax.ShapeDtypeStruct((S5_BLOCKS, S5_BLOCK_STATES, LANES), BF16)
    pr, pi, bre, bim, cre, cim, brew, bimw, crew, cimw = pl.pallas_call(
        _s5_prep_kernel, out_shape=[pw, pw, b1, b1, c1, c1, bw, bw, cw, cw], name="s5_prepare",
    )(rep(lam_re), rep(lam_im), jnp.broadcast_to(rep(log_dt[:, None]), (g * c, p)),
      tr(b_re), tr(b_im), trc(c_re), trc(c_im), tile_b, tile_c)
    pr = pr[:, ::c, :].reshape(SUBLANES, g * p)
    pi = pi[:, ::c, :].reshape(SUBLANES, g * p)
    return dict(pr=pr, pi=pi, step=(bre, bim, cre, cim), seq=(brew, bimw, crew, cimw))


def _s5_finish(y, u, dskip_ref, wglu_ref, bglu_ref):
    y = y + dskip_ref[...] * u
    y = jax.nn.gelu(y)
    return y * _sigmoid(_dot(y.astype(BF16), wglu_ref[...]) + bglu_ref[...])


def _s5_seq_kernel(x_ref, gin_ref, win_ref, pr_ref, pi_ref, lvr_ref, lvi_ref, bre_ref, bim_ref,
                   cre_ref, cim_ref, dskip_ref, wglu_ref, bglu_ref, y_ref, qm_ref, hr_ref, hi_ref,
                   xr_scr, xi_scr, y_scr, *, tc):
    @pl.when(pl.program_id(1) == 0)
    def _():
        hr_ref[...] = jnp.zeros_like(hr_ref)
        hi_ref[...] = jnp.zeros_like(hi_ref)

    hin = _rms(x_ref[0], gin_ref[...]).astype(BF16)
    u = _dot(hin, win_ref[:, :SEQ_WIDTH])
    qm_ref[0] = _dot(hin, win_ref[:, SEQ_WIDTH:])
    step = lax.broadcasted_iota(jnp.int32, (tc, 1), 0) % SUBLANES
    window = [u.astype(BF16)] + [
        jnp.where(step >= j, pltpu.roll(u, j, 0), 0.0).astype(BF16) for j in range(1, S5_WINDOW)]
    nb = S5_BLOCK_STATES
    last = SUBLANES - 1
    bc = lambda h: jnp.broadcast_to(h[last:last + 1, :], h.shape)
    block = lambda k: slice(k * nb, (k + 1) * nb)

    def project_in(k):
        cols = slice(k * LANES, (k + 1) * LANES)
        uk = jnp.concatenate([uj[:, cols] for uj in window], axis=1)
        xr_scr[k] = _dot(uk, bre_ref[k])
        xi_scr[k] = _dot(uk, bim_ref[k])

    def scan(k):
        sl = block(k)
        pr = pr_ref[:, sl]
        pi = pi_ref[:, sl]
        cr, ci = bc(hr_ref[0, :, sl]), bc(hi_ref[0, :, sl])
        for g in range(tc // SUBLANES):
            rows = slice(g * SUBLANES, (g + 1) * SUBLANES)
            xr = xr_scr[k, rows, :]
            xi = xi_scr[k, rows, :]
            sr = pltpu.roll(xr, 4, 0)
            si = pltpu.roll(xi, 4, 0)
            ar = lvr_ref[:, sl]
            ai = lvi_ref[:, sl]
            xr, xi = xr + ar * sr - ai * si, xi + ar * si + ai * sr
            hr = xr + pr * cr - pi * ci
            hi = xi + pr * ci + pi * cr
            xr_scr[k, rows, :] = hr
            xi_scr[k, rows, :] = hi
            cr, ci = bc(hr), bc(hi)
        hr_ref[0, :, sl] = cr
        hi_ref[0, :, sl] = ci

    def project_out(k):
        y_scr[:, k * LANES:(k + 1) * LANES] = (
            _dot(xr_scr[k].astype(BF16), cre_ref[k]) - _dot(xi_scr[k].astype(BF16), cim_ref[k]))

    project_in(0)
    for k in range(S5_BLOCKS):
        if k + 1 < S5_BLOCKS:
            project_in(k + 1)
        scan(k)
        project_out(k)

    y_ref[0] = _s5_finish(y_scr[...], u, dskip_ref, wglu_ref, bglu_ref)


def s5_mix_seq(x, g_in, w_in, prep, d_skip, w_glu, b_glu, tc_cap=512):
    b, t, d = x.shape
    w = SEQ_WIDTH
    tc = _row_tile(t, tc_cap)
    pr, pi = prep["pr"], prep["pi"]
    bre, bim, cre, cim = prep["seq"]
    late = np.arange(SUBLANES)[:, None] >= S5_WINDOW
    lvr = jnp.where(late, pr[S5_WINDOW - 1][None], 0.0)
    lvi = jnp.where(late, pi[S5_WINDOW - 1][None], 0.0)
    full = lambda a: pl.BlockSpec(a.shape, lambda i, j: (0,) * a.ndim)
    consts = [g_in.reshape(1, d), w_in, pr, pi, lvr, lvi, bre, bim, cre, cim, d_skip.reshape(1, w),
              w_glu, b_glu.reshape(1, w)]
    tok = lambda n: pl.BlockSpec((1, tc, n), lambda i, j: (i, j, 0))
    y, qm, hr, hi = pl.pallas_call(
        functools.partial(_s5_seq_kernel, tc=tc),
        grid=(b, t // tc),
        in_specs=[tok(d)] + [full(a) for a in consts],
        out_specs=[tok(w), tok(MEM_WIDTH),
                   pl.BlockSpec((1, SUBLANES, SSM_WIDTH), lambda i, j: (i, 0, 0)),
                   pl.BlockSpec((1, SUBLANES, SSM_WIDTH), lambda i, j: (i, 0, 0))],
        out_shape=[jax.ShapeDtypeStruct((b, t, w), F32),
                   jax.ShapeDtypeStruct((b, t, MEM_WIDTH), F32),
                   jax.ShapeDtypeStruct((b, SUBLANES, SSM_WIDTH), F32),
                   jax.ShapeDtypeStruct((b, SUBLANES, SSM_WIDTH), F32)],
        scratch_shapes=[pltpu.VMEM((S5_BLOCKS, tc, S5_BLOCK_STATES), F32),
                        pltpu.VMEM((S5_BLOCKS, tc, S5_BLOCK_STATES), F32),
                        pltpu.VMEM((tc, w), F32)],
        compiler_params=_params(("parallel", "arbitrary")),
        name="s5_mix_seq",
    )(x, *consts)
    return y, qm, hr[:, SUBLANES - 1], hi[:, SUBLANES - 1]


def _s5_step_kernel(u_ref, h0r_ref, h0i_ref, ar_ref, ai_ref, bre_ref, bim_ref, cre_ref, cim_ref,
                    dskip_ref, wglu_ref, bglu_ref, y_ref, hr_ref, hi_ref, y_scr):
    u = u_ref[...]
    ub = u.astype(BF16)
    nb = SSM_BLOCK_STATES
    for k in range(SSM_BLOCKS):
        sl = slice(k * nb, (k + 1) * nb)
        uk = ub[:, k * MXU_DIM:(k + 1) * MXU_DIM]
        ar, ai = ar_ref[:, sl], ai_ref[:, sl]
        h0r, h0i = h0r_ref[:, sl], h0i_ref[:, sl]
        hr = _dot(uk, bre_ref[k]) + ar * h0r - ai * h0i
        hi = _dot(uk, bim_ref[k]) + ar * h0i + ai * h0r
        hr_ref[:, sl] = hr
        hi_ref[:, sl] = hi
        y_scr[:, k * MXU_DIM:(k + 1) * MXU_DIM] = (
            _dot(hr.astype(BF16), cre_ref[k]) - _dot(hi.astype(BF16), cim_ref[k]))
    y_ref[...] = _s5_finish(y_scr[...], u, dskip_ref, wglu_ref, bglu_ref)


def s5_mix_step(u, h0r, h0i, prep, d_skip, w_glu, b_glu):
    s, w = u.shape
    pr, pi = prep["pr"], prep["pi"]
    bre, bim, cre, cim = prep["step"]
    st = jax.ShapeDtypeStruct((s, SSM_WIDTH), F32)
    return pl.pallas_call(
        _s5_step_kernel,
        out_shape=[jax.ShapeDtypeStruct((s, w), F32), st, st],
        scratch_shapes=[pltpu.VMEM((s, w), F32)],
        name="s5_mix_step",
    )(u, h0r, h0i, pr[0:1], pi[0:1], bre, bim, cre, cim, d_skip.reshape(1, w), w_glu,
      b_glu.reshape(1, w))


FOX_TQ = 2 * KV_TILE
FOX_STRIP = MXU_DIM


def _fox_seq_kernel(q_ref, kaug_ref, vt_ref, tri_ref, o_ref, s_scr, p_scr, acc_scr, m_scr, l_scr,
                    smax_scr, *, tq, tk):
    i = pl.program_id(2)
    qt = q_ref[0].astype(F32)
    row = lax.broadcasted_iota(jnp.int32, (LANES, 1), 0)
    qa = []
    for h in range(2):
        own = (row >= h * HEAD_DIM) & (row < (h + 1) * HEAD_DIM)
        head = 2 * pl.program_id(1) + h
        ones = (row >= head * BIAS_PIECES) & (row < (head + 1) * BIAS_PIECES)
        qa.append(jnp.concatenate(
            [jnp.where(own, qt, 0.0), jnp.broadcast_to(jnp.where(ones, 1.0, 0.0), qt.shape)],
            axis=0).astype(BF16))
    acc_scr[...] = jnp.zeros_like(acc_scr)
    m_scr[...] = jnp.full_like(m_scr, NEG_INF)
    l_scr[...] = jnp.zeros_like(l_scr)
    n_sub = tq // FOX_STRIP
    strips = [(h, qs) for h in range(2) for qs in range(n_sub)]
    lanes = lambda qs: slice(qs * FOX_STRIP, (qs + 1) * FOX_STRIP)
    ones_rows = jnp.ones((2 * SUBLANES, tk), BF16)

    def visible_keys(qs, c):
        if c is None:
            return tk, 0
        first = qs * FOX_STRIP - c * tk
        if first < 0:
            return 0, 0
        return (tk, 0) if first >= tk else (first, FOX_STRIP)

    def scores(j, slot, c=None):
        ka = kaug_ref[0, pl.ds(pl.multiple_of(j * tk, tk), tk), :]
        for n, (h, qs) in enumerate(strips):
            if sum(visible_keys(qs, c)):
                s = _dot(ka, qa[h][:, lanes(qs)])
                s_scr[slot, n] = s
                if c is None:
                    smax_scr[slot, n] = jnp.max(s, axis=0, keepdims=True)

    def absorb(j, slot, c=None):
        alphas, n_keys = [], []
        for n, (h, qs) in enumerate(strips):
            n_full_keys, n_tri = visible_keys(qs, c)
            n_keys.append(n_full_keys + n_tri)
            if not n_keys[n]:
                alphas.append(None)
                continue
            m = m_scr[n]
            full_rows = slice(0, n_full_keys)
            tri_rows = slice(n_full_keys, n_full_keys + n_tri)
            m_new = m
            if n_tri:
                tri = s_scr[slot, n, tri_rows, :] + tri_ref[...]
                m_new = jnp.maximum(m_new, jnp.max(tri, axis=0, keepdims=True))
            if n_full_keys:
                m_new = jnp.maximum(m_new, smax_scr[slot, n] if c is None else jnp.max(
                    s_scr[slot, n, full_rows, :], axis=0, keepdims=True))
                p_scr[n, full_rows, :] = jnp.exp2(
                    s_scr[slot, n, full_rows, :] - m_new).astype(BF16)
            if n_tri:
                p_scr[n, tri_rows, :] = jnp.exp2(tri - m_new).astype(BF16)
            m_scr[n] = m_new
            alphas.append(jnp.exp2(m - m_new))
        for n, (h, qs) in enumerate(strips):
            if not n_keys[n]:
                continue
            vt = jnp.concatenate(
                [vt_ref[0, j, h * HEAD_DIM:(h + 1) * HEAD_DIM, :n_keys[n]],
                 ones_rows[:, :n_keys[n]]], axis=0)
            pv = _dot(vt, p_scr[n, :n_keys[n], :])
            acc_scr[h, :, lanes(qs)] = alphas[n] * acc_scr[h, :, lanes(qs)] + pv[:HEAD_DIM]
            l_scr[n] = alphas[n] * l_scr[n] + pv[HEAD_DIM:HEAD_DIM + 1]

    n_before = 2 * i
    scores(0, 0)

    def pair(jj, _):
        j = 2 * jj
        scores(j + 1, 1)
        absorb(j, 0)
        scores(j + 2, 0)
        absorb(j + 1, 1)
        return 0

    lax.fori_loop(0, i, pair, 0)
    if tq == tk:
        absorb(0, 0, c=0)
    else:
        scores(n_before + 1, 1, c=1)
        absorb(n_before, 0, c=0)
        absorb(n_before + 1, 1, c=1)

    l_head = lambda h: jnp.concatenate(
        [l_scr[h * n_sub + qs] for qs in range(n_sub)], axis=1)
    ot = jnp.concatenate([acc_scr[0] / l_head(0), acc_scr[1] / l_head(1)], axis=0)
    o_ref[0] = jnp.transpose(ot)


def fox_attention_seq(qt, kaug, vtb):
    b, w, t = qt.shape
    n_chunks, tk = vtb.shape[1], vtb.shape[3]
    tq = _row_tile(t, FOX_TQ)
    assert tq in (tk, 2 * tk) and n_chunks * tk == t and tk % FOX_STRIP == 0
    n_strips = 2 * tq // FOX_STRIP
    idx = np.arange(FOX_STRIP)
    tri = jnp.asarray(np.where(idx[:, None] <= idx[None, :], 0.0, NEG_INF), F32)
    return pl.pallas_call(
        functools.partial(_fox_seq_kernel, tq=tq, tk=tk),
        grid=(b, HEAD_PAIRS, t // tq),
        in_specs=[pl.BlockSpec((1, LANES, tq), lambda bi, p, i: (bi, p, i)),
                  pl.BlockSpec((1, t, KAUG_WIDTH), lambda bi, p, i: (bi, 0, p)),
                  pl.BlockSpec((1, n_chunks, LANES, tk), lambda bi, p, i: (bi, 0, p, 0)),
                  pl.BlockSpec((FOX_STRIP, FOX_STRIP), lambda bi, p, i: (0, 0))],
        out_specs=pl.BlockSpec((1, tq, LANES), lambda bi, p, i: (bi, i, p)),
        out_shape=jax.ShapeDtypeStruct((b, t, w), F32),
        scratch_shapes=[pltpu.VMEM((2, n_strips, tk, FOX_STRIP), F32),
                        pltpu.VMEM((n_strips, tk, FOX_STRIP), BF16),
                        pltpu.VMEM((2, HEAD_DIM, tq), F32),
                        pltpu.VMEM((n_strips, 1, FOX_STRIP), F32),
                        pltpu.VMEM((n_strips, 1, FOX_STRIP), F32),
                        pltpu.VMEM((2, n_strips, 1, FOX_STRIP), F32)],
        compiler_params=_params(("parallel", "parallel", "arbitrary")),
        name="fox_attention_seq",
    )(qt, kaug, vtb, tri)


DEC_PAGES = 16
DEC_SLOTS = 3


def _fox_dec_kernel(pt_ref, q_ref, kn_ref, vn_ref, lfn_ref, tri_ref, kt_hbm, vt_hbm, lf_hbm, o_ref,
                    m_scr, l_scr, csum_scr, acc_scr, qcol_scr, kbuf, vbuf, lbuf, sems,
                    *, page, n_pages):
    g = pl.program_id(1)
    n_g = pl.num_programs(1)
    t = pl.program_id(0) * n_g + g
    n_steps = pl.num_programs(0) * n_g
    hp = FOX_HEADS
    qrow = q_ref[0] * ATTN_SCALE

    def page_copies(step):
        slot = step % DEC_SLOTS
        seq, grp = step // n_g, step % n_g
        cps = []
        for i in range(n_pages):
            pid = pt_ref[seq, grp * n_pages + i]
            cps.append(pltpu.make_async_copy(kt_hbm.at[pid], kbuf.at[slot, i], sems.at[0, slot]))
            cps.append(pltpu.make_async_copy(vt_hbm.at[pid], vbuf.at[slot, i], sems.at[1, slot]))
            cps.append(pltpu.make_async_copy(lf_hbm.at[pid], lbuf.at[slot, i], sems.at[2, slot]))
        return cps

    def start_all(step):
        for n, cp in enumerate(page_copies(step)):
            cp.start(priority=min(n % 3, 1))

    @pl.when(t == 0)
    def _():
        for step in range(DEC_SLOTS - 1):
            @pl.when(step < n_steps)
            def _():
                start_all(step)

    @pl.when(t + (DEC_SLOTS - 1) < n_steps)
    def _():
        start_all(t + (DEC_SLOTS - 1))

    for cp in page_copies(t):
        cp.wait()
    slot = t % DEC_SLOTS

    @pl.when(g == 0)
    def _():
        m_scr[...] = jnp.full_like(m_scr, NEG_INF)
        l_scr[...] = jnp.zeros_like(l_scr)
        csum_scr[...] = jnp.zeros_like(csum_scr)
        acc_scr[...] = jnp.zeros_like(acc_scr)
        qcol_scr[...] = jnp.transpose(jnp.broadcast_to(qrow, (page, SEQ_WIDTH)))

    tri = tri_ref[...]
    base = csum_scr[...]
    head_row = lax.broadcasted_iota(jnp.int32, (hp, page), 0)
    s_parts = []
    for i in range(n_pages):
        hi, mid, lo = _split3(lbuf[slot, i])
        cum = (_dot(hi, tri) + _dot(mid, tri)) + _dot(lo, tri) + base
        base = jnp.broadcast_to(cum[:, page - 1:page], cum.shape)
        qk = jnp.zeros((hp, page), F32)
        for h in range(FOX_HEADS):
            rows = slice(h * HEAD_DIM, (h + 1) * HEAD_DIM)
            r = jnp.sum(kbuf[slot, i, rows, :] * qcol_scr[rows, :], axis=0, keepdims=True)
            qk = jnp.where(head_row == h, r, qk)
        s_parts.append(qk - cum)
    csum_scr[...] = base
    s = jnp.concatenate(s_parts, axis=1)

    m_old = m_scr[...]
    m_new = jnp.maximum(m_old, jnp.max(s, axis=-1, keepdims=True))
    alpha = jnp.exp(m_old - m_new)
    p = jnp.exp(s - m_new)
    l_scr[...] = alpha * l_scr[...] + jnp.sum(p, axis=-1, keepdims=True)
    m_scr[...] = m_new
    for h in range(FOX_HEADS):
        rows = slice(h * HEAD_DIM, (h + 1) * HEAD_DIM)
        acc = acc_scr[rows, :] * alpha[h:h + 1, :]
        for i in range(n_pages):
            acc = acc + vbuf[slot, i, rows, :] * p[h:h + 1, i * page:(i + 1) * page]
        acc_scr[rows, :] = acc

    @pl.when(g == pl.num_programs(1) - 1)
    def _():
        lane = lax.broadcasted_iota(jnp.int32, (hp, SEQ_WIDTH), 1)
        head = lax.broadcasted_iota(jnp.int32, (hp, SEQ_WIDTH), 0)
        own = (lane >= head * HEAD_DIM) & (lane < (head + 1) * HEAD_DIM)
        spread = lambda col: jnp.sum(jnp.where(own, col, 0.0), axis=0, keepdims=True)
        c_new = csum_scr[:, 0:1] + lfn_ref[0]
        s_new = jnp.sum(jnp.where(own, qrow * kn_ref[0], 0.0), axis=-1, keepdims=True) - c_new
        m_fin = jnp.maximum(m_scr[...], s_new)
        a_fin = jnp.exp(m_scr[...] - m_fin)
        p_new = jnp.exp(s_new - m_fin)
        l_fin = a_fin * l_scr[...] + p_new
        acc_row = jnp.sum(jnp.transpose(acc_scr[...]), axis=0, keepdims=True)
        o_ref[0] = (spread(a_fin) * acc_row + spread(p_new) * vn_ref[0]) / spread(l_fin)


def fox_attention_decode(q, k_new, v_new, logf_new, cache_k, cache_v, cache_logf, page_table):
    s, w = q.shape
    n_pool, page = cache_k.shape[:2]
    assert page == LANES
    pages_per_seq = page_table.shape[1]
    n_pages = min(DEC_PAGES, pages_per_seq)
    assert pages_per_seq % n_pages == 0
    hp = FOX_HEADS
    ckt = jnp.transpose(cache_k, (0, 2, 3, 1)).reshape(n_pool, w, page)
    cvt = jnp.transpose(cache_v, (0, 2, 3, 1)).reshape(n_pool, w, page)
    clf = jnp.transpose(cache_logf, (0, 2, 1))
    lfn = logf_new.reshape(s, hp, 1)
    tri = jnp.asarray(np.triu(np.ones((page, page), np.float32)), BF16)
    row = pl.BlockSpec((1, 1, w), lambda b, g, pt: (b, 0, 0))
    in_hbm = pl.BlockSpec(memory_space=pl.ANY)
    grid_spec = pltpu.PrefetchScalarGridSpec(
        num_scalar_prefetch=1,
        grid=(s, pages_per_seq // n_pages),
        in_specs=[row, row, row, pl.BlockSpec((1, hp, 1), lambda b, g, pt: (b, 0, 0)),
                  pl.BlockSpec((page, page), lambda b, g, pt: (0, 0)), in_hbm, in_hbm, in_hbm],
        out_specs=row,
        scratch_shapes=[pltpu.VMEM((hp, 1), F32), pltpu.VMEM((hp, 1), F32),
                        pltpu.VMEM((hp, page), F32), pltpu.VMEM((w, page), F32),
                        pltpu.VMEM((w, page), F32),
                        pltpu.VMEM((DEC_SLOTS, n_pages, w, page), F32),
                        pltpu.VMEM((DEC_SLOTS, n_pages, w, page), F32),
                        pltpu.VMEM((DEC_SLOTS, n_pages, hp, page), F32),
                        pltpu.SemaphoreType.DMA((3, DEC_SLOTS))],
    )
    out = pl.pallas_call(
        functools.partial(_fox_dec_kernel, page=page, n_pages=n_pages),
        grid_spec=grid_spec,
        out_shape=jax.ShapeDtypeStruct((s, 1, w), F32),
        compiler_params=_params(("arbitrary", "arbitrary"), VMEM_LIMIT),
        name="fox_attention_decode",
    )(page_table, q.reshape(s, 1, w), k_new.reshape(s, 1, w), v_new.reshape(s, 1, w), lfn, tri,
      ckt, cvt, clf)
    return out.reshape(s, w)


def _trunk(x, mem_k, mem_v, conv_prev, ssm_state, fox_attend, p, s5_prep, sequential):
    assert N_A == 1
    b, t, d = x.shape
    m = b * t
    new_conv, ssm_out, kv = [], None, None
    for l in range(DEPTH):
        if sequential and l == N_A:
            kv = shared_kv_proj_seq(x, p["kv_norm"], p["w_kv"], p["b_f"],
                                    p["norm_mix_pre"][l], p["w_in"][l])
            z_seq, q_mem = kv[5], kv[6]
        elif not sequential:
            if l == N_A:
                kv = shared_kv_proj_step(x.reshape(m, d), p["kv_norm"], p["w_kv"], p["b_f"])
            z_seq, q_mem = norm_linear(x.reshape(m, d), p["norm_mix_pre"][l], p["w_in"][l],
                                       (SEQ_WIDTH, MEM_WIDTH))
        if l < N_A:
            if sequential:
                seq_out, q_mem, hr, hi = s5_mix_seq(x, p["norm_mix_pre"][l], p["w_in"][l],
                                                    s5_prep[l], p["d_skip"][l], p["w_glu"][l],
                                                    p["b_glu"][l])
            else:
                seq_out, hr, hi = s5_mix_step(z_seq, ssm_state[0][l], ssm_state[1][l], s5_prep[l],
                                              p["d_skip"][l], p["w_glu"][l], p["b_glu"][l])
            ssm_out = (hr.reshape(b, SSM_GROUPS, SSM_STATE), hi.reshape(b, SSM_GROUPS, SSM_STATE))
        else:
            seq_out = fox_attend(z_seq, kv)
        ffn_args = (p["norm_ffn_pre"][l], p["norm_ffn_post"][l], p["w_up"], p["conv_w"][l],
                    p["conv_b"][l], p["w_down"], l)
        if sequential:
            x3, cp = layer_tail_seq(x, seq_out.reshape(b, t, SEQ_WIDTH),
                                    q_mem.reshape(b, t, MEM_WIDTH), mem_k[l], mem_v[l],
                                    p["w_out"][l], p["norm_mix_post"][l], conv_prev[l], *ffn_args)
        else:
            mem_out = mem_attention(q_mem.reshape(b, t, MEM_WIDTH), mem_k, mem_v, l)
            x2 = mix_out(x.reshape(m, d), seq_out, mem_out.reshape(m, MEM_WIDTH),
                         p["w_out"][l], p["norm_mix_post"][l])
            x3, u_new = conv_ffn_step(x2, conv_prev[l], *ffn_args)
            cp = jnp.stack([conv_prev[l][:, 1], u_new], axis=1)
        new_conv.append(cp)
        x = x3.reshape(b, t, d)
    return x, ssm_out, jnp.stack(new_conv), kv


def kernel(x_prompt, x_sample, state_ssm_re, state_ssm_im, cache_k, cache_v, cache_logf,
           cache_mem_k, cache_mem_v, state_ffn_conv, page_table, mem_prompt,
           w_in, w_out, norm_mix_pre, norm_mix_post, norm_ffn_pre, norm_ffn_post,
           mem_norm, w_mem_kv, lam_re, lam_im, log_dt, b_re, b_im, c_re, c_im, d_skip,
           w_glu, b_glu, kv_norm, w_kv, b_f, w_up, conv_w, conv_b, w_down):
    per_layer_bf16 = lambda w: [w[l].astype(BF16) for l in range(w.shape[0])]
    p = dict(w_in=per_layer_bf16(w_in), w_out=per_layer_bf16(w_out), norm_mix_pre=norm_mix_pre,
             norm_mix_post=norm_mix_post, norm_ffn_pre=norm_ffn_pre, norm_ffn_post=norm_ffn_post,
             d_skip=d_skip, w_glu=per_layer_bf16(w_glu), b_glu=b_glu, kv_norm=kv_norm, w_kv=w_kv,
             b_f=b_f, w_up=w_up.astype(BF16), conv_w=conv_w, conv_b=conv_b,
             w_down=w_down.astype(BF16))
    s5_prep = [s5_prepare(lam_re[l], lam_im[l], log_dt[l], b_re[l], b_im[l], c_re[l], c_im[l])
               for l in range(N_A)]

    bp, tp, d = x_prompt.shape
    n_mem = mem_prompt.shape[1]
    mem_pairs = [mem_kv_proj(mem_prompt, mem_norm[l], w_mem_kv[l]) for l in range(DEPTH)]
    p_mem_kt = [mkt for mkt, _ in mem_pairs]
    p_mem_vt = [mvt for _, mvt in mem_pairs]
    zeros_conv = jnp.zeros((DEPTH, bp, CONV_W - 1, 2 * D_FF), F32)

    def fox_prompt(qt, kv):
        return fox_attention_seq(qt, kv[2], kv[3])

    y_prompt, p_ssm, p_conv, p_kv = _trunk(x_prompt, p_mem_kt, p_mem_vt, zeros_conv, None,
                                           fox_prompt, p, s5_prep, sequential=True)
    untr = lambda a, n: jnp.transpose(a.reshape(a.shape[0], n, HEAD_DIM, a.shape[2]), (0, 3, 1, 2))
    mem5 = lambda ms: jnp.stack([untr(a, MEM_HEADS) for a in ms])

    bs = x_sample.shape[0]
    tr_mem = lambda a: jnp.transpose(a, (0, 1, 3, 4, 2)).reshape(DEPTH, bs, MEM_WIDTH, n_mem)
    s_mem_kt = tr_mem(cache_mem_k)
    s_mem_vt = tr_mem(cache_mem_v)
    ssm0 = (state_ssm_re.reshape(N_A, bs, SSM_WIDTH), state_ssm_im.reshape(N_A, bs, SSM_WIDTH))

    def fox_sample(q, kv):
        k, v, logf = kv
        return fox_attention_decode(q, k, v, logf, cache_k, cache_v, cache_logf, page_table)

    y_sample, s_ssm, s_conv, s_kv = _trunk(x_sample, s_mem_kt, s_mem_vt, state_ffn_conv, ssm0,
                                           fox_sample, p, s5_prep, sequential=False)
    head4 = lambda a: a.reshape(bs, 1, FOX_HEADS, HEAD_DIM)

    return (y_prompt, y_sample, p_ssm[0][None], p_ssm[1][None],
            untr(p_kv[0], FOX_HEADS), untr(p_kv[1], FOX_HEADS), p_kv[4],
            mem5(p_mem_kt), mem5(p_mem_vt), p_conv,
            s_ssm[0][None], s_ssm[1][None],
            head4(s_kv[0]), head4(s_kv[1]), s_kv[2].reshape(bs, 1, FOX_HEADS), s_conv)
```
